```python
import math
import jax, jax.numpy as jnp
from jax import lax
import numpy as np

D_MODEL = 1024
BATCH = 2
SEQ = 8192
DEPTH = 1

GRID_W = 64
HEAD_DIM = 128
HALF_ROT = HEAD_DIM // 2
RET_HEADS = 4
RET_DIM = RET_HEADS * HEAD_DIM
ATT_HEADS = 4
ATT_KV_HEADS = 2
ATT_DIM = ATT_HEADS * HEAD_DIM
ATT_KV_DIM = ATT_KV_HEADS * HEAD_DIM
CHUNK = 128
Q_BLOCK = 128
ROPE_THETA = 10000.0
N_EXPERTS = 16
EXPERT_FF = 2816
CAPACITY_FACTOR = 2
EPS = 1e-6
IN_WIDTH = 4 * RET_DIM + ATT_DIM + 2 * ATT_KV_DIM + 2 * D_MODEL
SPLIT_POINTS = (RET_DIM, 2 * RET_DIM, 3 * RET_DIM, 4 * RET_DIM,
                4 * RET_DIM + ATT_DIM,
                4 * RET_DIM + ATT_DIM + ATT_KV_DIM,
                4 * RET_DIM + ATT_DIM + 2 * ATT_KV_DIM,
                4 * RET_DIM + ATT_DIM + 2 * ATT_KV_DIM + D_MODEL)

kernel_name = "hybrid_retention_axialgqa_ecmoe_encoder"


def rmsnorm(x, g):
    xf = x.astype(jnp.float32)
    y = xf * lax.rsqrt(jnp.mean(xf * xf, axis=-1, keepdims=True) + EPS)
    return (y * g.astype(jnp.float32)).astype(x.dtype)


def axial_rope_tables(seq):
    rows = seq // GRID_W
    row = jnp.repeat(jnp.arange(rows, dtype=jnp.float32), GRID_W)
    col = jnp.tile(jnp.arange(GRID_W, dtype=jnp.float32), rows)
    n_freq = HALF_ROT // 2
    freqs = ROPE_THETA ** (-jnp.arange(n_freq, dtype=jnp.float32) / n_freq)
    ang = jnp.concatenate([row[:, None] * freqs, col[:, None] * freqs], axis=-1)
    return jnp.cos(ang), jnp.sin(ang)


def apply_rope(x, cos, sin):
    c = cos[None, :, None, :]
    s = sin[None, :, None, :]
    x1 = x[..., :HALF_ROT].astype(jnp.float32)
    x2 = x[..., HALF_ROT:].astype(jnp.float32)
    return jnp.concatenate([x1 * c - x2 * s, x1 * s + x2 * c], axis=-1).astype(x.dtype)


def retention_chunkwise(q, k, v, log_gamma, strict):
    B, H, S, dk = q.shape
    dv = v.shape[-1]
    n = S // CHUNK
    q = q.reshape(B, H, n, CHUNK, dk)
    k = k.reshape(B, H, n, CHUNK, dk)
    v = v.reshape(B, H, n, CHUNK, dv)
    idx = jnp.arange(CHUNK, dtype=jnp.float32)
    lg = log_gamma.astype(jnp.float32)
    diff = idx[:, None] - idx[None, :]
    mask = (diff > 0) if strict else (diff >= 0)
    decay = jnp.where(mask[None], jnp.exp(lg[:, None, None] * jnp.maximum(diff, 0.0)[None]), 0.0)
    scores = jnp.einsum('bhncd,bhnmd->bhncm', q, k) * decay[None, :, None]
    intra = jnp.einsum('bhncm,bhnme->bhnce', scores, v)
    k_dec = k * jnp.exp(lg[:, None] * (CHUNK - 1.0 - idx)[None])[None, :, None, :, None]
    kv = jnp.einsum('bhncd,bhnce->nbhde', k_dec, v)
    chunk_decay = jnp.exp(lg * CHUNK)[None, :, None, None]

    def step(state, kv_c):
        return state * chunk_decay + kv_c, state

    _, prev = lax.scan(step, jnp.zeros((B, H, dk, dv), jnp.float32), kv)
    q_dec = q * jnp.exp(lg[:, None] * (idx + 1.0)[None])[None, :, None, :, None]
    inter = jnp.einsum('bhncd,nbhde->bhnce', q_dec, prev)
    return (intra + inter).reshape(B, H, S, dv)


def bidirectional_retention(q, k, v, p_fwd, p_bwd):
    lg_f = jnp.log1p(-jnp.exp(p_fwd.astype(jnp.float32)))
    lg_b = jnp.log1p(-jnp.exp(p_bwd.astype(jnp.float32)))
    qt = q.astype(jnp.float32).transpose(0, 2, 1, 3)
    kt = k.astype(jnp.float32).transpose(0, 2, 1, 3)
    vt = v.astype(jnp.float32).transpose(0, 2, 1, 3)
    fwd = retention_chunkwise(qt, kt, vt, lg_f, strict=False)
    bwd = retention_chunkwise(qt[:, :, ::-1], kt[:, :, ::-1], vt[:, :, ::-1], lg_b, strict=True)[:, :, ::-1]
    return (fwd + bwd).transpose(0, 2, 1, 3)


def group_norm_heads(y, gain):
    mu = jnp.mean(y, axis=-1, keepdims=True)
    var = jnp.mean(jnp.square(y - mu), axis=-1, keepdims=True)
    yn = (y - mu) * lax.rsqrt(var + EPS)
    B, S = y.shape[:2]
    return yn.reshape(B, S, RET_DIM) * gain.astype(jnp.float32)


def gqa_block_attention(q, k, v):
    B, S = q.shape[:2]
    nblk = S // Q_BLOCK
    R = ATT_HEADS // ATT_KV_HEADS
    scale = HEAD_DIM ** -0.5
    qb = q.reshape(B, nblk, Q_BLOCK, ATT_KV_HEADS, R, HEAD_DIM).transpose(1, 0, 3, 4, 2, 5)
    kt = k.transpose(0, 2, 1, 3)
    vt = v.transpose(0, 2, 1, 3)

    def one_block(qblk):
        s = jnp.einsum('bgrqd,bgkd->bgrqk', qblk, kt).astype(jnp.float32) * scale
        p = jax.nn.softmax(s, axis=-1).astype(vt.dtype)
        return jnp.einsum('bgrqk,bgkd->bgrqd', p, vt)

    o = lax.map(one_block, qb)
    return o.transpose(1, 0, 4, 2, 3, 5).reshape(B, S, ATT_DIM)


def expert_choice_ffn(h, w_router, w_gate, w_up, w_down):
    B, S, D = h.shape
    cap = CAPACITY_FACTOR * S // N_EXPERTS
    logits = jnp.einsum('bsd,de->bse', h, w_router)
    aff = jax.nn.softmax(logits.astype(jnp.float32), axis=-1)
    gates, idx = lax.top_k(aff.transpose(0, 2, 1), cap)
    xe = jax.vmap(lambda hb, ib: hb[ib])(h, idx)
    a = jnp.einsum('becd,edf->becf', xe, w_gate)
    u = jnp.einsum('becd,edf->becf', xe, w_up)
    y = jnp.einsum('becf,efd->becd', jax.nn.silu(a) * u, w_down)
    y = y * gates[..., None].astype(y.dtype)
    flat = (idx + (jnp.arange(B, dtype=jnp.int32) * S)[:, None, None]).reshape(-1)
    out = jnp.zeros((B * S, D), y.dtype).at[flat].add(y.reshape(-1, D))
    return out.reshape(B, S, D)


def setup_inputs(seed: int = 0) -> dict:
    key = jax.random.key(seed)
    ks = jax.random.split(key, 20)

    def nrm(k, shape, scale):
        return jax.random.normal(k, shape, jnp.float32) * scale

    base_p = -(5.0 + jnp.arange(RET_HEADS, dtype=jnp.float32)) * math.log(2.0)
    return {
        "x": nrm(ks[0], (BATCH, SEQ, D_MODEL), 1.0),
        "norm_mix": 1.0 + nrm(ks[1], (DEPTH, D_MODEL), 0.02),
        "w_in": nrm(ks[2], (DEPTH, D_MODEL, IN_WIDTH), D_MODEL ** -0.5),
        "ret_decay_fwd": base_p[None] + nrm(ks[3], (DEPTH, RET_HEADS), 0.1),
        "ret_decay_bwd": base_p[None] + nrm(ks[4], (DEPTH, RET_HEADS), 0.1),
        "ret_gn_gain": 1.0 + nrm(ks[5], (DEPTH, RET_DIM), 0.02),
        "w_ret_branch": nrm(ks[6], (DEPTH, RET_DIM, D_MODEL), RET_DIM ** -0.5),
        "q_norm": 1.0 + nrm(ks[7], (DEPTH, HEAD_DIM), 0.02),
        "k_norm": 1.0 + nrm(ks[8], (DEPTH, HEAD_DIM), 0.02),
        "w_att_branch": nrm(ks[9], (DEPTH, ATT_DIM, D_MODEL), ATT_DIM ** -0.5),
        "w_o": nrm(ks[10], (DEPTH, D_MODEL, D_MODEL), D_MODEL ** -0.5),
        "norm_ffn": 1.0 + nrm(ks[11], (DEPTH, D_MODEL), 0.02),
        "w_router": nrm(ks[12], (DEPTH, D_MODEL, N_EXPERTS), D_MODEL ** -0.5),
        "w_expert_gate": nrm(ks[13], (DEPTH, N_EXPERTS, D_MODEL, EXPERT_FF), D_MODEL ** -0.5),
        "w_expert_up": nrm(ks[14], (DEPTH, N_EXPERTS, D_MODEL, EXPERT_FF), D_MODEL ** -0.5),
        "w_expert_down": nrm(ks[15], (DEPTH, N_EXPERTS, EXPERT_FF, D_MODEL), EXPERT_FF ** -0.5),
        "norm_final": 1.0 + nrm(ks[16], (D_MODEL,), 0.02),
    }


def reference(x, norm_mix, w_in, ret_decay_fwd, ret_decay_bwd, ret_gn_gain, w_ret_branch,
              q_norm, k_norm, w_att_branch, w_o, norm_ffn, w_router, w_expert_gate,
              w_expert_up, w_expert_down, norm_final):
    B, S, _ = x.shape
    cos, sin = axial_rope_tables(S)
    for l in range(DEPTH):
        h = rmsnorm(x, norm_mix[l])
        proj = jnp.einsum('bsd,dk->bsk', h, w_in[l])
        rq, rk, rv, rg, aq, ak, av, gr, ga = jnp.split(proj, SPLIT_POINTS, axis=-1)

        rq = apply_rope(rq.reshape(B, S, RET_HEADS, HEAD_DIM), cos, sin)
        rk = apply_rope(rk.reshape(B, S, RET_HEADS, HEAD_DIM), cos, sin) * (HEAD_DIM ** -0.5)
        rv = rv.reshape(B, S, RET_HEADS, HEAD_DIM)
        y_ret = bidirectional_retention(rq, rk, rv, ret_decay_fwd[l], ret_decay_bwd[l])
        y_ret = group_norm_heads(y_ret, ret_gn_gain[l]).astype(x.dtype)
        y_ret = jnp.einsum('bsk,kd->bsd', jax.nn.silu(rg) * y_ret, w_ret_branch[l])

        aq = apply_rope(rmsnorm(aq.reshape(B, S, ATT_HEADS, HEAD_DIM), q_norm[l]), cos, sin)
        ak = apply_rope(rmsnorm(ak.reshape(B, S, ATT_KV_HEADS, HEAD_DIM), k_norm[l]), cos, sin)
        av = av.reshape(B, S, ATT_KV_HEADS, HEAD_DIM)
        y_att = gqa_block_attention(aq, ak, av)
        y_att = jnp.einsum('bsk,kd->bsd', y_att, w_att_branch[l])

        mixed = jax.nn.sigmoid(gr) * y_ret + jax.nn.sigmoid(ga) * y_att
        x = x + jnp.einsum('bsd,de->bse', mixed, w_o[l])

        h2 = rmsnorm(x, norm_ffn[l])
        x = x + expert_choice_ffn(h2, w_router[l], w_expert_gate[l], w_expert_up[l], w_expert_down[l])
    return rmsnorm(x, norm_final)
```

```python
import functools

import jax
import jax.numpy as jnp
from jax import lax
from jax.experimental import pallas as pl
from jax.experimental.pallas import tpu as pltpu

F32 = jnp.float32
BF16 = jnp.bfloat16
I32 = jnp.int32

GRID_W = 64
HEAD_DIM = 128
HALF_ROT = HEAD_DIM // 2
RET_HEADS = 4
RET_DIM = RET_HEADS * HEAD_DIM
ATT_HEADS = 4
ATT_KV_HEADS = 2
ATT_GROUP = ATT_HEADS // ATT_KV_HEADS
ATT_DIM = ATT_HEADS * HEAD_DIM
ATT_KV_DIM = ATT_KV_HEADS * HEAD_DIM
CHUNK = 128
ROPE_THETA = 10000.0
CAPACITY_FACTOR = 2
EPS = 1e-6

V7X_VMEM_LIMIT_BYTES = 56 * 1024 * 1024
TOKEN_TILE = 512
ATT_Q_TILE = 512
ATT_KV_TILE = 512
ROUTE_TILE = 256
SLAB_ALIGN = 16
SLAB_ROWS = ROUTE_TILE + SLAB_ALIGN
SLAB_PAD = 384
FF_TILE = 256
FFN_ROW_BLOCK = 512


def _params(sem, vmem=None):
    return pltpu.CompilerParams(dimension_semantics=sem, vmem_limit_bytes=vmem)


def _inproj_kernel(x_ref, g_ref, w_ref, cos_ref, sin_ref, qn_ref, kn_ref,
                   rq_ref, rk_ref, rv_ref, sg_ref, aq_ref, ak_ref, av_ref, gr_ref, ga_ref):
    x = x_ref[...]
    h = (x * lax.rsqrt(jnp.mean(x * x, axis=-1, keepdims=True) + EPS) * g_ref[...]).astype(BF16)
    cos = cos_ref[...]
    sin = sin_ref[...]
    d_model = x.shape[1]
    scale = HEAD_DIM ** -0.5

    def proj(lo, width):
        return jnp.dot(h, w_ref[:, lo:lo + width], preferred_element_type=F32)

    def rope(t):
        return t * cos + pltpu.roll(t, HALF_ROT, axis=1) * sin

    def head_norm(t, gain):
        return t * lax.rsqrt(jnp.mean(t * t, axis=-1, keepdims=True) + EPS) * gain

    def head(p, i):
        return p[:, i * HEAD_DIM:(i + 1) * HEAD_DIM]

    def put(ref, i, val):
        ref[:, i * HEAD_DIM:(i + 1) * HEAD_DIM] = val.astype(ref.dtype)

    off = 0
    p = proj(off, RET_DIM)
    for i in range(RET_HEADS):
        put(rq_ref, i, rope(head(p, i)))
    off += RET_DIM
    p = proj(off, RET_DIM)
    for i in range(RET_HEADS):
        put(rk_ref, i, rope(head(p, i)) * scale)
    off += RET_DIM
    rv_ref[...] = proj(off, RET_DIM).astype(BF16)
    off += RET_DIM
    p = proj(off, RET_DIM)
    sg_ref[...] = (p * jax.nn.sigmoid(p)).astype(BF16)
    off += RET_DIM
    p = proj(off, ATT_DIM)
    for i in range(ATT_HEADS):
        put(aq_ref, i, rope(head_norm(head(p, i), qn_ref[...])) * scale)
    off += ATT_DIM
    p = proj(off, ATT_KV_DIM)
    for i in range(ATT_KV_HEADS):
        put(ak_ref, i, rope(head_norm(head(p, i), kn_ref[...])))
    off += ATT_KV_DIM
    av_ref[...] = proj(off, ATT_KV_DIM).astype(BF16)
    off += ATT_KV_DIM
    gr_ref[...] = jax.nn.sigmoid(proj(off, d_model)).astype(BF16)
    off += d_model
    ga_ref[...] = jax.nn.sigmoid(proj(off, d_model)).astype(BF16)


def _inproj(xf, norm_g, w_in_bf, cos2, sin2, q_norm, k_norm, seq):
    t, d = xf.shape
    tm = TOKEN_TILE
    in_width = w_in_bf.shape[1]
    steps_per_seq = seq // tm
    row = lambda i: (i, 0)
    const = lambda i: (0, 0)
    pos = lambda i: (i % steps_per_seq, 0)
    widths = (RET_DIM, RET_DIM, RET_DIM, RET_DIM, ATT_DIM, ATT_KV_DIM, ATT_KV_DIM, d, d)
    return pl.pallas_call(
        _inproj_kernel,
        grid=(t // tm,),
        in_specs=[
            pl.BlockSpec((tm, d), row),
            pl.BlockSpec((1, d), const),
            pl.BlockSpec((d, in_width), const),
            pl.BlockSpec((tm, HEAD_DIM), pos),
            pl.BlockSpec((tm, HEAD_DIM), pos),
            pl.BlockSpec((1, HEAD_DIM), const),
            pl.BlockSpec((1, HEAD_DIM), const),
        ],
        out_specs=[pl.BlockSpec((tm, w), row) for w in widths],
        out_shape=[jax.ShapeDtypeStruct((t, w), BF16) for w in widths],
        compiler_params=_params(("parallel",), V7X_VMEM_LIMIT_BYTES),
        name="inproj",
    )(xf, norm_g, w_in_bf, cos2, sin2, q_norm, k_norm)


def _retention_kernel(lg_ref, q_ref, k_ref, v_ref, sg_ref, gain_ref, o_ref, of_ref, ob_ref):
    hd = pl.program_id(1)
    lgf = lg_ref[0, hd]
    lgb = lg_ref[1, hd]
    n_chunks = q_ref.shape[0] // CHUNK
    ii = lax.broadcasted_iota(I32, (CHUNK, CHUNK), 0)
    jj = lax.broadcasted_iota(I32, (CHUNK, CHUNK), 1)
    dist = (ii - jj).astype(F32)
    dec_f = jnp.where(dist >= 0, jnp.exp(lgf * jnp.maximum(dist, 0.0)), 0.0)
    dec_b = jnp.where(dist < 0, jnp.exp(lgb * jnp.maximum(-dist, 0.0)), 0.0)
    t = lax.broadcasted_iota(I32, (CHUNK, 1), 0).astype(F32)
    kd_f = jnp.exp(lgf * (CHUNK - 1.0 - t))
    qd_f = jnp.exp(lgf * (t + 1.0))
    kd_b = jnp.exp(lgb * t)
    qd_b = jnp.exp(lgb * (CHUNK - t))
    zero_row = jnp.zeros((1, HEAD_DIM), F32)
    cd_f = jnp.exp(zero_row + lgf * CHUNK)
    cd_b = jnp.exp(zero_row + lgb * CHUNK)

    def one_chunk(n, dec, kd, qd, cd, state):
        r0 = pl.multiple_of(n * CHUNK, CHUNK)
        q = q_ref[pl.ds(r0, CHUNK), :]
        k = k_ref[pl.ds(r0, CHUNK), :]
        v = v_ref[pl.ds(r0, CHUNK), :]
        s = lax.dot_general(q, k, (((1,), (1,)), ((), ())), preferred_element_type=F32) * dec
        intra = jnp.dot(s.astype(BF16), v, preferred_element_type=F32)
        q_dec = (q.astype(F32) * qd).astype(BF16)
        inter = jnp.dot(q_dec, state.astype(BF16), preferred_element_type=F32)
        k_dec_t = (k.astype(F32) * kd).T.astype(BF16)
        new_state = state * cd + jnp.dot(k_dec_t, v, preferred_element_type=F32)
        return r0, intra + inter, new_state

    def scan_body(n, carry):
        sf, sb = carry
        r0, of, sf = one_chunk(n, dec_f, kd_f, qd_f, cd_f, sf)
        of_ref[pl.ds(r0, CHUNK), :] = of
        r0, ob, sb = one_chunk(n_chunks - 1 - n, dec_b, kd_b, qd_b, cd_b, sb)
        ob_ref[pl.ds(r0, CHUNK), :] = ob
        return sf, sb

    zeros = jnp.zeros((HEAD_DIM, HEAD_DIM), F32)
    lax.fori_loop(0, n_chunks, scan_body, (zeros, zeros))

    rows = TOKEN_TILE

    def norm_body(n, carry):
        r0 = pl.multiple_of(n * rows, rows)
        o = of_ref[pl.ds(r0, rows), :] + ob_ref[pl.ds(r0, rows), :]
        mu = jnp.mean(o, axis=-1, keepdims=True)
        var = jnp.mean(jnp.square(o - mu), axis=-1, keepdims=True)
        yn = (o - mu) * lax.rsqrt(var + EPS) * gain_ref[...]
        o_ref[pl.ds(r0, rows), :] = (sg_ref[pl.ds(r0, rows), :].astype(F32) * yn).astype(BF16)
        return carry

    lax.fori_loop(0, q_ref.shape[0] // rows, norm_body, 0)


def _retention(lg, rq, rk, rv, sg, gn_gain, batch, seq):
    t = rq.shape[0]
    blk = pl.BlockSpec((seq, HEAD_DIM), lambda b, h: (b, h))
    return pl.pallas_call(
        _retention_kernel,
        grid=(batch, RET_HEADS),
        in_specs=[
            pl.BlockSpec(memory_space=pltpu.SMEM),
            blk, blk, blk, blk,
            pl.BlockSpec((1, HEAD_DIM), lambda b, h: (0, h)),
        ],
        out_specs=blk,
        out_shape=jax.ShapeDtypeStruct((t, RET_DIM), BF16),
        scratch_shapes=[pltpu.VMEM((seq, HEAD_DIM), F32), pltpu.VMEM((seq, HEAD_DIM), F32)],
        compiler_params=_params(("parallel", "parallel"), V7X_VMEM_LIMIT_BYTES),
        name="retention",
    )(lg, rq, rk, rv, sg, gn_gain)


def _attention_kernel(q_ref, k_ref, v_ref, o_ref):
    tq = q_ref.shape[0]
    tk = ATT_KV_TILE
    n_kv = k_ref.shape[0] // tk
    for hh in range(ATT_GROUP):
        q = q_ref[:, hh * HEAD_DIM:(hh + 1) * HEAD_DIM]

        def body(c, carry, q=q):
            m, l, acc = carry
            r0 = pl.multiple_of(c * tk, tk)
            k = k_ref[pl.ds(r0, tk), :]
            v = v_ref[pl.ds(r0, tk), :]
            s = lax.dot_general(q, k, (((1,), (1,)), ((), ())), preferred_element_type=F32)
            m_new = jnp.maximum(m, jnp.max(s, axis=-1, keepdims=True))
            alpha = jnp.exp(m - m_new)
            p = jnp.exp(s - m_new)
            l = alpha * l + jnp.sum(p, axis=-1, keepdims=True)
            acc = alpha * acc + jnp.dot(p.astype(BF16), v, preferred_element_type=F32)
            return m_new, l, acc

        init = (jnp.full((tq, 1), -jnp.inf, F32), jnp.zeros((tq, 1), F32),
                jnp.zeros((tq, HEAD_DIM), F32))
        _, l, acc = lax.fori_loop(0, n_kv, body, init)
        o_ref[:, hh * HEAD_DIM:(hh + 1) * HEAD_DIM] = (acc / l).astype(BF16)


def _attention(aq, ak, av, batch, seq):
    t = aq.shape[0]
    tq = ATT_Q_TILE
    nq = seq // tq
    gw = ATT_GROUP * HEAD_DIM
    q_spec = pl.BlockSpec((tq, gw), lambda b, g, i: (b * nq + i, g))
    kv_spec = pl.BlockSpec((seq, HEAD_DIM), lambda b, g, i: (b, g))
    return pl.pallas_call(
        _attention_kernel,
        grid=(batch, ATT_KV_HEADS, nq),
        in_specs=[q_spec, kv_spec, kv_spec],
        out_specs=q_spec,
        out_shape=jax.ShapeDtypeStruct((t, ATT_DIM), BF16),
        compiler_params=_params(("parallel", "parallel", "parallel"), V7X_VMEM_LIMIT_BYTES),
        name="attention",
    )(aq, ak, av)


def _merge_kernel(x_ref, ur_ref, ya_ref, gr_ref, ga_ref, wr_ref, wa_ref, wo_ref, nf_ref, wrt_ref,
                  x1_ref, h2_ref, aff_ref):
    y_ret = jnp.dot(ur_ref[...], wr_ref[...], preferred_element_type=F32)
    y_att = jnp.dot(ya_ref[...], wa_ref[...], preferred_element_type=F32)
    mixed = gr_ref[...].astype(F32) * y_ret + ga_ref[...].astype(F32) * y_att
    x1 = x_ref[...] + jnp.dot(mixed.astype(BF16), wo_ref[...], preferred_element_type=F32)
    x1_ref[...] = x1
    h2 = x1 * lax.rsqrt(jnp.mean(x1 * x1, axis=-1, keepdims=True) + EPS) * nf_ref[...]
    h2_ref[...] = h2
    logits = jnp.dot(h2, wrt_ref[...], preferred_element_type=F32, precision=lax.Precision.HIGHEST)
    e = jnp.exp(logits - jnp.max(logits, axis=-1, keepdims=True))
    aff_ref[...] = e / jnp.sum(e, axis=-1, keepdims=True)


def _merge(xf, u_ret, y_att, gr, ga, w_ret_bf, w_att_bf, w_o_bf, norm_ffn, w_router):
    t, d = xf.shape
    tm = TOKEN_TILE
    n_exp = w_router.shape[1]
    row = lambda i: (i, 0)
    const = lambda i: (0, 0)
    return pl.pallas_call(
        _merge_kernel,
        grid=(t // tm,),
        in_specs=[
            pl.BlockSpec((tm, d), row),
            pl.BlockSpec((tm, RET_DIM), row),
            pl.BlockSpec((tm, ATT_DIM), row),
            pl.BlockSpec((tm, d), row),
            pl.BlockSpec((tm, d), row),
            pl.BlockSpec((RET_DIM, d), const),
            pl.BlockSpec((ATT_DIM, d), const),
            pl.BlockSpec((d, d), const),
            pl.BlockSpec((1, d), const),
            pl.BlockSpec((d, n_exp), const),
        ],
        out_specs=[pl.BlockSpec((tm, d), row), pl.BlockSpec((tm, d), row),
                   pl.BlockSpec((tm, n_exp), row)],
        out_shape=[jax.ShapeDtypeStruct((t, d), F32), jax.ShapeDtypeStruct((t, d), F32),
                   jax.ShapeDtypeStruct((t, n_exp), F32)],
        compiler_params=_params(("parallel",), V7X_VMEM_LIMIT_BYTES),
        name="merge",
    )(xf, u_ret, y_att, gr, ga, w_ret_bf, w_att_bf, w_o_bf, norm_ffn, w_router)


def _route_kernel(aff_ref, idx_ref, pos_ref, slo_ref, msk_ref, cum_ref, *, cap):
    b = pl.program_id(0)
    seq, n_exp = aff_ref.shape
    n_tiles = seq // ROUTE_TILE
    bits = pltpu.bitcast(aff_ref[...], I32)

    def search(i, cur):
        cand = cur | jnp.left_shift(jnp.int32(1), 30 - i)
        cnt = jnp.sum((bits >= cand).astype(I32), axis=0, keepdims=True)
        return jnp.where(cnt >= cap, cand, cur)

    thr = lax.fori_loop(0, 31, search, jnp.zeros((1, n_exp), I32))
    gt = bits > thr
    eq = bits == thr
    need = cap - jnp.sum(gt.astype(I32), axis=0, keepdims=True)

    tri = (lax.broadcasted_iota(I32, (ROUTE_TILE, ROUTE_TILE), 0)
           >= lax.broadcasted_iota(I32, (ROUTE_TILE, ROUTE_TILE), 1))
    tri = jnp.where(tri, 1.0, 0.0).astype(BF16)

    def cumsum_tokens(store_tile_start):
        def body(c, carry):
            r0 = pl.multiple_of(c * ROUTE_TILE, ROUTE_TILE)
            if store_tile_start:
                slo_ref[0, pl.ds(c, 1), :] = carry.astype(I32)
            cs = jnp.dot(tri, msk_ref[pl.ds(r0, ROUTE_TILE), :], preferred_element_type=F32) + carry
            cum_ref[pl.ds(r0, ROUTE_TILE), :] = cs
            return cs[ROUTE_TILE - 1:ROUTE_TILE, :]

        lax.fori_loop(0, n_tiles, body, jnp.zeros((1, n_exp), F32))

    msk_ref[...] = jnp.where(eq, 1.0, 0.0).astype(BF16)
    cumsum_tokens(False)
    take = jnp.logical_and(eq, cum_ref[...] - 1.0 < need.astype(F32))
    mask = jnp.logical_or(gt, take)
    msk_ref[...] = jnp.where(mask, 1.0, 0.0).astype(BF16)
    cumsum_tokens(True)
    pos_ref[...] = jnp.where(mask, cum_ref[...] - 1.0, -1.0).astype(I32)

    slot_row = lax.broadcasted_iota(I32, (1, cap), 1).astype(F32)
    for e in range(n_exp):
        def count(c, acc, e=e):
            r0 = pl.multiple_of(c * ROUTE_TILE, ROUTE_TILE)
            col = cum_ref[pl.ds(r0, ROUTE_TILE), e:e + 1]
            hit = jnp.where(col <= slot_row, 1.0, 0.0)
            return acc + jnp.sum(hit.reshape(ROUTE_TILE // 8, 8, cap), axis=0)

        acc = lax.fori_loop(0, n_tiles, count, jnp.zeros((8, cap), F32))
        idx_ref[0, e:e + 1, :] = jnp.sum(acc, axis=0, keepdims=True).astype(I32) + b * seq


def _route(aff, batch, seq, cap):
    t, n_exp = aff.shape
    n_tiles = seq // ROUTE_TILE
    return pl.pallas_call(
        functools.partial(_route_kernel, cap=cap),
        grid=(batch,),
        in_specs=[pl.BlockSpec((seq, n_exp), lambda b: (b, 0))],
        out_specs=[pl.BlockSpec((1, n_exp, cap), lambda b: (b, 0, 0)),
                   pl.BlockSpec((seq, n_exp), lambda b: (b, 0)),
                   pl.BlockSpec((1, n_tiles, n_exp), lambda b: (b, 0, 0))],
        out_shape=[jax.ShapeDtypeStruct((batch, n_exp, cap), I32),
                   jax.ShapeDtypeStruct((t, n_exp), I32),
                   jax.ShapeDtypeStruct((batch, n_tiles, n_exp), I32)],
        scratch_shapes=[pltpu.VMEM((seq, n_exp), BF16), pltpu.VMEM((seq, n_exp), F32)],
        compiler_params=_params(("parallel",), V7X_VMEM_LIMIT_BYTES),
        name="route",
    )(aff)


def _ffn_kernel(idx_ref, h2_hbm, wg_ref, wu_ref, wd_ref, y_ref, stage_ref, xe_ref, acc_ref, sem,
                *, n_exp, n_ff):
    e = pl.program_id(0)
    f = pl.program_id(1)
    m = stage_ref.shape[0]
    batch = y_ref.shape[0]
    cap = y_ref.shape[2]

    def gather_rows(expert):
        def body(i, carry):
            r = idx_ref[expert * m + i]
            pltpu.make_async_copy(h2_hbm.at[pl.ds(r, 1)], stage_ref.at[pl.ds(i, 1)], sem).start()
            return carry

        lax.fori_loop(0, m, body, 0)

    @pl.when(f == 0)
    def _():
        @pl.when(e == 0)
        def _():
            gather_rows(0)

        pltpu.make_async_copy(h2_hbm.at[pl.ds(0, m)], stage_ref, sem).wait()
        xe_ref[...] = stage_ref[...].astype(BF16)

        @pl.when(e + 1 < n_exp)
        def _():
            gather_rows(e + 1)

    wg = wg_ref[0].astype(BF16)
    wu = wu_ref[0].astype(BF16)
    wd = wd_ref[0].astype(BF16)
    for rb in range(m // FFN_ROW_BLOCK):
        rows = pl.ds(rb * FFN_ROW_BLOCK, FFN_ROW_BLOCK)
        xb = xe_ref[rows, :]
        a = jnp.dot(xb, wg, preferred_element_type=F32)
        u = jnp.dot(xb, wu, preferred_element_type=F32)
        hm = (a * jax.nn.sigmoid(a) * u).astype(BF16)
        contrib = jnp.dot(hm, wd, preferred_element_type=F32)

        @pl.when(f == 0)
        def _():
            acc_ref[rows, :] = contrib

        @pl.when(f > 0)
        def _():
            acc_ref[rows, :] += contrib

    @pl.when(f == n_ff - 1)
    def _():
        for bb in range(batch):
            y_ref[bb, 0] = acc_ref[pl.ds(bb * cap, cap), :].astype(BF16)


def _ffn(idx_flat, h2, w_gate, w_up, w_down, batch, cap):
    t, d = h2.shape
    n_exp, _, ff = w_gate.shape
    n_ff = ff // FF_TILE
    m = batch * cap
    grid_spec = pltpu.PrefetchScalarGridSpec(
        num_scalar_prefetch=1,
        grid=(n_exp, n_ff),
        in_specs=[
            pl.BlockSpec(memory_space=pl.ANY),
            pl.BlockSpec((1, d, FF_TILE), lambda e, f, idx: (e, 0, f)),
            pl.BlockSpec((1, d, FF_TILE), lambda e, f, idx: (e, 0, f)),
            pl.BlockSpec((1, FF_TILE, d), lambda e, f, idx: (e, f, 0)),
        ],
        out_specs=pl.BlockSpec((batch, 1, cap, d), lambda e, f, idx: (0, e, 0, 0)),
        scratch_shapes=[pltpu.VMEM((m, d), F32), pltpu.VMEM((m, d), BF16), pltpu.VMEM((m, d), F32),
                        pltpu.SemaphoreType.DMA(())],
    )
    return pl.pallas_call(
        functools.partial(_ffn_kernel, n_exp=n_exp, n_ff=n_ff),
        grid_spec=grid_spec,
        out_shape=jax.ShapeDtypeStruct((batch, n_exp, cap, d), BF16),
        compiler_params=_params(("arbitrary", "arbitrary"), V7X_VMEM_LIMIT_BYTES),
        name="ffn",
    )(idx_flat, h2, w_gate, w_up, w_down)


def _combine_kernel(slo_ref, x1_ref, pos_ref, aff_ref, y_hbm, nf_ref, o_ref, slab_ref, acc_ref, sem,
                    *, n_exp, cap):
    b = pl.program_id(0)
    j = pl.program_id(1)
    n_tiles = pl.num_programs(1)

    def window_start(e):
        s_lo = slo_ref[(b * n_tiles + j) * n_exp + e]
        st = jnp.minimum((s_lo // SLAB_ALIGN) * SLAB_ALIGN, cap - SLAB_ROWS)
        return pl.multiple_of(st, SLAB_ALIGN)

    def slab_copy(e, slot):
        return pltpu.make_async_copy(y_hbm.at[b, e, pl.ds(window_start(e), SLAB_ROWS)],
                                     slab_ref.at[slot, pl.ds(0, SLAB_ROWS)], sem.at[slot])

    @pl.when((b == 0) & (j == 0))
    def _():
        slab_ref[...] = jnp.zeros(slab_ref.shape, slab_ref.dtype)

    slab_copy(0, 0).start()
    lane = lax.broadcasted_iota(I32, (1, SLAB_PAD), 1)
    for e in range(n_exp):
        slot = e % 2
        if e + 1 < n_exp:
            slab_copy(e + 1, 1 - slot).start()
        slab_copy(e, slot).wait()
        rel = pos_ref[:, e:e + 1] - window_start(e)
        onehot = jnp.where(rel == lane, 1.0, 0.0).astype(BF16)
        part = aff_ref[:, e:e + 1] * jnp.dot(onehot, slab_ref[slot], preferred_element_type=F32)
        if e == 0:
            acc_ref[...] = part
        else:
            acc_ref[...] += part
    x2 = x1_ref[...] + acc_ref[...]
    o_ref[...] = x2 * lax.rsqrt(jnp.mean(x2 * x2, axis=-1, keepdims=True) + EPS) * nf_ref[...]


def _combine(slo_flat, x1, pos, aff, y_slots, norm_final, batch, seq, cap):
    t, d = x1.shape
    n_exp = aff.shape[1]
    n_tiles = seq // ROUTE_TILE
    row = lambda b, j, slo: (b * n_tiles + j, 0)
    grid_spec = pltpu.PrefetchScalarGridSpec(
        num_scalar_prefetch=1,
        grid=(batch, n_tiles),
        in_specs=[
            pl.BlockSpec((ROUTE_TILE, d), row),
            pl.BlockSpec((ROUTE_TILE, n_exp), row),
            pl.BlockSpec((ROUTE_TILE, n_exp), row),
            pl.BlockSpec(memory_space=pl.ANY),
            pl.BlockSpec((1, d), lambda b, j, slo: (0, 0)),
        ],
        out_specs=pl.BlockSpec((ROUTE_TILE, d), row),
        scratch_shapes=[pltpu.VMEM((2, SLAB_PAD, d), BF16), pltpu.VMEM((ROUTE_TILE, d), F32),
                        pltpu.SemaphoreType.DMA((2,))],
    )
    return pl.pallas_call(
        functools.partial(_combine_kernel, n_exp=n_exp, cap=cap),
        grid_spec=grid_spec,
        out_shape=jax.ShapeDtypeStruct((t, d), F32),
        compiler_params=_params(("arbitrary", "arbitrary"), V7X_VMEM_LIMIT_BYTES),
        name="combine",
    )(slo_flat, x1, pos, aff, y_slots, norm_final)


def _rope_tables(seq):
    rows = seq // GRID_W
    row = jnp.repeat(jnp.arange(rows, dtype=F32), GRID_W)
    col = jnp.tile(jnp.arange(GRID_W, dtype=F32), rows)
    n_freq = HALF_ROT // 2
    freqs = ROPE_THETA ** (-jnp.arange(n_freq, dtype=F32) / n_freq)
    ang = jnp.concatenate([row[:, None] * freqs, col[:, None] * freqs], axis=-1)
    cos, sin = jnp.cos(ang), jnp.sin(ang)
    return jnp.concatenate([cos, cos], axis=-1), jnp.concatenate([-sin, sin], axis=-1)


def kernel(x, norm_mix, w_in, ret_decay_fwd, ret_decay_bwd, ret_gn_gain, w_ret_branch, q_norm, k_norm,
           w_att_branch, w_o, norm_ffn, w_router, w_expert_gate, w_expert_up, w_expert_down, norm_final):
    batch, seq, d = x.shape
    depth = norm_mix.shape[0]
    n_exp = w_router.shape[2]
    cap = CAPACITY_FACTOR * seq // n_exp
    assert seq % TOKEN_TILE == 0 and seq % ROUTE_TILE == 0 and seq % ATT_KV_TILE == 0
    assert cap >= SLAB_ROWS and (cap - SLAB_ROWS) % SLAB_ALIGN == 0
    assert w_expert_gate.shape[3] % FF_TILE == 0 and (batch * cap) % FFN_ROW_BLOCK == 0
    assert depth == 1, "the final RMSNorm is fused into the combine kernel of the single layer"
    l = 0
    cos2, sin2 = _rope_tables(seq)
    xf = x.reshape(batch * seq, d)
    lg = jnp.stack([jnp.log1p(-jnp.exp(ret_decay_fwd[l].astype(F32))),
                    jnp.log1p(-jnp.exp(ret_decay_bwd[l].astype(F32)))])
    rq, rk, rv, sg, aq, ak, av, gr, ga = _inproj(
        xf, norm_mix[l][None], w_in[l].astype(BF16), cos2, sin2, q_norm[l][None], k_norm[l][None], seq)
    u_ret = _retention(lg, rq, rk, rv, sg, ret_gn_gain[l][None], batch, seq)
    y_att = _attention(aq, ak, av, batch, seq)
    x1, h2, aff = _merge(xf, u_ret, y_att, gr, ga, w_ret_branch[l].astype(BF16),
                         w_att_branch[l].astype(BF16), w_o[l].astype(BF16), norm_ffn[l][None],
                         w_router[l])
    idx, pos, slo = _route(aff, batch, seq, cap)
    idx_flat = idx.transpose(1, 0, 2).reshape(-1)
    y_slots = _ffn(idx_flat, h2, w_expert_gate[l], w_expert_up[l], w_expert_down[l], batch, cap)
    out = _combine(slo.reshape(-1), x1, pos, aff, y_slots, norm_final[None], batch, seq, cap)
    return out.reshape(batch, seq, d)
```

```python
import functools

import jax
import jax.numpy as jnp
from jax import lax
from jax.experimental import pallas as pl
from jax.experimental.pallas import tpu as pltpu

F32 = jnp.float32
BF16 = jnp.bfloat16
I32 = jnp.int32

GRID_W = 64
HEAD_DIM = 128
HALF_ROT = HEAD_DIM // 2
RET_HEADS = 4
RET_DIM = RET_HEADS * HEAD_DIM
ATT_HEADS = 4
ATT_KV_HEADS = 2
ATT_GROUP = ATT_HEADS // ATT_KV_HEADS
ATT_DIM = ATT_HEADS * HEAD_DIM
ATT_KV_DIM = ATT_KV_HEADS * HEAD_DIM
CHUNK = 128
ROPE_THETA = 10000.0
CAPACITY_FACTOR = 2
EPS = 1e-6

V7X_VMEM_LIMIT_BYTES = 56 * 1024 * 1024
TOKEN_TILE = 512
ATT_Q_TILE = 512
ATT_KV_TILE = 512
LANES = 128
ROUTE_TILE = 512
COUNT_CHUNK = 256
SLAB_ALIGN = 16
SMALL_ROWS = LANES
BIG_ROWS = ROUTE_TILE + SLAB_ALIGN
BIG_PAD = 640
FF_TILE = 256
FFN_ROW_BLOCK = 512
BISECT_STEPS = 24


def _params(sem, vmem=None):
    return pltpu.CompilerParams(dimension_semantics=sem, vmem_limit_bytes=vmem)


def _inproj_kernel(x_ref, g_ref, w_ref, cos_ref, sin_ref, qn_ref, kn_ref,
                   rq_ref, rk_ref, rv_ref, sg_ref, aq_ref, ak_ref, av_ref, gr_ref, ga_ref):
    x = x_ref[...]
    h = (x * lax.rsqrt(jnp.mean(x * x, axis=-1, keepdims=True) + EPS) * g_ref[...]).astype(BF16)
    cos = cos_ref[...]
    sin = sin_ref[...]
    d_model = x.shape[1]
    scale = HEAD_DIM ** -0.5

    def proj(lo, width):
        return jnp.dot(h, w_ref[:, lo:lo + width], preferred_element_type=F32)

    def rope(t):
        return t * cos + pltpu.roll(t, HALF_ROT, axis=1) * sin

    def head_norm(t, gain):
        return t * lax.rsqrt(jnp.mean(t * t, axis=-1, keepdims=True) + EPS) * gain

    def head(p, i):
        return p[:, i * HEAD_DIM:(i + 1) * HEAD_DIM]

    def put(ref, i, val):
        ref[:, i * HEAD_DIM:(i + 1) * HEAD_DIM] = val.astype(ref.dtype)

    off = 0
    p = proj(off, RET_DIM)
    for i in range(RET_HEADS):
        put(rq_ref, i, rope(head(p, i)))
    off += RET_DIM
    p = proj(off, RET_DIM)
    for i in range(RET_HEADS):
        put(rk_ref, i, rope(head(p, i)) * scale)
    off += RET_DIM
    rv_ref[...] = proj(off, RET_DIM).astype(BF16)
    off += RET_DIM
    p = proj(off, RET_DIM)
    sg_ref[...] = (p * jax.nn.sigmoid(p)).astype(BF16)
    off += RET_DIM
    p = proj(off, ATT_DIM)
    for i in range(ATT_HEADS):
        put(aq_ref, i, rope(head_norm(head(p, i), qn_ref[...])) * scale)
    off += ATT_DIM
    p = proj(off, ATT_KV_DIM)
    for i in range(ATT_KV_HEADS):
        put(ak_ref, i, rope(head_norm(head(p, i), kn_ref[...])))
    off += ATT_KV_DIM
    av_ref[...] = proj(off, ATT_KV_DIM).astype(BF16)
    off += ATT_KV_DIM
    gr_ref[...] = jax.nn.sigmoid(proj(off, d_model)).astype(BF16)
    off += d_model
    ga_ref[...] = jax.nn.sigmoid(proj(off, d_model)).astype(BF16)


def _inproj(xf, norm_g, w_in_bf, cos2, sin2, q_norm, k_norm, seq):
    t, d = xf.shape
    tm = TOKEN_TILE
    in_width = w_in_bf.shape[1]
    steps_per_seq = seq // tm
    row = lambda i: (i, 0)
    const = lambda i: (0, 0)
    pos = lambda i: (i % steps_per_seq, 0)
    widths = (RET_DIM, RET_DIM, RET_DIM, RET_DIM, ATT_DIM, ATT_KV_DIM, ATT_KV_DIM, d, d)
    return pl.pallas_call(
        _inproj_kernel,
        grid=(t // tm,),
        in_specs=[
            pl.BlockSpec((tm, d), row),
            pl.BlockSpec((1, d), const),
            pl.BlockSpec((d, in_width), const),
            pl.BlockSpec((tm, HEAD_DIM), pos),
            pl.BlockSpec((tm, HEAD_DIM), pos),
            pl.BlockSpec((1, HEAD_DIM), const),
            pl.BlockSpec((1, HEAD_DIM), const),
        ],
        out_specs=[pl.BlockSpec((tm, w), row) for w in widths],
        out_shape=[jax.ShapeDtypeStruct((t, w), BF16) for w in widths],
        compiler_params=_params(("parallel",), V7X_VMEM_LIMIT_BYTES),
        name="inproj",
    )(xf, norm_g, w_in_bf, cos2, sin2, q_norm, k_norm)


def _retention_kernel(lg_ref, q_ref, k_ref, v_ref, sg_ref, gain_ref, o_ref, of_ref, ob_ref):
    hd = pl.program_id(1)
    lgf = lg_ref[0, hd]
    lgb = lg_ref[1, hd]
    n_chunks = q_ref.shape[0] // CHUNK
    ii = lax.broadcasted_iota(I32, (CHUNK, CHUNK), 0)
    jj = lax.broadcasted_iota(I32, (CHUNK, CHUNK), 1)
    dist = (ii - jj).astype(F32)
    dec_f = jnp.where(dist >= 0, jnp.exp(lgf * jnp.maximum(dist, 0.0)), 0.0)
    dec_b = jnp.where(dist < 0, jnp.exp(lgb * jnp.maximum(-dist, 0.0)), 0.0)
    t = lax.broadcasted_iota(I32, (CHUNK, 1), 0).astype(F32)
    kd_f = jnp.exp(lgf * (CHUNK - 1.0 - t))
    qd_f = jnp.exp(lgf * (t + 1.0))
    kd_b = jnp.exp(lgb * t)
    qd_b = jnp.exp(lgb * (CHUNK - t))
    zero_row = jnp.zeros((1, HEAD_DIM), F32)
    cd_f = jnp.exp(zero_row + lgf * CHUNK)
    cd_b = jnp.exp(zero_row + lgb * CHUNK)

    def one_chunk(n, dec, kd, qd, cd, state):
        r0 = pl.multiple_of(n * CHUNK, CHUNK)
        q = q_ref[pl.ds(r0, CHUNK), :]
        k = k_ref[pl.ds(r0, CHUNK), :]
        v = v_ref[pl.ds(r0, CHUNK), :]
        s = lax.dot_general(q, k, (((1,), (1,)), ((), ())), preferred_element_type=F32) * dec
        intra = jnp.dot(s.astype(BF16), v, preferred_element_type=F32)
        q_dec = (q.astype(F32) * qd).astype(BF16)
        inter = jnp.dot(q_dec, state.astype(BF16), preferred_element_type=F32)
        k_dec_t = (k.astype(F32) * kd).T.astype(BF16)
        new_state = state * cd + jnp.dot(k_dec_t, v, preferred_element_type=F32)
        return r0, intra + inter, new_state

    def scan_body(n, carry):
        sf, sb = carry
        r0, of, sf = one_chunk(n, dec_f, kd_f, qd_f, cd_f, sf)
        of_ref[pl.ds(r0, CHUNK), :] = of
        r0, ob, sb = one_chunk(n_chunks - 1 - n, dec_b, kd_b, qd_b, cd_b, sb)
        ob_ref[pl.ds(r0, CHUNK), :] = ob
        return sf, sb

    zeros = jnp.zeros((HEAD_DIM, HEAD_DIM), F32)
    lax.fori_loop(0, n_chunks, scan_body, (zeros, zeros))

    rows = TOKEN_TILE

    def norm_body(n, carry):
        r0 = pl.multiple_of(n * rows, rows)
        o = of_ref[pl.ds(r0, rows), :] + ob_ref[pl.ds(r0, rows), :]
        mu = jnp.mean(o, axis=-1, keepdims=True)
        var = jnp.mean(jnp.square(o - mu), axis=-1, keepdims=True)
        yn = (o - mu) * lax.rsqrt(var + EPS) * gain_ref[...]
        o_ref[pl.ds(r0, rows), :] = (sg_ref[pl.ds(r0, rows), :].astype(F32) * yn).astype(BF16)
        return carry

    lax.fori_loop(0, q_ref.shape[0] // rows, norm_body, 0)


def _retention(lg, rq, rk, rv, sg, gn_gain, batch, seq):
    t = rq.shape[0]
    blk = pl.BlockSpec((seq, HEAD_DIM), lambda b, h: (b, h))
    return pl.pallas_call(
        _retention_kernel,
        grid=(batch, RET_HEADS),
        in_specs=[
            pl.BlockSpec(memory_space=pltpu.SMEM),
            blk, blk, blk, blk,
            pl.BlockSpec((1, HEAD_DIM), lambda b, h: (0, h)),
        ],
        out_specs=blk,
        out_shape=jax.ShapeDtypeStruct((t, RET_DIM), BF16),
        scratch_shapes=[pltpu.VMEM((seq, HEAD_DIM), F32), pltpu.VMEM((seq, HEAD_DIM), F32)],
        compiler_params=_params(("parallel", "parallel"), V7X_VMEM_LIMIT_BYTES),
        name="retention",
    )(lg, rq, rk, rv, sg, gn_gain)


def _attention_kernel(q_ref, k_ref, v_ref, o_ref):
    tq = q_ref.shape[0]
    tk = ATT_KV_TILE
    n_kv = k_ref.shape[0] // tk
    for hh in range(ATT_GROUP):
        q = q_ref[:, hh * HEAD_DIM:(hh + 1) * HEAD_DIM]

        def body(c, carry, q=q):
            m, l, acc = carry
            r0 = pl.multiple_of(c * tk, tk)
            k = k_ref[pl.ds(r0, tk), :]
            v = v_ref[pl.ds(r0, tk), :]
            s = lax.dot_general(q, k, (((1,), (1,)), ((), ())), preferred_element_type=F32)
            m_new = jnp.maximum(m, jnp.max(s, axis=-1, keepdims=True))
            alpha = jnp.exp(m - m_new)
            p = jnp.exp(s - m_new)
            l = alpha * l + jnp.sum(p, axis=-1, keepdims=True)
            acc = alpha * acc + jnp.dot(p.astype(BF16), v, preferred_element_type=F32)
            return m_new, l, acc

        init = (jnp.full((tq, 1), -jnp.inf, F32), jnp.zeros((tq, 1), F32),
                jnp.zeros((tq, HEAD_DIM), F32))
        _, l, acc = lax.fori_loop(0, n_kv, body, init)
        o_ref[:, hh * HEAD_DIM:(hh + 1) * HEAD_DIM] = (acc / l).astype(BF16)


def _attention(aq, ak, av, batch, seq):
    t = aq.shape[0]
    tq = ATT_Q_TILE
    nq = seq // tq
    gw = ATT_GROUP * HEAD_DIM
    q_spec = pl.BlockSpec((tq, gw), lambda b, g, i: (b * nq + i, g))
    kv_spec = pl.BlockSpec((seq, HEAD_DIM), lambda b, g, i: (b, g))
    return pl.pallas_call(
        _attention_kernel,
        grid=(batch, ATT_KV_HEADS, nq),
        in_specs=[q_spec, kv_spec, kv_spec],
        out_specs=q_spec,
        out_shape=jax.ShapeDtypeStruct((t, ATT_DIM), BF16),
        compiler_params=_params(("parallel", "parallel", "parallel"), V7X_VMEM_LIMIT_BYTES),
        name="attention",
    )(aq, ak, av)


def _merge_kernel(x_ref, ur_ref, ya_ref, gr_ref, ga_ref, wr_ref, wa_ref, wo_ref, nf_ref, wrt_ref,
                  x1_ref, h2a_ref, *, n_exp):
    d = x_ref.shape[1]
    y_ret = jnp.dot(ur_ref[...], wr_ref[...], preferred_element_type=F32)
    y_att = jnp.dot(ya_ref[...], wa_ref[...], preferred_element_type=F32)
    mixed = gr_ref[...].astype(F32) * y_ret + ga_ref[...].astype(F32) * y_att
    x1 = x_ref[...] + jnp.dot(mixed.astype(BF16), wo_ref[...], preferred_element_type=F32)
    x1_ref[...] = x1
    h2 = x1 * lax.rsqrt(jnp.mean(x1 * x1, axis=-1, keepdims=True) + EPS) * nf_ref[...]
    h2a_ref[:, :d] = h2
    h_hi = h2.astype(BF16)
    h_lo = (h2 - h_hi.astype(F32)).astype(BF16)
    r_hi = jnp.dot(h_hi, wrt_ref[...], preferred_element_type=F32)
    r_lo = jnp.dot(h_lo, wrt_ref[:, :LANES], preferred_element_type=F32)
    logits = r_hi[:, :LANES] + r_hi[:, LANES:] + r_lo
    lane = lax.broadcasted_iota(I32, logits.shape, 1)
    logits = jnp.where(lane < n_exp, logits, -jnp.inf)
    e = jnp.exp(logits - jnp.max(logits, axis=-1, keepdims=True))
    h2a_ref[:, d:] = e / jnp.sum(e, axis=-1, keepdims=True)


def _merge(xf, u_ret, y_att, gr, ga, w_ret_bf, w_att_bf, w_o_bf, norm_ffn, w_router_split, n_exp):
    t, d = xf.shape
    tm = TOKEN_TILE
    row = lambda i: (i, 0)
    const = lambda i: (0, 0)
    return pl.pallas_call(
        functools.partial(_merge_kernel, n_exp=n_exp),
        grid=(t // tm,),
        in_specs=[
            pl.BlockSpec((tm, d), row),
            pl.BlockSpec((tm, RET_DIM), row),
            pl.BlockSpec((tm, ATT_DIM), row),
            pl.BlockSpec((tm, d), row),
            pl.BlockSpec((tm, d), row),
            pl.BlockSpec((RET_DIM, d), const),
            pl.BlockSpec((ATT_DIM, d), const),
            pl.BlockSpec((d, d), const),
            pl.BlockSpec((1, d), const),
            pl.BlockSpec((d, 2 * LANES), const),
        ],
        out_specs=[pl.BlockSpec((tm, d), row), pl.BlockSpec((tm, d + LANES), row)],
        out_shape=[jax.ShapeDtypeStruct((t, d), F32), jax.ShapeDtypeStruct((t, d + LANES), F32)],
        compiler_params=_params(("parallel",), V7X_VMEM_LIMIT_BYTES),
        name="merge",
    )(xf, u_ret, y_att, gr, ga, w_ret_bf, w_att_bf, w_o_bf, norm_ffn, w_router_split)


def _route_kernel(aff_ref, idx_ref, pos_ref, slo_ref, msk_ref, cum_ref, *, cap, n_exp):
    b = pl.program_id(0)
    seq = aff_ref.shape[0]
    n_tiles = seq // ROUTE_TILE
    aff = aff_ref[...]
    capf = float(cap)

    def bisect(lo, hi):
        mid = 0.5 * (lo + hi)
        ok = jnp.sum(jnp.where(aff >= mid, 1.0, 0.0), axis=0, keepdims=True) >= capf
        return jnp.where(ok, mid, lo), jnp.where(ok, hi, mid)

    def bracket(lo, hi):
        mn = jnp.min(jnp.where(aff >= lo, aff, jnp.inf), axis=0, keepdims=True)
        mx = jnp.max(jnp.where(aff < hi, aff, -jnp.inf), axis=0, keepdims=True)
        return mn, mx

    def not_isolated(lo, hi):
        mn, mx = bracket(lo, hi)
        return jnp.max(jnp.where(mn < mx, 1.0, 0.0)) > 0.0

    lo, hi = lax.fori_loop(0, BISECT_STEPS, lambda i, c: bisect(*c),
                           (jnp.zeros((1, LANES), F32), jnp.full((1, LANES), 2.0, F32)))

    def refine(c):
        lo, hi = bisect(c[0], c[1])
        return lo, hi, not_isolated(lo, hi)

    lo, hi, _ = lax.while_loop(lambda c: c[2], refine, (lo, hi, not_isolated(lo, hi)))
    thr, _ = bracket(lo, hi)
    gt = aff > thr
    eq = aff == thr
    need = capf - jnp.sum(jnp.where(gt, 1.0, 0.0), axis=0, keepdims=True)

    tri = (lax.broadcasted_iota(I32, (ROUTE_TILE, ROUTE_TILE), 0)
           >= lax.broadcasted_iota(I32, (ROUTE_TILE, ROUTE_TILE), 1))
    tri = jnp.where(tri, 1.0, 0.0).astype(BF16)

    def cumsum_tokens(store_tile_start):
        def body(c, carry):
            r0 = pl.multiple_of(c * ROUTE_TILE, ROUTE_TILE)
            if store_tile_start:
                slo_ref[0, pl.ds(c, 1), :] = carry[:, :n_exp].astype(I32)
            cs = jnp.dot(tri, msk_ref[pl.ds(r0, ROUTE_TILE), :], preferred_element_type=F32) + carry
            cum_ref[pl.ds(r0, ROUTE_TILE), :] = cs
            return cs[ROUTE_TILE - 1:ROUTE_TILE, :]

        lax.fori_loop(0, n_tiles, body, jnp.zeros((1, LANES), F32))

    msk_ref[...] = jnp.where(eq, 1.0, 0.0).astype(BF16)
    cumsum_tokens(False)
    take = jnp.logical_and(eq, cum_ref[...] - 1.0 < need)
    mask = jnp.logical_or(gt, take)
    msk_ref[...] = jnp.where(mask, 1.0, 0.0).astype(BF16)
    cumsum_tokens(True)
    pos_ref[...] = jnp.where(mask, cum_ref[...] - 1.0, -1.0)[:, :n_exp].astype(I32)

    slot_row = lax.broadcasted_iota(I32, (1, cap), 1).astype(F32)
    for e in range(n_exp):
        def count(c, acc, e=e):
            r0 = pl.multiple_of(c * COUNT_CHUNK, COUNT_CHUNK)
            col = cum_ref[pl.ds(r0, COUNT_CHUNK), e:e + 1]
            hit = jnp.where(col <= slot_row, 1.0, 0.0)
            return acc + jnp.sum(hit.reshape(COUNT_CHUNK // 8, 8, cap), axis=0)

        acc = lax.fori_loop(0, seq // COUNT_CHUNK, count, jnp.zeros((8, cap), F32))
        idx_ref[0, e:e + 1, :] = jnp.sum(acc, axis=0, keepdims=True).astype(I32) + b * seq


def _route(h2a, batch, seq, cap, n_exp):
    t = h2a.shape[0]
    aff_block = h2a.shape[1] // LANES - 1
    n_tiles = seq // ROUTE_TILE
    return pl.pallas_call(
        functools.partial(_route_kernel, cap=cap, n_exp=n_exp),
        grid=(batch,),
        in_specs=[pl.BlockSpec((seq, LANES), lambda b: (b, aff_block))],
        out_specs=[pl.BlockSpec((1, n_exp, cap), lambda b: (b, 0, 0)),
                   pl.BlockSpec((seq, n_exp), lambda b: (b, 0)),
                   pl.BlockSpec((1, n_tiles, n_exp), lambda b: (b, 0, 0))],
        out_shape=[jax.ShapeDtypeStruct((batch, n_exp, cap), I32),
                   jax.ShapeDtypeStruct((t, n_exp), I32),
                   jax.ShapeDtypeStruct((batch, n_tiles, n_exp), I32)],
        scratch_shapes=[pltpu.VMEM((seq, LANES), BF16), pltpu.VMEM((seq, LANES), F32)],
        compiler_params=_params(("parallel",), V7X_VMEM_LIMIT_BYTES),
        name="route",
    )(h2a)


def _ffn_kernel(idx_ref, h2a_hbm, wg_ref, wu_ref, wd_ref, y_ref, stage_ref, xe_ref, gate_ref, acc_ref, sem,
                *, n_exp, n_ff, rows_per_step):
    e = pl.program_id(0)
    f = pl.program_id(1)
    mp = stage_ref.shape[0]
    m, d = xe_ref.shape
    batch = y_ref.shape[0]
    cap = y_ref.shape[2]

    def row_copy(expert, i):
        r = idx_ref[expert * mp + i]
        return pltpu.make_async_copy(h2a_hbm.at[pl.ds(r, 1)], stage_ref.at[pl.ds(i, 1)], sem)

    def wait_rows():
        pltpu.make_async_copy(h2a_hbm.at[pl.ds(0, mp)], stage_ref, sem).wait()

    @pl.when(f == 0)
    def _():
        @pl.when(e == 0)
        def _():
            def issue(i, carry):
                row_copy(0, i).start()
                return carry

            lax.fori_loop(0, mp, issue, 0)

        wait_rows()
        xe_ref[...] = stage_ref[:m, :d].astype(BF16)
        lane = lax.broadcasted_iota(I32, (m, LANES), 1)
        gate_ref[...] = jnp.sum(jnp.where(lane == e, stage_ref[:m, d:], 0.0), axis=1, keepdims=True)
        acc_ref[...] = jnp.zeros(acc_ref.shape, F32)

    nxt = jnp.minimum(e + 1, n_exp - 1)
    for k in range(rows_per_step):
        row_copy(nxt, f * rows_per_step + k).start()

    wg = wg_ref[0].astype(BF16)
    wu = wu_ref[0].astype(BF16)
    wd = wd_ref[0].astype(BF16)
    for rb in range(m // FFN_ROW_BLOCK):
        rows = pl.ds(rb * FFN_ROW_BLOCK, FFN_ROW_BLOCK)
        xb = xe_ref[rows, :]
        a = jnp.dot(xb, wg, preferred_element_type=F32)
        u = jnp.dot(xb, wu, preferred_element_type=F32)
        hm = (a * jax.nn.sigmoid(a) * u).astype(BF16)
        acc_ref[rows, :] += jnp.dot(hm, wd, preferred_element_type=F32)

    @pl.when(f == n_ff - 1)
    def _():
        for bb in range(batch):
            rows = pl.ds(bb * cap, cap)
            y_ref[bb, 0] = (acc_ref[rows, :] * gate_ref[rows, :]).astype(BF16)

        @pl.when(e == n_exp - 1)
        def _():
            wait_rows()


def _ffn(idx, h2a, w_gate, w_up, w_down):
    batch, n_exp, cap = idx.shape
    d = h2a.shape[1] - LANES
    ff = w_gate.shape[2]
    n_ff = ff // FF_TILE
    m = batch * cap
    rows_per_step = -(-pl.cdiv(m, n_ff) // 8) * 8
    mp = n_ff * rows_per_step
    idx_flat = jnp.pad(idx.transpose(1, 0, 2).reshape(n_exp, m), ((0, 0), (0, mp - m))).reshape(-1)
    grid_spec = pltpu.PrefetchScalarGridSpec(
        num_scalar_prefetch=1,
        grid=(n_exp, n_ff),
        in_specs=[
            pl.BlockSpec(memory_space=pl.ANY),
            pl.BlockSpec((1, d, FF_TILE), lambda e, f, idx: (e, 0, f)),
            pl.BlockSpec((1, d, FF_TILE), lambda e, f, idx: (e, 0, f)),
            pl.BlockSpec((1, FF_TILE, d), lambda e, f, idx: (e, f, 0)),
        ],
        out_specs=pl.BlockSpec((batch, 1, cap, d), lambda e, f, idx: (0, e, 0, 0)),
        scratch_shapes=[pltpu.VMEM((mp, d + LANES), F32), pltpu.VMEM((m, d), BF16),
                        pltpu.VMEM((m, 1), F32), pltpu.VMEM((m, d), F32),
                        pltpu.SemaphoreType.DMA(())],
    )
    return pl.pallas_call(
        functools.partial(_ffn_kernel, n_exp=n_exp, n_ff=n_ff, rows_per_step=rows_per_step),
        grid_spec=grid_spec,
        out_shape=jax.ShapeDtypeStruct((batch, n_exp, cap, d), BF16),
        compiler_params=_params(("arbitrary", "arbitrary"), V7X_VMEM_LIMIT_BYTES),
        name="ffn",
    )(idx_flat, h2a, w_gate, w_up, w_down)


def _window_fits(s_hi, start, rows):
    return s_hi - start <= rows


def _combine_kernel(slo_ref, x1_ref, pos_ref, y_hbm, nf_ref, o_ref, slab_ref, big_ref, acc_ref, sem, big_sem,
                    *, n_exp, cap):
    b = pl.program_id(0)
    j = pl.program_id(1)
    n_batch = pl.num_programs(0)
    n_tiles = pl.num_programs(1)
    step = b * n_tiles + j
    slot = step % 2

    def bounds(bb, jj, e):
        base = (bb * (n_tiles + 1) + jj) * n_exp + e
        return slo_ref[base], slo_ref[base + n_exp]

    def window_start(bb, jj, e, rows):
        s_lo, _ = bounds(bb, jj, e)
        return pl.multiple_of(jnp.minimum((s_lo // SLAB_ALIGN) * SLAB_ALIGN, cap - rows), SLAB_ALIGN)

    def all_small(bb, jj):
        ok = None
        for e in range(n_exp):
            fits = _window_fits(bounds(bb, jj, e)[1], window_start(bb, jj, e, SMALL_ROWS), SMALL_ROWS)
            ok = fits if ok is None else jnp.logical_and(ok, fits)
        return ok

    def small_copy(bb, jj, e, dst_slot):
        return pltpu.make_async_copy(
            y_hbm.at[bb, e, pl.ds(window_start(bb, jj, e, SMALL_ROWS), SMALL_ROWS)],
            slab_ref.at[dst_slot, pl.ds(e * SMALL_ROWS, SMALL_ROWS)], sem.at[dst_slot])

    small_now = all_small(b, j)

    @pl.when(step == 0)
    def _():
        big_ref[...] = jnp.zeros(big_ref.shape, big_ref.dtype)

        @pl.when(small_now)
        def _():
            for e in range(n_exp):
                small_copy(b, j, e, slot).start()

    wrap = j + 1 == n_tiles
    nb = jnp.minimum(jnp.where(wrap, b + 1, b), n_batch - 1)
    nj = jnp.where(wrap, 0, j + 1)

    @pl.when(jnp.logical_and(step + 1 < n_batch * n_tiles, all_small(nb, nj)))
    def _():
        for e in range(n_exp):
            small_copy(nb, nj, e, 1 - slot).start()

    @pl.when(small_now)
    def _():
        lane = lax.broadcasted_iota(I32, (1, SMALL_ROWS), 1)
        tiles = []
        for e in range(n_exp):
            small_copy(b, j, e, slot).wait()
            rel = pos_ref[:, e:e + 1] - window_start(b, j, e, SMALL_ROWS)
            tiles.append(jnp.where(rel == lane, 1.0, 0.0).astype(BF16))
        onehot = jnp.concatenate(tiles, axis=1)
        acc_ref[...] = jnp.dot(onehot, slab_ref[slot], preferred_element_type=F32)

    @pl.when(jnp.logical_not(small_now))
    def _():
        lane = lax.broadcasted_iota(I32, (1, BIG_PAD), 1)
        for e in range(n_exp):
            st = window_start(b, j, e, BIG_ROWS)
            cp = pltpu.make_async_copy(y_hbm.at[b, e, pl.ds(st, BIG_ROWS)],
                                       big_ref.at[pl.ds(0, BIG_ROWS)], big_sem)
            cp.start()
            cp.wait()
            onehot = jnp.where(pos_ref[:, e:e + 1] - st == lane, 1.0, 0.0).astype(BF16)
            part = jnp.dot(onehot, big_ref[...], preferred_element_type=F32)
            if e == 0:
                acc_ref[...] = part
            else:
                acc_ref[...] += part

    x2 = x1_ref[...] + acc_ref[...]
    o_ref[...] = x2 * lax.rsqrt(jnp.mean(x2 * x2, axis=-1, keepdims=True) + EPS) * nf_ref[...]


def _combine(slo, x1, pos, y_slots, norm_final, seq):
    t, d = x1.shape
    batch, n_exp, cap, _ = y_slots.shape
    n_tiles = seq // ROUTE_TILE
    slo_flat = jnp.concatenate([slo, jnp.full((batch, 1, n_exp), cap, I32)], axis=1).reshape(-1)
    row = lambda b, j, slo: (b * n_tiles + j, 0)
    grid_spec = pltpu.PrefetchScalarGridSpec(
        num_scalar_prefetch=1,
        grid=(batch, n_tiles),
        in_specs=[
            pl.BlockSpec((ROUTE_TILE, d), row),
            pl.BlockSpec((ROUTE_TILE, n_exp), row),
            pl.BlockSpec(memory_space=pl.ANY),
            pl.BlockSpec((1, d), lambda b, j, slo: (0, 0)),
        ],
        out_specs=pl.BlockSpec((ROUTE_TILE, d), row),
        scratch_shapes=[pltpu.VMEM((2, n_exp * SMALL_ROWS, d), BF16), pltpu.VMEM((BIG_PAD, d), BF16),
                        pltpu.VMEM((ROUTE_TILE, d), F32),
                        pltpu.SemaphoreType.DMA((2,)), pltpu.SemaphoreType.DMA(())],
    )
    return pl.pallas_call(
        functools.partial(_combine_kernel, n_exp=n_exp, cap=cap),
        grid_spec=grid_spec,
        out_shape=jax.ShapeDtypeStruct((t, d), F32),
        compiler_params=_params(("arbitrary", "arbitrary"), V7X_VMEM_LIMIT_BYTES),
        name="combine",
    )(slo_flat, x1, pos, y_slots, norm_final)


def _rope_tables(seq):
    rows = seq // GRID_W
    row = jnp.repeat(jnp.arange(rows, dtype=F32), GRID_W)
    col = jnp.tile(jnp.arange(GRID_W, dtype=F32), rows)
    n_freq = HALF_ROT // 2
    freqs = ROPE_THETA ** (-jnp.arange(n_freq, dtype=F32) / n_freq)
    ang = jnp.concatenate([row[:, None] * freqs, col[:, None] * freqs], axis=-1)
    cos, sin = jnp.cos(ang), jnp.sin(ang)
    return jnp.concatenate([cos, cos], axis=-1), jnp.concatenate([-sin, sin], axis=-1)


def kernel(x, norm_mix, w_in, ret_decay_fwd, ret_decay_bwd, ret_gn_gain, w_ret_branch, q_norm, k_norm,
           w_att_branch, w_o, norm_ffn, w_router, w_expert_gate, w_expert_up, w_expert_down, norm_final):
    batch, seq, d = x.shape
    depth = norm_mix.shape[0]
    n_exp = w_router.shape[2]
    cap = CAPACITY_FACTOR * seq // n_exp
    assert seq % TOKEN_TILE == 0 and seq % ROUTE_TILE == 0 and seq % ATT_KV_TILE == 0
    assert n_exp <= LANES and d % LANES == 0
    assert cap >= BIG_ROWS and (cap - BIG_ROWS) % SLAB_ALIGN == 0 and (cap - SMALL_ROWS) % SLAB_ALIGN == 0
    assert w_expert_gate.shape[3] % FF_TILE == 0 and (batch * cap) % FFN_ROW_BLOCK == 0
    assert depth == 1, "the final RMSNorm is fused into the combine kernel of the single layer"
    l = 0
    cos2, sin2 = _rope_tables(seq)
    xf = x.reshape(batch * seq, d)
    lg = jnp.stack([jnp.log1p(-jnp.exp(ret_decay_fwd[l].astype(F32))),
                    jnp.log1p(-jnp.exp(ret_decay_bwd[l].astype(F32)))])
    rq, rk, rv, sg, aq, ak, av, gr, ga = _inproj(
        xf, norm_mix[l][None], w_in[l].astype(BF16), cos2, sin2, q_norm[l][None], k_norm[l][None], seq)
    u_ret = _retention(lg, rq, rk, rv, sg, ret_gn_gain[l][None], batch, seq)
    y_att = _attention(aq, ak, av, batch, seq)
    w_r = jnp.pad(w_router[l].astype(F32), ((0, 0), (0, LANES - n_exp)))
    w_r_hi = w_r.astype(BF16)
    w_r_lo = (w_r - w_r_hi.astype(F32)).astype(BF16)
    x1, h2a = _merge(xf, u_ret, y_att, gr, ga, w_ret_branch[l].astype(BF16),
                     w_att_branch[l].astype(BF16), w_o[l].astype(BF16), norm_ffn[l][None],
                     jnp.concatenate([w_r_hi, w_r_lo], axis=1), n_exp)
    idx, pos, slo = _route(h2a, batch, seq, cap, n_exp)
    y_slots = _ffn(idx, h2a, w_expert_gate[l], w_expert_up[l], w_expert_down[l])
    out = _combine(slo, x1, pos, y_slots, norm_final[None], seq)
    return out.reshape(batch, seq, d)
```

```python
import functools

import jax
import jax.numpy as jnp
from jax import lax
from jax.experimental import pallas as pl
from jax.experimental.pallas import tpu as pltpu

F32 = jnp.float32
BF16 = jnp.bfloat16
I32 = jnp.int32

GRID_W = 64
HEAD_DIM = 128
HALF_ROT = HEAD_DIM // 2
RET_HEADS = 4
RET_DIM = RET_HEADS * HEAD_DIM
ATT_HEADS = 4
ATT_KV_HEADS = 2
ATT_GROUP = ATT_HEADS // ATT_KV_HEADS
ATT_DIM = ATT_HEADS * HEAD_DIM
ATT_KV_DIM = ATT_KV_HEADS * HEAD_DIM
CHUNK = 128
ROPE_THETA = 10000.0
CAPACITY_FACTOR = 2
EPS = 1e-6
LOG2_E = 1.4426950408889634

V7X_VMEM_LIMIT_BYTES = 56 * 1024 * 1024
TOKEN_TILE = 512
ATT_Q_TILE = 512
ATT_KV_TILE = 2048
LANES = 128
ROUTE_TILE = 512
SLAB_ALIGN = 16
SMALL_ROWS = LANES
BIG_ROWS = ROUTE_TILE + SLAB_ALIGN
BIG_PAD = 640
FF_TILE = 256
FFN_ROW_BLOCK = 512
BISECT_STEPS = 24


def _params(sem, vmem=None):
    return pltpu.CompilerParams(dimension_semantics=sem, vmem_limit_bytes=vmem)


def _inproj_kernel(x_ref, g_ref, w_ref, cos_ref, sin_ref, qn_ref, kn_ref,
                   rq_ref, rk_ref, rv_ref, sg_ref, aq_ref, ak_ref, av_ref, gr_ref, ga_ref):
    x = x_ref[...]
    h = (x * lax.rsqrt(jnp.mean(x * x, axis=-1, keepdims=True) + EPS) * g_ref[...]).astype(BF16)
    cos = cos_ref[...]
    sin = sin_ref[...]
    d_model = x.shape[1]
    scale = HEAD_DIM ** -0.5

    def proj(lo, width):
        return jnp.dot(h, w_ref[:, lo:lo + width], preferred_element_type=F32)

    def rope(t):
        return t * cos + pltpu.roll(t, HALF_ROT, axis=1) * sin

    def head_norm(t, gain):
        return t * lax.rsqrt(jnp.mean(t * t, axis=-1, keepdims=True) + EPS) * gain

    def head(p, i):
        return p[:, i * HEAD_DIM:(i + 1) * HEAD_DIM]

    def put(ref, i, val):
        ref[:, i * HEAD_DIM:(i + 1) * HEAD_DIM] = val.astype(ref.dtype)

    off = 0
    p = proj(off, RET_DIM)
    for i in range(RET_HEADS):
        put(rq_ref, i, rope(head(p, i)))
    off += RET_DIM
    p = proj(off, RET_DIM)
    for i in range(RET_HEADS):
        put(rk_ref, i, rope(head(p, i)) * scale)
    off += RET_DIM
    rv_ref[...] = proj(off, RET_DIM).astype(BF16)
    off += RET_DIM
    p = proj(off, RET_DIM)
    sg_ref[...] = (p * jax.nn.sigmoid(p)).astype(BF16)
    off += RET_DIM
    p = proj(off, ATT_DIM)
    for i in range(ATT_HEADS):
        put(aq_ref, i, rope(head_norm(head(p, i), qn_ref[...])) * (scale * LOG2_E))
    off += ATT_DIM
    p = proj(off, ATT_KV_DIM)
    for i in range(ATT_KV_HEADS):
        put(ak_ref, i, rope(head_norm(head(p, i), kn_ref[...])))
    off += ATT_KV_DIM
    av_ref[...] = proj(off, ATT_KV_DIM).astype(BF16)
    off += ATT_KV_DIM
    gr_ref[...] = jax.nn.sigmoid(proj(off, d_model)).astype(BF16)
    off += d_model
    ga_ref[...] = jax.nn.sigmoid(proj(off, d_model)).astype(BF16)


def _inproj(xf, norm_g, w_in_bf, cos2, sin2, q_norm, k_norm, seq):
    t, d = xf.shape
    tm = TOKEN_TILE
    in_width = w_in_bf.shape[1]
    steps_per_seq = seq // tm
    row = lambda i: (i, 0)
    const = lambda i: (0, 0)
    pos = lambda i: (i % steps_per_seq, 0)
    widths = (RET_DIM, RET_DIM, RET_DIM, RET_DIM, ATT_DIM, ATT_KV_DIM, ATT_KV_DIM, d, d)
    return pl.pallas_call(
        _inproj_kernel,
        grid=(t // tm,),
        in_specs=[
            pl.BlockSpec((tm, d), row),
            pl.BlockSpec((1, d), const),
            pl.BlockSpec((d, in_width), const),
            pl.BlockSpec((tm, HEAD_DIM), pos),
            pl.BlockSpec((tm, HEAD_DIM), pos),
            pl.BlockSpec((1, HEAD_DIM), const),
            pl.BlockSpec((1, HEAD_DIM), const),
        ],
        out_specs=[pl.BlockSpec((tm, w), row) for w in widths],
        out_shape=[jax.ShapeDtypeStruct((t, w), BF16) for w in widths],
        compiler_params=_params(("parallel",), V7X_VMEM_LIMIT_BYTES),
        name="inproj",
    )(xf, norm_g, w_in_bf, cos2, sin2, q_norm, k_norm)


def _retention_kernel(lg_ref, q_ref, k_ref, v_ref, sg_ref, gain_ref, o_ref, of_ref, ob_ref):
    hd = pl.program_id(1)
    lgf = lg_ref[0, hd]
    lgb = lg_ref[1, hd]
    n_chunks = q_ref.shape[0] // CHUNK
    ii = lax.broadcasted_iota(I32, (CHUNK, CHUNK), 0)
    jj = lax.broadcasted_iota(I32, (CHUNK, CHUNK), 1)
    dist = (ii - jj).astype(F32)
    dec_f = jnp.where(dist >= 0, jnp.exp(lgf * jnp.maximum(dist, 0.0)), 0.0)
    dec_b = jnp.where(dist < 0, jnp.exp(lgb * jnp.maximum(-dist, 0.0)), 0.0)
    t = lax.broadcasted_iota(I32, (CHUNK, 1), 0).astype(F32)
    kd_f = jnp.exp(lgf * (CHUNK - 1.0 - t))
    qd_f = jnp.exp(lgf * (t + 1.0))
    kd_b = jnp.exp(lgb * t)
    qd_b = jnp.exp(lgb * (CHUNK - t))
    zero_row = jnp.zeros((1, HEAD_DIM), F32)
    cd_f = jnp.exp(zero_row + lgf * CHUNK)
    cd_b = jnp.exp(zero_row + lgb * CHUNK)

    def one_chunk(n, dec, kd, qd, cd, state):
        r0 = pl.multiple_of(n * CHUNK, CHUNK)
        q = q_ref[pl.ds(r0, CHUNK), :]
        k = k_ref[pl.ds(r0, CHUNK), :]
        v = v_ref[pl.ds(r0, CHUNK), :]
        s = lax.dot_general(q, k, (((1,), (1,)), ((), ())), preferred_element_type=F32) * dec
        intra = jnp.dot(s.astype(BF16), v, preferred_element_type=F32)
        q_dec = (q.astype(F32) * qd).astype(BF16)
        inter = jnp.dot(q_dec, state.astype(BF16), preferred_element_type=F32)
        k_dec_t = (k.astype(F32) * kd).T.astype(BF16)
        new_state = state * cd + jnp.dot(k_dec_t, v, preferred_element_type=F32)
        return r0, intra + inter, new_state

    def scan_body(n, carry):
        sf, sb = carry
        r0, of, sf = one_chunk(n, dec_f, kd_f, qd_f, cd_f, sf)
        of_ref[pl.ds(r0, CHUNK), :] = of
        r0, ob, sb = one_chunk(n_chunks - 1 - n, dec_b, kd_b, qd_b, cd_b, sb)
        ob_ref[pl.ds(r0, CHUNK), :] = ob
        return sf, sb

    zeros = jnp.zeros((HEAD_DIM, HEAD_DIM), F32)
    lax.fori_loop(0, n_chunks, scan_body, (zeros, zeros))

    rows = TOKEN_TILE

    def norm_body(n, carry):
        r0 = pl.multiple_of(n * rows, rows)
        o = of_ref[pl.ds(r0, rows), :] + ob_ref[pl.ds(r0, rows), :]
        mu = jnp.mean(o, axis=-1, keepdims=True)
        var = jnp.mean(jnp.square(o - mu), axis=-1, keepdims=True)
        yn = (o - mu) * lax.rsqrt(var + EPS) * gain_ref[...]
        o_ref[pl.ds(r0, rows), :] = (sg_ref[pl.ds(r0, rows), :].astype(F32) * yn).astype(BF16)
        return carry

    lax.fori_loop(0, q_ref.shape[0] // rows, norm_body, 0)


def _retention(lg, rq, rk, rv, sg, gn_gain, batch, seq):
    t = rq.shape[0]
    blk = pl.BlockSpec((seq, HEAD_DIM), lambda b, h: (b, h))
    return pl.pallas_call(
        _retention_kernel,
        grid=(batch, RET_HEADS),
        in_specs=[
            pl.BlockSpec(memory_space=pltpu.SMEM),
            blk, blk, blk, blk,
            pl.BlockSpec((1, HEAD_DIM), lambda b, h: (0, h)),
        ],
        out_specs=blk,
        out_shape=jax.ShapeDtypeStruct((t, RET_DIM), BF16),
        scratch_shapes=[pltpu.VMEM((seq, HEAD_DIM), F32), pltpu.VMEM((seq, HEAD_DIM), F32)],
        compiler_params=_params(("parallel", "parallel"), V7X_VMEM_LIMIT_BYTES),
        name="retention",
    )(lg, rq, rk, rv, sg, gn_gain)


def _attention_kernel(q_ref, k_ref, v_ref, o_ref):
    tq = q_ref.shape[0]
    tk = ATT_KV_TILE
    n_kv = k_ref.shape[0] // tk
    qs = [q_ref[:, hh * HEAD_DIM:(hh + 1) * HEAD_DIM] for hh in range(ATT_GROUP)]

    def body(c, carry):
        r0 = pl.multiple_of(c * tk, tk)
        k = k_ref[pl.ds(r0, tk), :]
        v = v_ref[pl.ds(r0, tk), :]
        out = []
        for q, (m, l, acc) in zip(qs, carry):
            s = lax.dot_general(q, k, (((1,), (1,)), ((), ())), preferred_element_type=F32)
            m_new = jnp.maximum(m, jnp.max(s, axis=-1, keepdims=True))
            alpha = jnp.exp2(m - m_new)
            p = jnp.exp2(s - m_new)
            l = alpha * l + jnp.sum(p, axis=-1, keepdims=True)
            acc = alpha * acc + jnp.dot(p.astype(BF16), v, preferred_element_type=F32)
            out.append((m_new, l, acc))
        return tuple(out)

    init = (jnp.full((tq, 1), -jnp.inf, F32), jnp.zeros((tq, 1), F32), jnp.zeros((tq, HEAD_DIM), F32))
    final = lax.fori_loop(0, n_kv, body, (init,) * ATT_GROUP)
    for hh, (_, l, acc) in enumerate(final):
        o_ref[:, hh * HEAD_DIM:(hh + 1) * HEAD_DIM] = (acc / l).astype(BF16)


def _attention(aq, ak, av, batch, seq):
    t = aq.shape[0]
    tq = ATT_Q_TILE
    nq = seq // tq
    gw = ATT_GROUP * HEAD_DIM
    q_spec = pl.BlockSpec((tq, gw), lambda b, g, i: (b * nq + i, g))
    kv_spec = pl.BlockSpec((seq, HEAD_DIM), lambda b, g, i: (b, g))
    return pl.pallas_call(
        _attention_kernel,
        grid=(batch, ATT_KV_HEADS, nq),
        in_specs=[q_spec, kv_spec, kv_spec],
        out_specs=q_spec,
        out_shape=jax.ShapeDtypeStruct((t, ATT_DIM), BF16),
        compiler_params=_params(("parallel", "parallel", "parallel"), V7X_VMEM_LIMIT_BYTES),
        name="attention",
    )(aq, ak, av)


def _merge_kernel(x_ref, ur_ref, ya_ref, gr_ref, ga_ref, wr_ref, wa_ref, wo_ref, nf_ref, wrt_ref,
                  x1_ref, h2a_ref, aff_ref, *, n_exp):
    d = x_ref.shape[1]
    y_ret = jnp.dot(ur_ref[...], wr_ref[...], preferred_element_type=F32)
    y_att = jnp.dot(ya_ref[...], wa_ref[...], preferred_element_type=F32)
    mixed = gr_ref[...].astype(F32) * y_ret + ga_ref[...].astype(F32) * y_att
    x1 = x_ref[...] + jnp.dot(mixed.astype(BF16), wo_ref[...], preferred_element_type=F32)
    x1_ref[...] = x1
    h2 = x1 * lax.rsqrt(jnp.mean(x1 * x1, axis=-1, keepdims=True) + EPS) * nf_ref[...]
    h2a_ref[:, :d] = h2
    h_hi = h2.astype(BF16)
    h_lo = (h2 - h_hi.astype(F32)).astype(BF16)
    r_hi = jnp.dot(h_hi, wrt_ref[...], preferred_element_type=F32)
    r_lo = jnp.dot(h_lo, wrt_ref[:, :LANES], preferred_element_type=F32)
    logits = r_hi[:, :LANES] + r_hi[:, LANES:] + r_lo
    lane = lax.broadcasted_iota(I32, logits.shape, 1)
    logits = jnp.where(lane < n_exp, logits, -jnp.inf)
    e = jnp.exp(logits - jnp.max(logits, axis=-1, keepdims=True))
    aff = e / jnp.sum(e, axis=-1, keepdims=True)
    h2a_ref[:, d:] = aff
    aff_ref[...] = aff[:, :n_exp]


def _merge(xf, u_ret, y_att, gr, ga, w_ret_bf, w_att_bf, w_o_bf, norm_ffn, w_router_split, n_exp):
    t, d = xf.shape
    tm = TOKEN_TILE
    row = lambda i: (i, 0)
    const = lambda i: (0, 0)
    return pl.pallas_call(
        functools.partial(_merge_kernel, n_exp=n_exp),
        grid=(t // tm,),
        in_specs=[
            pl.BlockSpec((tm, d), row),
            pl.BlockSpec((tm, RET_DIM), row),
            pl.BlockSpec((tm, ATT_DIM), row),
            pl.BlockSpec((tm, d), row),
            pl.BlockSpec((tm, d), row),
            pl.BlockSpec((RET_DIM, d), const),
            pl.BlockSpec((ATT_DIM, d), const),
            pl.BlockSpec((d, d), const),
            pl.BlockSpec((1, d), const),
            pl.BlockSpec((d, 2 * LANES), const),
        ],
        out_specs=[pl.BlockSpec((tm, d), row), pl.BlockSpec((tm, d + LANES), row),
                   pl.BlockSpec((tm, n_exp), row)],
        out_shape=[jax.ShapeDtypeStruct((t, d), F32), jax.ShapeDtypeStruct((t, d + LANES), F32),
                   jax.ShapeDtypeStruct((t, n_exp), F32)],
        compiler_params=_params(("parallel",), V7X_VMEM_LIMIT_BYTES),
        name="merge",
    )(xf, u_ret, y_att, gr, ga, w_ret_bf, w_att_bf, w_o_bf, norm_ffn, w_router_split)


def _route_kernel(aff_ref, affc_ref, idx_ref, pos_ref, slo_ref, msk_ref, cum_ref, bnd_ref, bnd_smem, cnt_ref,
                  sem, *, cap, n_exp):
    b = pl.program_id(0)
    seq = aff_ref.shape[0]
    n_tiles = seq // ROUTE_TILE
    aff = aff_ref[...]
    affc = affc_ref[...]
    capf = float(cap)

    def per_expert(x, op, reduce):
        r = reduce(x.reshape(x.shape[0] // 8, 8, LANES), axis=0)
        shift = n_exp
        while shift < LANES:
            r = op(r, pltpu.roll(r, shift, axis=1))
            shift *= 2
        return reduce(r, axis=0, keepdims=True)

    def bisect(lo, hi):
        mid = 0.5 * (lo + hi)
        ok = per_expert(jnp.where(affc >= mid, 1.0, 0.0), jnp.add, jnp.sum) >= capf
        return jnp.where(ok, mid, lo), jnp.where(ok, hi, mid)

    def bracket(lo, hi):
        mn = per_expert(jnp.where(affc >= lo, affc, jnp.inf), jnp.minimum, jnp.min)
        mx = per_expert(jnp.where(affc < hi, affc, -jnp.inf), jnp.maximum, jnp.max)
        return mn, mx

    def not_isolated(lo, hi):
        mn, mx = bracket(lo, hi)
        return jnp.max(jnp.where(mn < mx, 1.0, 0.0)) > 0.0

    lo, hi = lax.fori_loop(0, BISECT_STEPS, lambda i, c: bisect(*c),
                           (jnp.zeros((1, LANES), F32), jnp.full((1, LANES), 2.0, F32)))

    def refine(c):
        lo, hi = bisect(c[0], c[1])
        return lo, hi, not_isolated(lo, hi)

    lo, hi, _ = lax.while_loop(lambda c: c[2], refine, (lo, hi, not_isolated(lo, hi)))
    thr, _ = bracket(lo, hi)
    gt = aff > thr
    eq = aff == thr
    n_gt = jnp.sum(jnp.where(gt, 1.0, 0.0).reshape(seq // 8, 8, LANES), axis=0)
    need = capf - jnp.sum(n_gt, axis=0, keepdims=True)

    tri = (lax.broadcasted_iota(I32, (ROUTE_TILE, ROUTE_TILE), 0)
           >= lax.broadcasted_iota(I32, (ROUTE_TILE, ROUTE_TILE), 1))
    tri = jnp.where(tri, 1.0, 0.0).astype(BF16)

    def cumsum_tokens(store_tile_start):
        def body(c, carry):
            r0 = pl.multiple_of(c * ROUTE_TILE, ROUTE_TILE)
            if store_tile_start:
                slo_ref[0, pl.ds(c, 1), :] = carry[:, :n_exp].astype(I32)
                bnd_ref[pl.ds(c, 1), :] = carry.astype(I32)
            cs = jnp.dot(tri, msk_ref[pl.ds(r0, ROUTE_TILE), :], preferred_element_type=F32) + carry
            cum_ref[pl.ds(r0, ROUTE_TILE), :] = cs
            return cs[ROUTE_TILE - 1:ROUTE_TILE, :]

        return lax.fori_loop(0, n_tiles, body, jnp.zeros((1, LANES), F32))

    msk_ref[...] = jnp.where(eq, 1.0, 0.0).astype(BF16)
    cumsum_tokens(False)
    take = jnp.logical_and(eq, cum_ref[...] - 1.0 < need)
    mask = jnp.logical_or(gt, take)
    msk_ref[...] = jnp.where(mask, 1.0, 0.0).astype(BF16)
    bnd_ref[...] = jnp.zeros(bnd_ref.shape, I32)
    bnd_ref[pl.ds(n_tiles, 1), :] = cumsum_tokens(True).astype(I32)
    pos_ref[...] = jnp.where(mask, cum_ref[...] - 1.0, -1.0)[:, :n_exp].astype(I32)

    to_smem = pltpu.make_async_copy(bnd_ref, bnd_smem, sem)
    to_smem.start()
    to_smem.wait()

    slot_row = lax.broadcasted_iota(I32, (1, cap), 1).astype(F32)
    lane_row = lax.broadcasted_iota(I32, (1, LANES), 1)
    for e in range(n_exp):
        ends = bnd_ref[pl.ds(1, n_tiles), e:e + 1].astype(F32)
        whole = jnp.sum(jnp.where(slot_row >= ends, float(ROUTE_TILE), 0.0), axis=0, keepdims=True)
        cnt_ref[...] = jnp.zeros(cnt_ref.shape, F32)

        def tile_body(c, carry, e=e):
            lo = bnd_smem[c, e]
            hi = bnd_smem[c + 1, e]
            r0 = pl.multiple_of(c * ROUTE_TILE, ROUTE_TILE)
            col = cum_ref[pl.ds(r0, ROUTE_TILE), e:e + 1]

            def lane_tile(k, carry2):
                slots = k * LANES + lane_row
                below_hi = jnp.where(slots < hi, 1.0, 0.0)
                hit = jnp.where(col <= slots.astype(F32), below_hi, 0.0)
                cnt_ref[k] += jnp.sum(hit.reshape(ROUTE_TILE // 8, 8, LANES), axis=0)
                return carry2

            lax.fori_loop(lo // LANES, (hi + LANES - 1) // LANES, lane_tile, 0)
            return carry

        lax.fori_loop(0, n_tiles, tile_body, 0)
        for k in range(cap // LANES):
            part = jnp.sum(cnt_ref[k], axis=0, keepdims=True) + whole[:, k * LANES:(k + 1) * LANES]
            idx_ref[0, e:e + 1, k * LANES:(k + 1) * LANES] = part.astype(I32) + b * seq


def _route(h2a, aff, batch, seq, cap):
    t, n_exp = aff.shape
    pack = LANES // n_exp
    aff_packed = aff.reshape(t // pack, LANES)
    aff_block = h2a.shape[1] // LANES - 1
    n_tiles = seq // ROUTE_TILE
    bnd_rows = -(-(n_tiles + 1) // 8) * 8
    return pl.pallas_call(
        functools.partial(_route_kernel, cap=cap, n_exp=n_exp),
        grid=(batch,),
        in_specs=[pl.BlockSpec((seq, LANES), lambda b: (b, aff_block)),
                  pl.BlockSpec((seq // pack, LANES), lambda b: (b, 0))],
        out_specs=[pl.BlockSpec((1, n_exp, cap), lambda b: (b, 0, 0)),
                   pl.BlockSpec((seq, n_exp), lambda b: (b, 0)),
                   pl.BlockSpec((1, n_tiles, n_exp), lambda b: (b, 0, 0))],
        out_shape=[jax.ShapeDtypeStruct((batch, n_exp, cap), I32),
                   jax.ShapeDtypeStruct((t, n_exp), I32),
                   jax.ShapeDtypeStruct((batch, n_tiles, n_exp), I32)],
        scratch_shapes=[pltpu.VMEM((seq, LANES), BF16), pltpu.VMEM((seq, LANES), F32),
                        pltpu.VMEM((bnd_rows, LANES), I32), pltpu.SMEM((bnd_rows, LANES), I32),
                        pltpu.VMEM((cap // LANES, 8, LANES), F32), pltpu.SemaphoreType.DMA(())],
        compiler_params=_params(("parallel",), V7X_VMEM_LIMIT_BYTES),
        name="route",
    )(h2a, aff_packed)


def _ffn_kernel(idx_ref, h2a_hbm, wg_ref, wu_ref, wd_ref, y_ref, stage_ref, xe_ref, gate_ref, acc_ref, sem,
                *, n_exp, n_ff, rows_per_step):
    e = pl.program_id(0)
    f = pl.program_id(1)
    mp = stage_ref.shape[0]
    m, d = xe_ref.shape
    batch = y_ref.shape[0]
    cap = y_ref.shape[2]

    def row_copy(expert, i):
        r = idx_ref[expert * mp + i]
        return pltpu.make_async_copy(h2a_hbm.at[pl.ds(r, 1)], stage_ref.at[pl.ds(i, 1)], sem)

    def wait_rows():
        pltpu.make_async_copy(h2a_hbm.at[pl.ds(0, mp)], stage_ref, sem).wait()

    @pl.when(f == 0)
    def _():
        @pl.when(e == 0)
        def _():
            def issue(i, carry):
                row_copy(0, i).start()
                return carry

            lax.fori_loop(0, mp, issue, 0)

        wait_rows()
        xe_ref[...] = stage_ref[:m, :d].astype(BF16)
        lane = lax.broadcasted_iota(I32, (m, LANES), 1)
        gate_ref[...] = jnp.sum(jnp.where(lane == e, stage_ref[:m, d:], 0.0), axis=1, keepdims=True)
        acc_ref[...] = jnp.zeros(acc_ref.shape, F32)

    nxt = jnp.minimum(e + 1, n_exp - 1)
    for k in range(rows_per_step):
        row_copy(nxt, f * rows_per_step + k).start()

    wg = wg_ref[0].astype(BF16)
    wu = wu_ref[0].astype(BF16)
    wd = wd_ref[0].astype(BF16)
    for rb in range(m // FFN_ROW_BLOCK):
        rows = pl.ds(rb * FFN_ROW_BLOCK, FFN_ROW_BLOCK)
        xb = xe_ref[rows, :]
        a = jnp.dot(xb, wg, preferred_element_type=F32)
        u = jnp.dot(xb, wu, preferred_element_type=F32)
        hm = (a * jax.nn.sigmoid(a) * u).astype(BF16)
        acc_ref[rows, :] += jnp.dot(hm, wd, preferred_element_type=F32)

    @pl.when(f == n_ff - 1)
    def _():
        for bb in range(batch):
            rows = pl.ds(bb * cap, cap)
            y_ref[bb, 0] = (acc_ref[rows, :] * gate_ref[rows, :]).astype(BF16)

        @pl.when(e == n_exp - 1)
        def _():
            wait_rows()


def _ffn(idx, h2a, w_gate, w_up, w_down):
    batch, n_exp, cap = idx.shape
    d = h2a.shape[1] - LANES
    ff = w_gate.shape[2]
    n_ff = ff // FF_TILE
    m = batch * cap
    rows_per_step = -(-pl.cdiv(m, n_ff) // 8) * 8
    mp = n_ff * rows_per_step
    idx_flat = jnp.pad(idx.transpose(1, 0, 2).reshape(n_exp, m), ((0, 0), (0, mp - m))).reshape(-1)
    grid_spec = pltpu.PrefetchScalarGridSpec(
        num_scalar_prefetch=1,
        grid=(n_exp, n_ff),
        in_specs=[
            pl.BlockSpec(memory_space=pl.ANY),
            pl.BlockSpec((1, d, FF_TILE), lambda e, f, idx: (e, 0, f)),
            pl.BlockSpec((1, d, FF_TILE), lambda e, f, idx: (e, 0, f)),
            pl.BlockSpec((1, FF_TILE, d), lambda e, f, idx: (e, f, 0)),
        ],
        out_specs=pl.BlockSpec((batch, 1, cap, d), lambda e, f, idx: (0, e, 0, 0)),
        scratch_shapes=[pltpu.VMEM((mp, d + LANES), F32), pltpu.VMEM((m, d), BF16),
                        pltpu.VMEM((m, 1), F32), pltpu.VMEM((m, d), F32),
                        pltpu.SemaphoreType.DMA(())],
    )
    return pl.pallas_call(
        functools.partial(_ffn_kernel, n_exp=n_exp, n_ff=n_ff, rows_per_step=rows_per_step),
        grid_spec=grid_spec,
        out_shape=jax.ShapeDtypeStruct((batch, n_exp, cap, d), BF16),
        compiler_params=_params(("arbitrary", "arbitrary"), V7X_VMEM_LIMIT_BYTES),
        name="ffn",
    )(idx_flat, h2a, w_gate, w_up, w_down)


def _window_fits(s_hi, start, rows):
    return s_hi - start <= rows


def _combine_kernel(slo_ref, x1_ref, pos_ref, y_hbm, nf_ref, o_ref, slab_ref, big_ref, acc_ref, sem, big_sem,
                    *, n_exp, cap):
    b = pl.program_id(0)
    j = pl.program_id(1)
    n_batch = pl.num_programs(0)
    n_tiles = pl.num_programs(1)
    step = b * n_tiles + j
    slot = step % 2

    def bounds(bb, jj, e):
        base = (bb * (n_tiles + 1) + jj) * n_exp + e
        return slo_ref[base], slo_ref[base + n_exp]

    def window_start(bb, jj, e, rows):
        s_lo, _ = bounds(bb, jj, e)
        return pl.multiple_of(jnp.minimum((s_lo // SLAB_ALIGN) * SLAB_ALIGN, cap - rows), SLAB_ALIGN)

    def all_small(bb, jj):
        ok = None
        for e in range(n_exp):
            fits = _window_fits(bounds(bb, jj, e)[1], window_start(bb, jj, e, SMALL_ROWS), SMALL_ROWS)
            ok = fits if ok is None else jnp.logical_and(ok, fits)
        return ok

    def small_copy(bb, jj, e, dst_slot):
        return pltpu.make_async_copy(
            y_hbm.at[bb, e, pl.ds(window_start(bb, jj, e, SMALL_ROWS), SMALL_ROWS)],
            slab_ref.at[dst_slot, pl.ds(e * SMALL_ROWS, SMALL_ROWS)], sem.at[dst_slot])

    small_now = all_small(b, j)

    @pl.when(step == 0)
    def _():
        big_ref[...] = jnp.zeros(big_ref.shape, big_ref.dtype)

        @pl.when(small_now)
        def _():
            for e in range(n_exp):
                small_copy(b, j, e, slot).start()

    wrap = j + 1 == n_tiles
    nb = jnp.minimum(jnp.where(wrap, b + 1, b), n_batch - 1)
    nj = jnp.where(wrap, 0, j + 1)

    @pl.when(jnp.logical_and(step + 1 < n_batch * n_tiles, all_small(nb, nj)))
    def _():
        for e in range(n_exp):
            small_copy(nb, nj, e, 1 - slot).start()

    @pl.when(small_now)
    def _():
        lane = lax.broadcasted_iota(I32, (1, SMALL_ROWS), 1)
        tiles = []
        for e in range(n_exp):
            small_copy(b, j, e, slot).wait()
            rel = pos_ref[:, e:e + 1] - window_start(b, j, e, SMALL_ROWS)
            tiles.append(jnp.where(rel == lane, 1.0, 0.0).astype(BF16))
        onehot = jnp.concatenate(tiles, axis=1)
        acc_ref[...] = jnp.dot(onehot, slab_ref[slot], preferred_element_type=F32)

    @pl.when(jnp.logical_not(small_now))
    def _():
        lane = lax.broadcasted_iota(I32, (1, BIG_PAD), 1)
        for e in range(n_exp):
            st = window_start(b, j, e, BIG_ROWS)
            cp = pltpu.make_async_copy(y_hbm.at[b, e, pl.ds(st, BIG_ROWS)],
                                       big_ref.at[pl.ds(0, BIG_ROWS)], big_sem)
            cp.start()
            cp.wait()
            onehot = jnp.where(pos_ref[:, e:e + 1] - st == lane, 1.0, 0.0).astype(BF16)
            part = jnp.dot(onehot, big_ref[...], preferred_element_type=F32)
            if e == 0:
                acc_ref[...] = part
            else:
                acc_ref[...] += part

    x2 = x1_ref[...] + acc_ref[...]
    o_ref[...] = x2 * lax.rsqrt(jnp.mean(x2 * x2, axis=-1, keepdims=True) + EPS) * nf_ref[...]


def _combine(slo, x1, pos, y_slots, norm_final, seq):
    t, d = x1.shape
    batch, n_exp, cap, _ = y_slots.shape
    n_tiles = seq // ROUTE_TILE
    slo_flat = jnp.concatenate([slo, jnp.full((batch, 1, n_exp), cap, I32)], axis=1).reshape(-1)
    row = lambda b, j, slo: (b * n_tiles + j, 0)
    grid_spec = pltpu.PrefetchScalarGridSpec(
        num_scalar_prefetch=1,
        grid=(batch, n_tiles),
        in_specs=[
            pl.BlockSpec((ROUTE_TILE, d), row),
            pl.BlockSpec((ROUTE_TILE, n_exp), row),
            pl.BlockSpec(memory_space=pl.ANY),
            pl.BlockSpec((1, d), lambda b, j, slo: (0, 0)),
        ],
        out_specs=pl.BlockSpec((ROUTE_TILE, d), row),
        scratch_shapes=[pltpu.VMEM((2, n_exp * SMALL_ROWS, d), BF16), pltpu.VMEM((BIG_PAD, d), BF16),
                        pltpu.VMEM((ROUTE_TILE, d), F32),
                        pltpu.SemaphoreType.DMA((2,)), pltpu.SemaphoreType.DMA(())],
    )
    return pl.pallas_call(
        functools.partial(_combine_kernel, n_exp=n_exp, cap=cap),
        grid_spec=grid_spec,
        out_shape=jax.ShapeDtypeStruct((t, d), F32),
        compiler_params=_params(("arbitrary", "arbitrary"), V7X_VMEM_LIMIT_BYTES),
        name="combine",
    )(slo_flat, x1, pos, y_slots, norm_final)


def _rope_tables(seq):
    rows = seq // GRID_W
    row = jnp.repeat(jnp.arange(rows, dtype=F32), GRID_W)
    col = jnp.tile(jnp.arange(GRID_W, dtype=F32), rows)
    n_freq = HALF_ROT // 2
    freqs = ROPE_THETA ** (-jnp.arange(n_freq, dtype=F32) / n_freq)
    ang = jnp.concatenate([row[:, None] * freqs, col[:, None] * freqs], axis=-1)
    cos, sin = jnp.cos(ang), jnp.sin(ang)
    return jnp.concatenate([cos, cos], axis=-1), jnp.concatenate([-sin, sin], axis=-1)


def kernel(x, norm_mix, w_in, ret_decay_fwd, ret_decay_bwd, ret_gn_gain, w_ret_branch, q_norm, k_norm,
           w_att_branch, w_o, norm_ffn, w_router, w_expert_gate, w_expert_up, w_expert_down, norm_final):
    batch, seq, d = x.shape
    depth = norm_mix.shape[0]
    n_exp = w_router.shape[2]
    cap = CAPACITY_FACTOR * seq // n_exp
    assert seq % TOKEN_TILE == 0 and seq % ROUTE_TILE == 0 and seq % ATT_KV_TILE == 0
    assert LANES % n_exp == 0 and d % LANES == 0 and seq % (8 * LANES // n_exp) == 0
    assert cap >= BIG_ROWS and (cap - BIG_ROWS) % SLAB_ALIGN == 0 and (cap - SMALL_ROWS) % SLAB_ALIGN == 0
    assert w_expert_gate.shape[3] % FF_TILE == 0 and (batch * cap) % FFN_ROW_BLOCK == 0
    assert depth == 1, "the final RMSNorm is fused into the combine kernel of the single layer"
    l = 0
    cos2, sin2 = _rope_tables(seq)
    xf = x.reshape(batch * seq, d)
    lg = jnp.stack([jnp.log1p(-jnp.exp(ret_decay_fwd[l].astype(F32))),
                    jnp.log1p(-jnp.exp(ret_decay_bwd[l].astype(F32)))])
    rq, rk, rv, sg, aq, ak, av, gr, ga = _inproj(
        xf, norm_mix[l][None], w_in[l].astype(BF16), cos2, sin2, q_norm[l][None], k_norm[l][None], seq)
    u_ret = _retention(lg, rq, rk, rv, sg, ret_gn_gain[l][None], batch, seq)
    y_att = _attention(aq, ak, av, batch, seq)
    w_r = jnp.pad(w_router[l].astype(F32), ((0, 0), (0, LANES - n_exp)))
    w_r_hi = w_r.astype(BF16)
    w_r_lo = (w_r - w_r_hi.astype(F32)).astype(BF16)
    x1, h2a, aff = _merge(xf, u_ret, y_att, gr, ga, w_ret_branch[l].astype(BF16),
                          w_att_branch[l].astype(BF16), w_o[l].astype(BF16), norm_ffn[l][None],
                          jnp.concatenate([w_r_hi, w_r_lo], axis=1), n_exp)
    idx, pos, slo = _route(h2a, aff, batch, seq, cap)
    y_slots = _ffn(idx, h2a, w_expert_gate[l], w_expert_up[l], w_expert_down[l])
    out = _combine(slo, x1, pos, y_slots, norm_final[None], seq)
    return out.reshape(batch, seq, d)
```

```python
import functools

import jax
import jax.numpy as jnp
import numpy as np
from jax import lax
from jax.experimental import pallas as pl
from jax.experimental.pallas import tpu as pltpu

F32 = jnp.float32
BF16 = jnp.bfloat16
I32 = jnp.int32

GRID_W = 64
HEAD_DIM = 128
HALF_ROT = HEAD_DIM // 2
RET_HEADS = 4
RET_DIM = RET_HEADS * HEAD_DIM
ATT_HEADS = 4
ATT_KV_HEADS = 2
ATT_GROUP = ATT_HEADS // ATT_KV_HEADS
ATT_DIM = ATT_HEADS * HEAD_DIM
ATT_KV_DIM = ATT_KV_HEADS * HEAD_DIM
CHUNK = 128
ROPE_THETA = 10000.0
CAPACITY_FACTOR = 2
EPS = 1e-6
LOG2_E = 1.4426950408889634

V7X_VMEM_LIMIT_BYTES = 56 * 1024 * 1024
TOKEN_TILE = 512
ATT_Q_TILE = 512
ATT_KV_TILE = 2048
RET_UNROLL = 8
LANES = 128
ROUTE_TILE = 512
SLAB_ALIGN = 16
SMALL_ROWS = LANES
BIG_ROWS = ROUTE_TILE + SLAB_ALIGN
BIG_PAD = 640
COMBINE_GROUP = 4
FF_TILE = 256
FFN_ROW_BLOCK = 512
BISECT_STEPS = 24


def _params(sem, vmem=None):
    return pltpu.CompilerParams(dimension_semantics=sem, vmem_limit_bytes=vmem)


def _inproj_kernel(x_ref, g_ref, w_ref, cos_ref, sin_ref, qn_ref, kn_ref,
                   rq_ref, rk_ref, rv_ref, sg_ref, aq_ref, ak_ref, av_ref, gr_ref, ga_ref):
    x = x_ref[...]
    h = (x * lax.rsqrt(jnp.mean(x * x, axis=-1, keepdims=True) + EPS) * g_ref[...]).astype(BF16)
    cos = cos_ref[...]
    sin = sin_ref[...]
    d_model = x.shape[1]
    scale = HEAD_DIM ** -0.5

    def proj(lo, width):
        return jnp.dot(h, w_ref[:, lo:lo + width], preferred_element_type=F32)

    def rope(t):
        return t * cos + pltpu.roll(t, HALF_ROT, axis=1) * sin

    def head_norm(t, gain):
        return t * lax.rsqrt(jnp.mean(t * t, axis=-1, keepdims=True) + EPS) * gain

    def head(p, i):
        return p[:, i * HEAD_DIM:(i + 1) * HEAD_DIM]

    def put(ref, i, val):
        ref[:, i * HEAD_DIM:(i + 1) * HEAD_DIM] = val.astype(ref.dtype)

    off = 0
    p = proj(off, RET_DIM)
    for i in range(RET_HEADS):
        put(rq_ref, i, rope(head(p, i)))
    off += RET_DIM
    p = proj(off, RET_DIM)
    for i in range(RET_HEADS):
        put(rk_ref, i, rope(head(p, i)) * scale)
    off += RET_DIM
    rv_ref[...] = proj(off, RET_DIM).astype(BF16)
    off += RET_DIM
    p = proj(off, RET_DIM)
    sg_ref[...] = (p * jax.nn.sigmoid(p)).astype(BF16)
    off += RET_DIM
    p = proj(off, ATT_DIM)
    for i in range(ATT_HEADS):
        put(aq_ref, i, rope(head_norm(head(p, i), qn_ref[...])) * (scale * LOG2_E))
    off += ATT_DIM
    p = proj(off, ATT_KV_DIM)
    for i in range(ATT_KV_HEADS):
        put(ak_ref, i, rope(head_norm(head(p, i), kn_ref[...])))
    off += ATT_KV_DIM
    av_ref[...] = proj(off, ATT_KV_DIM).astype(BF16)
    off += ATT_KV_DIM
    gr_ref[...] = jax.nn.sigmoid(proj(off, d_model)).astype(BF16)
    off += d_model
    ga_ref[...] = jax.nn.sigmoid(proj(off, d_model)).astype(BF16)


def _inproj(xf, norm_g, w_in_bf, cos2, sin2, q_norm, k_norm, seq):
    t, d = xf.shape
    tm = TOKEN_TILE
    in_width = w_in_bf.shape[1]
    steps_per_seq = seq // tm
    row = lambda i: (i, 0)
    const = lambda i: (0, 0)
    pos = lambda i: (i % steps_per_seq, 0)
    widths = (RET_DIM, RET_DIM, RET_DIM, RET_DIM, ATT_DIM, ATT_KV_DIM, ATT_KV_DIM, d, d)
    return pl.pallas_call(
        _inproj_kernel,
        grid=(t // tm,),
        in_specs=[
            pl.BlockSpec((tm, d), row),
            pl.BlockSpec((1, d), const),
            pl.BlockSpec((d, in_width), const),
            pl.BlockSpec((tm, HEAD_DIM), pos),
            pl.BlockSpec((tm, HEAD_DIM), pos),
            pl.BlockSpec((1, HEAD_DIM), const),
            pl.BlockSpec((1, HEAD_DIM), const),
        ],
        out_specs=[pl.BlockSpec((tm, w), row) for w in widths],
        out_shape=[jax.ShapeDtypeStruct((t, w), BF16) for w in widths],
        compiler_params=_params(("parallel",), V7X_VMEM_LIMIT_BYTES),
        name="inproj",
    )(xf, norm_g, w_in_bf, cos2, sin2, q_norm, k_norm)


def _retention_kernel(lg_ref, q_ref, k_ref, v_ref, sg_ref, gain_ref, o_ref, of_ref, ob_ref):
    hd = pl.program_id(1)
    lgf = lg_ref[0, hd]
    lgb = lg_ref[1, hd]
    n_chunks = q_ref.shape[0] // CHUNK
    ii = lax.broadcasted_iota(I32, (CHUNK, CHUNK), 0)
    jj = lax.broadcasted_iota(I32, (CHUNK, CHUNK), 1)
    dist = (ii - jj).astype(F32)
    dec_f = jnp.where(dist >= 0, jnp.exp(lgf * jnp.maximum(dist, 0.0)), 0.0)
    dec_b = jnp.where(dist < 0, jnp.exp(lgb * jnp.maximum(-dist, 0.0)), 0.0)
    t = lax.broadcasted_iota(I32, (CHUNK, 1), 0).astype(F32)
    kd_f = jnp.exp(lgf * (CHUNK - 1.0 - t))
    qd_f = jnp.exp(lgf * (t + 1.0))
    kd_b = jnp.exp(lgb * t)
    qd_b = jnp.exp(lgb * (CHUNK - t))
    zero_row = jnp.zeros((1, HEAD_DIM), F32)
    cd_f = jnp.exp(zero_row + lgf * CHUNK)
    cd_b = jnp.exp(zero_row + lgb * CHUNK)

    def one_chunk(n, dec, kd, qd, cd, state):
        r0 = pl.multiple_of(n * CHUNK, CHUNK)
        q = q_ref[pl.ds(r0, CHUNK), :]
        k = k_ref[pl.ds(r0, CHUNK), :]
        v = v_ref[pl.ds(r0, CHUNK), :]
        s = lax.dot_general(q, k, (((1,), (1,)), ((), ())), preferred_element_type=F32) * dec
        intra = jnp.dot(s.astype(BF16), v, preferred_element_type=F32)
        q_dec = (q.astype(F32) * qd).astype(BF16)
        inter = jnp.dot(q_dec, state.astype(BF16), preferred_element_type=F32)
        k_dec_t = (k.astype(F32) * kd).T.astype(BF16)
        new_state = state * cd + jnp.dot(k_dec_t, v, preferred_element_type=F32)
        return r0, intra + inter, new_state

    def scan_body(n, carry):
        sf, sb = carry
        r0, of, sf = one_chunk(n, dec_f, kd_f, qd_f, cd_f, sf)
        of_ref[pl.ds(r0, CHUNK), :] = of
        r0, ob, sb = one_chunk(n_chunks - 1 - n, dec_b, kd_b, qd_b, cd_b, sb)
        ob_ref[pl.ds(r0, CHUNK), :] = ob
        return sf, sb

    zeros = jnp.zeros((HEAD_DIM, HEAD_DIM), F32)
    lax.fori_loop(0, n_chunks, scan_body, (zeros, zeros), unroll=RET_UNROLL)

    rows = TOKEN_TILE

    def norm_body(n, carry):
        r0 = pl.multiple_of(n * rows, rows)
        o = of_ref[pl.ds(r0, rows), :] + ob_ref[pl.ds(r0, rows), :]
        mu = jnp.mean(o, axis=-1, keepdims=True)
        var = jnp.mean(jnp.square(o - mu), axis=-1, keepdims=True)
        yn = (o - mu) * lax.rsqrt(var + EPS) * gain_ref[...]
        o_ref[pl.ds(r0, rows), :] = (sg_ref[pl.ds(r0, rows), :].astype(F32) * yn).astype(BF16)
        return carry

    lax.fori_loop(0, q_ref.shape[0] // rows, norm_body, 0)


def _retention(lg, rq, rk, rv, sg, gn_gain, batch, seq):
    t = rq.shape[0]
    blk = pl.BlockSpec((seq, HEAD_DIM), lambda b, h: (b, h))
    return pl.pallas_call(
        _retention_kernel,
        grid=(batch, RET_HEADS),
        in_specs=[
            pl.BlockSpec(memory_space=pltpu.SMEM),
            blk, blk, blk, blk,
            pl.BlockSpec((1, HEAD_DIM), lambda b, h: (0, h)),
        ],
        out_specs=blk,
        out_shape=jax.ShapeDtypeStruct((t, RET_DIM), BF16),
        scratch_shapes=[pltpu.VMEM((seq, HEAD_DIM), F32), pltpu.VMEM((seq, HEAD_DIM), F32)],
        compiler_params=_params(("parallel", "parallel"), V7X_VMEM_LIMIT_BYTES),
        name="retention",
    )(lg, rq, rk, rv, sg, gn_gain)


def _attention_kernel(q_ref, k_ref, v_ref, o_ref):
    tq = q_ref.shape[0]
    tk = ATT_KV_TILE
    n_kv = k_ref.shape[0] // tk
    qs = [q_ref[:, hh * HEAD_DIM:(hh + 1) * HEAD_DIM] for hh in range(ATT_GROUP)]

    def body(c, carry):
        r0 = pl.multiple_of(c * tk, tk)
        k = k_ref[pl.ds(r0, tk), :]
        v = v_ref[pl.ds(r0, tk), :]
        out = []
        for q, (m, l, acc) in zip(qs, carry):
            s = lax.dot_general(q, k, (((1,), (1,)), ((), ())), preferred_element_type=F32)
            m_new = jnp.maximum(m, jnp.max(s, axis=-1, keepdims=True))
            alpha = jnp.exp2(m - m_new)
            p = jnp.exp2(s - m_new)
            l = alpha * l + jnp.sum(p, axis=-1, keepdims=True)
            acc = alpha * acc + jnp.dot(p.astype(BF16), v, preferred_element_type=F32)
            out.append((m_new, l, acc))
        return tuple(out)

    init = (jnp.full((tq, 1), -jnp.inf, F32), jnp.zeros((tq, 1), F32), jnp.zeros((tq, HEAD_DIM), F32))
    final = lax.fori_loop(0, n_kv, body, (init,) * ATT_GROUP, unroll=True)
    for hh, (_, l, acc) in enumerate(final):
        o_ref[:, hh * HEAD_DIM:(hh + 1) * HEAD_DIM] = (acc / l).astype(BF16)


def _attention(aq, ak, av, batch, seq):
    t = aq.shape[0]
    tq = ATT_Q_TILE
    nq = seq // tq
    gw = ATT_GROUP * HEAD_DIM
    q_spec = pl.BlockSpec((tq, gw), lambda b, g, i: (b * nq + i, g))
    kv_spec = pl.BlockSpec((seq, HEAD_DIM), lambda b, g, i: (b, g))
    return pl.pallas_call(
        _attention_kernel,
        grid=(batch, ATT_KV_HEADS, nq),
        in_specs=[q_spec, kv_spec, kv_spec],
        out_specs=q_spec,
        out_shape=jax.ShapeDtypeStruct((t, ATT_DIM), BF16),
        compiler_params=_params(("parallel", "parallel", "parallel"), V7X_VMEM_LIMIT_BYTES),
        name="attention",
    )(aq, ak, av)


def _merge_kernel(x_ref, ur_ref, ya_ref, gr_ref, ga_ref, wr_ref, wa_ref, wo_ref, nf_ref, wrt_ref,
                  x1_ref, h2a_ref, aff_ref, *, n_exp):
    d = x_ref.shape[1]
    y_ret = jnp.dot(ur_ref[...], wr_ref[...], preferred_element_type=F32)
    y_att = jnp.dot(ya_ref[...], wa_ref[...], preferred_element_type=F32)
    mixed = gr_ref[...].astype(F32) * y_ret + ga_ref[...].astype(F32) * y_att
    x1 = x_ref[...] + jnp.dot(mixed.astype(BF16), wo_ref[...], preferred_element_type=F32)
    x1_ref[...] = x1
    h2 = x1 * lax.rsqrt(jnp.mean(x1 * x1, axis=-1, keepdims=True) + EPS) * nf_ref[...]
    h2a_ref[:, :d] = h2
    h_hi = h2.astype(BF16)
    h_lo = (h2 - h_hi.astype(F32)).astype(BF16)
    r_hi = jnp.dot(h_hi, wrt_ref[...], preferred_element_type=F32)
    r_lo = jnp.dot(h_lo, wrt_ref[:, :LANES], preferred_element_type=F32)
    logits = r_hi[:, :LANES] + r_hi[:, LANES:] + r_lo
    lane = lax.broadcasted_iota(I32, logits.shape, 1)
    logits = jnp.where(lane < n_exp, logits, -jnp.inf)
    e = jnp.exp(logits - jnp.max(logits, axis=-1, keepdims=True))
    aff = e / jnp.sum(e, axis=-1, keepdims=True)
    h2a_ref[:, d:] = aff
    aff_ref[...] = aff[:, :n_exp]


def _merge(xf, u_ret, y_att, gr, ga, w_ret_bf, w_att_bf, w_o_bf, norm_ffn, w_router_split, n_exp):
    t, d = xf.shape
    tm = TOKEN_TILE
    row = lambda i: (i, 0)
    const = lambda i: (0, 0)
    return pl.pallas_call(
        functools.partial(_merge_kernel, n_exp=n_exp),
        grid=(t // tm,),
        in_specs=[
            pl.BlockSpec((tm, d), row),
            pl.BlockSpec((tm, RET_DIM), row),
            pl.BlockSpec((tm, ATT_DIM), row),
            pl.BlockSpec((tm, d), row),
            pl.BlockSpec((tm, d), row),
            pl.BlockSpec((RET_DIM, d), const),
            pl.BlockSpec((ATT_DIM, d), const),
            pl.BlockSpec((d, d), const),
            pl.BlockSpec((1, d), const),
            pl.BlockSpec((d, 2 * LANES), const),
        ],
        out_specs=[pl.BlockSpec((tm, d), row), pl.BlockSpec((tm, d + LANES), row),
                   pl.BlockSpec((tm, n_exp), row)],
        out_shape=[jax.ShapeDtypeStruct((t, d), F32), jax.ShapeDtypeStruct((t, d + LANES), F32),
                   jax.ShapeDtypeStruct((t, n_exp), F32)],
        compiler_params=_params(("parallel",), V7X_VMEM_LIMIT_BYTES),
        name="merge",
    )(xf, u_ret, y_att, gr, ga, w_ret_bf, w_att_bf, w_o_bf, norm_ffn, w_router_split)


def _route_kernel(aff_ref, affc_ref, idx_ref, pos_ref, slo_ref, msk_ref, cum_ref, bnd_ref, bnd_smem, cnt_ref,
                  sem, *, cap, n_exp):
    b = pl.program_id(0)
    seq = aff_ref.shape[0]
    n_tiles = seq // ROUTE_TILE
    aff = aff_ref[...]
    affc = affc_ref[...]
    capf = float(cap)

    def per_expert(x, op, reduce):
        r = reduce(x.reshape(x.shape[0] // 8, 8, LANES), axis=0)
        shift = n_exp
        while shift < LANES:
            r = op(r, pltpu.roll(r, shift, axis=1))
            shift *= 2
        return reduce(r, axis=0, keepdims=True)

    def bisect(lo, hi):
        mid = 0.5 * (lo + hi)
        ok = per_expert(jnp.where(affc >= mid, 1.0, 0.0), jnp.add, jnp.sum) >= capf
        return jnp.where(ok, mid, lo), jnp.where(ok, hi, mid)

    def bracket(lo, hi):
        mn = per_expert(jnp.where(affc >= lo, affc, jnp.inf), jnp.minimum, jnp.min)
        mx = per_expert(jnp.where(affc < hi, affc, -jnp.inf), jnp.maximum, jnp.max)
        return mn, mx

    def not_isolated(lo, hi):
        mn, mx = bracket(lo, hi)
        return jnp.max(jnp.where(mn < mx, 1.0, 0.0)) > 0.0

    lo, hi = lax.fori_loop(0, BISECT_STEPS, lambda i, c: bisect(*c),
                           (jnp.zeros((1, LANES), F32), jnp.full((1, LANES), 2.0, F32)))

    def refine(c):
        lo, hi = bisect(c[0], c[1])
        return lo, hi, not_isolated(lo, hi)

    lo, hi, _ = lax.while_loop(lambda c: c[2], refine, (lo, hi, not_isolated(lo, hi)))
    thr, _ = bracket(lo, hi)
    gt = aff > thr
    eq = aff == thr
    n_gt = jnp.sum(jnp.where(gt, 1.0, 0.0).reshape(seq // 8, 8, LANES), axis=0)
    need = capf - jnp.sum(n_gt, axis=0, keepdims=True)

    tri = (lax.broadcasted_iota(I32, (ROUTE_TILE, ROUTE_TILE), 0)
           >= lax.broadcasted_iota(I32, (ROUTE_TILE, ROUTE_TILE), 1))
    tri = jnp.where(tri, 1.0, 0.0).astype(BF16)

    def cumsum_tokens(store_tile_start):
        def body(c, carry):
            r0 = pl.multiple_of(c * ROUTE_TILE, ROUTE_TILE)
            if store_tile_start:
                slo_ref[0, pl.ds(c, 1), :] = carry[:, :n_exp].astype(I32)
                bnd_ref[pl.ds(c, 1), :] = carry.astype(I32)
            cs = jnp.dot(tri, msk_ref[pl.ds(r0, ROUTE_TILE), :], preferred_element_type=F32) + carry
            cum_ref[pl.ds(r0, ROUTE_TILE), :] = cs
            return cs[ROUTE_TILE - 1:ROUTE_TILE, :]

        return lax.fori_loop(0, n_tiles, body, jnp.zeros((1, LANES), F32))

    msk_ref[...] = jnp.where(eq, 1.0, 0.0).astype(BF16)
    cumsum_tokens(False)
    take = jnp.logical_and(eq, cum_ref[...] - 1.0 < need)
    mask = jnp.logical_or(gt, take)
    msk_ref[...] = jnp.where(mask, 1.0, 0.0).astype(BF16)
    bnd_ref[...] = jnp.zeros(bnd_ref.shape, I32)
    bnd_ref[pl.ds(n_tiles, 1), :] = cumsum_tokens(True).astype(I32)
    pos_ref[...] = jnp.where(mask, cum_ref[...] - 1.0, -1.0)[:, :n_exp].astype(I32)

    to_smem = pltpu.make_async_copy(bnd_ref, bnd_smem, sem)
    to_smem.start()
    to_smem.wait()

    slot_row = lax.broadcasted_iota(I32, (1, cap), 1).astype(F32)
    lane_row = lax.broadcasted_iota(I32, (1, LANES), 1)
    for e in range(n_exp):
        ends = bnd_ref[pl.ds(1, n_tiles), e:e + 1].astype(F32)
        whole = jnp.sum(jnp.where(slot_row >= ends, float(ROUTE_TILE), 0.0), axis=0, keepdims=True)
        cnt_ref[...] = jnp.zeros(cnt_ref.shape, F32)

        def tile_body(c, carry, e=e):
            lo = bnd_smem[c, e]
            hi = bnd_smem[c + 1, e]
            r0 = pl.multiple_of(c * ROUTE_TILE, ROUTE_TILE)
            col = cum_ref[pl.ds(r0, ROUTE_TILE), e:e + 1]

            def lane_tile(k, carry2):
                slots = k * LANES + lane_row
                below_hi = jnp.where(slots < hi, 1.0, 0.0)
                hit = jnp.where(col <= slots.astype(F32), below_hi, 0.0)
                cnt_ref[k] += jnp.sum(hit.reshape(ROUTE_TILE // 8, 8, LANES), axis=0)
                return carry2

            lax.fori_loop(lo // LANES, (hi + LANES - 1) // LANES, lane_tile, 0)
            return carry

        lax.fori_loop(0, n_tiles, tile_body, 0)
        for k in range(cap // LANES):
            part = jnp.sum(cnt_ref[k], axis=0, keepdims=True) + whole[:, k * LANES:(k + 1) * LANES]
            idx_ref[0, e:e + 1, k * LANES:(k + 1) * LANES] = part.astype(I32) + b * seq


def _route(h2a, aff, batch, seq, cap):
    t, n_exp = aff.shape
    pack = LANES // n_exp
    aff_packed = aff.reshape(t // pack, LANES)
    aff_block = h2a.shape[1] // LANES - 1
    n_tiles = seq // ROUTE_TILE
    bnd_rows = -(-(n_tiles + 1) // 8) * 8
    return pl.pallas_call(
        functools.partial(_route_kernel, cap=cap, n_exp=n_exp),
        grid=(batch,),
        in_specs=[pl.BlockSpec((seq, LANES), lambda b: (b, aff_block)),
                  pl.BlockSpec((seq // pack, LANES), lambda b: (b, 0))],
        out_specs=[pl.BlockSpec((1, n_exp, cap), lambda b: (b, 0, 0)),
                   pl.BlockSpec((seq, n_exp), lambda b: (b, 0)),
                   pl.BlockSpec((1, n_tiles, n_exp), lambda b: (b, 0, 0))],
        out_shape=[jax.ShapeDtypeStruct((batch, n_exp, cap), I32),
                   jax.ShapeDtypeStruct((t, n_exp), I32),
                   jax.ShapeDtypeStruct((batch, n_tiles, n_exp), I32)],
        scratch_shapes=[pltpu.VMEM((seq, LANES), BF16), pltpu.VMEM((seq, LANES), F32),
                        pltpu.VMEM((bnd_rows, LANES), I32), pltpu.SMEM((bnd_rows, LANES), I32),
                        pltpu.VMEM((cap // LANES, 8, LANES), F32), pltpu.SemaphoreType.DMA(())],
        compiler_params=_params(("parallel",), V7X_VMEM_LIMIT_BYTES),
        name="route",
    )(h2a, aff_packed)


def _ffn_kernel(idx_ref, h2a_hbm, wg_ref, wu_ref, wd_ref, y_ref, stage_ref, xe_ref, gate_ref, acc_ref, sem,
                *, n_exp, n_ff, rows_per_step):
    e = pl.program_id(0)
    f = pl.program_id(1)
    mp = stage_ref.shape[0]
    m, d = xe_ref.shape
    batch = y_ref.shape[0]
    cap = y_ref.shape[2]

    def row_copy(expert, i):
        r = idx_ref[expert * mp + i]
        return pltpu.make_async_copy(h2a_hbm.at[pl.ds(r, 1)], stage_ref.at[pl.ds(i, 1)], sem)

    def wait_rows():
        pltpu.make_async_copy(h2a_hbm.at[pl.ds(0, mp)], stage_ref, sem).wait()

    @pl.when(f == 0)
    def _():
        @pl.when(e == 0)
        def _():
            def issue(i, carry):
                row_copy(0, i).start()
                return carry

            lax.fori_loop(0, mp, issue, 0)

        wait_rows()
        xe_ref[...] = stage_ref[:m, :d].astype(BF16)
        lane = lax.broadcasted_iota(I32, (m, LANES), 1)
        gate_ref[...] = jnp.sum(jnp.where(lane == e, stage_ref[:m, d:], 0.0), axis=1, keepdims=True)
        acc_ref[...] = jnp.zeros(acc_ref.shape, F32)

    nxt = jnp.minimum(e + 1, n_exp - 1)
    for k in range(rows_per_step):
        row_copy(nxt, f * rows_per_step + k).start()

    wg = wg_ref[0].astype(BF16)
    wu = wu_ref[0].astype(BF16)
    wd = wd_ref[0].astype(BF16)
    for rb in range(m // FFN_ROW_BLOCK):
        rows = pl.ds(rb * FFN_ROW_BLOCK, FFN_ROW_BLOCK)
        xb = xe_ref[rows, :]
        a = jnp.dot(xb, wg, preferred_element_type=F32)
        u = jnp.dot(xb, wu, preferred_element_type=F32)
        hm = (a * jax.nn.sigmoid(a) * u).astype(BF16)
        acc_ref[rows, :] += jnp.dot(hm, wd, preferred_element_type=F32)

    @pl.when(f == n_ff - 1)
    def _():
        for bb in range(batch):
            rows = pl.ds(bb * cap, cap)
            y_ref[bb, 0] = (acc_ref[rows, :] * gate_ref[rows, :]).astype(BF16)

        @pl.when(e == n_exp - 1)
        def _():
            wait_rows()


def _ffn(idx, h2a, w_gate, w_up, w_down):
    batch, n_exp, cap = idx.shape
    d = h2a.shape[1] - LANES
    ff = w_gate.shape[2]
    n_ff = ff // FF_TILE
    m = batch * cap
    rows_per_step = -(-pl.cdiv(m, n_ff) // 8) * 8
    mp = n_ff * rows_per_step
    idx_flat = jnp.pad(idx.transpose(1, 0, 2).reshape(n_exp, m), ((0, 0), (0, mp - m))).reshape(-1)
    grid_spec = pltpu.PrefetchScalarGridSpec(
        num_scalar_prefetch=1,
        grid=(n_exp, n_ff),
        in_specs=[
            pl.BlockSpec(memory_space=pl.ANY),
            pl.BlockSpec((1, d, FF_TILE), lambda e, f, idx: (e, 0, f)),
            pl.BlockSpec((1, d, FF_TILE), lambda e, f, idx: (e, 0, f)),
            pl.BlockSpec((1, FF_TILE, d), lambda e, f, idx: (e, f, 0)),
        ],
        out_specs=pl.BlockSpec((batch, 1, cap, d), lambda e, f, idx: (0, e, 0, 0)),
        scratch_shapes=[pltpu.VMEM((mp, d + LANES), F32), pltpu.VMEM((m, d), BF16),
                        pltpu.VMEM((m, 1), F32), pltpu.VMEM((m, d), F32),
                        pltpu.SemaphoreType.DMA(())],
    )
    return pl.pallas_call(
        functools.partial(_ffn_kernel, n_exp=n_exp, n_ff=n_ff, rows_per_step=rows_per_step),
        grid_spec=grid_spec,
        out_shape=jax.ShapeDtypeStruct((batch, n_exp, cap, d), BF16),
        compiler_params=_params(("arbitrary", "arbitrary"), V7X_VMEM_LIMIT_BYTES),
        name="ffn",
    )(idx_flat, h2a, w_gate, w_up, w_down)


def _window_fits(s_hi, start, rows):
    return s_hi - start <= rows


def _combine_kernel(slo_ref, x1_ref, pos_ref, y_hbm, nf_ref, o_ref, slab_ref, big_ref, acc_ref, sem, big_sem,
                    *, n_exp, cap):
    b = pl.program_id(0)
    j = pl.program_id(1)
    n_batch = pl.num_programs(0)
    n_tiles = pl.num_programs(1)
    step = b * n_tiles + j
    slot = step % 2

    def bounds(bb, jj, e):
        base = (bb * (n_tiles + 1) + jj) * n_exp + e
        return slo_ref[base], slo_ref[base + n_exp]

    def window_start(bb, jj, e, rows):
        s_lo, _ = bounds(bb, jj, e)
        return pl.multiple_of(jnp.minimum((s_lo // SLAB_ALIGN) * SLAB_ALIGN, cap - rows), SLAB_ALIGN)

    def all_small(bb, jj):
        ok = None
        for e in range(n_exp):
            fits = _window_fits(bounds(bb, jj, e)[1], window_start(bb, jj, e, SMALL_ROWS), SMALL_ROWS)
            ok = fits if ok is None else jnp.logical_and(ok, fits)
        return ok

    def small_copy(bb, jj, e, dst_slot):
        return pltpu.make_async_copy(
            y_hbm.at[bb, e, pl.ds(window_start(bb, jj, e, SMALL_ROWS), SMALL_ROWS)],
            slab_ref.at[dst_slot, pl.ds(e * SMALL_ROWS, SMALL_ROWS)], sem.at[dst_slot])

    small_now = all_small(b, j)

    @pl.when(step == 0)
    def _():
        big_ref[...] = jnp.zeros(big_ref.shape, big_ref.dtype)

        @pl.when(small_now)
        def _():
            for e in range(n_exp):
                small_copy(b, j, e, slot).start()

    wrap = j + 1 == n_tiles
    nb = jnp.minimum(jnp.where(wrap, b + 1, b), n_batch - 1)
    nj = jnp.where(wrap, 0, j + 1)

    @pl.when(jnp.logical_and(step + 1 < n_batch * n_tiles, all_small(nb, nj)))
    def _():
        for e in range(n_exp):
            small_copy(nb, nj, e, 1 - slot).start()

    @pl.when(small_now)
    def _():
        lane = lax.broadcasted_iota(I32, (1, SMALL_ROWS), 1)
        acc = None
        for e in range(n_exp):
            small_copy(b, j, e, slot).wait()
        for e0 in range(0, n_exp, COMBINE_GROUP):
            tiles = []
            for e in range(e0, min(e0 + COMBINE_GROUP, n_exp)):
                rel = pos_ref[:, e:e + 1] - window_start(b, j, e, SMALL_ROWS)
                tiles.append(jnp.where(rel == lane, 1.0, 0.0).astype(BF16))
            rows = pl.ds(e0 * SMALL_ROWS, len(tiles) * SMALL_ROWS)
            part = jnp.dot(jnp.concatenate(tiles, axis=1), slab_ref[slot, rows, :],
                           preferred_element_type=F32)
            acc = part if acc is None else acc + part
        acc_ref[...] = acc

    @pl.when(jnp.logical_not(small_now))
    def _():
        lane = lax.broadcasted_iota(I32, (1, BIG_PAD), 1)
        for e in range(n_exp):
            st = window_start(b, j, e, BIG_ROWS)
            cp = pltpu.make_async_copy(y_hbm.at[b, e, pl.ds(st, BIG_ROWS)],
                                       big_ref.at[pl.ds(0, BIG_ROWS)], big_sem)
            cp.start()
            cp.wait()
            onehot = jnp.where(pos_ref[:, e:e + 1] - st == lane, 1.0, 0.0).astype(BF16)
            part = jnp.dot(onehot, big_ref[...], preferred_element_type=F32)
            if e == 0:
                acc_ref[...] = part
            else:
                acc_ref[...] += part

    x2 = x1_ref[...] + acc_ref[...]
    o_ref[...] = x2 * lax.rsqrt(jnp.mean(x2 * x2, axis=-1, keepdims=True) + EPS) * nf_ref[...]


def _combine(slo, x1, pos, y_slots, norm_final, seq):
    t, d = x1.shape
    batch, n_exp, cap, _ = y_slots.shape
    n_tiles = seq // ROUTE_TILE
    slo_flat = jnp.concatenate([slo, jnp.full((batch, 1, n_exp), cap, I32)], axis=1).reshape(-1)
    row = lambda b, j, slo: (b * n_tiles + j, 0)
    grid_spec = pltpu.PrefetchScalarGridSpec(
        num_scalar_prefetch=1,
        grid=(batch, n_tiles),
        in_specs=[
            pl.BlockSpec((ROUTE_TILE, d), row),
            pl.BlockSpec((ROUTE_TILE, n_exp), row),
            pl.BlockSpec(memory_space=pl.ANY),
            pl.BlockSpec((1, d), lambda b, j, slo: (0, 0)),
        ],
        out_specs=pl.BlockSpec((ROUTE_TILE, d), row),
        scratch_shapes=[pltpu.VMEM((2, n_exp * SMALL_ROWS, d), BF16), pltpu.VMEM((BIG_PAD, d), BF16),
                        pltpu.VMEM((ROUTE_TILE, d), F32),
                        pltpu.SemaphoreType.DMA((2,)), pltpu.SemaphoreType.DMA(())],
    )
    return pl.pallas_call(
        functools.partial(_combine_kernel, n_exp=n_exp, cap=cap),
        grid_spec=grid_spec,
        out_shape=jax.ShapeDtypeStruct((t, d), F32),
        compiler_params=_params(("arbitrary", "arbitrary"), V7X_VMEM_LIMIT_BYTES),
        name="combine",
    )(slo_flat, x1, pos, y_slots, norm_final)


def _rope_tables(seq):
    rows = seq // GRID_W
    row = np.repeat(np.arange(rows, dtype=np.float64), GRID_W)
    col = np.tile(np.arange(GRID_W, dtype=np.float64), rows)
    n_freq = HALF_ROT // 2
    freqs = ROPE_THETA ** (-np.arange(n_freq, dtype=np.float64) / n_freq)
    ang = np.concatenate([row[:, None] * freqs, col[:, None] * freqs], axis=-1)
    cos, sin = np.cos(ang), np.sin(ang)
    return (jnp.asarray(np.concatenate([cos, cos], axis=-1), F32),
            jnp.asarray(np.concatenate([-sin, sin], axis=-1), F32))


def kernel(x, norm_mix, w_in, ret_decay_fwd, ret_decay_bwd, ret_gn_gain, w_ret_branch, q_norm, k_norm,
           w_att_branch, w_o, norm_ffn, w_router, w_expert_gate, w_expert_up, w_expert_down, norm_final):
    batch, seq, d = x.shape
    depth = norm_mix.shape[0]
    n_exp = w_router.shape[2]
    cap = CAPACITY_FACTOR * seq // n_exp
    assert seq % TOKEN_TILE == 0 and seq % ROUTE_TILE == 0 and seq % ATT_KV_TILE == 0
    assert LANES % n_exp == 0 and d % LANES == 0 and seq % (8 * LANES // n_exp) == 0
    assert cap >= BIG_ROWS and (cap - BIG_ROWS) % SLAB_ALIGN == 0 and (cap - SMALL_ROWS) % SLAB_ALIGN == 0
    assert w_expert_gate.shape[3] % FF_TILE == 0 and (batch * cap) % FFN_ROW_BLOCK == 0
    assert depth == 1, "the final RMSNorm is fused into the combine kernel of the single layer"
    l = 0
    cos2, sin2 = _rope_tables(seq)
    xf = x.reshape(batch * seq, d)
    lg = jnp.stack([jnp.log1p(-jnp.exp(ret_decay_fwd[l].astype(F32))),
                    jnp.log1p(-jnp.exp(ret_decay_bwd[l].astype(F32)))])
    rq, rk, rv, sg, aq, ak, av, gr, ga = _inproj(
        xf, norm_mix[l][None], w_in[l].astype(BF16), cos2, sin2, q_norm[l][None], k_norm[l][None], seq)
    u_ret = _retention(lg, rq, rk, rv, sg, ret_gn_gain[l][None], batch, seq)
    y_att = _attention(aq, ak, av, batch, seq)
    w_r = jnp.pad(w_router[l].astype(F32), ((0, 0), (0, LANES - n_exp)))
    w_r_hi = w_r.astype(BF16)
    w_r_lo = (w_r - w_r_hi.astype(F32)).astype(BF16)
    x1, h2a, aff = _merge(xf, u_ret, y_att, gr, ga, w_ret_branch[l].astype(BF16),
                          w_att_branch[l].astype(BF16), w_o[l].astype(BF16), norm_ffn[l][None],
                          jnp.concatenate([w_r_hi, w_r_lo], axis=1), n_exp)
    idx, pos, slo = _route(h2a, aff, batch, seq, cap)
    y_slots = _ffn(idx, h2a, w_expert_gate[l], w_expert_up[l], w_expert_down[l])
    out = _combine(slo, x1, pos, y_slots, norm_final[None], seq)
    return out.reshape(batch, seq, d)
```

```python
import functools

import jax
import jax.numpy as jnp
import numpy as np
from jax import lax
from jax.experimental import pallas as pl
from jax.experimental.pallas import tpu as pltpu

F32 = jnp.float32
BF16 = jnp.bfloat16
I32 = jnp.int32

GRID_W = 64
HEAD_DIM = 128
HALF_ROT = HEAD_DIM // 2
RET_HEADS = 4
RET_DIM = RET_HEADS * HEAD_DIM
ATT_HEADS = 4
ATT_KV_HEADS = 2
ATT_GROUP = ATT_HEADS // ATT_KV_HEADS
ATT_DIM = ATT_HEADS * HEAD_DIM
ATT_KV_DIM = ATT_KV_HEADS * HEAD_DIM
CHUNK = 256
ROPE_THETA = 10000.0
CAPACITY_FACTOR = 2
EPS = 1e-6
LOG2_E = 1.4426950408889634

V7X_VMEM_LIMIT_BYTES = 56 * 1024 * 1024
TOKEN_TILE = 512
ATT_Q_TILE = 512
ATT_KV_TILE = 2048
RET_UNROLL = 4
LANES = 128
ROUTE_TILE = 512
SLAB_ALIGN = 16
SMALL_ROWS = LANES
BIG_ROWS = ROUTE_TILE + SLAB_ALIGN
BIG_PAD = 640
COMBINE_GROUP = 4
FF_TILE = 256
FFN_ROW_BLOCK = 512
BISECT_STEPS = 24


def _params(sem, vmem=None):
    return pltpu.CompilerParams(dimension_semantics=sem, vmem_limit_bytes=vmem)


def _inproj_kernel(x_ref, g_ref, w_ref, cos_ref, sin_ref, qn_ref, kn_ref,
                   rq_ref, rk_ref, rv_ref, sg_ref, aq_ref, ak_ref, av_ref, gr_ref, ga_ref):
    x = x_ref[...]
    h = (x * lax.rsqrt(jnp.mean(x * x, axis=-1, keepdims=True) + EPS) * g_ref[...]).astype(BF16)
    cos = cos_ref[...]
    sin = sin_ref[...]
    d_model = x.shape[1]
    scale = HEAD_DIM ** -0.5

    def proj(lo, width):
        return jnp.dot(h, w_ref[:, lo:lo + width], preferred_element_type=F32)

    def rope(t):
        return t * cos + pltpu.roll(t, HALF_ROT, axis=1) * sin

    def head_norm(t, gain):
        return t * lax.rsqrt(jnp.mean(t * t, axis=-1, keepdims=True) + EPS) * gain

    def head(p, i):
        return p[:, i * HEAD_DIM:(i + 1) * HEAD_DIM]

    def put(ref, i, val):
        ref[:, i * HEAD_DIM:(i + 1) * HEAD_DIM] = val.astype(ref.dtype)

    off = 0
    p = proj(off, RET_DIM)
    for i in range(RET_HEADS):
        put(rq_ref, i, rope(head(p, i)))
    off += RET_DIM
    p = proj(off, RET_DIM)
    for i in range(RET_HEADS):
        put(rk_ref, i, rope(head(p, i)) * scale)
    off += RET_DIM
    rv_ref[...] = proj(off, RET_DIM).astype(BF16)
    off += RET_DIM
    p = proj(off, RET_DIM)
    sg_ref[...] = (p * jax.nn.sigmoid(p)).astype(BF16)
    off += RET_DIM
    p = proj(off, ATT_DIM)
    for i in range(ATT_HEADS):
        put(aq_ref, i, rope(head_norm(head(p, i), qn_ref[...])) * (scale * LOG2_E))
    off += ATT_DIM
    p = proj(off, ATT_KV_DIM)
    for i in range(ATT_KV_HEADS):
        put(ak_ref, i, rope(head_norm(head(p, i), kn_ref[...])))
    off += ATT_KV_DIM
    av_ref[...] = proj(off, ATT_KV_DIM).astype(BF16)
    off += ATT_KV_DIM
    gr_ref[...] = jax.nn.sigmoid(proj(off, d_model)).astype(BF16)
    off += d_model
    ga_ref[...] = jax.nn.sigmoid(proj(off, d_model)).astype(BF16)


def _inproj(xf, norm_g, w_in_bf, cos2, sin2, q_norm, k_norm, seq):
    t, d = xf.shape
    tm = TOKEN_TILE
    in_width = w_in_bf.shape[1]
    steps_per_seq = seq // tm
    row = lambda i: (i, 0)
    const = lambda i: (0, 0)
    pos = lambda i: (i % steps_per_seq, 0)
    widths = (RET_DIM, RET_DIM, RET_DIM, RET_DIM, ATT_DIM, ATT_KV_DIM, ATT_KV_DIM, d, d)
    return pl.pallas_call(
        _inproj_kernel,
        grid=(t // tm,),
        in_specs=[
            pl.BlockSpec((tm, d), row),
            pl.BlockSpec((1, d), const),
            pl.BlockSpec((d, in_width), const),
            pl.BlockSpec((tm, HEAD_DIM), pos),
            pl.BlockSpec((tm, HEAD_DIM), pos),
            pl.BlockSpec((1, HEAD_DIM), const),
            pl.BlockSpec((1, HEAD_DIM), const),
        ],
        out_specs=[pl.BlockSpec((tm, w), row) for w in widths],
        out_shape=[jax.ShapeDtypeStruct((t, w), BF16) for w in widths],
        compiler_params=_params(("parallel",), V7X_VMEM_LIMIT_BYTES),
        name="inproj",
    )(xf, norm_g, w_in_bf, cos2, sin2, q_norm, k_norm)


def _retention_kernel(lg_ref, q_ref, k_ref, v_ref, sg_ref, gain_ref, o_ref, of_ref, ob_ref):
    hd = pl.program_id(1)
    lgf = lg_ref[0, hd]
    lgb = lg_ref[1, hd]
    n_chunks = q_ref.shape[0] // CHUNK
    ii = lax.broadcasted_iota(I32, (CHUNK, CHUNK), 0)
    jj = lax.broadcasted_iota(I32, (CHUNK, CHUNK), 1)
    dist = (ii - jj).astype(F32)
    dec_f = jnp.where(dist >= 0, jnp.exp(lgf * jnp.maximum(dist, 0.0)), 0.0)
    dec_b = jnp.where(dist < 0, jnp.exp(lgb * jnp.maximum(-dist, 0.0)), 0.0)
    t = lax.broadcasted_iota(I32, (CHUNK, 1), 0).astype(F32)
    kd_f = jnp.exp(lgf * (CHUNK - 1.0 - t))
    qd_f = jnp.exp(lgf * (t + 1.0))
    kd_b = jnp.exp(lgb * t)
    qd_b = jnp.exp(lgb * (CHUNK - t))
    zero_row = jnp.zeros((1, HEAD_DIM), F32)
    cd_f = jnp.exp(zero_row + lgf * CHUNK)
    cd_b = jnp.exp(zero_row + lgb * CHUNK)

    def one_chunk(n, dec, kd, qd, cd, state):
        r0 = pl.multiple_of(n * CHUNK, CHUNK)
        q = q_ref[pl.ds(r0, CHUNK), :]
        k = k_ref[pl.ds(r0, CHUNK), :]
        v = v_ref[pl.ds(r0, CHUNK), :]
        s = lax.dot_general(q, k, (((1,), (1,)), ((), ())), preferred_element_type=F32) * dec
        intra = jnp.dot(s.astype(BF16), v, preferred_element_type=F32)
        q_dec = (q.astype(F32) * qd).astype(BF16)
        inter = jnp.dot(q_dec, state.astype(BF16), preferred_element_type=F32)
        k_dec_t = (k.astype(F32) * kd).T.astype(BF16)
        new_state = state * cd + jnp.dot(k_dec_t, v, preferred_element_type=F32)
        return r0, intra + inter, new_state

    def scan_body(n, carry):
        sf, sb = carry
        r0, of, sf = one_chunk(n, dec_f, kd_f, qd_f, cd_f, sf)
        of_ref[pl.ds(r0, CHUNK), :] = of
        r0, ob, sb = one_chunk(n_chunks - 1 - n, dec_b, kd_b, qd_b, cd_b, sb)
        ob_ref[pl.ds(r0, CHUNK), :] = ob
        return sf, sb

    zeros = jnp.zeros((HEAD_DIM, HEAD_DIM), F32)
    lax.fori_loop(0, n_chunks, scan_body, (zeros, zeros), unroll=RET_UNROLL)

    rows = TOKEN_TILE

    def norm_body(n, carry):
        r0 = pl.multiple_of(n * rows, rows)
        o = of_ref[pl.ds(r0, rows), :] + ob_ref[pl.ds(r0, rows), :]
        mu = jnp.mean(o, axis=-1, keepdims=True)
        var = jnp.mean(jnp.square(o - mu), axis=-1, keepdims=True)
        yn = (o - mu) * lax.rsqrt(var + EPS) * gain_ref[...]
        o_ref[pl.ds(r0, rows), :] = (sg_ref[pl.ds(r0, rows), :].astype(F32) * yn).astype(BF16)
        return carry

    lax.fori_loop(0, q_ref.shape[0] // rows, norm_body, 0)


def _retention(lg, rq, rk, rv, sg, gn_gain, batch, seq):
    t = rq.shape[0]
    blk = pl.BlockSpec((seq, HEAD_DIM), lambda b, h: (b, h))
    return pl.pallas_call(
        _retention_kernel,
        grid=(batch, RET_HEADS),
        in_specs=[
            pl.BlockSpec(memory_space=pltpu.SMEM),
            blk, blk, blk, blk,
            pl.BlockSpec((1, HEAD_DIM), lambda b, h: (0, h)),
        ],
        out_specs=blk,
        out_shape=jax.ShapeDtypeStruct((t, RET_DIM), BF16),
        scratch_shapes=[pltpu.VMEM((seq, HEAD_DIM), F32), pltpu.VMEM((seq, HEAD_DIM), F32)],
        compiler_params=_params(("parallel", "parallel"), V7X_VMEM_LIMIT_BYTES),
        name="retention",
    )(lg, rq, rk, rv, sg, gn_gain)


def _attention_kernel(q_ref, k_ref, v_ref, o_ref):
    tq = q_ref.shape[0]
    tk = ATT_KV_TILE
    n_kv = k_ref.shape[0] // tk
    qs = [q_ref[:, hh * HEAD_DIM:(hh + 1) * HEAD_DIM] for hh in range(ATT_GROUP)]

    def body(c, carry):
        r0 = pl.multiple_of(c * tk, tk)
        k = k_ref[pl.ds(r0, tk), :]
        v = v_ref[pl.ds(r0, tk), :]
        out = []
        for q, (m, l, acc) in zip(qs, carry):
            s = lax.dot_general(q, k, (((1,), (1,)), ((), ())), preferred_element_type=F32)
            m_new = jnp.maximum(m, jnp.max(s, axis=-1, keepdims=True))
            alpha = jnp.exp2(m - m_new)
            p = jnp.exp2(s - m_new)
            l = alpha * l + jnp.sum(p, axis=-1, keepdims=True)
            acc = alpha * acc + jnp.dot(p.astype(BF16), v, preferred_element_type=F32)
            out.append((m_new, l, acc))
        return tuple(out)

    init = (jnp.full((tq, 1), -jnp.inf, F32), jnp.zeros((tq, 1), F32), jnp.zeros((tq, HEAD_DIM), F32))
    final = lax.fori_loop(0, n_kv, body, (init,) * ATT_GROUP, unroll=True)
    for hh, (_, l, acc) in enumerate(final):
        o_ref[:, hh * HEAD_DIM:(hh + 1) * HEAD_DIM] = (acc / l).astype(BF16)


def _attention(aq, ak, av, batch, seq):
    t = aq.shape[0]
    tq = ATT_Q_TILE
    nq = seq // tq
    gw = ATT_GROUP * HEAD_DIM
    q_spec = pl.BlockSpec((tq, gw), lambda b, g, i: (b * nq + i, g))
    kv_spec = pl.BlockSpec((seq, HEAD_DIM), lambda b, g, i: (b, g))
    return pl.pallas_call(
        _attention_kernel,
        grid=(batch, ATT_KV_HEADS, nq),
        in_specs=[q_spec, kv_spec, kv_spec],
        out_specs=q_spec,
        out_shape=jax.ShapeDtypeStruct((t, ATT_DIM), BF16),
        compiler_params=_params(("parallel", "parallel", "parallel"), V7X_VMEM_LIMIT_BYTES),
        name="attention",
    )(aq, ak, av)


def _merge_kernel(x_ref, ur_ref, ya_ref, gr_ref, ga_ref, wr_ref, wa_ref, wo_ref, nf_ref, wrt_ref,
                  x1_ref, h2a_ref, aff_ref, *, n_exp):
    d = x_ref.shape[1]
    y_ret = jnp.dot(ur_ref[...], wr_ref[...], preferred_element_type=F32)
    y_att = jnp.dot(ya_ref[...], wa_ref[...], preferred_element_type=F32)
    mixed = gr_ref[...].astype(F32) * y_ret + ga_ref[...].astype(F32) * y_att
    x1 = x_ref[...] + jnp.dot(mixed.astype(BF16), wo_ref[...], preferred_element_type=F32)
    x1_ref[...] = x1
    h2 = x1 * lax.rsqrt(jnp.mean(x1 * x1, axis=-1, keepdims=True) + EPS) * nf_ref[...]
    h2a_ref[:, :d] = h2
    h_hi = h2.astype(BF16)
    h_lo = (h2 - h_hi.astype(F32)).astype(BF16)
    r_hi = jnp.dot(h_hi, wrt_ref[...], preferred_element_type=F32)
    r_lo = jnp.dot(h_lo, wrt_ref[:, :LANES], preferred_element_type=F32)
    logits = r_hi[:, :LANES] + r_hi[:, LANES:] + r_lo
    lane = lax.broadcasted_iota(I32, logits.shape, 1)
    logits = jnp.where(lane < n_exp, logits, -jnp.inf)
    e = jnp.exp(logits - jnp.max(logits, axis=-1, keepdims=True))
    aff = e / jnp.sum(e, axis=-1, keepdims=True)
    h2a_ref[:, d:] = aff
    aff_ref[...] = aff[:, :n_exp]


def _merge(xf, u_ret, y_att, gr, ga, w_ret_bf, w_att_bf, w_o_bf, norm_ffn, w_router_split, n_exp):
    t, d = xf.shape
    tm = TOKEN_TILE
    row = lambda i: (i, 0)
    const = lambda i: (0, 0)
    return pl.pallas_call(
        functools.partial(_merge_kernel, n_exp=n_exp),
        grid=(t // tm,),
        in_specs=[
            pl.BlockSpec((tm, d), row),
            pl.BlockSpec((tm, RET_DIM), row),
            pl.BlockSpec((tm, ATT_DIM), row),
            pl.BlockSpec((tm, d), row),
            pl.BlockSpec((tm, d), row),
            pl.BlockSpec((RET_DIM, d), const),
            pl.BlockSpec((ATT_DIM, d), const),
            pl.BlockSpec((d, d), const),
            pl.BlockSpec((1, d), const),
            pl.BlockSpec((d, 2 * LANES), const),
        ],
        out_specs=[pl.BlockSpec((tm, d), row), pl.BlockSpec((tm, d + LANES), row),
                   pl.BlockSpec((tm, n_exp), row)],
        out_shape=[jax.ShapeDtypeStruct((t, d), F32), jax.ShapeDtypeStruct((t, d + LANES), F32),
                   jax.ShapeDtypeStruct((t, n_exp), F32)],
        compiler_params=_params(("parallel",), V7X_VMEM_LIMIT_BYTES),
        name="merge",
    )(xf, u_ret, y_att, gr, ga, w_ret_bf, w_att_bf, w_o_bf, norm_ffn, w_router_split)


def _route_kernel(aff_ref, affc_ref, idx_ref, pos_ref, slo_ref, msk_ref, cum_ref, cum_t_ref, bnd_ref, bnd_smem,
                  cnt_ref, sem, *, cap, n_exp):
    b = pl.program_id(0)
    seq = aff_ref.shape[0]
    n_tiles = seq // ROUTE_TILE
    aff = aff_ref[...]
    affc = affc_ref[...]
    capf = float(cap)

    def per_expert(x, op, reduce):
        r = reduce(x.reshape(x.shape[0] // 8, 8, LANES), axis=0)
        shift = n_exp
        while shift < LANES:
            r = op(r, pltpu.roll(r, shift, axis=1))
            shift *= 2
        return reduce(r, axis=0, keepdims=True)

    def bisect(lo, hi):
        mid = 0.5 * (lo + hi)
        ok = per_expert(jnp.where(affc >= mid, 1.0, 0.0), jnp.add, jnp.sum) >= capf
        return jnp.where(ok, mid, lo), jnp.where(ok, hi, mid)

    def bracket(lo, hi):
        mn = per_expert(jnp.where(affc >= lo, affc, jnp.inf), jnp.minimum, jnp.min)
        mx = per_expert(jnp.where(affc < hi, affc, -jnp.inf), jnp.maximum, jnp.max)
        return mn, mx

    def not_isolated(lo, hi):
        mn, mx = bracket(lo, hi)
        return jnp.max(jnp.where(mn < mx, 1.0, 0.0)) > 0.0

    lo, hi = lax.fori_loop(0, BISECT_STEPS, lambda i, c: bisect(*c),
                           (jnp.zeros((1, LANES), F32), jnp.full((1, LANES), 2.0, F32)))

    def refine(c):
        lo, hi = bisect(c[0], c[1])
        return lo, hi, not_isolated(lo, hi)

    lo, hi, _ = lax.while_loop(lambda c: c[2], refine, (lo, hi, not_isolated(lo, hi)))
    thr, _ = bracket(lo, hi)
    gt = aff > thr
    eq = aff == thr
    n_gt = jnp.sum(jnp.where(gt, 1.0, 0.0).reshape(seq // 8, 8, LANES), axis=0)
    need = capf - jnp.sum(n_gt, axis=0, keepdims=True)

    tri = (lax.broadcasted_iota(I32, (ROUTE_TILE, ROUTE_TILE), 0)
           >= lax.broadcasted_iota(I32, (ROUTE_TILE, ROUTE_TILE), 1))
    tri = jnp.where(tri, 1.0, 0.0).astype(BF16)

    def cumsum_tokens(store_tile_start):
        def body(c, carry):
            r0 = pl.multiple_of(c * ROUTE_TILE, ROUTE_TILE)
            if store_tile_start:
                slo_ref[0, pl.ds(c, 1), :] = carry[:, :n_exp].astype(I32)
                bnd_ref[pl.ds(c, 1), :] = carry.astype(I32)
            cs = jnp.dot(tri, msk_ref[pl.ds(r0, ROUTE_TILE), :], preferred_element_type=F32) + carry
            cum_ref[pl.ds(r0, ROUTE_TILE), :] = cs
            return cs[ROUTE_TILE - 1:ROUTE_TILE, :]

        return lax.fori_loop(0, n_tiles, body, jnp.zeros((1, LANES), F32))

    msk_ref[...] = jnp.where(eq, 1.0, 0.0).astype(BF16)
    cumsum_tokens(False)
    take = jnp.logical_and(eq, cum_ref[...] - 1.0 < need)
    mask = jnp.logical_or(gt, take)
    msk_ref[...] = jnp.where(mask, 1.0, 0.0).astype(BF16)
    bnd_ref[...] = jnp.zeros(bnd_ref.shape, I32)
    bnd_ref[pl.ds(n_tiles, 1), :] = cumsum_tokens(True).astype(I32)
    pos_ref[...] = jnp.where(mask, cum_ref[...] - 1.0, -1.0)[:, :n_exp].astype(I32)

    to_smem = pltpu.make_async_copy(bnd_ref, bnd_smem, sem)
    to_smem.start()
    to_smem.wait()

    def transpose_tile(c, carry):
        r0 = pl.multiple_of(c * ROUTE_TILE, ROUTE_TILE)
        cum_t_ref[:, pl.ds(r0, ROUTE_TILE)] = cum_ref[pl.ds(r0, ROUTE_TILE), :].T
        return carry

    lax.fori_loop(0, n_tiles, transpose_tile, 0)
    slot_row = lax.broadcasted_iota(I32, (1, cap), 1).astype(F32)
    slot_sub = lax.broadcasted_iota(I32, (LANES, ROUTE_TILE), 0)
    eye = lax.broadcasted_iota(I32, (LANES, LANES), 0) == lax.broadcasted_iota(I32, (LANES, LANES), 1)
    for e in range(n_exp):
        ends = bnd_ref[pl.ds(1, n_tiles), e:e + 1].astype(F32)
        whole = jnp.sum(jnp.where(slot_row >= ends, float(ROUTE_TILE), 0.0), axis=0, keepdims=True)
        cnt_ref[...] = jnp.zeros(cnt_ref.shape, F32)

        def tile_body(c, carry, e=e):
            lo = bnd_smem[c, e]
            hi = bnd_smem[c + 1, e]
            r0 = pl.multiple_of(c * ROUTE_TILE, ROUTE_TILE)
            counts = cum_t_ref[e:e + 1, pl.ds(r0, ROUTE_TILE)]

            def lane_tile(k, carry2):
                slots = k * LANES + slot_sub
                below_hi = jnp.where(slots < hi, 1.0, 0.0)
                hit = jnp.where(counts <= slots.astype(F32), below_hi, 0.0)
                folded = hit[:, :LANES]
                for t0 in range(LANES, ROUTE_TILE, LANES):
                    folded = folded + hit[:, t0:t0 + LANES]
                cnt_ref[k] += folded
                return carry2

            lax.fori_loop(lo // LANES, (hi + LANES - 1) // LANES, lane_tile, 0)
            return carry

        lax.fori_loop(0, n_tiles, tile_body, 0)
        for k in range(cap // LANES):
            per_slot = jnp.sum(cnt_ref[k], axis=1, keepdims=True)
            part = jnp.sum(jnp.where(eye, per_slot, 0.0), axis=0, keepdims=True)
            part = part + whole[:, k * LANES:(k + 1) * LANES]
            idx_ref[0, e:e + 1, k * LANES:(k + 1) * LANES] = part.astype(I32) + b * seq


def _route(h2a, aff, batch, seq, cap):
    t, n_exp = aff.shape
    pack = LANES // n_exp
    aff_packed = aff.reshape(t // pack, LANES)
    aff_block = h2a.shape[1] // LANES - 1
    n_tiles = seq // ROUTE_TILE
    bnd_rows = -(-(n_tiles + 1) // 8) * 8
    return pl.pallas_call(
        functools.partial(_route_kernel, cap=cap, n_exp=n_exp),
        grid=(batch,),
        in_specs=[pl.BlockSpec((seq, LANES), lambda b: (b, aff_block)),
                  pl.BlockSpec((seq // pack, LANES), lambda b: (b, 0))],
        out_specs=[pl.BlockSpec((1, n_exp, cap), lambda b: (b, 0, 0)),
                   pl.BlockSpec((seq, n_exp), lambda b: (b, 0)),
                   pl.BlockSpec((1, n_tiles, n_exp), lambda b: (b, 0, 0))],
        out_shape=[jax.ShapeDtypeStruct((batch, n_exp, cap), I32),
                   jax.ShapeDtypeStruct((t, n_exp), I32),
                   jax.ShapeDtypeStruct((batch, n_tiles, n_exp), I32)],
        scratch_shapes=[pltpu.VMEM((seq, LANES), BF16), pltpu.VMEM((seq, LANES), F32),
                        pltpu.VMEM((LANES, seq), F32),
                        pltpu.VMEM((bnd_rows, LANES), I32), pltpu.SMEM((bnd_rows, LANES), I32),
                        pltpu.VMEM((cap // LANES, LANES, LANES), F32), pltpu.SemaphoreType.DMA(())],
        compiler_params=_params(("parallel",), V7X_VMEM_LIMIT_BYTES),
        name="route",
    )(h2a, aff_packed)


def _ffn_kernel(idx_ref, h2a_hbm, wg_ref, wu_ref, wd_ref, y_ref, stage_ref, xe_ref, gate_ref, acc_ref, sem,
                *, n_exp, n_ff, rows_per_step):
    e = pl.program_id(0)
    f = pl.program_id(1)
    mp = stage_ref.shape[0]
    m, d = xe_ref.shape
    batch = y_ref.shape[0]
    cap = y_ref.shape[2]

    def row_copy(expert, i):
        r = idx_ref[expert * mp + i]
        return pltpu.make_async_copy(h2a_hbm.at[pl.ds(r, 1)], stage_ref.at[pl.ds(i, 1)], sem)

    def wait_rows():
        pltpu.make_async_copy(h2a_hbm.at[pl.ds(0, mp)], stage_ref, sem).wait()

    @pl.when(f == 0)
    def _():
        @pl.when(e == 0)
        def _():
            def issue(i, carry):
                row_copy(0, i).start()
                return carry

            lax.fori_loop(0, mp, issue, 0)

        wait_rows()
        xe_ref[...] = stage_ref[:m, :d].astype(BF16)
        lane = lax.broadcasted_iota(I32, (m, LANES), 1)
        gate_ref[...] = jnp.sum(jnp.where(lane == e, stage_ref[:m, d:], 0.0), axis=1, keepdims=True)
        acc_ref[...] = jnp.zeros(acc_ref.shape, F32)

    nxt = jnp.minimum(e + 1, n_exp - 1)
    for k in range(rows_per_step):
        row_copy(nxt, f * rows_per_step + k).start()

    wg = wg_ref[0].astype(BF16)
    wu = wu_ref[0].astype(BF16)
    wd = wd_ref[0].astype(BF16)
    for rb in range(m // FFN_ROW_BLOCK):
        rows = pl.ds(rb * FFN_ROW_BLOCK, FFN_ROW_BLOCK)
        xb = xe_ref[rows, :]
        a = jnp.dot(xb, wg, preferred_element_type=F32)
        u = jnp.dot(xb, wu, preferred_element_type=F32)
        hm = (a * jax.nn.sigmoid(a) * u).astype(BF16)
        acc_ref[rows, :] += jnp.dot(hm, wd, preferred_element_type=F32)

    @pl.when(f == n_ff - 1)
    def _():
        for bb in range(batch):
            rows = pl.ds(bb * cap, cap)
            y_ref[bb, 0] = (acc_ref[rows, :] * gate_ref[rows, :]).astype(BF16)

        @pl.when(e == n_exp - 1)
        def _():
            wait_rows()


def _ffn(idx, h2a, w_gate, w_up, w_down):
    batch, n_exp, cap = idx.shape
    d = h2a.shape[1] - LANES
    ff = w_gate.shape[2]
    n_ff = ff // FF_TILE
    m = batch * cap
    rows_per_step = -(-pl.cdiv(m, n_ff) // 8) * 8
    mp = n_ff * rows_per_step
    idx_flat = jnp.pad(idx.transpose(1, 0, 2).reshape(n_exp, m), ((0, 0), (0, mp - m))).reshape(-1)
    grid_spec = pltpu.PrefetchScalarGridSpec(
        num_scalar_prefetch=1,
        grid=(n_exp, n_ff),
        in_specs=[
            pl.BlockSpec(memory_space=pl.ANY),
            pl.BlockSpec((1, d, FF_TILE), lambda e, f, idx: (e, 0, f)),
            pl.BlockSpec((1, d, FF_TILE), lambda e, f, idx: (e, 0, f)),
            pl.BlockSpec((1, FF_TILE, d), lambda e, f, idx: (e, f, 0)),
        ],
        out_specs=pl.BlockSpec((batch, 1, cap, d), lambda e, f, idx: (0, e, 0, 0)),
        scratch_shapes=[pltpu.VMEM((mp, d + LANES), F32), pltpu.VMEM((m, d), BF16),
                        pltpu.VMEM((m, 1), F32), pltpu.VMEM((m, d), F32),
                        pltpu.SemaphoreType.DMA(())],
    )
    return pl.pallas_call(
        functools.partial(_ffn_kernel, n_exp=n_exp, n_ff=n_ff, rows_per_step=rows_per_step),
        grid_spec=grid_spec,
        out_shape=jax.ShapeDtypeStruct((batch, n_exp, cap, d), BF16),
        compiler_params=_params(("arbitrary", "arbitrary"), V7X_VMEM_LIMIT_BYTES),
        name="ffn",
    )(idx_flat, h2a, w_gate, w_up, w_down)


def _window_fits(s_hi, start, rows):
    return s_hi - start <= rows


def _combine_kernel(slo_ref, x1_ref, pos_ref, y_hbm, nf_ref, o_ref, slab_ref, big_ref, acc_ref, sem, big_sem,
                    *, n_exp, cap):
    b = pl.program_id(0)
    j = pl.program_id(1)
    n_batch = pl.num_programs(0)
    n_tiles = pl.num_programs(1)
    step = b * n_tiles + j
    slot = step % 2

    def bounds(bb, jj, e):
        base = (bb * (n_tiles + 1) + jj) * n_exp + e
        return slo_ref[base], slo_ref[base + n_exp]

    def window_start(bb, jj, e, rows):
        s_lo, _ = bounds(bb, jj, e)
        return pl.multiple_of(jnp.minimum((s_lo // SLAB_ALIGN) * SLAB_ALIGN, cap - rows), SLAB_ALIGN)

    def all_small(bb, jj):
        ok = None
        for e in range(n_exp):
            fits = _window_fits(bounds(bb, jj, e)[1], window_start(bb, jj, e, SMALL_ROWS), SMALL_ROWS)
            ok = fits if ok is None else jnp.logical_and(ok, fits)
        return ok

    def small_copy(bb, jj, e, dst_slot):
        return pltpu.make_async_copy(
            y_hbm.at[bb, e, pl.ds(window_start(bb, jj, e, SMALL_ROWS), SMALL_ROWS)],
            slab_ref.at[dst_slot, pl.ds(e * SMALL_ROWS, SMALL_ROWS)], sem.at[dst_slot])

    small_now = all_small(b, j)

    @pl.when(step == 0)
    def _():
        big_ref[...] = jnp.zeros(big_ref.shape, big_ref.dtype)

        @pl.when(small_now)
        def _():
            for e in range(n_exp):
                small_copy(b, j, e, slot).start()

    wrap = j + 1 == n_tiles
    nb = jnp.minimum(jnp.where(wrap, b + 1, b), n_batch - 1)
    nj = jnp.where(wrap, 0, j + 1)

    @pl.when(jnp.logical_and(step + 1 < n_batch * n_tiles, all_small(nb, nj)))
    def _():
        for e in range(n_exp):
            small_copy(nb, nj, e, 1 - slot).start()

    @pl.when(small_now)
    def _():
        lane = lax.broadcasted_iota(I32, (1, SMALL_ROWS), 1)
        acc = None
        for e in range(n_exp):
            small_copy(b, j, e, slot).wait()
        for e0 in range(0, n_exp, COMBINE_GROUP):
            tiles = []
            for e in range(e0, min(e0 + COMBINE_GROUP, n_exp)):
                rel = pos_ref[:, e:e + 1] - window_start(b, j, e, SMALL_ROWS)
                tiles.append(jnp.where(rel == lane, 1.0, 0.0).astype(BF16))
            rows = pl.ds(e0 * SMALL_ROWS, len(tiles) * SMALL_ROWS)
            part = jnp.dot(jnp.concatenate(tiles, axis=1), slab_ref[slot, rows, :],
                           preferred_element_type=F32)
            acc = part if acc is None else acc + part
        acc_ref[...] = acc

    @pl.when(jnp.logical_not(small_now))
    def _():
        lane = lax.broadcasted_iota(I32, (1, BIG_PAD), 1)
        for e in range(n_exp):
            st = window_start(b, j, e, BIG_ROWS)
            cp = pltpu.make_async_copy(y_hbm.at[b, e, pl.ds(st, BIG_ROWS)],
                                       big_ref.at[pl.ds(0, BIG_ROWS)], big_sem)
            cp.start()
            cp.wait()
            onehot = jnp.where(pos_ref[:, e:e + 1] - st == lane, 1.0, 0.0).astype(BF16)
            part = jnp.dot(onehot, big_ref[...], preferred_element_type=F32)
            if e == 0:
                acc_ref[...] = part
            else:
                acc_ref[...] += part

    x2 = x1_ref[...] + acc_ref[...]
    o_ref[...] = x2 * lax.rsqrt(jnp.mean(x2 * x2, axis=-1, keepdims=True) + EPS) * nf_ref[...]


def _combine(slo, x1, pos, y_slots, norm_final, seq):
    t, d = x1.shape
    batch, n_exp, cap, _ = y_slots.shape
    n_tiles = seq // ROUTE_TILE
    slo_flat = jnp.concatenate([slo, jnp.full((batch, 1, n_exp), cap, I32)], axis=1).reshape(-1)
    row = lambda b, j, slo: (b * n_tiles + j, 0)
    grid_spec = pltpu.PrefetchScalarGridSpec(
        num_scalar_prefetch=1,
        grid=(batch, n_tiles),
        in_specs=[
            pl.BlockSpec((ROUTE_TILE, d), row),
            pl.BlockSpec((ROUTE_TILE, n_exp), row),
            pl.BlockSpec(memory_space=pl.ANY),
            pl.BlockSpec((1, d), lambda b, j, slo: (0, 0)),
        ],
        out_specs=pl.BlockSpec((ROUTE_TILE, d), row),
        scratch_shapes=[pltpu.VMEM((2, n_exp * SMALL_ROWS, d), BF16), pltpu.VMEM((BIG_PAD, d), BF16),
                        pltpu.VMEM((ROUTE_TILE, d), F32),
                        pltpu.SemaphoreType.DMA((2,)), pltpu.SemaphoreType.DMA(())],
    )
    return pl.pallas_call(
        functools.partial(_combine_kernel, n_exp=n_exp, cap=cap),
        grid_spec=grid_spec,
        out_shape=jax.ShapeDtypeStruct((t, d), F32),
        compiler_params=_params(("arbitrary", "arbitrary"), V7X_VMEM_LIMIT_BYTES),
        name="combine",
    )(slo_flat, x1, pos, y_slots, norm_final)


def _rope_tables(seq):
    rows = seq // GRID_W
    row = np.repeat(np.arange(rows, dtype=np.float64), GRID_W)
    col = np.tile(np.arange(GRID_W, dtype=np.float64), rows)
    n_freq = HALF_ROT // 2
    freqs = ROPE_THETA ** (-np.arange(n_freq, dtype=np.float64) / n_freq)
    ang = np.concatenate([row[:, None] * freqs, col[:, None] * freqs], axis=-1)
    cos, sin = np.cos(ang), np.sin(ang)
    return (jnp.asarray(np.concatenate([cos, cos], axis=-1), F32),
            jnp.asarray(np.concatenate([-sin, sin], axis=-1), F32))


def kernel(x, norm_mix, w_in, ret_decay_fwd, ret_decay_bwd, ret_gn_gain, w_ret_branch, q_norm, k_norm,
           w_att_branch, w_o, norm_ffn, w_router, w_expert_gate, w_expert_up, w_expert_down, norm_final):
    batch, seq, d = x.shape
    depth = norm_mix.shape[0]
    n_exp = w_router.shape[2]
    cap = CAPACITY_FACTOR * seq // n_exp
    assert seq % TOKEN_TILE == 0 and seq % ROUTE_TILE == 0 and seq % ATT_KV_TILE == 0
    assert LANES % n_exp == 0 and d % LANES == 0 and seq % (8 * LANES // n_exp) == 0
    assert cap >= BIG_ROWS and (cap - BIG_ROWS) % SLAB_ALIGN == 0 and (cap - SMALL_ROWS) % SLAB_ALIGN == 0
    assert w_expert_gate.shape[3] % FF_TILE == 0 and (batch * cap) % FFN_ROW_BLOCK == 0
    assert depth == 1, "the final RMSNorm is fused into the combine kernel of the single layer"
    l = 0
    cos2, sin2 = _rope_tables(seq)
    xf = x.reshape(batch * seq, d)
    lg = jnp.stack([jnp.log1p(-jnp.exp(ret_decay_fwd[l].astype(F32))),
                    jnp.log1p(-jnp.exp(ret_decay_bwd[l].astype(F32)))])
    rq, rk, rv, sg, aq, ak, av, gr, ga = _inproj(
        xf, norm_mix[l][None], w_in[l].astype(BF16), cos2, sin2, q_norm[l][None], k_norm[l][None], seq)
    u_ret = _retention(lg, rq, rk, rv, sg, ret_gn_gain[l][None], batch, seq)
    y_att = _attention(aq, ak, av, batch, seq)
    w_r = jnp.pad(w_router[l].astype(F32), ((0, 0), (0, LANES - n_exp)))
    w_r_hi = w_r.astype(BF16)
    w_r_lo = (w_r - w_r_hi.astype(F32)).astype(BF16)
    x1, h2a, aff = _merge(xf, u_ret, y_att, gr, ga, w_ret_branch[l].astype(BF16),
                          w_att_branch[l].astype(BF16), w_o[l].astype(BF16), norm_ffn[l][None],
                          jnp.concatenate([w_r_hi, w_r_lo], axis=1), n_exp)
    idx, pos, slo = _route(h2a, aff, batch, seq, cap)
    y_slots = _ffn(idx, h2a, w_expert_gate[l], w_expert_up[l], w_expert_down[l])
    out = _combine(slo, x1, pos, y_slots, norm_final[None], seq)
    return out.reshape(batch, seq, d)
```

```python
import functools

import jax
import jax.numpy as jnp
import numpy as np
from jax import lax
from jax.experimental import pallas as pl
from jax.experimental.pallas import tpu as pltpu

F32 = jnp.float32
BF16 = jnp.bfloat16
I32 = jnp.int32

GRID_W = 64
HEAD_DIM = 128
HALF_ROT = HEAD_DIM // 2
RET_HEADS = 4
RET_DIM = RET_HEADS * HEAD_DIM
ATT_HEADS = 4
ATT_KV_HEADS = 2
ATT_GROUP = ATT_HEADS // ATT_KV_HEADS
ATT_DIM = ATT_HEADS * HEAD_DIM
ATT_KV_DIM = ATT_KV_HEADS * HEAD_DIM
CHUNK = 256
ROPE_THETA = 10000.0
CAPACITY_FACTOR = 2
EPS = 1e-6
LOG2_E = 1.4426950408889634

V7X_VMEM_LIMIT_BYTES = 56 * 1024 * 1024
TOKEN_TILE = 512
ATT_Q_TILE = 512
ATT_KV_TILE = 2048
RET_UNROLL = 4
LANES = 128
ROUTE_TILE = 512
SLAB_ALIGN = 16
SMALL_ROWS = LANES
BIG_ROWS = ROUTE_TILE + SLAB_ALIGN
BIG_PAD = 640
COMBINE_GROUP = 4
FF_TILE = 256
FFN_ROW_BLOCK = 512
BISECT_STEPS = 24


def _params(sem, vmem=None):
    return pltpu.CompilerParams(dimension_semantics=sem, vmem_limit_bytes=vmem)


def _inproj_kernel(x_ref, g_ref, w_ref, cos_ref, sin_ref, qn_ref, kn_ref,
                   rq_ref, rk_ref, rv_ref, sg_ref, aq_ref, ak_ref, av_ref, gr_ref, ga_ref):
    x = x_ref[...]
    h = (x * lax.rsqrt(jnp.mean(x * x, axis=-1, keepdims=True) + EPS) * g_ref[...]).astype(BF16)
    cos = cos_ref[...]
    sin = sin_ref[...]
    d_model = x.shape[1]
    scale = HEAD_DIM ** -0.5

    def proj(lo, width):
        return jnp.dot(h, w_ref[:, lo:lo + width], preferred_element_type=F32)

    def rope(t):
        return t * cos + pltpu.roll(t, HALF_ROT, axis=1) * sin

    def head_norm(t, gain):
        return t * lax.rsqrt(jnp.mean(t * t, axis=-1, keepdims=True) + EPS) * gain

    def head(p, i):
        return p[:, i * HEAD_DIM:(i + 1) * HEAD_DIM]

    def put(ref, i, val):
        ref[:, i * HEAD_DIM:(i + 1) * HEAD_DIM] = val.astype(ref.dtype)

    off = 0
    p = proj(off, RET_DIM)
    for i in range(RET_HEADS):
        put(rq_ref, i, rope(head(p, i)))
    off += RET_DIM
    p = proj(off, RET_DIM)
    for i in range(RET_HEADS):
        put(rk_ref, i, rope(head(p, i)) * scale)
    off += RET_DIM
    rv_ref[...] = proj(off, RET_DIM).astype(BF16)
    off += RET_DIM
    p = proj(off, RET_DIM)
    sg_ref[...] = (p * jax.nn.sigmoid(p)).astype(BF16)
    off += RET_DIM
    p = proj(off, ATT_DIM)
    for i in range(ATT_HEADS):
        put(aq_ref, i, rope(head_norm(head(p, i), qn_ref[...])) * (scale * LOG2_E))
    off += ATT_DIM
    p = proj(off, ATT_KV_DIM)
    for i in range(ATT_KV_HEADS):
        put(ak_ref, i, rope(head_norm(head(p, i), kn_ref[...])))
    off += ATT_KV_DIM
    av_ref[...] = proj(off, ATT_KV_DIM).astype(BF16)
    off += ATT_KV_DIM
    gr_ref[...] = jax.nn.sigmoid(proj(off, d_model)).astype(BF16)
    off += d_model
    ga_ref[...] = jax.nn.sigmoid(proj(off, d_model)).astype(BF16)


def _inproj(xf, norm_g, w_in_bf, cos2, sin2, q_norm, k_norm, seq):
    t, d = xf.shape
    tm = TOKEN_TILE
    in_width = w_in_bf.shape[1]
    steps_per_seq = seq // tm
    row = lambda i: (i, 0)
    const = lambda i: (0, 0)
    pos = lambda i: (i % steps_per_seq, 0)
    widths = (RET_DIM, RET_DIM, RET_DIM, RET_DIM, ATT_DIM, ATT_KV_DIM, ATT_KV_DIM, d, d)
    return pl.pallas_call(
        _inproj_kernel,
        grid=(t // tm,),
        in_specs=[
            pl.BlockSpec((tm, d), row),
            pl.BlockSpec((1, d), const),
            pl.BlockSpec((d, in_width), const),
            pl.BlockSpec((tm, HEAD_DIM), pos),
            pl.BlockSpec((tm, HEAD_DIM), pos),
            pl.BlockSpec((1, HEAD_DIM), const),
            pl.BlockSpec((1, HEAD_DIM), const),
        ],
        out_specs=[pl.BlockSpec((tm, w), row) for w in widths],
        out_shape=[jax.ShapeDtypeStruct((t, w), BF16) for w in widths],
        compiler_params=_params(("parallel",), V7X_VMEM_LIMIT_BYTES),
        name="inproj",
    )(xf, norm_g, w_in_bf, cos2, sin2, q_norm, k_norm)


def _retention_kernel(lg_ref, q_ref, k_ref, v_ref, sg_ref, gain_ref, o_ref, of_ref, ob_ref):
    hd = pl.program_id(1)
    lgf = lg_ref[0, hd]
    lgb = lg_ref[1, hd]
    n_chunks = q_ref.shape[0] // CHUNK
    ii = lax.broadcasted_iota(I32, (CHUNK, CHUNK), 0)
    jj = lax.broadcasted_iota(I32, (CHUNK, CHUNK), 1)
    dist = (ii - jj).astype(F32)
    dec_f = jnp.where(dist >= 0, jnp.exp(lgf * jnp.maximum(dist, 0.0)), 0.0)
    dec_b = jnp.where(dist < 0, jnp.exp(lgb * jnp.maximum(-dist, 0.0)), 0.0)
    t = lax.broadcasted_iota(I32, (CHUNK, 1), 0).astype(F32)
    kd_f = jnp.exp(lgf * (CHUNK - 1.0 - t))
    qd_f = jnp.exp(lgf * (t + 1.0))
    kd_b = jnp.exp(lgb * t)
    qd_b = jnp.exp(lgb * (CHUNK - t))
    zero_row = jnp.zeros((1, HEAD_DIM), F32)
    cd_f = jnp.exp(zero_row + lgf * CHUNK)
    cd_b = jnp.exp(zero_row + lgb * CHUNK)

    def one_chunk(n, dec, kd, qd, cd, state):
        r0 = pl.multiple_of(n * CHUNK, CHUNK)
        q = q_ref[pl.ds(r0, CHUNK), :]
        k = k_ref[pl.ds(r0, CHUNK), :]
        v = v_ref[pl.ds(r0, CHUNK), :]
        s = lax.dot_general(q, k, (((1,), (1,)), ((), ())), preferred_element_type=F32) * dec
        intra = jnp.dot(s.astype(BF16), v, preferred_element_type=F32)
        q_dec = (q.astype(F32) * qd).astype(BF16)
        inter = jnp.dot(q_dec, state.astype(BF16), preferred_element_type=F32)
        k_dec_t = (k.astype(F32) * kd).T.astype(BF16)
        new_state = state * cd + jnp.dot(k_dec_t, v, preferred_element_type=F32)
        return r0, intra + inter, new_state

    def scan_body(n, carry):
        sf, sb = carry
        r0, of, sf = one_chunk(n, dec_f, kd_f, qd_f, cd_f, sf)
        of_ref[pl.ds(r0, CHUNK), :] = of
        r0, ob, sb = one_chunk(n_chunks - 1 - n, dec_b, kd_b, qd_b, cd_b, sb)
        ob_ref[pl.ds(r0, CHUNK), :] = ob
        return sf, sb

    zeros = jnp.zeros((HEAD_DIM, HEAD_DIM), F32)
    lax.fori_loop(0, n_chunks, scan_body, (zeros, zeros), unroll=RET_UNROLL)

    rows = TOKEN_TILE

    def norm_body(n, carry):
        r0 = pl.multiple_of(n * rows, rows)
        o = of_ref[pl.ds(r0, rows), :] + ob_ref[pl.ds(r0, rows), :]
        mu = jnp.mean(o, axis=-1, keepdims=True)
        var = jnp.mean(jnp.square(o - mu), axis=-1, keepdims=True)
        yn = (o - mu) * lax.rsqrt(var + EPS) * gain_ref[...]
        o_ref[pl.ds(r0, rows), :] = (sg_ref[pl.ds(r0, rows), :].astype(F32) * yn).astype(BF16)
        return carry

    lax.fori_loop(0, q_ref.shape[0] // rows, norm_body, 0)


def _retention(lg, rq, rk, rv, sg, gn_gain, batch, seq):
    t = rq.shape[0]
    blk = pl.BlockSpec((seq, HEAD_DIM), lambda b, h: (b, h))
    return pl.pallas_call(
        _retention_kernel,
        grid=(batch, RET_HEADS),
        in_specs=[
            pl.BlockSpec(memory_space=pltpu.SMEM),
            blk, blk, blk, blk,
            pl.BlockSpec((1, HEAD_DIM), lambda b, h: (0, h)),
        ],
        out_specs=blk,
        out_shape=jax.ShapeDtypeStruct((t, RET_DIM), BF16),
        scratch_shapes=[pltpu.VMEM((seq, HEAD_DIM), F32), pltpu.VMEM((seq, HEAD_DIM), F32)],
        compiler_params=_params(("parallel", "parallel"), V7X_VMEM_LIMIT_BYTES),
        name="retention",
    )(lg, rq, rk, rv, sg, gn_gain)


def _attention_kernel(q_ref, k_ref, v_ref, o_ref):
    tq = q_ref.shape[0]
    tk = ATT_KV_TILE
    n_kv = k_ref.shape[0] // tk
    qs = [q_ref[:, hh * HEAD_DIM:(hh + 1) * HEAD_DIM] for hh in range(ATT_GROUP)]

    def body(c, carry):
        r0 = pl.multiple_of(c * tk, tk)
        k = k_ref[pl.ds(r0, tk), :]
        v = v_ref[pl.ds(r0, tk), :]
        out = []
        for q, (m, l, acc) in zip(qs, carry):
            s = lax.dot_general(q, k, (((1,), (1,)), ((), ())), preferred_element_type=F32)
            m_new = jnp.maximum(m, jnp.max(s, axis=-1, keepdims=True))
            alpha = jnp.exp2(m - m_new)
            p = jnp.exp2(s - m_new)
            l = alpha * l + jnp.sum(p, axis=-1, keepdims=True)
            acc = alpha * acc + jnp.dot(p.astype(BF16), v, preferred_element_type=F32)
            out.append((m_new, l, acc))
        return tuple(out)

    init = (jnp.full((tq, 1), -jnp.inf, F32), jnp.zeros((tq, 1), F32), jnp.zeros((tq, HEAD_DIM), F32))
    final = lax.fori_loop(0, n_kv, body, (init,) * ATT_GROUP, unroll=True)
    for hh, (_, l, acc) in enumerate(final):
        o_ref[:, hh * HEAD_DIM:(hh + 1) * HEAD_DIM] = (acc / l).astype(BF16)


def _attention(aq, ak, av, batch, seq):
    t = aq.shape[0]
    tq = ATT_Q_TILE
    nq = seq // tq
    gw = ATT_GROUP * HEAD_DIM
    q_spec = pl.BlockSpec((tq, gw), lambda b, g, i: (b * nq + i, g))
    kv_spec = pl.BlockSpec((seq, HEAD_DIM), lambda b, g, i: (b, g))
    return pl.pallas_call(
        _attention_kernel,
        grid=(batch, ATT_KV_HEADS, nq),
        in_specs=[q_spec, kv_spec, kv_spec],
        out_specs=q_spec,
        out_shape=jax.ShapeDtypeStruct((t, ATT_DIM), BF16),
        compiler_params=_params(("parallel", "parallel", "parallel"), V7X_VMEM_LIMIT_BYTES),
        name="attention",
    )(aq, ak, av)


def _merge_kernel(x_ref, ur_ref, ya_ref, gr_ref, ga_ref, wr_ref, wa_ref, wo_ref, nf_ref, wrt_ref,
                  x1_ref, h2a_ref, aff_ref, *, n_exp):
    d = x_ref.shape[1]
    y_ret = jnp.dot(ur_ref[...], wr_ref[...], preferred_element_type=F32)
    y_att = jnp.dot(ya_ref[...], wa_ref[...], preferred_element_type=F32)
    mixed = gr_ref[...].astype(F32) * y_ret + ga_ref[...].astype(F32) * y_att
    x1 = x_ref[...] + jnp.dot(mixed.astype(BF16), wo_ref[...], preferred_element_type=F32)
    x1_ref[...] = x1
    h2 = x1 * lax.rsqrt(jnp.mean(x1 * x1, axis=-1, keepdims=True) + EPS) * nf_ref[...]
    h2a_ref[:, :d] = h2
    h_hi = h2.astype(BF16)
    h_lo = (h2 - h_hi.astype(F32)).astype(BF16)
    r_hi = jnp.dot(h_hi, wrt_ref[...], preferred_element_type=F32)
    r_lo = jnp.dot(h_lo, wrt_ref[:, :LANES], preferred_element_type=F32)
    logits = r_hi[:, :LANES] + r_hi[:, LANES:] + r_lo
    lane = lax.broadcasted_iota(I32, logits.shape, 1)
    logits = jnp.where(lane < n_exp, logits, -jnp.inf)
    e = jnp.exp(logits - jnp.max(logits, axis=-1, keepdims=True))
    aff = e / jnp.sum(e, axis=-1, keepdims=True)
    h2a_ref[:, d:] = aff
    aff_ref[...] = aff[:, :n_exp]


def _merge(xf, u_ret, y_att, gr, ga, w_ret_bf, w_att_bf, w_o_bf, norm_ffn, w_router_split, n_exp):
    t, d = xf.shape
    tm = TOKEN_TILE
    row = lambda i: (i, 0)
    const = lambda i: (0, 0)
    return pl.pallas_call(
        functools.partial(_merge_kernel, n_exp=n_exp),
        grid=(t // tm,),
        in_specs=[
            pl.BlockSpec((tm, d), row),
            pl.BlockSpec((tm, RET_DIM), row),
            pl.BlockSpec((tm, ATT_DIM), row),
            pl.BlockSpec((tm, d), row),
            pl.BlockSpec((tm, d), row),
            pl.BlockSpec((RET_DIM, d), const),
            pl.BlockSpec((ATT_DIM, d), const),
            pl.BlockSpec((d, d), const),
            pl.BlockSpec((1, d), const),
            pl.BlockSpec((d, 2 * LANES), const),
        ],
        out_specs=[pl.BlockSpec((tm, d), row), pl.BlockSpec((tm, d + LANES), row),
                   pl.BlockSpec((tm, n_exp), row)],
        out_shape=[jax.ShapeDtypeStruct((t, d), F32), jax.ShapeDtypeStruct((t, d + LANES), F32),
                   jax.ShapeDtypeStruct((t, n_exp), F32)],
        compiler_params=_params(("parallel",), V7X_VMEM_LIMIT_BYTES),
        name="merge",
    )(xf, u_ret, y_att, gr, ga, w_ret_bf, w_att_bf, w_o_bf, norm_ffn, w_router_split)


def _route_kernel(aff_ref, affc_ref, idx_ref, pos_ref, slo_ref, msk_ref, cum_ref, cum_t_ref, bnd_ref, bnd_smem,
                  cnt_ref, sem, *, cap, n_exp):
    b = pl.program_id(0)
    seq = aff_ref.shape[0]
    n_tiles = seq // ROUTE_TILE
    aff = aff_ref[...]
    affc = affc_ref[...]
    capf = float(cap)

    def per_expert(x, op, reduce):
        r = reduce(x.reshape(x.shape[0] // 8, 8, LANES), axis=0)
        shift = n_exp
        while shift < LANES:
            r = op(r, pltpu.roll(r, shift, axis=1))
            shift *= 2
        return reduce(r, axis=0, keepdims=True)

    def bisect(lo, hi):
        mid = 0.5 * (lo + hi)
        ok = per_expert(jnp.where(affc >= mid, 1.0, 0.0), jnp.add, jnp.sum) >= capf
        return jnp.where(ok, mid, lo), jnp.where(ok, hi, mid)

    def bracket(lo, hi):
        mn = per_expert(jnp.where(affc >= lo, affc, jnp.inf), jnp.minimum, jnp.min)
        mx = per_expert(jnp.where(affc < hi, affc, -jnp.inf), jnp.maximum, jnp.max)
        return mn, mx

    def not_isolated(lo, hi):
        mn, mx = bracket(lo, hi)
        return jnp.max(jnp.where(mn < mx, 1.0, 0.0)) > 0.0

    lo, hi = lax.fori_loop(0, BISECT_STEPS, lambda i, c: bisect(*c),
                           (jnp.zeros((1, LANES), F32), jnp.full((1, LANES), 2.0, F32)))

    def refine(c):
        lo, hi = bisect(c[0], c[1])
        return lo, hi, not_isolated(lo, hi)

    lo, hi, _ = lax.while_loop(lambda c: c[2], refine, (lo, hi, not_isolated(lo, hi)))
    thr, _ = bracket(lo, hi)
    gt = aff > thr
    eq = aff == thr
    n_gt = jnp.sum(jnp.where(gt, 1.0, 0.0).reshape(seq // 8, 8, LANES), axis=0)
    need = capf - jnp.sum(n_gt, axis=0, keepdims=True)

    tri = (lax.broadcasted_iota(I32, (ROUTE_TILE, ROUTE_TILE), 0)
           >= lax.broadcasted_iota(I32, (ROUTE_TILE, ROUTE_TILE), 1))
    tri = jnp.where(tri, 1.0, 0.0).astype(BF16)

    def cumsum_tokens(store_tile_start):
        def body(c, carry):
            r0 = pl.multiple_of(c * ROUTE_TILE, ROUTE_TILE)
            if store_tile_start:
                slo_ref[0, pl.ds(c, 1), :] = carry[:, :n_exp].astype(I32)
                bnd_ref[pl.ds(c, 1), :] = carry.astype(I32)
            cs = jnp.dot(tri, msk_ref[pl.ds(r0, ROUTE_TILE), :], preferred_element_type=F32) + carry
            cum_ref[pl.ds(r0, ROUTE_TILE), :] = cs
            return cs[ROUTE_TILE - 1:ROUTE_TILE, :]

        return lax.fori_loop(0, n_tiles, body, jnp.zeros((1, LANES), F32))

    msk_ref[...] = jnp.where(eq, 1.0, 0.0).astype(BF16)
    cumsum_tokens(False)
    take = jnp.logical_and(eq, cum_ref[...] - 1.0 < need)
    mask = jnp.logical_or(gt, take)
    msk_ref[...] = jnp.where(mask, 1.0, 0.0).astype(BF16)
    bnd_ref[...] = jnp.zeros(bnd_ref.shape, I32)
    bnd_ref[pl.ds(n_tiles, 1), :] = cumsum_tokens(True).astype(I32)
    pos_ref[...] = jnp.where(mask, cum_ref[...] - 1.0, -1.0)[:, :n_exp].astype(I32)

    to_smem = pltpu.make_async_copy(bnd_ref, bnd_smem, sem)
    to_smem.start()
    to_smem.wait()

    def transpose_tile(c, carry):
        r0 = pl.multiple_of(c * ROUTE_TILE, ROUTE_TILE)
        cum_t_ref[:, pl.ds(r0, ROUTE_TILE)] = cum_ref[pl.ds(r0, ROUTE_TILE), :].T
        return carry

    lax.fori_loop(0, n_tiles, transpose_tile, 0)
    slot_row = lax.broadcasted_iota(I32, (1, cap), 1).astype(F32)
    slot_sub = lax.broadcasted_iota(I32, (LANES, ROUTE_TILE), 0)
    eye = lax.broadcasted_iota(I32, (LANES, LANES), 0) == lax.broadcasted_iota(I32, (LANES, LANES), 1)
    for e in range(n_exp):
        ends = bnd_ref[pl.ds(1, n_tiles), e:e + 1].astype(F32)
        whole = jnp.sum(jnp.where(slot_row >= ends, float(ROUTE_TILE), 0.0), axis=0, keepdims=True)
        cnt_ref[...] = jnp.zeros(cnt_ref.shape, F32)

        def tile_body(c, carry, e=e):
            lo = bnd_smem[c, e]
            hi = bnd_smem[c + 1, e]
            r0 = pl.multiple_of(c * ROUTE_TILE, ROUTE_TILE)
            counts = cum_t_ref[e:e + 1, pl.ds(r0, ROUTE_TILE)]

            def lane_tile(k, carry2):
                slots = k * LANES + slot_sub
                below_hi = jnp.where(slots < hi, 1.0, 0.0)
                hit = jnp.where(counts <= slots.astype(F32), below_hi, 0.0)
                folded = hit[:, :LANES]
                for t0 in range(LANES, ROUTE_TILE, LANES):
                    folded = folded + hit[:, t0:t0 + LANES]
                cnt_ref[k] += folded
                return carry2

            lax.fori_loop(lo // LANES, (hi + LANES - 1) // LANES, lane_tile, 0)
            return carry

        lax.fori_loop(0, n_tiles, tile_body, 0)
        for k in range(cap // LANES):
            per_slot = jnp.sum(cnt_ref[k], axis=1, keepdims=True)
            part = jnp.sum(jnp.where(eye, per_slot, 0.0), axis=0, keepdims=True)
            part = part + whole[:, k * LANES:(k + 1) * LANES]
            idx_ref[0, e:e + 1, k * LANES:(k + 1) * LANES] = part.astype(I32) + b * seq


def _route(h2a, aff, batch, seq, cap):
    t, n_exp = aff.shape
    pack = LANES // n_exp
    aff_packed = aff.reshape(t // pack, LANES)
    aff_block = h2a.shape[1] // LANES - 1
    n_tiles = seq // ROUTE_TILE
    bnd_rows = -(-(n_tiles + 1) // 8) * 8
    return pl.pallas_call(
        functools.partial(_route_kernel, cap=cap, n_exp=n_exp),
        grid=(batch,),
        in_specs=[pl.BlockSpec((seq, LANES), lambda b: (b, aff_block)),
                  pl.BlockSpec((seq // pack, LANES), lambda b: (b, 0))],
        out_specs=[pl.BlockSpec((1, n_exp, cap), lambda b: (b, 0, 0)),
                   pl.BlockSpec((seq, n_exp), lambda b: (b, 0)),
                   pl.BlockSpec((1, n_tiles, n_exp), lambda b: (b, 0, 0))],
        out_shape=[jax.ShapeDtypeStruct((batch, n_exp, cap), I32),
                   jax.ShapeDtypeStruct((t, n_exp), I32),
                   jax.ShapeDtypeStruct((batch, n_tiles, n_exp), I32)],
        scratch_shapes=[pltpu.VMEM((seq, LANES), BF16), pltpu.VMEM((seq, LANES), F32),
                        pltpu.VMEM((LANES, seq), F32),
                        pltpu.VMEM((bnd_rows, LANES), I32), pltpu.SMEM((bnd_rows, LANES), I32),
                        pltpu.VMEM((cap // LANES, LANES, LANES), F32), pltpu.SemaphoreType.DMA(())],
        compiler_params=_params(("parallel",), V7X_VMEM_LIMIT_BYTES),
        name="route",
    )(h2a, aff_packed)


def _ffn_kernel(idx_ref, h2a_hbm, wg_hbm, wu_hbm, wd_hbm, y_ref, stage_ref, xe_ref, gate_ref, acc_ref,
                wg_buf, wu_buf, wd_buf, sem, wsem, *, n_exp, n_ff, rows_per_step):
    e = pl.program_id(0)
    mp = stage_ref.shape[0]
    m, d = xe_ref.shape
    batch = y_ref.shape[0]
    cap = y_ref.shape[2]

    def row_copy(expert, i):
        r = idx_ref[expert * mp + i]
        return pltpu.make_async_copy(h2a_hbm.at[pl.ds(r, 1)], stage_ref.at[pl.ds(i, 1)], sem)

    def wait_rows():
        pltpu.make_async_copy(h2a_hbm.at[pl.ds(0, mp)], stage_ref, sem).wait()

    def weight_copies(chunk, slot):
        ex = chunk // n_ff
        c0 = pl.multiple_of((chunk % n_ff) * FF_TILE, FF_TILE)
        return (pltpu.make_async_copy(wg_hbm.at[ex, :, pl.ds(c0, FF_TILE)], wg_buf.at[slot], wsem.at[slot]),
                pltpu.make_async_copy(wu_hbm.at[ex, :, pl.ds(c0, FF_TILE)], wu_buf.at[slot], wsem.at[slot]),
                pltpu.make_async_copy(wd_hbm.at[ex, pl.ds(c0, FF_TILE), :], wd_buf.at[slot], wsem.at[slot]))

    @pl.when(e == 0)
    def _():
        def issue(i, carry):
            row_copy(0, i).start()
            return carry

        lax.fori_loop(0, mp, issue, 0)
        for cp in weight_copies(0, 0):
            cp.start()

    wait_rows()
    xe_ref[...] = stage_ref[:m, :d].astype(BF16)
    lane = lax.broadcasted_iota(I32, (m, LANES), 1)
    gate_ref[...] = jnp.sum(jnp.where(lane == e, stage_ref[:m, d:], 0.0), axis=1, keepdims=True)
    acc_ref[...] = jnp.zeros(acc_ref.shape, F32)

    nxt = jnp.minimum(e + 1, n_exp - 1)

    def ff_tile(f, carry):
        chunk = e * n_ff + f
        slot = chunk % 2

        for cp in weight_copies(chunk, slot):
            cp.wait()

        @pl.when(chunk + 1 < n_exp * n_ff)
        def _():
            for cp in weight_copies(chunk + 1, 1 - slot):
                cp.start()

        for k in range(rows_per_step):
            row_copy(nxt, f * rows_per_step + k).start()
        wg = wg_buf[slot].astype(BF16)
        wu = wu_buf[slot].astype(BF16)
        wd = wd_buf[slot].astype(BF16)
        for rb in range(m // FFN_ROW_BLOCK):
            rows = pl.ds(rb * FFN_ROW_BLOCK, FFN_ROW_BLOCK)
            xb = xe_ref[rows, :]
            a = jnp.dot(xb, wg, preferred_element_type=F32)
            u = jnp.dot(xb, wu, preferred_element_type=F32)
            hm = (a * jax.nn.sigmoid(a) * u).astype(BF16)
            acc_ref[rows, :] += jnp.dot(hm, wd, preferred_element_type=F32)
        return carry

    lax.fori_loop(0, n_ff, ff_tile, 0)

    for bb in range(batch):
        rows = pl.ds(bb * cap, cap)
        y_ref[bb, 0] = (acc_ref[rows, :] * gate_ref[rows, :]).astype(BF16)

    @pl.when(e == n_exp - 1)
    def _():
        wait_rows()


def _ffn(idx, h2a, w_gate, w_up, w_down):
    batch, n_exp, cap = idx.shape
    d = h2a.shape[1] - LANES
    ff = w_gate.shape[2]
    n_ff = ff // FF_TILE
    m = batch * cap
    rows_per_step = -(-pl.cdiv(m, n_ff) // 8) * 8
    mp = n_ff * rows_per_step
    idx_flat = jnp.pad(idx.transpose(1, 0, 2).reshape(n_exp, m), ((0, 0), (0, mp - m))).reshape(-1)
    any_spec = pl.BlockSpec(memory_space=pl.ANY)
    grid_spec = pltpu.PrefetchScalarGridSpec(
        num_scalar_prefetch=1,
        grid=(n_exp,),
        in_specs=[any_spec, any_spec, any_spec, any_spec],
        out_specs=pl.BlockSpec((batch, 1, cap, d), lambda e, idx: (0, e, 0, 0)),
        scratch_shapes=[pltpu.VMEM((mp, d + LANES), F32), pltpu.VMEM((m, d), BF16),
                        pltpu.VMEM((m, 1), F32), pltpu.VMEM((m, d), F32),
                        pltpu.VMEM((2, d, FF_TILE), F32), pltpu.VMEM((2, d, FF_TILE), F32),
                        pltpu.VMEM((2, FF_TILE, d), F32),
                        pltpu.SemaphoreType.DMA(()), pltpu.SemaphoreType.DMA((2,))],
    )
    return pl.pallas_call(
        functools.partial(_ffn_kernel, n_exp=n_exp, n_ff=n_ff, rows_per_step=rows_per_step),
        grid_spec=grid_spec,
        out_shape=jax.ShapeDtypeStruct((batch, n_exp, cap, d), BF16),
        compiler_params=_params(("arbitrary",), V7X_VMEM_LIMIT_BYTES),
        name="ffn",
    )(idx_flat, h2a, w_gate, w_up, w_down)


def _window_fits(s_hi, start, rows):
    return s_hi - start <= rows


def _combine_kernel(slo_ref, x1_ref, pos_ref, y_hbm, nf_ref, o_ref, slab_ref, big_ref, acc_ref, sem, big_sem,
                    *, n_exp, cap):
    b = pl.program_id(0)
    j = pl.program_id(1)
    n_batch = pl.num_programs(0)
    n_tiles = pl.num_programs(1)
    step = b * n_tiles + j
    slot = step % 2

    def bounds(bb, jj, e):
        base = (bb * (n_tiles + 1) + jj) * n_exp + e
        return slo_ref[base], slo_ref[base + n_exp]

    def window_start(bb, jj, e, rows):
        s_lo, _ = bounds(bb, jj, e)
        return pl.multiple_of(jnp.minimum((s_lo // SLAB_ALIGN) * SLAB_ALIGN, cap - rows), SLAB_ALIGN)

    def all_small(bb, jj):
        ok = None
        for e in range(n_exp):
            fits = _window_fits(bounds(bb, jj, e)[1], window_start(bb, jj, e, SMALL_ROWS), SMALL_ROWS)
            ok = fits if ok is None else jnp.logical_and(ok, fits)
        return ok

    def small_copy(bb, jj, e, dst_slot):
        return pltpu.make_async_copy(
            y_hbm.at[bb, e, pl.ds(window_start(bb, jj, e, SMALL_ROWS), SMALL_ROWS)],
            slab_ref.at[dst_slot, pl.ds(e * SMALL_ROWS, SMALL_ROWS)], sem.at[dst_slot])

    small_now = all_small(b, j)

    @pl.when(step == 0)
    def _():
        big_ref[...] = jnp.zeros(big_ref.shape, big_ref.dtype)

        @pl.when(small_now)
        def _():
            for e in range(n_exp):
                small_copy(b, j, e, slot).start()

    wrap = j + 1 == n_tiles
    nb = jnp.minimum(jnp.where(wrap, b + 1, b), n_batch - 1)
    nj = jnp.where(wrap, 0, j + 1)

    @pl.when(jnp.logical_and(step + 1 < n_batch * n_tiles, all_small(nb, nj)))
    def _():
        for e in range(n_exp):
            small_copy(nb, nj, e, 1 - slot).start()

    @pl.when(small_now)
    def _():
        lane = lax.broadcasted_iota(I32, (1, SMALL_ROWS), 1)
        acc = None
        for e in range(n_exp):
            small_copy(b, j, e, slot).wait()
        for e0 in range(0, n_exp, COMBINE_GROUP):
            tiles = []
            for e in range(e0, min(e0 + COMBINE_GROUP, n_exp)):
                rel = pos_ref[:, e:e + 1] - window_start(b, j, e, SMALL_ROWS)
                tiles.append(jnp.where(rel == lane, 1.0, 0.0).astype(BF16))
            rows = pl.ds(e0 * SMALL_ROWS, len(tiles) * SMALL_ROWS)
            part = jnp.dot(jnp.concatenate(tiles, axis=1), slab_ref[slot, rows, :],
                           preferred_element_type=F32)
            acc = part if acc is None else acc + part
        acc_ref[...] = acc

    @pl.when(jnp.logical_not(small_now))
    def _():
        lane = lax.broadcasted_iota(I32, (1, BIG_PAD), 1)
        for e in range(n_exp):
            st = window_start(b, j, e, BIG_ROWS)
            cp = pltpu.make_async_copy(y_hbm.at[b, e, pl.ds(st, BIG_ROWS)],
                                       big_ref.at[pl.ds(0, BIG_ROWS)], big_sem)
            cp.start()
            cp.wait()
            onehot = jnp.where(pos_ref[:, e:e + 1] - st == lane, 1.0, 0.0).astype(BF16)
            part = jnp.dot(onehot, big_ref[...], preferred_element_type=F32)
            if e == 0:
                acc_ref[...] = part
            else:
                acc_ref[...] += part

    x2 = x1_ref[...] + acc_ref[...]
    o_ref[...] = x2 * lax.rsqrt(jnp.mean(x2 * x2, axis=-1, keepdims=True) + EPS) * nf_ref[...]


def _combine(slo, x1, pos, y_slots, norm_final, seq):
    t, d = x1.shape
    batch, n_exp, cap, _ = y_slots.shape
    n_tiles = seq // ROUTE_TILE
    slo_flat = jnp.concatenate([slo, jnp.full((batch, 1, n_exp), cap, I32)], axis=1).reshape(-1)
    row = lambda b, j, slo: (b * n_tiles + j, 0)
    grid_spec = pltpu.PrefetchScalarGridSpec(
        num_scalar_prefetch=1,
        grid=(batch, n_tiles),
        in_specs=[
            pl.BlockSpec((ROUTE_TILE, d), row),
            pl.BlockSpec((ROUTE_TILE, n_exp), row),
            pl.BlockSpec(memory_space=pl.ANY),
            pl.BlockSpec((1, d), lambda b, j, slo: (0, 0)),
        ],
        out_specs=pl.BlockSpec((ROUTE_TILE, d), row),
        scratch_shapes=[pltpu.VMEM((2, n_exp * SMALL_ROWS, d), BF16), pltpu.VMEM((BIG_PAD, d), BF16),
                        pltpu.VMEM((ROUTE_TILE, d), F32),
                        pltpu.SemaphoreType.DMA((2,)), pltpu.SemaphoreType.DMA(())],
    )
    return pl.pallas_call(
        functools.partial(_combine_kernel, n_exp=n_exp, cap=cap),
        grid_spec=grid_spec,
        out_shape=jax.ShapeDtypeStruct((t, d), F32),
        compiler_params=_params(("arbitrary", "arbitrary"), V7X_VMEM_LIMIT_BYTES),
        name="combine",
    )(slo_flat, x1, pos, y_slots, norm_final)


def _rope_tables(seq):
    rows = seq // GRID_W
    row = np.repeat(np.arange(rows, dtype=np.float64), GRID_W)
    col = np.tile(np.arange(GRID_W, dtype=np.float64), rows)
    n_freq = HALF_ROT // 2
    freqs = ROPE_THETA ** (-np.arange(n_freq, dtype=np.float64) / n_freq)
    ang = np.concatenate([row[:, None] * freqs, col[:, None] * freqs], axis=-1)
    cos, sin = np.cos(ang), np.sin(ang)
    return (jnp.asarray(np.concatenate([cos, cos], axis=-1), F32),
            jnp.asarray(np.concatenate([-sin, sin], axis=-1), F32))


def kernel(x, norm_mix, w_in, ret_decay_fwd, ret_decay_bwd, ret_gn_gain, w_ret_branch, q_norm, k_norm,
           w_att_branch, w_o, norm_ffn, w_router, w_expert_gate, w_expert_up, w_expert_down, norm_final):
    batch, seq, d = x.shape
    depth = norm_mix.shape[0]
    n_exp = w_router.shape[2]
    cap = CAPACITY_FACTOR * seq // n_exp
    assert seq % TOKEN_TILE == 0 and seq % ROUTE_TILE == 0 and seq % ATT_KV_TILE == 0
    assert LANES % n_exp == 0 and d % LANES == 0 and seq % (8 * LANES // n_exp) == 0
    assert cap >= BIG_ROWS and (cap - BIG_ROWS) % SLAB_ALIGN == 0 and (cap - SMALL_ROWS) % SLAB_ALIGN == 0
    assert w_expert_gate.shape[3] % FF_TILE == 0 and (batch * cap) % FFN_ROW_BLOCK == 0
    assert depth == 1, "the final RMSNorm is fused into the combine kernel of the single layer"
    l = 0
    cos2, sin2 = _rope_tables(seq)
    xf = x.reshape(batch * seq, d)
    lg = jnp.stack([jnp.log1p(-jnp.exp(ret_decay_fwd[l].astype(F32))),
                    jnp.log1p(-jnp.exp(ret_decay_bwd[l].astype(F32)))])
    rq, rk, rv, sg, aq, ak, av, gr, ga = _inproj(
        xf, norm_mix[l][None], w_in[l].astype(BF16), cos2, sin2, q_norm[l][None], k_norm[l][None], seq)
    u_ret = _retention(lg, rq, rk, rv, sg, ret_gn_gain[l][None], batch, seq)
    y_att = _attention(aq, ak, av, batch, seq)
    w_r = jnp.pad(w_router[l].astype(F32), ((0, 0), (0, LANES - n_exp)))
    w_r_hi = w_r.astype(BF16)
    w_r_lo = (w_r - w_r_hi.astype(F32)).astype(BF16)
    x1, h2a, aff = _merge(xf, u_ret, y_att, gr, ga, w_ret_branch[l].astype(BF16),
                          w_att_branch[l].astype(BF16), w_o[l].astype(BF16), norm_ffn[l][None],
                          jnp.concatenate([w_r_hi, w_r_lo], axis=1), n_exp)
    idx, pos, slo = _route(h2a, aff, batch, seq, cap)
    y_slots = _ffn(idx, h2a, w_expert_gate[l], w_expert_up[l], w_expert_down[l])
    out = _combine(slo, x1, pos, y_slots, norm_final[None], seq)
    return out.reshape(batch, seq, d)
```

```python
import functools

import jax
import jax.numpy as jnp
import numpy as np
from jax import lax
from jax.experimental import pallas as pl
from jax.experimental.pallas import tpu as pltpu

F32 = jnp.float32
BF16 = jnp.bfloat16
I32 = jnp.int32

GRID_W = 64
HEAD_DIM = 128
HALF_ROT = HEAD_DIM // 2
RET_HEADS = 4
RET_DIM = RET_HEADS * HEAD_DIM
ATT_HEADS = 4
ATT_KV_HEADS = 2
ATT_GROUP = ATT_HEADS // ATT_KV_HEADS
ATT_DIM = ATT_HEADS * HEAD_DIM
ATT_KV_DIM = ATT_KV_HEADS * HEAD_DIM
CHUNK = 256
ROPE_THETA = 10000.0
CAPACITY_FACTOR = 2
EPS = 1e-6
LOG2_E = 1.4426950408889634

V7X_VMEM_LIMIT_BYTES = 56 * 1024 * 1024
TOKEN_TILE = 512
ATT_Q_TILE = 1024
ATT_KV_TILE = 4096
RET_UNROLL = 4
LANES = 128
ROUTE_TILE = 512
SLAB_ALIGN = 16
SMALL_ROWS = LANES
BIG_ROWS = ROUTE_TILE + SLAB_ALIGN
BIG_PAD = 640
COMBINE_GROUP = 4
FF_TILE = 256
FFN_ROW_BLOCK = 512
BISECT_STEPS = 24


def _params(sem, vmem=None):
    return pltpu.CompilerParams(dimension_semantics=sem, vmem_limit_bytes=vmem)


def _inproj_kernel(x_ref, g_ref, w_ref, cos_ref, sin_ref, qn_ref, kn_ref,
                   rq_ref, rk_ref, rv_ref, sg_ref, aq_ref, ak_ref, av_ref, gr_ref, ga_ref):
    x = x_ref[...]
    h = (x * lax.rsqrt(jnp.mean(x * x, axis=-1, keepdims=True) + EPS) * g_ref[...]).astype(BF16)
    cos = cos_ref[...]
    sin = sin_ref[...]
    d_model = x.shape[1]
    scale = HEAD_DIM ** -0.5

    def proj(lo, width):
        return jnp.dot(h, w_ref[:, lo:lo + width], preferred_element_type=F32)

    def rope(t):
        return t * cos + pltpu.roll(t, HALF_ROT, axis=1) * sin

    def head_norm(t, gain):
        return t * lax.rsqrt(jnp.mean(t * t, axis=-1, keepdims=True) + EPS) * gain

    def head(p, i):
        return p[:, i * HEAD_DIM:(i + 1) * HEAD_DIM]

    def put(ref, i, val):
        ref[:, i * HEAD_DIM:(i + 1) * HEAD_DIM] = val.astype(ref.dtype)

    off = 0
    p = proj(off, RET_DIM)
    for i in range(RET_HEADS):
        put(rq_ref, i, rope(head(p, i)))
    off += RET_DIM
    p = proj(off, RET_DIM)
    for i in range(RET_HEADS):
        put(rk_ref, i, rope(head(p, i)) * scale)
    off += RET_DIM
    rv_ref[...] = proj(off, RET_DIM).astype(BF16)
    off += RET_DIM
    p = proj(off, RET_DIM)
    sg_ref[...] = (p * jax.nn.sigmoid(p)).astype(BF16)
    off += RET_DIM
    p = proj(off, ATT_DIM)
    for i in range(ATT_HEADS):
        put(aq_ref, i, rope(head_norm(head(p, i), qn_ref[...])) * (scale * LOG2_E))
    off += ATT_DIM
    p = proj(off, ATT_KV_DIM)
    for i in range(ATT_KV_HEADS):
        put(ak_ref, i, rope(head_norm(head(p, i), kn_ref[...])))
    off += ATT_KV_DIM
    av_ref[...] = proj(off, ATT_KV_DIM).T.astype(BF16)
    off += ATT_KV_DIM
    gr_ref[...] = jax.nn.sigmoid(proj(off, d_model)).astype(BF16)
    off += d_model
    ga_ref[...] = jax.nn.sigmoid(proj(off, d_model)).astype(BF16)


def _inproj(xf, norm_g, w_in_bf, cos2, sin2, q_norm, k_norm, seq):
    t, d = xf.shape
    tm = TOKEN_TILE
    in_width = w_in_bf.shape[1]
    steps_per_seq = seq // tm
    row = lambda i: (i, 0)
    const = lambda i: (0, 0)
    pos = lambda i: (i % steps_per_seq, 0)
    widths = (RET_DIM, RET_DIM, RET_DIM, RET_DIM, ATT_DIM, ATT_KV_DIM, ATT_KV_DIM, d, d)
    av_index = 6
    return pl.pallas_call(
        _inproj_kernel,
        grid=(t // tm,),
        in_specs=[
            pl.BlockSpec((tm, d), row),
            pl.BlockSpec((1, d), const),
            pl.BlockSpec((d, in_width), const),
            pl.BlockSpec((tm, HEAD_DIM), pos),
            pl.BlockSpec((tm, HEAD_DIM), pos),
            pl.BlockSpec((1, HEAD_DIM), const),
            pl.BlockSpec((1, HEAD_DIM), const),
        ],
        out_specs=[pl.BlockSpec((ATT_KV_DIM, tm), lambda i: (0, i)) if k == av_index
                   else pl.BlockSpec((tm, w), row) for k, w in enumerate(widths)],
        out_shape=[jax.ShapeDtypeStruct((ATT_KV_DIM, t) if k == av_index else (t, w), BF16)
                   for k, w in enumerate(widths)],
        compiler_params=_params(("parallel",), V7X_VMEM_LIMIT_BYTES),
        name="inproj",
    )(xf, norm_g, w_in_bf, cos2, sin2, q_norm, k_norm)


def _retention_kernel(lg_ref, q_ref, k_ref, v_ref, sg_ref, gain_ref, o_ref, of_ref, ob_ref):
    hd = pl.program_id(1)
    lgf = lg_ref[0, hd]
    lgb = lg_ref[1, hd]
    n_chunks = q_ref.shape[0] // CHUNK
    ii = lax.broadcasted_iota(I32, (CHUNK, CHUNK), 0)
    jj = lax.broadcasted_iota(I32, (CHUNK, CHUNK), 1)
    dist = (ii - jj).astype(F32)
    dec_f = jnp.where(dist >= 0, jnp.exp(lgf * jnp.maximum(dist, 0.0)), 0.0)
    dec_b = jnp.where(dist < 0, jnp.exp(lgb * jnp.maximum(-dist, 0.0)), 0.0)
    t = lax.broadcasted_iota(I32, (CHUNK, 1), 0).astype(F32)
    kd_f = jnp.exp(lgf * (CHUNK - 1.0 - t))
    qd_f = jnp.exp(lgf * (t + 1.0))
    kd_b = jnp.exp(lgb * t)
    qd_b = jnp.exp(lgb * (CHUNK - t))
    zero_row = jnp.zeros((1, HEAD_DIM), F32)
    cd_f = jnp.exp(zero_row + lgf * CHUNK)
    cd_b = jnp.exp(zero_row + lgb * CHUNK)

    def one_chunk(n, dec, kd, qd, cd, state):
        r0 = pl.multiple_of(n * CHUNK, CHUNK)
        q = q_ref[pl.ds(r0, CHUNK), :]
        k = k_ref[pl.ds(r0, CHUNK), :]
        v = v_ref[pl.ds(r0, CHUNK), :]
        s = lax.dot_general(q, k, (((1,), (1,)), ((), ())), preferred_element_type=F32) * dec
        intra = jnp.dot(s.astype(BF16), v, preferred_element_type=F32)
        q_dec = (q.astype(F32) * qd).astype(BF16)
        inter = jnp.dot(q_dec, state.astype(BF16), preferred_element_type=F32)
        k_dec_t = (k.astype(F32) * kd).T.astype(BF16)
        new_state = state * cd + jnp.dot(k_dec_t, v, preferred_element_type=F32)
        return r0, intra + inter, new_state

    def scan_body(n, carry):
        sf, sb = carry
        r0, of, sf = one_chunk(n, dec_f, kd_f, qd_f, cd_f, sf)
        of_ref[pl.ds(r0, CHUNK), :] = of
        r0, ob, sb = one_chunk(n_chunks - 1 - n, dec_b, kd_b, qd_b, cd_b, sb)
        ob_ref[pl.ds(r0, CHUNK), :] = ob
        return sf, sb

    zeros = jnp.zeros((HEAD_DIM, HEAD_DIM), F32)
    lax.fori_loop(0, n_chunks, scan_body, (zeros, zeros), unroll=RET_UNROLL)

    rows = TOKEN_TILE

    def norm_body(n, carry):
        r0 = pl.multiple_of(n * rows, rows)
        o = of_ref[pl.ds(r0, rows), :] + ob_ref[pl.ds(r0, rows), :]
        mu = jnp.mean(o, axis=-1, keepdims=True)
        var = jnp.mean(jnp.square(o - mu), axis=-1, keepdims=True)
        yn = (o - mu) * lax.rsqrt(var + EPS) * gain_ref[...]
        o_ref[pl.ds(r0, rows), :] = (sg_ref[pl.ds(r0, rows), :].astype(F32) * yn).astype(BF16)
        return carry

    lax.fori_loop(0, q_ref.shape[0] // rows, norm_body, 0)


def _retention(lg, rq, rk, rv, sg, gn_gain, batch, seq):
    t = rq.shape[0]
    blk = pl.BlockSpec((seq, HEAD_DIM), lambda b, h: (b, h))
    return pl.pallas_call(
        _retention_kernel,
        grid=(batch, RET_HEADS),
        in_specs=[
            pl.BlockSpec(memory_space=pltpu.SMEM),
            blk, blk, blk, blk,
            pl.BlockSpec((1, HEAD_DIM), lambda b, h: (0, h)),
        ],
        out_specs=blk,
        out_shape=jax.ShapeDtypeStruct((t, RET_DIM), BF16),
        scratch_shapes=[pltpu.VMEM((seq, HEAD_DIM), F32), pltpu.VMEM((seq, HEAD_DIM), F32)],
        compiler_params=_params(("parallel", "parallel"), V7X_VMEM_LIMIT_BYTES),
        name="retention",
    )(lg, rq, rk, rv, sg, gn_gain)


def _attention_kernel(q_ref, k_ref, vt_ref, o_ref):
    tq = q_ref.shape[0]
    tk = ATT_KV_TILE
    n_kv = k_ref.shape[0] // tk
    qs = [q_ref[:, hh * HEAD_DIM:(hh + 1) * HEAD_DIM] for hh in range(ATT_GROUP)]

    def body(c, carry):
        r0 = pl.multiple_of(c * tk, tk)
        k = k_ref[pl.ds(r0, tk), :]
        vt = vt_ref[:, pl.ds(r0, tk)]
        out = []
        for q, (m, l, acc) in zip(qs, carry):
            s = lax.dot_general(k, q, (((1,), (1,)), ((), ())), preferred_element_type=F32)
            m_new = jnp.maximum(m, jnp.max(s, axis=0, keepdims=True))
            alpha = jnp.exp2(m - m_new)
            p = jnp.exp2(s - m_new)
            l = alpha * l + jnp.sum(p, axis=0, keepdims=True)
            acc = alpha * acc + jnp.dot(vt, p.astype(BF16), preferred_element_type=F32)
            out.append((m_new, l, acc))
        return tuple(out)

    init = (jnp.full((1, tq), -jnp.inf, F32), jnp.zeros((1, tq), F32), jnp.zeros((HEAD_DIM, tq), F32))
    final = lax.fori_loop(0, n_kv, body, (init,) * ATT_GROUP, unroll=True)
    for hh, (_, l, acc) in enumerate(final):
        o_ref[:, hh * HEAD_DIM:(hh + 1) * HEAD_DIM] = (acc / l).T.astype(BF16)


def _attention(aq, ak, av_t, batch, seq):
    t = aq.shape[0]
    tq = ATT_Q_TILE
    nq = seq // tq
    gw = ATT_GROUP * HEAD_DIM
    q_spec = pl.BlockSpec((tq, gw), lambda b, g, i: (b * nq + i, g))
    k_spec = pl.BlockSpec((seq, HEAD_DIM), lambda b, g, i: (b, g))
    vt_spec = pl.BlockSpec((HEAD_DIM, seq), lambda b, g, i: (g, b))
    return pl.pallas_call(
        _attention_kernel,
        grid=(batch, ATT_KV_HEADS, nq),
        in_specs=[q_spec, k_spec, vt_spec],
        out_specs=q_spec,
        out_shape=jax.ShapeDtypeStruct((t, ATT_DIM), BF16),
        compiler_params=_params(("parallel", "parallel", "parallel"), V7X_VMEM_LIMIT_BYTES),
        name="attention",
    )(aq, ak, av_t)


def _merge_kernel(x_ref, ur_ref, ya_ref, gr_ref, ga_ref, wr_ref, wa_ref, wo_ref, nf_ref, wrt_ref,
                  x1_ref, h2a_ref, aff_ref, *, n_exp):
    d = x_ref.shape[1]
    y_ret = jnp.dot(ur_ref[...], wr_ref[...], preferred_element_type=F32)
    y_att = jnp.dot(ya_ref[...], wa_ref[...], preferred_element_type=F32)
    mixed = gr_ref[...].astype(F32) * y_ret + ga_ref[...].astype(F32) * y_att
    x1 = x_ref[...] + jnp.dot(mixed.astype(BF16), wo_ref[...], preferred_element_type=F32)
    x1_ref[...] = x1
    h2 = x1 * lax.rsqrt(jnp.mean(x1 * x1, axis=-1, keepdims=True) + EPS) * nf_ref[...]
    h2a_ref[:, :d] = h2
    h_hi = h2.astype(BF16)
    h_lo = (h2 - h_hi.astype(F32)).astype(BF16)
    r_hi = jnp.dot(h_hi, wrt_ref[...], preferred_element_type=F32)
    r_lo = jnp.dot(h_lo, wrt_ref[:, :LANES], preferred_element_type=F32)
    logits = r_hi[:, :LANES] + r_hi[:, LANES:] + r_lo
    lane = lax.broadcasted_iota(I32, logits.shape, 1)
    logits = jnp.where(lane < n_exp, logits, -jnp.inf)
    e = jnp.exp(logits - jnp.max(logits, axis=-1, keepdims=True))
    aff = e / jnp.sum(e, axis=-1, keepdims=True)
    h2a_ref[:, d:] = aff
    aff_ref[...] = aff[:, :n_exp]


def _merge(xf, u_ret, y_att, gr, ga, w_ret_bf, w_att_bf, w_o_bf, norm_ffn, w_router_split, n_exp):
    t, d = xf.shape
    tm = TOKEN_TILE
    row = lambda i: (i, 0)
    const = lambda i: (0, 0)
    return pl.pallas_call(
        functools.partial(_merge_kernel, n_exp=n_exp),
        grid=(t // tm,),
        in_specs=[
            pl.BlockSpec((tm, d), row),
            pl.BlockSpec((tm, RET_DIM), row),
            pl.BlockSpec((tm, ATT_DIM), row),
            pl.BlockSpec((tm, d), row),
            pl.BlockSpec((tm, d), row),
            pl.BlockSpec((RET_DIM, d), const),
            pl.BlockSpec((ATT_DIM, d), const),
            pl.BlockSpec((d, d), const),
            pl.BlockSpec((1, d), const),
            pl.BlockSpec((d, 2 * LANES), const),
        ],
        out_specs=[pl.BlockSpec((tm, d), row), pl.BlockSpec((tm, d + LANES), row),
                   pl.BlockSpec((tm, n_exp), row)],
        out_shape=[jax.ShapeDtypeStruct((t, d), F32), jax.ShapeDtypeStruct((t, d + LANES), F32),
                   jax.ShapeDtypeStruct((t, n_exp), F32)],
        compiler_params=_params(("parallel",), V7X_VMEM_LIMIT_BYTES),
        name="merge",
    )(xf, u_ret, y_att, gr, ga, w_ret_bf, w_att_bf, w_o_bf, norm_ffn, w_router_split)


def _route_kernel(aff_ref, affc_ref, idx_ref, pos_ref, slo_ref, msk_ref, cum_ref, cum_t_ref, bnd_ref, bnd_smem,
                  cnt_ref, sem, *, cap, n_exp):
    b = pl.program_id(0)
    seq = aff_ref.shape[0]
    n_tiles = seq // ROUTE_TILE
    aff = aff_ref[...]
    affc = affc_ref[...]
    capf = float(cap)

    def per_expert(x, op, reduce):
        r = reduce(x.reshape(x.shape[0] // 8, 8, LANES), axis=0)
        shift = n_exp
        while shift < LANES:
            r = op(r, pltpu.roll(r, shift, axis=1))
            shift *= 2
        return reduce(r, axis=0, keepdims=True)

    def bisect(lo, hi):
        mid = 0.5 * (lo + hi)
        ok = per_expert(jnp.where(affc >= mid, 1.0, 0.0), jnp.add, jnp.sum) >= capf
        return jnp.where(ok, mid, lo), jnp.where(ok, hi, mid)

    def bracket(lo, hi):
        mn = per_expert(jnp.where(affc >= lo, affc, jnp.inf), jnp.minimum, jnp.min)
        mx = per_expert(jnp.where(affc < hi, affc, -jnp.inf), jnp.maximum, jnp.max)
        return mn, mx

    def not_isolated(lo, hi):
        mn, mx = bracket(lo, hi)
        return jnp.max(jnp.where(mn < mx, 1.0, 0.0)) > 0.0

    lo, hi = lax.fori_loop(0, BISECT_STEPS, lambda i, c: bisect(*c),
                           (jnp.zeros((1, LANES), F32), jnp.full((1, LANES), 2.0, F32)))

    def refine(c):
        lo, hi = bisect(c[0], c[1])
        return lo, hi, not_isolated(lo, hi)

    lo, hi, _ = lax.while_loop(lambda c: c[2], refine, (lo, hi, not_isolated(lo, hi)))
    thr, _ = bracket(lo, hi)
    gt = aff > thr
    eq = aff == thr
    n_gt = jnp.sum(jnp.where(gt, 1.0, 0.0).reshape(seq // 8, 8, LANES), axis=0)
    need = capf - jnp.sum(n_gt, axis=0, keepdims=True)

    tri = (lax.broadcasted_iota(I32, (ROUTE_TILE, ROUTE_TILE), 0)
           >= lax.broadcasted_iota(I32, (ROUTE_TILE, ROUTE_TILE), 1))
    tri = jnp.where(tri, 1.0, 0.0).astype(BF16)

    def cumsum_tokens(store_tile_start):
        def body(c, carry):
            r0 = pl.multiple_of(c * ROUTE_TILE, ROUTE_TILE)
            if store_tile_start:
                slo_ref[0, pl.ds(c, 1), :] = carry[:, :n_exp].astype(I32)
                bnd_ref[pl.ds(c, 1), :] = carry.astype(I32)
            cs = jnp.dot(tri, msk_ref[pl.ds(r0, ROUTE_TILE), :], preferred_element_type=F32) + carry
            cum_ref[pl.ds(r0, ROUTE_TILE), :] = cs
            return cs[ROUTE_TILE - 1:ROUTE_TILE, :]

        return lax.fori_loop(0, n_tiles, body, jnp.zeros((1, LANES), F32))

    msk_ref[...] = jnp.where(eq, 1.0, 0.0).astype(BF16)
    cumsum_tokens(False)
    take = jnp.logical_and(eq, cum_ref[...] - 1.0 < need)
    mask = jnp.logical_or(gt, take)
    msk_ref[...] = jnp.where(mask, 1.0, 0.0).astype(BF16)
    bnd_ref[...] = jnp.zeros(bnd_ref.shape, I32)
    bnd_ref[pl.ds(n_tiles, 1), :] = cumsum_tokens(True).astype(I32)
    pos_ref[...] = jnp.where(mask, cum_ref[...] - 1.0, -1.0)[:, :n_exp].astype(I32)

    to_smem = pltpu.make_async_copy(bnd_ref, bnd_smem, sem)
    to_smem.start()
    to_smem.wait()

    def transpose_tile(c, carry):
        r0 = pl.multiple_of(c * ROUTE_TILE, ROUTE_TILE)
        cum_t_ref[:, pl.ds(r0, ROUTE_TILE)] = cum_ref[pl.ds(r0, ROUTE_TILE), :].T
        return carry

    lax.fori_loop(0, n_tiles, transpose_tile, 0)
    slot_row = lax.broadcasted_iota(I32, (1, cap), 1).astype(F32)
    slot_sub = lax.broadcasted_iota(I32, (LANES, ROUTE_TILE), 0)
    eye = lax.broadcasted_iota(I32, (LANES, LANES), 0) == lax.broadcasted_iota(I32, (LANES, LANES), 1)
    for e in range(n_exp):
        ends = bnd_ref[pl.ds(1, n_tiles), e:e + 1].astype(F32)
        whole = jnp.sum(jnp.where(slot_row >= ends, float(ROUTE_TILE), 0.0), axis=0, keepdims=True)
        cnt_ref[...] = jnp.zeros(cnt_ref.shape, F32)

        def tile_body(c, carry, e=e):
            lo = bnd_smem[c, e]
            hi = bnd_smem[c + 1, e]
            r0 = pl.multiple_of(c * ROUTE_TILE, ROUTE_TILE)
            counts = cum_t_ref[e:e + 1, pl.ds(r0, ROUTE_TILE)]

            def lane_tile(k, carry2):
                slots = k * LANES + slot_sub
                below_hi = jnp.where(slots < hi, 1.0, 0.0)
                hit = jnp.where(counts <= slots.astype(F32), below_hi, 0.0)
                folded = hit[:, :LANES]
                for t0 in range(LANES, ROUTE_TILE, LANES):
                    folded = folded + hit[:, t0:t0 + LANES]
                cnt_ref[k] += folded
                return carry2

            lax.fori_loop(lo // LANES, (hi + LANES - 1) // LANES, lane_tile, 0)
            return carry

        lax.fori_loop(0, n_tiles, tile_body, 0)
        for k in range(cap // LANES):
            per_slot = jnp.sum(cnt_ref[k], axis=1, keepdims=True)
            part = jnp.sum(jnp.where(eye, per_slot, 0.0), axis=0, keepdims=True)
            part = part + whole[:, k * LANES:(k + 1) * LANES]
            idx_ref[0, e:e + 1, k * LANES:(k + 1) * LANES] = part.astype(I32) + b * seq


def _route(h2a, aff, batch, seq, cap):
    t, n_exp = aff.shape
    pack = LANES // n_exp
    aff_packed = aff.reshape(t // pack, LANES)
    aff_block = h2a.shape[1] // LANES - 1
    n_tiles = seq // ROUTE_TILE
    bnd_rows = -(-(n_tiles + 1) // 8) * 8
    return pl.pallas_call(
        functools.partial(_route_kernel, cap=cap, n_exp=n_exp),
        grid=(batch,),
        in_specs=[pl.BlockSpec((seq, LANES), lambda b: (b, aff_block)),
                  pl.BlockSpec((seq // pack, LANES), lambda b: (b, 0))],
        out_specs=[pl.BlockSpec((1, n_exp, cap), lambda b: (b, 0, 0)),
                   pl.BlockSpec((seq, n_exp), lambda b: (b, 0)),
                   pl.BlockSpec((1, n_tiles, n_exp), lambda b: (b, 0, 0))],
        out_shape=[jax.ShapeDtypeStruct((batch, n_exp, cap), I32),
                   jax.ShapeDtypeStruct((t, n_exp), I32),
                   jax.ShapeDtypeStruct((batch, n_tiles, n_exp), I32)],
        scratch_shapes=[pltpu.VMEM((seq, LANES), BF16), pltpu.VMEM((seq, LANES), F32),
                        pltpu.VMEM((LANES, seq), F32),
                        pltpu.VMEM((bnd_rows, LANES), I32), pltpu.SMEM((bnd_rows, LANES), I32),
                        pltpu.VMEM((cap // LANES, LANES, LANES), F32), pltpu.SemaphoreType.DMA(())],
        compiler_params=_params(("parallel",), V7X_VMEM_LIMIT_BYTES),
        name="route",
    )(h2a, aff_packed)


def _ffn_kernel(idx_ref, h2a_hbm, wg_hbm, wu_hbm, wd_hbm, y_ref, stage_ref, xe_ref, gate_ref, acc_ref,
                wg_buf, wu_buf, wd_buf, sem, wsem, *, n_exp, n_ff, rows_per_step):
    e = pl.program_id(0)
    mp = stage_ref.shape[0]
    m, d = xe_ref.shape
    batch = y_ref.shape[0]
    cap = y_ref.shape[2]

    def row_copy(expert, i):
        r = idx_ref[expert * mp + i]
        return pltpu.make_async_copy(h2a_hbm.at[pl.ds(r, 1)], stage_ref.at[pl.ds(i, 1)], sem)

    def wait_rows():
        pltpu.make_async_copy(h2a_hbm.at[pl.ds(0, mp)], stage_ref, sem).wait()

    def weight_copies(chunk, slot):
        ex = chunk // n_ff
        c0 = pl.multiple_of((chunk % n_ff) * FF_TILE, FF_TILE)
        return (pltpu.make_async_copy(wg_hbm.at[ex, :, pl.ds(c0, FF_TILE)], wg_buf.at[slot], wsem.at[slot]),
                pltpu.make_async_copy(wu_hbm.at[ex, :, pl.ds(c0, FF_TILE)], wu_buf.at[slot], wsem.at[slot]),
                pltpu.make_async_copy(wd_hbm.at[ex, pl.ds(c0, FF_TILE), :], wd_buf.at[slot], wsem.at[slot]))

    @pl.when(e == 0)
    def _():
        def issue(i, carry):
            row_copy(0, i).start()
            return carry

        lax.fori_loop(0, mp, issue, 0)
        for cp in weight_copies(0, 0):
            cp.start()

    wait_rows()
    xe_ref[...] = stage_ref[:m, :d].astype(BF16)
    lane = lax.broadcasted_iota(I32, (m, LANES), 1)
    gate_ref[...] = jnp.sum(jnp.where(lane == e, stage_ref[:m, d:], 0.0), axis=1, keepdims=True)
    acc_ref[...] = jnp.zeros(acc_ref.shape, F32)

    nxt = jnp.minimum(e + 1, n_exp - 1)

    def ff_tile(f, carry):
        chunk = e * n_ff + f
        slot = chunk % 2

        for cp in weight_copies(chunk, slot):
            cp.wait()

        @pl.when(chunk + 1 < n_exp * n_ff)
        def _():
            for cp in weight_copies(chunk + 1, 1 - slot):
                cp.start()

        for k in range(rows_per_step):
            row_copy(nxt, f * rows_per_step + k).start()
        wg = wg_buf[slot].astype(BF16)
        wu = wu_buf[slot].astype(BF16)
        wd = wd_buf[slot].astype(BF16)
        for rb in range(m // FFN_ROW_BLOCK):
            rows = pl.ds(rb * FFN_ROW_BLOCK, FFN_ROW_BLOCK)
            xb = xe_ref[rows, :]
            a = jnp.dot(xb, wg, preferred_element_type=F32)
            u = jnp.dot(xb, wu, preferred_element_type=F32)
            hm = (a * jax.nn.sigmoid(a) * u).astype(BF16)
            acc_ref[rows, :] += jnp.dot(hm, wd, preferred_element_type=F32)
        return carry

    lax.fori_loop(0, n_ff, ff_tile, 0)

    for bb in range(batch):
        rows = pl.ds(bb * cap, cap)
        y_ref[bb, 0] = (acc_ref[rows, :] * gate_ref[rows, :]).astype(BF16)

    @pl.when(e == n_exp - 1)
    def _():
        wait_rows()


def _ffn(idx, h2a, w_gate, w_up, w_down):
    batch, n_exp, cap = idx.shape
    d = h2a.shape[1] - LANES
    ff = w_gate.shape[2]
    n_ff = ff // FF_TILE
    m = batch * cap
    rows_per_step = -(-pl.cdiv(m, n_ff) // 8) * 8
    mp = n_ff * rows_per_step
    idx_flat = jnp.pad(idx.transpose(1, 0, 2).reshape(n_exp, m), ((0, 0), (0, mp - m))).reshape(-1)
    any_spec = pl.BlockSpec(memory_space=pl.ANY)
    grid_spec = pltpu.PrefetchScalarGridSpec(
        num_scalar_prefetch=1,
        grid=(n_exp,),
        in_specs=[any_spec, any_spec, any_spec, any_spec],
        out_specs=pl.BlockSpec((batch, 1, cap, d), lambda e, idx: (0, e, 0, 0)),
        scratch_shapes=[pltpu.VMEM((mp, d + LANES), F32), pltpu.VMEM((m, d), BF16),
                        pltpu.VMEM((m, 1), F32), pltpu.VMEM((m, d), F32),
                        pltpu.VMEM((2, d, FF_TILE), F32), pltpu.VMEM((2, d, FF_TILE), F32),
                        pltpu.VMEM((2, FF_TILE, d), F32),
                        pltpu.SemaphoreType.DMA(()), pltpu.SemaphoreType.DMA((2,))],
    )
    return pl.pallas_call(
        functools.partial(_ffn_kernel, n_exp=n_exp, n_ff=n_ff, rows_per_step=rows_per_step),
        grid_spec=grid_spec,
        out_shape=jax.ShapeDtypeStruct((batch, n_exp, cap, d), BF16),
        compiler_params=_params(("arbitrary",), V7X_VMEM_LIMIT_BYTES),
        name="ffn",
    )(idx_flat, h2a, w_gate, w_up, w_down)


def _window_fits(s_hi, start, rows):
    return s_hi - start <= rows


def _combine_kernel(slo_ref, x1_ref, pos_ref, y_hbm, nf_ref, o_ref, slab_ref, big_ref, acc_ref, sem, big_sem,
                    *, n_exp, cap):
    b = pl.program_id(0)
    j = pl.program_id(1)
    n_batch = pl.num_programs(0)
    n_tiles = pl.num_programs(1)
    step = b * n_tiles + j
    slot = step % 2

    def bounds(bb, jj, e):
        base = (bb * (n_tiles + 1) + jj) * n_exp + e
        return slo_ref[base], slo_ref[base + n_exp]

    def window_start(bb, jj, e, rows):
        s_lo, _ = bounds(bb, jj, e)
        return pl.multiple_of(jnp.minimum((s_lo // SLAB_ALIGN) * SLAB_ALIGN, cap - rows), SLAB_ALIGN)

    def all_small(bb, jj):
        ok = None
        for e in range(n_exp):
            fits = _window_fits(bounds(bb, jj, e)[1], window_start(bb, jj, e, SMALL_ROWS), SMALL_ROWS)
            ok = fits if ok is None else jnp.logical_and(ok, fits)
        return ok

    def small_copy(bb, jj, e, dst_slot):
        return pltpu.make_async_copy(
            y_hbm.at[bb, e, pl.ds(window_start(bb, jj, e, SMALL_ROWS), SMALL_ROWS)],
            slab_ref.at[dst_slot, pl.ds(e * SMALL_ROWS, SMALL_ROWS)], sem.at[dst_slot])

    def finish(moe):
        x2 = x1_ref[...] + moe
        o_ref[...] = x2 * lax.rsqrt(jnp.mean(x2 * x2, axis=-1, keepdims=True) + EPS) * nf_ref[...]

    last = step + 1 == n_batch * n_tiles

    @pl.when(step == 0)
    def _():
        big_ref[...] = jnp.zeros(big_ref.shape, big_ref.dtype)
        for e in range(n_exp):
            small_copy(b, j, e, slot).start()

    for e in range(n_exp):
        small_copy(b, j, e, slot).wait()
    wrap = j + 1 == n_tiles
    nb = jnp.where(last, b, jnp.where(wrap, b + 1, b))
    nj = jnp.where(last, j, jnp.where(wrap, 0, j + 1))
    for e in range(n_exp):
        small_copy(nb, nj, e, 1 - slot).start()

    lane = lax.broadcasted_iota(I32, (1, SMALL_ROWS), 1)
    acc = None
    for e0 in range(0, n_exp, COMBINE_GROUP):
        tiles = []
        for e in range(e0, min(e0 + COMBINE_GROUP, n_exp)):
            rel = pos_ref[:, e:e + 1] - window_start(b, j, e, SMALL_ROWS)
            tiles.append(jnp.where(rel == lane, 1.0, 0.0).astype(BF16))
        rows = pl.ds(e0 * SMALL_ROWS, len(tiles) * SMALL_ROWS)
        part = jnp.dot(jnp.concatenate(tiles, axis=1), slab_ref[slot, rows, :],
                       preferred_element_type=F32)
        acc = part if acc is None else acc + part

    small_now = all_small(b, j)

    @pl.when(small_now)
    def _():
        finish(acc)

    @pl.when(jnp.logical_not(small_now))
    def _():
        wide_lane = lax.broadcasted_iota(I32, (1, BIG_PAD), 1)
        for e in range(n_exp):
            st = window_start(b, j, e, BIG_ROWS)
            cp = pltpu.make_async_copy(y_hbm.at[b, e, pl.ds(st, BIG_ROWS)],
                                       big_ref.at[pl.ds(0, BIG_ROWS)], big_sem)
            cp.start()
            cp.wait()
            onehot = jnp.where(pos_ref[:, e:e + 1] - st == wide_lane, 1.0, 0.0).astype(BF16)
            part = jnp.dot(onehot, big_ref[...], preferred_element_type=F32)
            if e == 0:
                acc_ref[...] = part
            else:
                acc_ref[...] += part
        finish(acc_ref[...])

    @pl.when(last)
    def _():
        for e in range(n_exp):
            small_copy(b, j, e, 1 - slot).wait()


def _combine(slo, x1, pos, y_slots, norm_final, seq):
    t, d = x1.shape
    batch, n_exp, cap, _ = y_slots.shape
    n_tiles = seq // ROUTE_TILE
    slo_flat = jnp.concatenate([slo, jnp.full((batch, 1, n_exp), cap, I32)], axis=1).reshape(-1)
    row = lambda b, j, slo: (b * n_tiles + j, 0)
    grid_spec = pltpu.PrefetchScalarGridSpec(
        num_scalar_prefetch=1,
        grid=(batch, n_tiles),
        in_specs=[
            pl.BlockSpec((ROUTE_TILE, d), row),
            pl.BlockSpec((ROUTE_TILE, n_exp), row),
            pl.BlockSpec(memory_space=pl.ANY),
            pl.BlockSpec((1, d), lambda b, j, slo: (0, 0)),
        ],
        out_specs=pl.BlockSpec((ROUTE_TILE, d), row),
        scratch_shapes=[pltpu.VMEM((2, n_exp * SMALL_ROWS, d), BF16), pltpu.VMEM((BIG_PAD, d), BF16),
                        pltpu.VMEM((ROUTE_TILE, d), F32),
                        pltpu.SemaphoreType.DMA((2,)), pltpu.SemaphoreType.DMA(())],
    )
    return pl.pallas_call(
        functools.partial(_combine_kernel, n_exp=n_exp, cap=cap),
        grid_spec=grid_spec,
        out_shape=jax.ShapeDtypeStruct((t, d), F32),
        compiler_params=_params(("arbitrary", "arbitrary"), V7X_VMEM_LIMIT_BYTES),
        name="combine",
    )(slo_flat, x1, pos, y_slots, norm_final)


def _rope_tables(seq):
    rows = seq // GRID_W
    row = np.repeat(np.arange(rows, dtype=np.float64), GRID_W)
    col = np.tile(np.arange(GRID_W, dtype=np.float64), rows)
    n_freq = HALF_ROT // 2
    freqs = ROPE_THETA ** (-np.arange(n_freq, dtype=np.float64) / n_freq)
    ang = np.concatenate([row[:, None] * freqs, col[:, None] * freqs], axis=-1)
    cos, sin = np.cos(ang), np.sin(ang)
    return (jnp.asarray(np.concatenate([cos, cos], axis=-1), F32),
            jnp.asarray(np.concatenate([-sin, sin], axis=-1), F32))


def kernel(x, norm_mix, w_in, ret_decay_fwd, ret_decay_bwd, ret_gn_gain, w_ret_branch, q_norm, k_norm,
           w_att_branch, w_o, norm_ffn, w_router, w_expert_gate, w_expert_up, w_expert_down, norm_final):
    batch, seq, d = x.shape
    depth = norm_mix.shape[0]
    n_exp = w_router.shape[2]
    cap = CAPACITY_FACTOR * seq // n_exp
    assert seq % TOKEN_TILE == 0 and seq % ROUTE_TILE == 0
    assert seq % ATT_KV_TILE == 0 and seq % ATT_Q_TILE == 0 and seq % CHUNK == 0
    assert LANES % n_exp == 0 and d % LANES == 0 and seq % (8 * LANES // n_exp) == 0
    assert cap >= BIG_ROWS and (cap - BIG_ROWS) % SLAB_ALIGN == 0 and (cap - SMALL_ROWS) % SLAB_ALIGN == 0
    assert w_expert_gate.shape[3] % FF_TILE == 0 and (batch * cap) % FFN_ROW_BLOCK == 0
    assert depth == 1, "the final RMSNorm is fused into the combine kernel of the single layer"
    l = 0
    cos2, sin2 = _rope_tables(seq)
    xf = x.reshape(batch * seq, d)
    lg = jnp.stack([jnp.log1p(-jnp.exp(ret_decay_fwd[l].astype(F32))),
                    jnp.log1p(-jnp.exp(ret_decay_bwd[l].astype(F32)))])
    rq, rk, rv, sg, aq, ak, av, gr, ga = _inproj(
        xf, norm_mix[l][None], w_in[l].astype(BF16), cos2, sin2, q_norm[l][None], k_norm[l][None], seq)
    u_ret = _retention(lg, rq, rk, rv, sg, ret_gn_gain[l][None], batch, seq)
    y_att = _attention(aq, ak, av, batch, seq)
    w_r = jnp.pad(w_router[l].astype(F32), ((0, 0), (0, LANES - n_exp)))
    w_r_hi = w_r.astype(BF16)
    w_r_lo = (w_r - w_r_hi.astype(F32)).astype(BF16)
    x1, h2a, aff = _merge(xf, u_ret, y_att, gr, ga, w_ret_branch[l].astype(BF16),
                          w_att_branch[l].astype(BF16), w_o[l].astype(BF16), norm_ffn[l][None],
                          jnp.concatenate([w_r_hi, w_r_lo], axis=1), n_exp)
    idx, pos, slo = _route(h2a, aff, batch, seq, cap)
    y_slots = _ffn(idx, h2a, w_expert_gate[l], w_expert_up[l], w_expert_down[l])
    out = _combine(slo, x1, pos, y_slots, norm_final[None], seq)
    return out.reshape(batch, seq, d)
```

```python
import functools

import jax
import jax.numpy as jnp
import numpy as np
from jax import lax
from jax.experimental import pallas as pl
from jax.experimental.pallas import tpu as pltpu

F32 = jnp.float32
BF16 = jnp.bfloat16
I32 = jnp.int32

GRID_W = 64
HEAD_DIM = 128
HALF_ROT = HEAD_DIM // 2
RET_HEADS = 4
RET_DIM = RET_HEADS * HEAD_DIM
ATT_HEADS = 4
ATT_KV_HEADS = 2
ATT_GROUP = ATT_HEADS // ATT_KV_HEADS
ATT_DIM = ATT_HEADS * HEAD_DIM
ATT_KV_DIM = ATT_KV_HEADS * HEAD_DIM
CHUNK = 256
ROPE_THETA = 10000.0
CAPACITY_FACTOR = 2
EPS = 1e-6
LOG2_E = 1.4426950408889634

V7X_VMEM_LIMIT_BYTES = 56 * 1024 * 1024
TOKEN_TILE = 512
ATT_Q_TILE = 1024
ATT_KV_TILE = 4096
RET_UNROLL = 4
LANES = 128
ROUTE_TILE = 512
SLAB_ALIGN = 16
SMALL_ROWS = LANES
BIG_ROWS = ROUTE_TILE + SLAB_ALIGN
BIG_PAD = 640
COMBINE_GROUP = 4
FF_TILE = 256
FFN_ROW_BLOCK = 512
BISECT_STEPS = 24


def _params(sem, vmem=None):
    return pltpu.CompilerParams(dimension_semantics=sem, vmem_limit_bytes=vmem)


def _inproj_kernel(x_ref, g_ref, w_ref, cos_ref, sin_ref, qn_ref, kn_ref,
                   rq_ref, rk_ref, rv_ref, sg_ref, aq_ref, ak_ref, av_ref, gr_ref, ga_ref):
    x = x_ref[...]
    h = (x * lax.rsqrt(jnp.mean(x * x, axis=-1, keepdims=True) + EPS) * g_ref[...]).astype(BF16)
    cos = cos_ref[...]
    sin = sin_ref[...]
    d_model = x.shape[1]
    scale = HEAD_DIM ** -0.5

    def proj(lo, width):
        return jnp.dot(h, w_ref[:, lo:lo + width].astype(BF16), preferred_element_type=F32)

    def rope(t):
        return t * cos + pltpu.roll(t, HALF_ROT, axis=1) * sin

    def head_norm(t, gain):
        return t * lax.rsqrt(jnp.mean(t * t, axis=-1, keepdims=True) + EPS) * gain

    def head(p, i):
        return p[:, i * HEAD_DIM:(i + 1) * HEAD_DIM]

    def put(ref, i, val):
        ref[:, i * HEAD_DIM:(i + 1) * HEAD_DIM] = val.astype(ref.dtype)

    off = 0
    p = proj(off, RET_DIM)
    for i in range(RET_HEADS):
        put(rq_ref, i, rope(head(p, i)))
    off += RET_DIM
    p = proj(off, RET_DIM)
    for i in range(RET_HEADS):
        put(rk_ref, i, rope(head(p, i)) * scale)
    off += RET_DIM
    rv_ref[...] = proj(off, RET_DIM).astype(BF16)
    off += RET_DIM
    p = proj(off, RET_DIM)
    sg_ref[...] = (p * jax.nn.sigmoid(p)).astype(BF16)
    off += RET_DIM
    p = proj(off, ATT_DIM)
    for i in range(ATT_HEADS):
        put(aq_ref, i, rope(head_norm(head(p, i), qn_ref[...])) * (scale * LOG2_E))
    off += ATT_DIM
    p = proj(off, ATT_KV_DIM)
    for i in range(ATT_KV_HEADS):
        put(ak_ref, i, rope(head_norm(head(p, i), kn_ref[...])))
    off += ATT_KV_DIM
    av_ref[...] = proj(off, ATT_KV_DIM).T.astype(BF16)
    off += ATT_KV_DIM
    gr_ref[...] = jax.nn.sigmoid(proj(off, d_model)).astype(BF16)
    off += d_model
    ga_ref[...] = jax.nn.sigmoid(proj(off, d_model)).astype(BF16)


def _inproj(xf, norm_g, w_in, cos2, sin2, q_norm, k_norm, seq):
    t, d = xf.shape
    tm = TOKEN_TILE
    in_width = w_in.shape[1]
    steps_per_seq = seq // tm
    row = lambda i: (i, 0)
    const = lambda i: (0, 0)
    pos = lambda i: (i % steps_per_seq, 0)
    widths = (RET_DIM, RET_DIM, RET_DIM, RET_DIM, ATT_DIM, ATT_KV_DIM, ATT_KV_DIM, d, d)
    av_index = 6
    return pl.pallas_call(
        _inproj_kernel,
        grid=(t // tm,),
        in_specs=[
            pl.BlockSpec((tm, d), row),
            pl.BlockSpec((1, d), const),
            pl.BlockSpec((d, in_width), const, pipeline_mode=pl.Buffered(1)),
            pl.BlockSpec((tm, HEAD_DIM), pos),
            pl.BlockSpec((tm, HEAD_DIM), pos),
            pl.BlockSpec((1, HEAD_DIM), const),
            pl.BlockSpec((1, HEAD_DIM), const),
        ],
        out_specs=[pl.BlockSpec((ATT_KV_DIM, tm), lambda i: (0, i)) if k == av_index
                   else pl.BlockSpec((tm, w), row) for k, w in enumerate(widths)],
        out_shape=[jax.ShapeDtypeStruct((ATT_KV_DIM, t) if k == av_index else (t, w), BF16)
                   for k, w in enumerate(widths)],
        compiler_params=_params(("parallel",), V7X_VMEM_LIMIT_BYTES),
        name="inproj",
    )(xf, norm_g, w_in, cos2, sin2, q_norm, k_norm)


def _retention_kernel(lg_ref, q_ref, k_ref, v_ref, sg_ref, gain_ref, o_ref, of_ref, ob_ref):
    hd = pl.program_id(1)
    lgf = lg_ref[0, hd]
    lgb = lg_ref[1, hd]
    n_chunks = q_ref.shape[0] // CHUNK
    ii = lax.broadcasted_iota(I32, (CHUNK, CHUNK), 0)
    jj = lax.broadcasted_iota(I32, (CHUNK, CHUNK), 1)
    dist = (ii - jj).astype(F32)
    dec_f = jnp.where(dist >= 0, jnp.exp(lgf * jnp.maximum(dist, 0.0)), 0.0)
    dec_b = jnp.where(dist < 0, jnp.exp(lgb * jnp.maximum(-dist, 0.0)), 0.0)
    t = lax.broadcasted_iota(I32, (CHUNK, 1), 0).astype(F32)
    kd_f = jnp.exp(lgf * (CHUNK - 1.0 - t))
    qd_f = jnp.exp(lgf * (t + 1.0))
    kd_b = jnp.exp(lgb * t)
    qd_b = jnp.exp(lgb * (CHUNK - t))
    zero_row = jnp.zeros((1, HEAD_DIM), F32)
    cd_f = jnp.exp(zero_row + lgf * CHUNK)
    cd_b = jnp.exp(zero_row + lgb * CHUNK)

    def one_chunk(n, dec, kd, qd, cd, state):
        r0 = pl.multiple_of(n * CHUNK, CHUNK)
        q = q_ref[pl.ds(r0, CHUNK), :]
        k = k_ref[pl.ds(r0, CHUNK), :]
        v = v_ref[pl.ds(r0, CHUNK), :]
        s = lax.dot_general(q, k, (((1,), (1,)), ((), ())), preferred_element_type=F32) * dec
        intra = jnp.dot(s.astype(BF16), v, preferred_element_type=F32)
        q_dec = (q.astype(F32) * qd).astype(BF16)
        inter = jnp.dot(q_dec, state.astype(BF16), preferred_element_type=F32)
        k_dec_t = (k.astype(F32) * kd).T.astype(BF16)
        new_state = state * cd + jnp.dot(k_dec_t, v, preferred_element_type=F32)
        return r0, intra + inter, new_state

    def scan_body(n, carry):
        sf, sb = carry
        r0, of, sf = one_chunk(n, dec_f, kd_f, qd_f, cd_f, sf)
        of_ref[pl.ds(r0, CHUNK), :] = of
        r0, ob, sb = one_chunk(n_chunks - 1 - n, dec_b, kd_b, qd_b, cd_b, sb)
        ob_ref[pl.ds(r0, CHUNK), :] = ob
        return sf, sb

    zeros = jnp.zeros((HEAD_DIM, HEAD_DIM), F32)
    lax.fori_loop(0, n_chunks, scan_body, (zeros, zeros), unroll=RET_UNROLL)

    rows = TOKEN_TILE

    def norm_body(n, carry):
        r0 = pl.multiple_of(n * rows, rows)
        o = of_ref[pl.ds(r0, rows), :] + ob_ref[pl.ds(r0, rows), :]
        mu = jnp.mean(o, axis=-1, keepdims=True)
        var = jnp.mean(jnp.square(o - mu), axis=-1, keepdims=True)
        yn = (o - mu) * lax.rsqrt(var + EPS) * gain_ref[...]
        o_ref[pl.ds(r0, rows), :] = (sg_ref[pl.ds(r0, rows), :].astype(F32) * yn).astype(BF16)
        return carry

    lax.fori_loop(0, q_ref.shape[0] // rows, norm_body, 0)


def _retention(lg, rq, rk, rv, sg, gn_gain, batch, seq):
    t = rq.shape[0]
    blk = pl.BlockSpec((seq, HEAD_DIM), lambda b, h: (b, h))
    return pl.pallas_call(
        _retention_kernel,
        grid=(batch, RET_HEADS),
        in_specs=[
            pl.BlockSpec(memory_space=pltpu.SMEM),
            blk, blk, blk, blk,
            pl.BlockSpec((1, HEAD_DIM), lambda b, h: (0, h)),
        ],
        out_specs=blk,
        out_shape=jax.ShapeDtypeStruct((t, RET_DIM), BF16),
        scratch_shapes=[pltpu.VMEM((seq, HEAD_DIM), F32), pltpu.VMEM((seq, HEAD_DIM), F32)],
        compiler_params=_params(("parallel", "parallel"), V7X_VMEM_LIMIT_BYTES),
        name="retention",
    )(lg, rq, rk, rv, sg, gn_gain)


def _attention_kernel(q_ref, k_ref, vt_ref, o_ref):
    tq = q_ref.shape[0]
    tk = ATT_KV_TILE
    n_kv = k_ref.shape[0] // tk
    qs = [q_ref[:, hh * HEAD_DIM:(hh + 1) * HEAD_DIM] for hh in range(ATT_GROUP)]

    def body(c, carry):
        r0 = pl.multiple_of(c * tk, tk)
        k = k_ref[pl.ds(r0, tk), :]
        vt = vt_ref[:, pl.ds(r0, tk)]
        out = []
        for q, (m, l, acc) in zip(qs, carry):
            s = lax.dot_general(k, q, (((1,), (1,)), ((), ())), preferred_element_type=F32)
            m_new = jnp.maximum(m, jnp.max(s, axis=0, keepdims=True))
            alpha = jnp.exp2(m - m_new)
            p = jnp.exp2(s - m_new)
            l = alpha * l + jnp.sum(p, axis=0, keepdims=True)
            acc = alpha * acc + jnp.dot(vt, p.astype(BF16), preferred_element_type=F32)
            out.append((m_new, l, acc))
        return tuple(out)

    init = (jnp.full((1, tq), -jnp.inf, F32), jnp.zeros((1, tq), F32), jnp.zeros((HEAD_DIM, tq), F32))
    final = lax.fori_loop(0, n_kv, body, (init,) * ATT_GROUP, unroll=True)
    for hh, (_, l, acc) in enumerate(final):
        o_ref[:, hh * HEAD_DIM:(hh + 1) * HEAD_DIM] = (acc / l).T.astype(BF16)


def _attention(aq, ak, av_t, batch, seq):
    t = aq.shape[0]
    tq = ATT_Q_TILE
    nq = seq // tq
    gw = ATT_GROUP * HEAD_DIM
    q_spec = pl.BlockSpec((tq, gw), lambda b, g, i: (b * nq + i, g))
    k_spec = pl.BlockSpec((seq, HEAD_DIM), lambda b, g, i: (b, g))
    vt_spec = pl.BlockSpec((HEAD_DIM, seq), lambda b, g, i: (g, b))
    return pl.pallas_call(
        _attention_kernel,
        grid=(batch, ATT_KV_HEADS, nq),
        in_specs=[q_spec, k_spec, vt_spec],
        out_specs=q_spec,
        out_shape=jax.ShapeDtypeStruct((t, ATT_DIM), BF16),
        compiler_params=_params(("parallel", "parallel", "parallel"), V7X_VMEM_LIMIT_BYTES),
        name="attention",
    )(aq, ak, av_t)


def _merge_kernel(x_ref, ur_ref, ya_ref, gr_ref, ga_ref, wr_ref, wa_ref, wo_ref, nf_ref, wrt_ref,
                  x1_ref, h2a_ref, aff_ref, *, n_exp):
    d = x_ref.shape[1]
    y_ret = jnp.dot(ur_ref[...], wr_ref[...].astype(BF16), preferred_element_type=F32)
    y_att = jnp.dot(ya_ref[...], wa_ref[...].astype(BF16), preferred_element_type=F32)
    mixed = gr_ref[...].astype(F32) * y_ret + ga_ref[...].astype(F32) * y_att
    x1 = x_ref[...] + jnp.dot(mixed.astype(BF16), wo_ref[...].astype(BF16), preferred_element_type=F32)
    x1_ref[...] = x1
    h2 = x1 * lax.rsqrt(jnp.mean(x1 * x1, axis=-1, keepdims=True) + EPS) * nf_ref[...]
    h2a_ref[:, :d] = h2
    h_hi = h2.astype(BF16)
    h_lo = (h2 - h_hi.astype(F32)).astype(BF16)
    r_hi = jnp.dot(h_hi, wrt_ref[...], preferred_element_type=F32)
    r_lo = jnp.dot(h_lo, wrt_ref[:, :LANES], preferred_element_type=F32)
    logits = r_hi[:, :LANES] + r_hi[:, LANES:] + r_lo
    lane = lax.broadcasted_iota(I32, logits.shape, 1)
    logits = jnp.where(lane < n_exp, logits, -jnp.inf)
    e = jnp.exp(logits - jnp.max(logits, axis=-1, keepdims=True))
    aff = e / jnp.sum(e, axis=-1, keepdims=True)
    h2a_ref[:, d:] = aff
    aff_ref[...] = aff[:, :n_exp]


def _merge(xf, u_ret, y_att, gr, ga, w_ret, w_att, w_o, norm_ffn, w_router_split, n_exp):
    t, d = xf.shape
    tm = TOKEN_TILE
    row = lambda i: (i, 0)
    const = lambda i: (0, 0)
    return pl.pallas_call(
        functools.partial(_merge_kernel, n_exp=n_exp),
        grid=(t // tm,),
        in_specs=[
            pl.BlockSpec((tm, d), row),
            pl.BlockSpec((tm, RET_DIM), row),
            pl.BlockSpec((tm, ATT_DIM), row),
            pl.BlockSpec((tm, d), row),
            pl.BlockSpec((tm, d), row),
            pl.BlockSpec((RET_DIM, d), const, pipeline_mode=pl.Buffered(1)),
            pl.BlockSpec((ATT_DIM, d), const, pipeline_mode=pl.Buffered(1)),
            pl.BlockSpec((d, d), const, pipeline_mode=pl.Buffered(1)),
            pl.BlockSpec((1, d), const),
            pl.BlockSpec((d, 2 * LANES), const),
        ],
        out_specs=[pl.BlockSpec((tm, d), row), pl.BlockSpec((tm, d + LANES), row),
                   pl.BlockSpec((tm, n_exp), row)],
        out_shape=[jax.ShapeDtypeStruct((t, d), F32), jax.ShapeDtypeStruct((t, d + LANES), F32),
                   jax.ShapeDtypeStruct((t, n_exp), F32)],
        compiler_params=_params(("parallel",), V7X_VMEM_LIMIT_BYTES),
        name="merge",
    )(xf, u_ret, y_att, gr, ga, w_ret, w_att, w_o, norm_ffn, w_router_split)


def _route_kernel(aff_ref, affc_ref, idx_ref, pos_ref, slo_ref, msk_ref, cum_ref, cum_t_ref, bnd_ref, bnd_smem,
                  cnt_ref, sem, *, cap, n_exp):
    b = pl.program_id(0)
    seq = aff_ref.shape[0]
    n_tiles = seq // ROUTE_TILE
    aff = aff_ref[...]
    affc = affc_ref[...]
    capf = float(cap)

    def per_expert(x, op, reduce):
        r = reduce(x.reshape(x.shape[0] // 8, 8, LANES), axis=0)
        shift = n_exp
        while shift < LANES:
            r = op(r, pltpu.roll(r, shift, axis=1))
            shift *= 2
        return reduce(r, axis=0, keepdims=True)

    def bisect(lo, hi):
        mid = 0.5 * (lo + hi)
        ok = per_expert(jnp.where(affc >= mid, 1.0, 0.0), jnp.add, jnp.sum) >= capf
        return jnp.where(ok, mid, lo), jnp.where(ok, hi, mid)

    def bracket(lo, hi):
        mn = per_expert(jnp.where(affc >= lo, affc, jnp.inf), jnp.minimum, jnp.min)
        mx = per_expert(jnp.where(affc < hi, affc, -jnp.inf), jnp.maximum, jnp.max)
        return mn, mx

    def not_isolated(lo, hi):
        mn, mx = bracket(lo, hi)
        return jnp.max(jnp.where(mn < mx, 1.0, 0.0)) > 0.0

    lo, hi = lax.fori_loop(0, BISECT_STEPS, lambda i, c: bisect(*c),
                           (jnp.zeros((1, LANES), F32), jnp.full((1, LANES), 2.0, F32)))

    def refine(c):
        lo, hi = bisect(c[0], c[1])
        return lo, hi, not_isolated(lo, hi)

    lo, hi, _ = lax.while_loop(lambda c: c[2], refine, (lo, hi, not_isolated(lo, hi)))
    thr, _ = bracket(lo, hi)
    gt = aff > thr
    eq = aff == thr
    n_gt = jnp.sum(jnp.where(gt, 1.0, 0.0).reshape(seq // 8, 8, LANES), axis=0)
    need = capf - jnp.sum(n_gt, axis=0, keepdims=True)

    tri = (lax.broadcasted_iota(I32, (ROUTE_TILE, ROUTE_TILE), 0)
           >= lax.broadcasted_iota(I32, (ROUTE_TILE, ROUTE_TILE), 1))
    tri = jnp.where(tri, 1.0, 0.0).astype(BF16)

    def cumsum_tokens(store_tile_start):
        def body(c, carry):
            r0 = pl.multiple_of(c * ROUTE_TILE, ROUTE_TILE)
            if store_tile_start:
                slo_ref[0, pl.ds(c, 1), :] = carry[:, :n_exp].astype(I32)
                bnd_ref[pl.ds(c, 1), :] = carry.astype(I32)
            cs = jnp.dot(tri, msk_ref[pl.ds(r0, ROUTE_TILE), :], preferred_element_type=F32) + carry
            cum_ref[pl.ds(r0, ROUTE_TILE), :] = cs
            return cs[ROUTE_TILE - 1:ROUTE_TILE, :]

        return lax.fori_loop(0, n_tiles, body, jnp.zeros((1, LANES), F32))

    msk_ref[...] = jnp.where(eq, 1.0, 0.0).astype(BF16)
    cumsum_tokens(False)
    take = jnp.logical_and(eq, cum_ref[...] - 1.0 < need)
    mask = jnp.logical_or(gt, take)
    msk_ref[...] = jnp.where(mask, 1.0, 0.0).astype(BF16)
    bnd_ref[...] = jnp.zeros(bnd_ref.shape, I32)
    bnd_ref[pl.ds(n_tiles, 1), :] = cumsum_tokens(True).astype(I32)
    pos_ref[...] = jnp.where(mask, cum_ref[...] - 1.0, -1.0)[:, :n_exp].astype(I32)

    to_smem = pltpu.make_async_copy(bnd_ref, bnd_smem, sem)
    to_smem.start()
    to_smem.wait()

    def transpose_tile(c, carry):
        r0 = pl.multiple_of(c * ROUTE_TILE, ROUTE_TILE)
        cum_t_ref[:, pl.ds(r0, ROUTE_TILE)] = cum_ref[pl.ds(r0, ROUTE_TILE), :].T
        return carry

    lax.fori_loop(0, n_tiles, transpose_tile, 0)
    slot_row = lax.broadcasted_iota(I32, (1, cap), 1).astype(F32)
    slot_sub = lax.broadcasted_iota(I32, (LANES, ROUTE_TILE), 0)
    eye = lax.broadcasted_iota(I32, (LANES, LANES), 0) == lax.broadcasted_iota(I32, (LANES, LANES), 1)
    for e in range(n_exp):
        ends = bnd_ref[pl.ds(1, n_tiles), e:e + 1].astype(F32)
        whole = jnp.sum(jnp.where(slot_row >= ends, float(ROUTE_TILE), 0.0), axis=0, keepdims=True)
        cnt_ref[...] = jnp.zeros(cnt_ref.shape, F32)

        def tile_body(c, carry, e=e):
            lo = bnd_smem[c, e]
            hi = bnd_smem[c + 1, e]
            r0 = pl.multiple_of(c * ROUTE_TILE, ROUTE_TILE)
            counts = cum_t_ref[e:e + 1, pl.ds(r0, ROUTE_TILE)]

            def lane_tile(k, carry2):
                slots = k * LANES + slot_sub
                below_hi = jnp.where(slots < hi, 1.0, 0.0)
                hit = jnp.where(counts <= slots.astype(F32), below_hi, 0.0)
                folded = hit[:, :LANES]
                for t0 in range(LANES, ROUTE_TILE, LANES):
                    folded = folded + hit[:, t0:t0 + LANES]
                cnt_ref[k] += folded
                return carry2

            lax.fori_loop(lo // LANES, (hi + LANES - 1) // LANES, lane_tile, 0)
            return carry

        lax.fori_loop(0, n_tiles, tile_body, 0)
        for k in range(cap // LANES):
            per_slot = jnp.sum(cnt_ref[k], axis=1, keepdims=True)
            part = jnp.sum(jnp.where(eye, per_slot, 0.0), axis=0, keepdims=True)
            part = part + whole[:, k * LANES:(k + 1) * LANES]
            idx_ref[0, e:e + 1, k * LANES:(k + 1) * LANES] = part.astype(I32) + b * seq


def _route(h2a, aff, batch, seq, cap):
    t, n_exp = aff.shape
    pack = LANES // n_exp
    aff_packed = aff.reshape(t // pack, LANES)
    aff_block = h2a.shape[1] // LANES - 1
    n_tiles = seq // ROUTE_TILE
    bnd_rows = -(-(n_tiles + 1) // 8) * 8
    return pl.pallas_call(
        functools.partial(_route_kernel, cap=cap, n_exp=n_exp),
        grid=(batch,),
        in_specs=[pl.BlockSpec((seq, LANES), lambda b: (b, aff_block)),
                  pl.BlockSpec((seq // pack, LANES), lambda b: (b, 0))],
        out_specs=[pl.BlockSpec((1, n_exp, cap), lambda b: (b, 0, 0)),
                   pl.BlockSpec((seq, n_exp), lambda b: (b, 0)),
                   pl.BlockSpec((1, n_tiles, n_exp), lambda b: (b, 0, 0))],
        out_shape=[jax.ShapeDtypeStruct((batch, n_exp, cap), I32),
                   jax.ShapeDtypeStruct((t, n_exp), I32),
                   jax.ShapeDtypeStruct((batch, n_tiles, n_exp), I32)],
        scratch_shapes=[pltpu.VMEM((seq, LANES), BF16), pltpu.VMEM((seq, LANES), F32),
                        pltpu.VMEM((LANES, seq), F32),
                        pltpu.VMEM((bnd_rows, LANES), I32), pltpu.SMEM((bnd_rows, LANES), I32),
                        pltpu.VMEM((cap // LANES, LANES, LANES), F32), pltpu.SemaphoreType.DMA(())],
        compiler_params=_params(("parallel",), V7X_VMEM_LIMIT_BYTES),
        name="route",
    )(h2a, aff_packed)


def _ffn_kernel(idx_ref, h2a_hbm, wg_hbm, wu_hbm, wd_hbm, y_ref, stage_ref, xe_ref, gate_ref, acc_ref,
                wg_buf, wu_buf, wd_buf, sem, wsem, *, n_exp, n_ff, rows_per_step):
    e = pl.program_id(0)
    mp = stage_ref.shape[0]
    m, d = xe_ref.shape
    batch = y_ref.shape[0]
    cap = y_ref.shape[2]

    def row_copy(expert, i):
        r = idx_ref[expert * mp + i]
        return pltpu.make_async_copy(h2a_hbm.at[pl.ds(r, 1)], stage_ref.at[pl.ds(i, 1)], sem)

    def wait_rows():
        pltpu.make_async_copy(h2a_hbm.at[pl.ds(0, mp)], stage_ref, sem).wait()

    def weight_copies(chunk, slot):
        ex = chunk // n_ff
        c0 = pl.multiple_of((chunk % n_ff) * FF_TILE, FF_TILE)
        return (pltpu.make_async_copy(wg_hbm.at[ex, :, pl.ds(c0, FF_TILE)], wg_buf.at[slot], wsem.at[slot]),
                pltpu.make_async_copy(wu_hbm.at[ex, :, pl.ds(c0, FF_TILE)], wu_buf.at[slot], wsem.at[slot]),
                pltpu.make_async_copy(wd_hbm.at[ex, pl.ds(c0, FF_TILE), :], wd_buf.at[slot], wsem.at[slot]))

    @pl.when(e == 0)
    def _():
        def issue(i, carry):
            row_copy(0, i).start()
            return carry

        lax.fori_loop(0, mp, issue, 0)
        for cp in weight_copies(0, 0):
            cp.start()

    wait_rows()
    xe_ref[...] = stage_ref[:m, :d].astype(BF16)
    lane = lax.broadcasted_iota(I32, (m, LANES), 1)
    gate_ref[...] = jnp.sum(jnp.where(lane == e, stage_ref[:m, d:], 0.0), axis=1, keepdims=True)
    acc_ref[...] = jnp.zeros(acc_ref.shape, F32)

    nxt = jnp.minimum(e + 1, n_exp - 1)

    def ff_tile(f, carry):
        chunk = e * n_ff + f
        slot = chunk % 2

        for cp in weight_copies(chunk, slot):
            cp.wait()

        @pl.when(chunk + 1 < n_exp * n_ff)
        def _():
            for cp in weight_copies(chunk + 1, 1 - slot):
                cp.start()

        for k in range(rows_per_step):
            row_copy(nxt, f * rows_per_step + k).start()
        wg = wg_buf[slot].astype(BF16)
        wu = wu_buf[slot].astype(BF16)
        wd = wd_buf[slot].astype(BF16)
        for rb in range(m // FFN_ROW_BLOCK):
            rows = pl.ds(rb * FFN_ROW_BLOCK, FFN_ROW_BLOCK)
            xb = xe_ref[rows, :]
            a = jnp.dot(xb, wg, preferred_element_type=F32)
            u = jnp.dot(xb, wu, preferred_element_type=F32)
            hm = (a * jax.nn.sigmoid(a) * u).astype(BF16)
            acc_ref[rows, :] += jnp.dot(hm, wd, preferred_element_type=F32)
        return carry

    lax.fori_loop(0, n_ff, ff_tile, 0)

    for bb in range(batch):
        rows = pl.ds(bb * cap, cap)
        y_ref[bb, 0] = (acc_ref[rows, :] * gate_ref[rows, :]).astype(BF16)

    @pl.when(e == n_exp - 1)
    def _():
        wait_rows()


def _ffn(idx, h2a, w_gate, w_up, w_down):
    batch, n_exp, cap = idx.shape
    d = h2a.shape[1] - LANES
    ff = w_gate.shape[2]
    n_ff = ff // FF_TILE
    m = batch * cap
    rows_per_step = -(-pl.cdiv(m, n_ff) // 8) * 8
    mp = n_ff * rows_per_step
    idx_flat = jnp.pad(idx.transpose(1, 0, 2).reshape(n_exp, m), ((0, 0), (0, mp - m))).reshape(-1)
    any_spec = pl.BlockSpec(memory_space=pl.ANY)
    grid_spec = pltpu.PrefetchScalarGridSpec(
        num_scalar_prefetch=1,
        grid=(n_exp,),
        in_specs=[any_spec, any_spec, any_spec, any_spec],
        out_specs=pl.BlockSpec((batch, 1, cap, d), lambda e, idx: (0, e, 0, 0)),
        scratch_shapes=[pltpu.VMEM((mp, d + LANES), F32), pltpu.VMEM((m, d), BF16),
                        pltpu.VMEM((m, 1), F32), pltpu.VMEM((m, d), F32),
                        pltpu.VMEM((2, d, FF_TILE), F32), pltpu.VMEM((2, d, FF_TILE), F32),
                        pltpu.VMEM((2, FF_TILE, d), F32),
                        pltpu.SemaphoreType.DMA(()), pltpu.SemaphoreType.DMA((2,))],
    )
    return pl.pallas_call(
        functools.partial(_ffn_kernel, n_exp=n_exp, n_ff=n_ff, rows_per_step=rows_per_step),
        grid_spec=grid_spec,
        out_shape=jax.ShapeDtypeStruct((batch, n_exp, cap, d), BF16),
        compiler_params=_params(("arbitrary",), V7X_VMEM_LIMIT_BYTES),
        name="ffn",
    )(idx_flat, h2a, w_gate, w_up, w_down)


def _window_fits(s_hi, start, rows):
    return s_hi - start <= rows


def _combine_kernel(slo_ref, x1_ref, pos_ref, y_hbm, nf_ref, o_ref, slab_ref, big_ref, acc_ref, sem, big_sem,
                    *, n_exp, cap):
    b = pl.program_id(0)
    j = pl.program_id(1)
    n_batch = pl.num_programs(0)
    n_tiles = pl.num_programs(1)
    step = b * n_tiles + j
    slot = step % 2

    def bounds(bb, jj, e):
        base = (bb * (n_tiles + 1) + jj) * n_exp + e
        return slo_ref[base], slo_ref[base + n_exp]

    def window_start(bb, jj, e, rows):
        s_lo, _ = bounds(bb, jj, e)
        return pl.multiple_of(jnp.minimum((s_lo // SLAB_ALIGN) * SLAB_ALIGN, cap - rows), SLAB_ALIGN)

    def all_small(bb, jj):
        ok = None
        for e in range(n_exp):
            fits = _window_fits(bounds(bb, jj, e)[1], window_start(bb, jj, e, SMALL_ROWS), SMALL_ROWS)
            ok = fits if ok is None else jnp.logical_and(ok, fits)
        return ok

    def small_copy(bb, jj, e, dst_slot):
        return pltpu.make_async_copy(
            y_hbm.at[bb, e, pl.ds(window_start(bb, jj, e, SMALL_ROWS), SMALL_ROWS)],
            slab_ref.at[dst_slot, pl.ds(e * SMALL_ROWS, SMALL_ROWS)], sem.at[dst_slot])

    def finish(moe):
        x2 = x1_ref[...] + moe
        o_ref[...] = x2 * lax.rsqrt(jnp.mean(x2 * x2, axis=-1, keepdims=True) + EPS) * nf_ref[...]

    last = step + 1 == n_batch * n_tiles

    @pl.when(step == 0)
    def _():
        big_ref[...] = jnp.zeros(big_ref.shape, big_ref.dtype)
        for e in range(n_exp):
            small_copy(b, j, e, slot).start()

    for e in range(n_exp):
        small_copy(b, j, e, slot).wait()
    wrap = j + 1 == n_tiles
    nb = jnp.where(last, b, jnp.where(wrap, b + 1, b))
    nj = jnp.where(last, j, jnp.where(wrap, 0, j + 1))
    for e in range(n_exp):
        small_copy(nb, nj, e, 1 - slot).start()

    lane = lax.broadcasted_iota(I32, (1, SMALL_ROWS), 1)
    acc = None
    for e0 in range(0, n_exp, COMBINE_GROUP):
        tiles = []
        for e in range(e0, min(e0 + COMBINE_GROUP, n_exp)):
            rel = pos_ref[:, e:e + 1] - window_start(b, j, e, SMALL_ROWS)
            tiles.append(jnp.where(rel == lane, 1.0, 0.0).astype(BF16))
        rows = pl.ds(e0 * SMALL_ROWS, len(tiles) * SMALL_ROWS)
        part = jnp.dot(jnp.concatenate(tiles, axis=1), slab_ref[slot, rows, :],
                       preferred_element_type=F32)
        acc = part if acc is None else acc + part

    small_now = all_small(b, j)

    @pl.when(small_now)
    def _():
        finish(acc)

    @pl.when(jnp.logical_not(small_now))
    def _():
        wide_lane = lax.broadcasted_iota(I32, (1, BIG_PAD), 1)
        for e in range(n_exp):
            st = window_start(b, j, e, BIG_ROWS)
            cp = pltpu.make_async_copy(y_hbm.at[b, e, pl.ds(st, BIG_ROWS)],
                                       big_ref.at[pl.ds(0, BIG_ROWS)], big_sem)
            cp.start()
            cp.wait()
            onehot = jnp.where(pos_ref[:, e:e + 1] - st == wide_lane, 1.0, 0.0).astype(BF16)
            part = jnp.dot(onehot, big_ref[...], preferred_element_type=F32)
            if e == 0:
                acc_ref[...] = part
            else:
                acc_ref[...] += part
        finish(acc_ref[...])

    @pl.when(last)
    def _():
        for e in range(n_exp):
            small_copy(b, j, e, 1 - slot).wait()


def _combine(slo, x1, pos, y_slots, norm_final, seq):
    t, d = x1.shape
    batch, n_exp, cap, _ = y_slots.shape
    n_tiles = seq // ROUTE_TILE
    slo_flat = jnp.concatenate([slo, jnp.full((batch, 1, n_exp), cap, I32)], axis=1).reshape(-1)
    row = lambda b, j, slo: (b * n_tiles + j, 0)
    grid_spec = pltpu.PrefetchScalarGridSpec(
        num_scalar_prefetch=1,
        grid=(batch, n_tiles),
        in_specs=[
            pl.BlockSpec((ROUTE_TILE, d), row),
            pl.BlockSpec((ROUTE_TILE, n_exp), row),
            pl.BlockSpec(memory_space=pl.ANY),
            pl.BlockSpec((1, d), lambda b, j, slo: (0, 0)),
        ],
        out_specs=pl.BlockSpec((ROUTE_TILE, d), row),
        scratch_shapes=[pltpu.VMEM((2, n_exp * SMALL_ROWS, d), BF16), pltpu.VMEM((BIG_PAD, d), BF16),
                        pltpu.VMEM((ROUTE_TILE, d), F32),
                        pltpu.SemaphoreType.DMA((2,)), pltpu.SemaphoreType.DMA(())],
    )
    return pl.pallas_call(
        functools.partial(_combine_kernel, n_exp=n_exp, cap=cap),
        grid_spec=grid_spec,
        out_shape=jax.ShapeDtypeStruct((t, d), F32),
        compiler_params=_params(("arbitrary", "arbitrary"), V7X_VMEM_LIMIT_BYTES),
        name="combine",
    )(slo_flat, x1, pos, y_slots, norm_final)


def _rope_tables(seq):
    rows = seq // GRID_W
    row = np.repeat(np.arange(rows, dtype=np.float64), GRID_W)
    col = np.tile(np.arange(GRID_W, dtype=np.float64), rows)
    n_freq = HALF_ROT // 2
    freqs = ROPE_THETA ** (-np.arange(n_freq, dtype=np.float64) / n_freq)
    ang = np.concatenate([row[:, None] * freqs, col[:, None] * freqs], axis=-1)
    cos, sin = np.cos(ang), np.sin(ang)
    return (jnp.asarray(np.concatenate([cos, cos], axis=-1), F32),
            jnp.asarray(np.concatenate([-sin, sin], axis=-1), F32))


def kernel(x, norm_mix, w_in, ret_decay_fwd, ret_decay_bwd, ret_gn_gain, w_ret_branch, q_norm, k_norm,
           w_att_branch, w_o, norm_ffn, w_router, w_expert_gate, w_expert_up, w_expert_down, norm_final):
    batch, seq, d = x.shape
    depth = norm_mix.shape[0]
    n_exp = w_router.shape[2]
    cap = CAPACITY_FACTOR * seq // n_exp
    assert seq % TOKEN_TILE == 0 and seq % ROUTE_TILE == 0
    assert seq % ATT_KV_TILE == 0 and seq % ATT_Q_TILE == 0 and seq % CHUNK == 0
    assert LANES % n_exp == 0 and d % LANES == 0 and seq % (8 * LANES // n_exp) == 0
    assert cap >= BIG_ROWS and (cap - BIG_ROWS) % SLAB_ALIGN == 0 and (cap - SMALL_ROWS) % SLAB_ALIGN == 0
    assert w_expert_gate.shape[3] % FF_TILE == 0 and (batch * cap) % FFN_ROW_BLOCK == 0
    assert depth == 1, "the final RMSNorm is fused into the combine kernel of the single layer"
    l = 0
    cos2, sin2 = _rope_tables(seq)
    xf = x.reshape(batch * seq, d)
    lg = jnp.stack([jnp.log1p(-jnp.exp(ret_decay_fwd[l].astype(F32))),
                    jnp.log1p(-jnp.exp(ret_decay_bwd[l].astype(F32)))])
    rq, rk, rv, sg, aq, ak, av, gr, ga = _inproj(
        xf, norm_mix[l][None], w_in[l], cos2, sin2, q_norm[l][None], k_norm[l][None], seq)
    u_ret = _retention(lg, rq, rk, rv, sg, ret_gn_gain[l][None], batch, seq)
    y_att = _attention(aq, ak, av, batch, seq)
    w_r = jnp.pad(w_router[l].astype(F32), ((0, 0), (0, LANES - n_exp)))
    w_r_hi = w_r.astype(BF16)
    w_r_lo = (w_r - w_r_hi.astype(F32)).astype(BF16)
    x1, h2a, aff = _merge(xf, u_ret, y_att, gr, ga, w_ret_branch[l], w_att_branch[l], w_o[l],
                          norm_ffn[l][None], jnp.concatenate([w_r_hi, w_r_lo], axis=1), n_exp)
    idx, pos, slo = _route(h2a, aff, batch, seq, cap)
    y_slots = _ffn(idx, h2a, w_expert_gate[l], w_expert_up[l], w_expert_down[l])
    out = _combine(slo, x1, pos, y_slots, norm_final[None], seq)
    return out.reshape(batch, seq, d)
```

```python
import functools

import jax
import jax.numpy as jnp
import numpy as np
from jax import lax
from jax.experimental import pallas as pl
from jax.experimental.pallas import tpu as pltpu

F32 = jnp.float32
BF16 = jnp.bfloat16
I32 = jnp.int32

GRID_W = 64
HEAD_DIM = 128
HALF_ROT = HEAD_DIM // 2
RET_HEADS = 4
RET_DIM = RET_HEADS * HEAD_DIM
ATT_HEADS = 4
ATT_KV_HEADS = 2
ATT_GROUP = ATT_HEADS // ATT_KV_HEADS
ATT_DIM = ATT_HEADS * HEAD_DIM
ATT_KV_DIM = ATT_KV_HEADS * HEAD_DIM
CHUNK = 256
ROPE_THETA = 10000.0
CAPACITY_FACTOR = 2
EPS = 1e-6
LOG2_E = 1.4426950408889634

V7X_VMEM_LIMIT_BYTES = 56 * 1024 * 1024
TOKEN_TILE = 512
ATT_Q_TILE = 1024
ATT_FIRST_KV = 512
ATT_KV_TILE = 3840
ATT_SHIFT_SLACK = 60.0
RET_UNROLL = 4
LANES = 128
ROUTE_TILE = 512
SLAB_ALIGN = 16
SMALL_ROWS = LANES
BIG_ROWS = ROUTE_TILE + SLAB_ALIGN
BIG_PAD = 640
COMBINE_GROUP = 4
FF_TILE = 256
FFN_ROW_BLOCK = 512
BISECT_STEPS = 24


def _params(sem, vmem=None):
    return pltpu.CompilerParams(dimension_semantics=sem, vmem_limit_bytes=vmem)


def _inproj_kernel(x_ref, g_ref, w_ref, cos_ref, sin_ref, qn_ref, kn_ref,
                   rq_ref, rk_ref, rv_ref, sg_ref, aq_ref, ak_ref, av_ref, gr_ref, ga_ref):
    x = x_ref[...]
    h = (x * lax.rsqrt(jnp.mean(x * x, axis=-1, keepdims=True) + EPS) * g_ref[...]).astype(BF16)
    cos = cos_ref[...]
    sin = sin_ref[...]
    d_model = x.shape[1]
    scale = HEAD_DIM ** -0.5

    def proj(lo, width):
        return jnp.dot(h, w_ref[:, lo:lo + width].astype(BF16), preferred_element_type=F32)

    def rope(t):
        return t * cos + pltpu.roll(t, HALF_ROT, axis=1) * sin

    def head_norm(t, gain):
        return t * lax.rsqrt(jnp.mean(t * t, axis=-1, keepdims=True) + EPS) * gain

    def head(p, i):
        return p[:, i * HEAD_DIM:(i + 1) * HEAD_DIM]

    def put(ref, i, val):
        ref[:, i * HEAD_DIM:(i + 1) * HEAD_DIM] = val.astype(ref.dtype)

    off = 0
    p = proj(off, RET_DIM)
    for i in range(RET_HEADS):
        put(rq_ref, i, rope(head(p, i)))
    off += RET_DIM
    p = proj(off, RET_DIM)
    for i in range(RET_HEADS):
        put(rk_ref, i, rope(head(p, i)) * scale)
    off += RET_DIM
    rv_ref[...] = proj(off, RET_DIM).astype(BF16)
    off += RET_DIM
    p = proj(off, RET_DIM)
    sg_ref[...] = (p * jax.nn.sigmoid(p)).astype(BF16)
    off += RET_DIM
    p = proj(off, ATT_DIM)
    for i in range(ATT_HEADS):
        put(aq_ref, i, rope(head_norm(head(p, i), qn_ref[...])) * (scale * LOG2_E))
    off += ATT_DIM
    p = proj(off, ATT_KV_DIM)
    ones_col = jnp.where(lax.broadcasted_iota(I32, (x.shape[0], HEAD_DIM), 1) == 0, 1.0, 0.0)
    for i in range(ATT_KV_HEADS):
        put(ak_ref, 2 * i, rope(head_norm(head(p, i), kn_ref[...])))
        put(ak_ref, 2 * i + 1, ones_col)
    off += ATT_KV_DIM
    av_ref[...] = proj(off, ATT_KV_DIM).T.astype(BF16)
    off += ATT_KV_DIM
    gr_ref[...] = jax.nn.sigmoid(proj(off, d_model)).astype(BF16)
    off += d_model
    ga_ref[...] = jax.nn.sigmoid(proj(off, d_model)).astype(BF16)


def _inproj(xf, norm_g, w_in, cos2, sin2, q_norm, k_norm, seq):
    t, d = xf.shape
    tm = TOKEN_TILE
    in_width = w_in.shape[1]
    steps_per_seq = seq // tm
    row = lambda i: (i, 0)
    const = lambda i: (0, 0)
    pos = lambda i: (i % steps_per_seq, 0)
    widths = (RET_DIM, RET_DIM, RET_DIM, RET_DIM, ATT_DIM, 2 * ATT_KV_DIM, ATT_KV_DIM, d, d)
    av_index = 6
    return pl.pallas_call(
        _inproj_kernel,
        grid=(t // tm,),
        in_specs=[
            pl.BlockSpec((tm, d), row),
            pl.BlockSpec((1, d), const),
            pl.BlockSpec((d, in_width), const, pipeline_mode=pl.Buffered(1)),
            pl.BlockSpec((tm, HEAD_DIM), pos),
            pl.BlockSpec((tm, HEAD_DIM), pos),
            pl.BlockSpec((1, HEAD_DIM), const),
            pl.BlockSpec((1, HEAD_DIM), const),
        ],
        out_specs=[pl.BlockSpec((ATT_KV_DIM, tm), lambda i: (0, i)) if k == av_index
                   else pl.BlockSpec((tm, w), row) for k, w in enumerate(widths)],
        out_shape=[jax.ShapeDtypeStruct((ATT_KV_DIM, t) if k == av_index else (t, w), BF16)
                   for k, w in enumerate(widths)],
        compiler_params=_params(("parallel",), V7X_VMEM_LIMIT_BYTES),
        name="inproj",
    )(xf, norm_g, w_in, cos2, sin2, q_norm, k_norm)


def _retention_kernel(lg_ref, q_ref, k_ref, v_ref, sg_ref, gain_ref, o_ref, of_ref, ob_ref):
    hd = pl.program_id(1)
    lgf = lg_ref[0, hd]
    lgb = lg_ref[1, hd]
    n_chunks = q_ref.shape[0] // CHUNK
    ii = lax.broadcasted_iota(I32, (CHUNK, CHUNK), 0)
    jj = lax.broadcasted_iota(I32, (CHUNK, CHUNK), 1)
    dist = (ii - jj).astype(F32)
    dec_f = jnp.where(dist >= 0, jnp.exp(lgf * jnp.maximum(dist, 0.0)), 0.0)
    dec_b = jnp.where(dist < 0, jnp.exp(lgb * jnp.maximum(-dist, 0.0)), 0.0)
    t = lax.broadcasted_iota(I32, (CHUNK, 1), 0).astype(F32)
    kd_f = jnp.exp(lgf * (CHUNK - 1.0 - t))
    qd_f = jnp.exp(lgf * (t + 1.0))
    kd_b = jnp.exp(lgb * t)
    qd_b = jnp.exp(lgb * (CHUNK - t))
    zero_row = jnp.zeros((1, HEAD_DIM), F32)
    cd_f = jnp.exp(zero_row + lgf * CHUNK)
    cd_b = jnp.exp(zero_row + lgb * CHUNK)

    def one_chunk(n, dec, kd, qd, cd, state):
        r0 = pl.multiple_of(n * CHUNK, CHUNK)
        q = q_ref[pl.ds(r0, CHUNK), :]
        k = k_ref[pl.ds(r0, CHUNK), :]
        v = v_ref[pl.ds(r0, CHUNK), :]
        s = lax.dot_general(q, k, (((1,), (1,)), ((), ())), preferred_element_type=F32) * dec
        intra = jnp.dot(s.astype(BF16), v, preferred_element_type=F32)
        q_dec = (q.astype(F32) * qd).astype(BF16)
        inter = jnp.dot(q_dec, state.astype(BF16), preferred_element_type=F32)
        k_dec_t = (k.astype(F32) * kd).T.astype(BF16)
        new_state = state * cd + jnp.dot(k_dec_t, v, preferred_element_type=F32)
        return r0, intra + inter, new_state

    def scan_body(n, carry):
        sf, sb = carry
        r0, of, sf = one_chunk(n, dec_f, kd_f, qd_f, cd_f, sf)
        of_ref[pl.ds(r0, CHUNK), :] = of
        r0, ob, sb = one_chunk(n_chunks - 1 - n, dec_b, kd_b, qd_b, cd_b, sb)
        ob_ref[pl.ds(r0, CHUNK), :] = ob
        return sf, sb

    zeros = jnp.zeros((HEAD_DIM, HEAD_DIM), F32)
    lax.fori_loop(0, n_chunks, scan_body, (zeros, zeros), unroll=RET_UNROLL)

    rows = TOKEN_TILE

    def norm_body(n, carry):
        r0 = pl.multiple_of(n * rows, rows)
        o = of_ref[pl.ds(r0, rows), :] + ob_ref[pl.ds(r0, rows), :]
        mu = jnp.mean(o, axis=-1, keepdims=True)
        var = jnp.mean(jnp.square(o - mu), axis=-1, keepdims=True)
        yn = (o - mu) * lax.rsqrt(var + EPS) * gain_ref[...]
        o_ref[pl.ds(r0, rows), :] = (sg_ref[pl.ds(r0, rows), :].astype(F32) * yn).astype(BF16)
        return carry

    lax.fori_loop(0, q_ref.shape[0] // rows, norm_body, 0)


def _retention(lg, rq, rk, rv, sg, gn_gain, batch, seq):
    t = rq.shape[0]
    blk = pl.BlockSpec((seq, HEAD_DIM), lambda b, h: (b, h))
    return pl.pallas_call(
        _retention_kernel,
        grid=(batch, RET_HEADS),
        in_specs=[
            pl.BlockSpec(memory_space=pltpu.SMEM),
            blk, blk, blk, blk,
            pl.BlockSpec((1, HEAD_DIM), lambda b, h: (0, h)),
        ],
        out_specs=blk,
        out_shape=jax.ShapeDtypeStruct((t, RET_DIM), BF16),
        scratch_shapes=[pltpu.VMEM((seq, HEAD_DIM), F32), pltpu.VMEM((seq, HEAD_DIM), F32)],
        compiler_params=_params(("parallel", "parallel"), V7X_VMEM_LIMIT_BYTES),
        name="retention",
    )(lg, rq, rk, rv, sg, gn_gain)


def _attention_kernel(q_ref, k_ref, vt_ref, o_ref, qa_ref):
    tq = q_ref.shape[0]
    n_rest = (k_ref.shape[0] - ATT_FIRST_KV) // ATT_KV_TILE
    lane = lax.broadcasted_iota(I32, (tq, 2 * HEAD_DIM), 1)

    def set_query(hh, shift):
        q = q_ref[:, hh * HEAD_DIM:(hh + 1) * HEAD_DIM]
        qa = jnp.concatenate([q, jnp.zeros((tq, HEAD_DIM), BF16)], axis=1)
        if shift is not None:
            qa = jnp.where(lane == HEAD_DIM, (-shift.T).astype(BF16), qa)
        qa_ref[hh] = qa

    def shifted_scores(hh, r0, rows):
        return lax.dot_general(k_ref[pl.ds(r0, rows), :], qa_ref[hh], (((1,), (1,)), ((), ())),
                               preferred_element_type=F32)

    def weigh(p, vt):
        return jnp.sum(p, axis=0, keepdims=True), jnp.dot(vt, p.astype(BF16), preferred_element_type=F32)

    vt = vt_ref[:, pl.ds(0, ATT_FIRST_KV)]
    state = []
    for hh in range(ATT_GROUP):
        set_query(hh, None)
        s = shifted_scores(hh, 0, ATT_FIRST_KV)
        shift = jnp.max(s, axis=0, keepdims=True).astype(BF16).astype(F32)
        l, acc = weigh(jnp.exp2(s - shift), vt)
        set_query(hh, shift)
        state.append((shift, l, acc))

    def body(c, carry):
        r0 = pl.multiple_of(ATT_FIRST_KV + c * ATT_KV_TILE, 2 * LANES)
        vt = vt_ref[:, pl.ds(r0, ATT_KV_TILE)]
        fast = []
        excess = None
        for hh, (shift, l, acc) in enumerate(carry):
            s = shifted_scores(hh, r0, ATT_KV_TILE)
            dl, dacc = weigh(jnp.exp2(s), vt)
            fast.append((shift, l + dl, acc + dacc))
            top = jnp.max(jnp.max(s, axis=0, keepdims=True))
            excess = top if excess is None else jnp.maximum(excess, top)

        def redo():
            out = []
            for hh, (shift, l, acc) in enumerate(carry):
                s = shifted_scores(hh, r0, ATT_KV_TILE)
                raised = (shift + jnp.maximum(jnp.max(s, axis=0, keepdims=True), 0.0)).astype(BF16).astype(F32)
                step = raised - shift
                alpha = jnp.exp2(-step)
                dl, dacc = weigh(jnp.exp2(s - step), vt)
                out.append((raised, alpha * l + dl, alpha * acc + dacc))
                set_query(hh, raised)
            return tuple(out)

        return lax.cond(excess <= ATT_SHIFT_SLACK, lambda: tuple(fast), redo)

    final = lax.fori_loop(0, n_rest, body, tuple(state))
    for hh, (_, l, acc) in enumerate(final):
        o_ref[:, hh * HEAD_DIM:(hh + 1) * HEAD_DIM] = (acc / l).T.astype(BF16)


def _attention(aq, ak, av_t, batch, seq):
    t = aq.shape[0]
    tq = ATT_Q_TILE
    nq = seq // tq
    gw = ATT_GROUP * HEAD_DIM
    q_spec = pl.BlockSpec((tq, gw), lambda b, g, i: (b * nq + i, g))
    k_spec = pl.BlockSpec((seq, 2 * HEAD_DIM), lambda b, g, i: (b, g))
    vt_spec = pl.BlockSpec((HEAD_DIM, seq), lambda b, g, i: (g, b))
    return pl.pallas_call(
        _attention_kernel,
        grid=(batch, ATT_KV_HEADS, nq),
        in_specs=[q_spec, k_spec, vt_spec],
        out_specs=q_spec,
        out_shape=jax.ShapeDtypeStruct((t, ATT_DIM), BF16),
        scratch_shapes=[pltpu.VMEM((ATT_GROUP, tq, 2 * HEAD_DIM), BF16)],
        compiler_params=_params(("parallel", "parallel", "parallel"), V7X_VMEM_LIMIT_BYTES),
        name="attention",
    )(aq, ak, av_t)


def _merge_kernel(x_ref, ur_ref, ya_ref, gr_ref, ga_ref, wr_ref, wa_ref, wo_ref, nf_ref, wrt_ref,
                  x1_ref, h2a_ref, aff_ref, *, n_exp):
    d = x_ref.shape[1]
    y_ret = jnp.dot(ur_ref[...], wr_ref[...].astype(BF16), preferred_element_type=F32)
    y_att = jnp.dot(ya_ref[...], wa_ref[...].astype(BF16), preferred_element_type=F32)
    mixed = gr_ref[...].astype(F32) * y_ret + ga_ref[...].astype(F32) * y_att
    x1 = x_ref[...] + jnp.dot(mixed.astype(BF16), wo_ref[...].astype(BF16), preferred_element_type=F32)
    x1_ref[...] = x1
    h2 = x1 * lax.rsqrt(jnp.mean(x1 * x1, axis=-1, keepdims=True) + EPS) * nf_ref[...]
    h2a_ref[:, :d] = h2
    h_hi = h2.astype(BF16)
    h_lo = (h2 - h_hi.astype(F32)).astype(BF16)
    r_hi = jnp.dot(h_hi, wrt_ref[...], preferred_element_type=F32)
    r_lo = jnp.dot(h_lo, wrt_ref[:, :LANES], preferred_element_type=F32)
    logits = r_hi[:, :LANES] + r_hi[:, LANES:] + r_lo
    lane = lax.broadcasted_iota(I32, logits.shape, 1)
    logits = jnp.where(lane < n_exp, logits, -jnp.inf)
    e = jnp.exp(logits - jnp.max(logits, axis=-1, keepdims=True))
    aff = e / jnp.sum(e, axis=-1, keepdims=True)
    h2a_ref[:, d:] = aff
    aff_ref[...] = aff[:, :n_exp]


def _merge(xf, u_ret, y_att, gr, ga, w_ret, w_att, w_o, norm_ffn, w_router_split, n_exp):
    t, d = xf.shape
    tm = TOKEN_TILE
    row = lambda i: (i, 0)
    const = lambda i: (0, 0)
    return pl.pallas_call(
        functools.partial(_merge_kernel, n_exp=n_exp),
        grid=(t // tm,),
        in_specs=[
            pl.BlockSpec((tm, d), row),
            pl.BlockSpec((tm, RET_DIM), row),
            pl.BlockSpec((tm, ATT_DIM), row),
            pl.BlockSpec((tm, d), row),
            pl.BlockSpec((tm, d), row),
            pl.BlockSpec((RET_DIM, d), const, pipeline_mode=pl.Buffered(1)),
            pl.BlockSpec((ATT_DIM, d), const, pipeline_mode=pl.Buffered(1)),
            pl.BlockSpec((d, d), const, pipeline_mode=pl.Buffered(1)),
            pl.BlockSpec((1, d), const),
            pl.BlockSpec((d, 2 * LANES), const),
        ],
        out_specs=[pl.BlockSpec((tm, d), row), pl.BlockSpec((tm, d + LANES), row),
                   pl.BlockSpec((tm, n_exp), row)],
        out_shape=[jax.ShapeDtypeStruct((t, d), F32), jax.ShapeDtypeStruct((t, d + LANES), F32),
                   jax.ShapeDtypeStruct((t, n_exp), F32)],
        compiler_params=_params(("parallel",), V7X_VMEM_LIMIT_BYTES),
        name="merge",
    )(xf, u_ret, y_att, gr, ga, w_ret, w_att, w_o, norm_ffn, w_router_split)


def _route_kernel(aff_ref, affc_ref, idx_ref, pos_ref, slo_ref, msk_ref, cum_ref, cum_t_ref, bnd_ref, bnd_smem,
                  cnt_ref, sem, *, cap, n_exp):
    b = pl.program_id(0)
    seq = aff_ref.shape[0]
    n_tiles = seq // ROUTE_TILE
    aff = aff_ref[...]
    affc = affc_ref[...]
    capf = float(cap)

    def per_expert(x, op, reduce):
        r = reduce(x.reshape(x.shape[0] // 8, 8, LANES), axis=0)
        shift = n_exp
        while shift < LANES:
            r = op(r, pltpu.roll(r, shift, axis=1))
            shift *= 2
        return reduce(r, axis=0, keepdims=True)

    def bisect(lo, hi):
        mid = 0.5 * (lo + hi)
        ok = per_expert(jnp.where(affc >= mid, 1.0, 0.0), jnp.add, jnp.sum) >= capf
        return jnp.where(ok, mid, lo), jnp.where(ok, hi, mid)

    def bracket(lo, hi):
        mn = per_expert(jnp.where(affc >= lo, affc, jnp.inf), jnp.minimum, jnp.min)
        mx = per_expert(jnp.where(affc < hi, affc, -jnp.inf), jnp.maximum, jnp.max)
        return mn, mx

    def not_isolated(lo, hi):
        mn, mx = bracket(lo, hi)
        return jnp.max(jnp.where(mn < mx, 1.0, 0.0)) > 0.0

    lo, hi = lax.fori_loop(0, BISECT_STEPS, lambda i, c: bisect(*c),
                           (jnp.zeros((1, LANES), F32), jnp.full((1, LANES), 2.0, F32)))

    def refine(c):
        lo, hi = bisect(c[0], c[1])
        return lo, hi, not_isolated(lo, hi)

    lo, hi, _ = lax.while_loop(lambda c: c[2], refine, (lo, hi, not_isolated(lo, hi)))
    thr, _ = bracket(lo, hi)
    gt = aff > thr
    eq = aff == thr
    n_gt = jnp.sum(jnp.where(gt, 1.0, 0.0).reshape(seq // 8, 8, LANES), axis=0)
    need = capf - jnp.sum(n_gt, axis=0, keepdims=True)

    tri = (lax.broadcasted_iota(I32, (ROUTE_TILE, ROUTE_TILE), 0)
           >= lax.broadcasted_iota(I32, (ROUTE_TILE, ROUTE_TILE), 1))
    tri = jnp.where(tri, 1.0, 0.0).astype(BF16)

    def cumsum_tokens(store_tile_start):
        def body(c, carry):
            r0 = pl.multiple_of(c * ROUTE_TILE, ROUTE_TILE)
            if store_tile_start:
                slo_ref[0, pl.ds(c, 1), :] = carry[:, :n_exp].astype(I32)
                bnd_ref[pl.ds(c, 1), :] = carry.astype(I32)
            cs = jnp.dot(tri, msk_ref[pl.ds(r0, ROUTE_TILE), :], preferred_element_type=F32) + carry
            cum_ref[pl.ds(r0, ROUTE_TILE), :] = cs
            return cs[ROUTE_TILE - 1:ROUTE_TILE, :]

        return lax.fori_loop(0, n_tiles, body, jnp.zeros((1, LANES), F32))

    msk_ref[...] = jnp.where(eq, 1.0, 0.0).astype(BF16)
    cumsum_tokens(False)
    take = jnp.logical_and(eq, cum_ref[...] - 1.0 < need)
    mask = jnp.logical_or(gt, take)
    msk_ref[...] = jnp.where(mask, 1.0, 0.0).astype(BF16)
    bnd_ref[...] = jnp.zeros(bnd_ref.shape, I32)
    bnd_ref[pl.ds(n_tiles, 1), :] = cumsum_tokens(True).astype(I32)
    pos_ref[...] = jnp.where(mask, cum_ref[...] - 1.0, -1.0)[:, :n_exp].astype(I32)

    to_smem = pltpu.make_async_copy(bnd_ref, bnd_smem, sem)
    to_smem.start()
    to_smem.wait()

    def transpose_tile(c, carry):
        r0 = pl.multiple_of(c * ROUTE_TILE, ROUTE_TILE)
        cum_t_ref[:, pl.ds(r0, ROUTE_TILE)] = cum_ref[pl.ds(r0, ROUTE_TILE), :].T
        return carry

    lax.fori_loop(0, n_tiles, transpose_tile, 0)
    slot_row = lax.broadcasted_iota(I32, (1, cap), 1).astype(F32)
    slot_sub = lax.broadcasted_iota(I32, (LANES, ROUTE_TILE), 0)
    eye = lax.broadcasted_iota(I32, (LANES, LANES), 0) == lax.broadcasted_iota(I32, (LANES, LANES), 1)
    for e in range(n_exp):
        ends = bnd_ref[pl.ds(1, n_tiles), e:e + 1].astype(F32)
        whole = jnp.sum(jnp.where(slot_row >= ends, float(ROUTE_TILE), 0.0), axis=0, keepdims=True)
        cnt_ref[...] = jnp.zeros(cnt_ref.shape, F32)

        def tile_body(c, carry, e=e):
            lo = bnd_smem[c, e]
            hi = bnd_smem[c + 1, e]
            r0 = pl.multiple_of(c * ROUTE_TILE, ROUTE_TILE)
            counts = cum_t_ref[e:e + 1, pl.ds(r0, ROUTE_TILE)]

            def lane_tile(k, carry2):
                slots = k * LANES + slot_sub
                below_hi = jnp.where(slots < hi, 1.0, 0.0)
                hit = jnp.where(counts <= slots.astype(F32), below_hi, 0.0)
                folded = hit[:, :LANES]
                for t0 in range(LANES, ROUTE_TILE, LANES):
                    folded = folded + hit[:, t0:t0 + LANES]
                cnt_ref[k] += folded
                return carry2

            lax.fori_loop(lo // LANES, (hi + LANES - 1) // LANES, lane_tile, 0)
            return carry

        lax.fori_loop(0, n_tiles, tile_body, 0)
        for k in range(cap // LANES):
            per_slot = jnp.sum(cnt_ref[k], axis=1, keepdims=True)
            part = jnp.sum(jnp.where(eye, per_slot, 0.0), axis=0, keepdims=True)
            part = part + whole[:, k * LANES:(k + 1) * LANES]
            idx_ref[0, e:e + 1, k * LANES:(k + 1) * LANES] = part.astype(I32) + b * seq


def _route(h2a, aff, batch, seq, cap):
    t, n_exp = aff.shape
    pack = LANES // n_exp
    aff_packed = aff.reshape(t // pack, LANES)
    aff_block = h2a.shape[1] // LANES - 1
    n_tiles = seq // ROUTE_TILE
    bnd_rows = -(-(n_tiles + 1) // 8) * 8
    return pl.pallas_call(
        functools.partial(_route_kernel, cap=cap, n_exp=n_exp),
        grid=(batch,),
        in_specs=[pl.BlockSpec((seq, LANES), lambda b: (b, aff_block)),
                  pl.BlockSpec((seq // pack, LANES), lambda b: (b, 0))],
        out_specs=[pl.BlockSpec((1, n_exp, cap), lambda b: (b, 0, 0)),
                   pl.BlockSpec((seq, n_exp), lambda b: (b, 0)),
                   pl.BlockSpec((1, n_tiles, n_exp), lambda b: (b, 0, 0))],
        out_shape=[jax.ShapeDtypeStruct((batch, n_exp, cap), I32),
                   jax.ShapeDtypeStruct((t, n_exp), I32),
                   jax.ShapeDtypeStruct((batch, n_tiles, n_exp), I32)],
        scratch_shapes=[pltpu.VMEM((seq, LANES), BF16), pltpu.VMEM((seq, LANES), F32),
                        pltpu.VMEM((LANES, seq), F32),
                        pltpu.VMEM((bnd_rows, LANES), I32), pltpu.SMEM((bnd_rows, LANES), I32),
                        pltpu.VMEM((cap // LANES, LANES, LANES), F32), pltpu.SemaphoreType.DMA(())],
        compiler_params=_params(("parallel",), V7X_VMEM_LIMIT_BYTES),
        name="route",
    )(h2a, aff_packed)


def _ffn_kernel(idx_ref, h2a_hbm, wg_hbm, wu_hbm, wd_hbm, y_ref, stage_ref, xe_ref, gate_ref, acc_ref,
                wg_buf, wu_buf, wd_buf, sem, wsem, *, n_exp, n_ff, rows_per_step):
    e = pl.program_id(0)
    mp = stage_ref.shape[0]
    m, d = xe_ref.shape
    batch = y_ref.shape[0]
    cap = y_ref.shape[2]

    def row_copy(expert, i):
        r = idx_ref[expert * mp + i]
        return pltpu.make_async_copy(h2a_hbm.at[pl.ds(r, 1)], stage_ref.at[pl.ds(i, 1)], sem)

    def wait_rows():
        pltpu.make_async_copy(h2a_hbm.at[pl.ds(0, mp)], stage_ref, sem).wait()

    def weight_copies(chunk, slot):
        ex = chunk // n_ff
        c0 = pl.multiple_of((chunk % n_ff) * FF_TILE, FF_TILE)
        return (pltpu.make_async_copy(wg_hbm.at[ex, :, pl.ds(c0, FF_TILE)], wg_buf.at[slot], wsem.at[slot]),
                pltpu.make_async_copy(wu_hbm.at[ex, :, pl.ds(c0, FF_TILE)], wu_buf.at[slot], wsem.at[slot]),
                pltpu.make_async_copy(wd_hbm.at[ex, pl.ds(c0, FF_TILE), :], wd_buf.at[slot], wsem.at[slot]))

    @pl.when(e == 0)
    def _():
        def issue(i, carry):
            row_copy(0, i).start()
            return carry

        lax.fori_loop(0, mp, issue, 0)
        for cp in weight_copies(0, 0):
            cp.start()

    wait_rows()
    xe_ref[...] = stage_ref[:m, :d].astype(BF16)
    lane = lax.broadcasted_iota(I32, (m, LANES), 1)
    gate_ref[...] = jnp.sum(jnp.where(lane == e, stage_ref[:m, d:], 0.0), axis=1, keepdims=True)
    acc_ref[...] = jnp.zeros(acc_ref.shape, F32)

    nxt = jnp.minimum(e + 1, n_exp - 1)

    def ff_tile(f, carry):
        chunk = e * n_ff + f
        slot = chunk % 2

        for cp in weight_copies(chunk, slot):
            cp.wait()

        @pl.when(chunk + 1 < n_exp * n_ff)
        def _():
            for cp in weight_copies(chunk + 1, 1 - slot):
                cp.start()

        for k in range(rows_per_step):
            row_copy(nxt, f * rows_per_step + k).start()
        wg = wg_buf[slot].astype(BF16)
        wu = wu_buf[slot].astype(BF16)
        wd = wd_buf[slot].astype(BF16)
        for rb in range(m // FFN_ROW_BLOCK):
            rows = pl.ds(rb * FFN_ROW_BLOCK, FFN_ROW_BLOCK)
            xb = xe_ref[rows, :]
            a = jnp.dot(xb, wg, preferred_element_type=F32)
            u = jnp.dot(xb, wu, preferred_element_type=F32)
            hm = (a * jax.nn.sigmoid(a) * u).astype(BF16)
            acc_ref[rows, :] += jnp.dot(hm, wd, preferred_element_type=F32)
        return carry

    lax.fori_loop(0, n_ff, ff_tile, 0)

    for bb in range(batch):
        rows = pl.ds(bb * cap, cap)
        y_ref[bb, 0] = (acc_ref[rows, :] * gate_ref[rows, :]).astype(BF16)

    @pl.when(e == n_exp - 1)
    def _():
        wait_rows()


def _ffn(idx, h2a, w_gate, w_up, w_down):
    batch, n_exp, cap = idx.shape
    d = h2a.shape[1] - LANES
    ff = w_gate.shape[2]
    n_ff = ff // FF_TILE
    m = batch * cap
    rows_per_step = -(-pl.cdiv(m, n_ff) // 8) * 8
    mp = n_ff * rows_per_step
    idx_flat = jnp.pad(idx.transpose(1, 0, 2).reshape(n_exp, m), ((0, 0), (0, mp - m))).reshape(-1)
    any_spec = pl.BlockSpec(memory_space=pl.ANY)
    grid_spec = pltpu.PrefetchScalarGridSpec(
        num_scalar_prefetch=1,
        grid=(n_exp,),
        in_specs=[any_spec, any_spec, any_spec, any_spec],
        out_specs=pl.BlockSpec((batch, 1, cap, d), lambda e, idx: (0, e, 0, 0)),
        scratch_shapes=[pltpu.VMEM((mp, d + LANES), F32), pltpu.VMEM((m, d), BF16),
                        pltpu.VMEM((m, 1), F32), pltpu.VMEM((m, d), F32),
                        pltpu.VMEM((2, d, FF_TILE), F32), pltpu.VMEM((2, d, FF_TILE), F32),
                        pltpu.VMEM((2, FF_TILE, d), F32),
                        pltpu.SemaphoreType.DMA(()), pltpu.SemaphoreType.DMA((2,))],
    )
    return pl.pallas_call(
        functools.partial(_ffn_kernel, n_exp=n_exp, n_ff=n_ff, rows_per_step=rows_per_step),
        grid_spec=grid_spec,
        out_shape=jax.ShapeDtypeStruct((batch, n_exp, cap, d), BF16),
        compiler_params=_params(("arbitrary",), V7X_VMEM_LIMIT_BYTES),
        name="ffn",
    )(idx_flat, h2a, w_gate, w_up, w_down)


def _window_fits(s_hi, start, rows):
    return s_hi - start <= rows


def _combine_kernel(slo_ref, x1_ref, pos_ref, y_hbm, nf_ref, o_ref, slab_ref, big_ref, acc_ref, sem, big_sem,
                    *, n_exp, cap):
    b = pl.program_id(0)
    j = pl.program_id(1)
    n_batch = pl.num_programs(0)
    n_tiles = pl.num_programs(1)
    step = b * n_tiles + j
    slot = step % 2

    def bounds(bb, jj, e):
        base = (bb * (n_tiles + 1) + jj) * n_exp + e
        return slo_ref[base], slo_ref[base + n_exp]

    def window_start(bb, jj, e, rows):
        s_lo, _ = bounds(bb, jj, e)
        return pl.multiple_of(jnp.minimum((s_lo // SLAB_ALIGN) * SLAB_ALIGN, cap - rows), SLAB_ALIGN)

    def all_small(bb, jj):
        ok = None
        for e in range(n_exp):
            fits = _window_fits(bounds(bb, jj, e)[1], window_start(bb, jj, e, SMALL_ROWS), SMALL_ROWS)
            ok = fits if ok is None else jnp.logical_and(ok, fits)
        return ok

    def small_copy(bb, jj, e, dst_slot):
        return pltpu.make_async_copy(
            y_hbm.at[bb, e, pl.ds(window_start(bb, jj, e, SMALL_ROWS), SMALL_ROWS)],
            slab_ref.at[dst_slot, pl.ds(e * SMALL_ROWS, SMALL_ROWS)], sem.at[dst_slot])

    def finish(moe):
        x2 = x1_ref[...] + moe
        o_ref[...] = x2 * lax.rsqrt(jnp.mean(x2 * x2, axis=-1, keepdims=True) + EPS) * nf_ref[...]

    last = step + 1 == n_batch * n_tiles

    @pl.when(step == 0)
    def _():
        big_ref[...] = jnp.zeros(big_ref.shape, big_ref.dtype)
        for e in range(n_exp):
            small_copy(b, j, e, slot).start()

    for e in range(n_exp):
        small_copy(b, j, e, slot).wait()
    wrap = j + 1 == n_tiles
    nb = jnp.where(last, b, jnp.where(wrap, b + 1, b))
    nj = jnp.where(last, j, jnp.where(wrap, 0, j + 1))
    for e in range(n_exp):
        small_copy(nb, nj, e, 1 - slot).start()

    lane = lax.broadcasted_iota(I32, (1, SMALL_ROWS), 1)
    acc = None
    for e0 in range(0, n_exp, COMBINE_GROUP):
        tiles = []
        for e in range(e0, min(e0 + COMBINE_GROUP, n_exp)):
            rel = pos_ref[:, e:e + 1] - window_start(b, j, e, SMALL_ROWS)
            tiles.append(jnp.where(rel == lane, 1.0, 0.0).astype(BF16))
        rows = pl.ds(e0 * SMALL_ROWS, len(tiles) * SMALL_ROWS)
        part = jnp.dot(jnp.concatenate(tiles, axis=1), slab_ref[slot, rows, :],
                       preferred_element_type=F32)
        acc = part if acc is None else acc + part

    small_now = all_small(b, j)

    @pl.when(small_now)
    def _():
        finish(acc)

    @pl.when(jnp.logical_not(small_now))
    def _():
        wide_lane = lax.broadcasted_iota(I32, (1, BIG_PAD), 1)
        for e in range(n_exp):
            st = window_start(b, j, e, BIG_ROWS)
            cp = pltpu.make_async_copy(y_hbm.at[b, e, pl.ds(st, BIG_ROWS)],
                                       big_ref.at[pl.ds(0, BIG_ROWS)], big_sem)
            cp.start()
            cp.wait()
            onehot = jnp.where(pos_ref[:, e:e + 1] - st == wide_lane, 1.0, 0.0).astype(BF16)
            part = jnp.dot(onehot, big_ref[...], preferred_element_type=F32)
            if e == 0:
                acc_ref[...] = part
            else:
                acc_ref[...] += part
        finish(acc_ref[...])

    @pl.when(last)
    def _():
        for e in range(n_exp):
            small_copy(b, j, e, 1 - slot).wait()


def _combine(slo, x1, pos, y_slots, norm_final, seq):
    t, d = x1.shape
    batch, n_exp, cap, _ = y_slots.shape
    n_tiles = seq // ROUTE_TILE
    slo_flat = jnp.concatenate([slo, jnp.full((batch, 1, n_exp), cap, I32)], axis=1).reshape(-1)
    row = lambda b, j, slo: (b * n_tiles + j, 0)
    grid_spec = pltpu.PrefetchScalarGridSpec(
        num_scalar_prefetch=1,
        grid=(batch, n_tiles),
        in_specs=[
            pl.BlockSpec((ROUTE_TILE, d), row),
            pl.BlockSpec((ROUTE_TILE, n_exp), row),
            pl.BlockSpec(memory_space=pl.ANY),
            pl.BlockSpec((1, d), lambda b, j, slo: (0, 0)),
        ],
        out_specs=pl.BlockSpec((ROUTE_TILE, d), row),
        scratch_shapes=[pltpu.VMEM((2, n_exp * SMALL_ROWS, d), BF16), pltpu.VMEM((BIG_PAD, d), BF16),
                        pltpu.VMEM((ROUTE_TILE, d), F32),
                        pltpu.SemaphoreType.DMA((2,)), pltpu.SemaphoreType.DMA(())],
    )
    return pl.pallas_call(
        functools.partial(_combine_kernel, n_exp=n_exp, cap=cap),
        grid_spec=grid_spec,
        out_shape=jax.ShapeDtypeStruct((t, d), F32),
        compiler_params=_params(("arbitrary", "arbitrary"), V7X_VMEM_LIMIT_BYTES),
        name="combine",
    )(slo_flat, x1, pos, y_slots, norm_final)


def _rope_tables(seq):
    rows = seq // GRID_W
    row = np.repeat(np.arange(rows, dtype=np.float64), GRID_W)
    col = np.tile(np.arange(GRID_W, dtype=np.float64), rows)
    n_freq = HALF_ROT // 2
    freqs = ROPE_THETA ** (-np.arange(n_freq, dtype=np.float64) / n_freq)
    ang = np.concatenate([row[:, None] * freqs, col[:, None] * freqs], axis=-1)
    cos, sin = np.cos(ang), np.sin(ang)
    return (jnp.asarray(np.concatenate([cos, cos], axis=-1), F32),
            jnp.asarray(np.concatenate([-sin, sin], axis=-1), F32))


def kernel(x, norm_mix, w_in, ret_decay_fwd, ret_decay_bwd, ret_gn_gain, w_ret_branch, q_norm, k_norm,
           w_att_branch, w_o, norm_ffn, w_router, w_expert_gate, w_expert_up, w_expert_down, norm_final):
    batch, seq, d = x.shape
    depth = norm_mix.shape[0]
    n_exp = w_router.shape[2]
    cap = CAPACITY_FACTOR * seq // n_exp
    assert seq % TOKEN_TILE == 0 and seq % ROUTE_TILE == 0
    assert (seq - ATT_FIRST_KV) % ATT_KV_TILE == 0 and ATT_KV_TILE % (2 * LANES) == 0
    assert ATT_FIRST_KV % (2 * LANES) == 0 and seq % ATT_Q_TILE == 0 and seq % CHUNK == 0
    assert LANES % n_exp == 0 and d % LANES == 0 and seq % (8 * LANES // n_exp) == 0
    assert cap >= BIG_ROWS and (cap - BIG_ROWS) % SLAB_ALIGN == 0 and (cap - SMALL_ROWS) % SLAB_ALIGN == 0
    assert w_expert_gate.shape[3] % FF_TILE == 0 and (batch * cap) % FFN_ROW_BLOCK == 0
    assert depth == 1, "the final RMSNorm is fused into the combine kernel of the single layer"
    l = 0
    cos2, sin2 = _rope_tables(seq)
    xf = x.reshape(batch * seq, d)
    lg = jnp.stack([jnp.log1p(-jnp.exp(ret_decay_fwd[l].astype(F32))),
                    jnp.log1p(-jnp.exp(ret_decay_bwd[l].astype(F32)))])
    rq, rk, rv, sg, aq, ak, av, gr, ga = _inproj(
        xf, norm_mix[l][None], w_in[l], cos2, sin2, q_norm[l][None], k_norm[l][None], seq)
    u_ret = _retention(lg, rq, rk, rv, sg, ret_gn_gain[l][None], batch, seq)
    y_att = _attention(aq, ak, av, batch, seq)
    w_r = jnp.pad(w_router[l].astype(F32), ((0, 0), (0, LANES - n_exp)))
    w_r_hi = w_r.astype(BF16)
    w_r_lo = (w_r - w_r_hi.astype(F32)).astype(BF16)
    x1, h2a, aff = _merge(xf, u_ret, y_att, gr, ga, w_ret_branch[l], w_att_branch[l], w_o[l],
                          norm_ffn[l][None], jnp.concatenate([w_r_hi, w_r_lo], axis=1), n_exp)
    idx, pos, slo = _route(h2a, aff, batch, seq, cap)
    y_slots = _ffn(idx, h2a, w_expert_gate[l], w_expert_up[l], w_expert_down[l])
    out = _combine(slo, x1, pos, y_slots, norm_final[None], seq)
    return out.reshape(batch, seq, d)
```

```python
import functools

import jax
import jax.numpy as jnp
import numpy as np
from jax import lax
from jax.experimental import pallas as pl
from jax.experimental.pallas import tpu as pltpu

F32 = jnp.float32
BF16 = jnp.bfloat16
I32 = jnp.int32

GRID_W = 64
HEAD_DIM = 128
HALF_ROT = HEAD_DIM // 2
RET_HEADS = 4
RET_DIM = RET_HEADS * HEAD_DIM
ATT_HEADS = 4
ATT_KV_HEADS = 2
ATT_GROUP = ATT_HEADS // ATT_KV_HEADS
ATT_DIM = ATT_HEADS * HEAD_DIM
ATT_KV_DIM = ATT_KV_HEADS * HEAD_DIM
CHUNK = 256
ROPE_THETA = 10000.0
CAPACITY_FACTOR = 2
EPS = 1e-6
LOG2_E = 1.4426950408889634

V7X_VMEM_LIMIT_BYTES = 56 * 1024 * 1024
TOKEN_TILE = 512
ATT_Q_TILE = 1024
ATT_FIRST_KV = 512
ATT_KV_TILE = 3840
RET_NORM_ROWS = 2048
RET_UNROLL = 4
LANES = 128
ROUTE_TILE = 512
SLAB_ALIGN = 16
SMALL_ROWS = LANES
BIG_ROWS = ROUTE_TILE + SLAB_ALIGN
BIG_PAD = 640
COMBINE_GROUP = 4
FF_TILE = 256
FFN_ROW_BLOCK = 512
BISECT_STEPS = 24


def _params(sem, vmem=None):
    return pltpu.CompilerParams(dimension_semantics=sem, vmem_limit_bytes=vmem)


def _inproj_kernel(x_ref, g_ref, w_ref, cos_ref, sin_ref, qn_ref, kn_ref,
                   rq_ref, rk_ref, rv_ref, sg_ref, aq_ref, ak_ref, av_ref, gr_ref, ga_ref):
    x = x_ref[...]
    h = (x * lax.rsqrt(jnp.mean(x * x, axis=-1, keepdims=True) + EPS) * g_ref[...]).astype(BF16)
    cos = cos_ref[...]
    sin = sin_ref[...]
    d_model = x.shape[1]
    scale = HEAD_DIM ** -0.5

    def proj(lo, width):
        return jnp.dot(h, w_ref[:, lo:lo + width].astype(BF16), preferred_element_type=F32)

    def rope(t):
        return t * cos + pltpu.roll(t, HALF_ROT, axis=1) * sin

    def head_norm(t, gain):
        return t * lax.rsqrt(jnp.mean(t * t, axis=-1, keepdims=True) + EPS) * gain

    def head(p, i):
        return p[:, i * HEAD_DIM:(i + 1) * HEAD_DIM]

    def put(ref, i, val):
        ref[:, i * HEAD_DIM:(i + 1) * HEAD_DIM] = val.astype(ref.dtype)

    off = 0
    p = proj(off, RET_DIM)
    for i in range(RET_HEADS):
        put(rq_ref, i, rope(head(p, i)))
    off += RET_DIM
    p = proj(off, RET_DIM)
    for i in range(RET_HEADS):
        put(rk_ref, i, rope(head(p, i)) * scale)
    off += RET_DIM
    rv_ref[...] = proj(off, RET_DIM).astype(BF16)
    off += RET_DIM
    p = proj(off, RET_DIM)
    sg_ref[...] = (p * jax.nn.sigmoid(p)).astype(BF16)
    off += RET_DIM
    p = proj(off, ATT_DIM)
    for i in range(ATT_HEADS):
        put(aq_ref, i, rope(head_norm(head(p, i), qn_ref[...])) * (scale * LOG2_E))
    off += ATT_DIM
    p = proj(off, ATT_KV_DIM)
    ones_col = jnp.where(lax.broadcasted_iota(I32, (x.shape[0], HEAD_DIM), 1) == 0, 1.0, 0.0)
    for i in range(ATT_KV_HEADS):
        put(ak_ref, 2 * i, rope(head_norm(head(p, i), kn_ref[...])))
        put(ak_ref, 2 * i + 1, ones_col)
    off += ATT_KV_DIM
    av_ref[...] = proj(off, ATT_KV_DIM).T.astype(BF16)
    off += ATT_KV_DIM
    gr_ref[...] = jax.nn.sigmoid(proj(off, d_model)).astype(BF16)
    off += d_model
    ga_ref[...] = jax.nn.sigmoid(proj(off, d_model)).astype(BF16)


def _inproj(xf, norm_g, w_in, cos2, sin2, q_norm, k_norm, seq):
    t, d = xf.shape
    tm = TOKEN_TILE
    in_width = w_in.shape[1]
    steps_per_seq = seq // tm
    row = lambda i: (i, 0)
    const = lambda i: (0, 0)
    pos = lambda i: (i % steps_per_seq, 0)
    widths = (RET_DIM, RET_DIM, RET_DIM, RET_DIM, ATT_DIM, 2 * ATT_KV_DIM, ATT_KV_DIM, d, d)
    av_index = 6
    return pl.pallas_call(
        _inproj_kernel,
        grid=(t // tm,),
        in_specs=[
            pl.BlockSpec((tm, d), row),
            pl.BlockSpec((1, d), const),
            pl.BlockSpec((d, in_width), const, pipeline_mode=pl.Buffered(1)),
            pl.BlockSpec((tm, HEAD_DIM), pos),
            pl.BlockSpec((tm, HEAD_DIM), pos),
            pl.BlockSpec((1, HEAD_DIM), const),
            pl.BlockSpec((1, HEAD_DIM), const),
        ],
        out_specs=[pl.BlockSpec((ATT_KV_DIM, tm), lambda i: (0, i)) if k == av_index
                   else pl.BlockSpec((tm, w), row) for k, w in enumerate(widths)],
        out_shape=[jax.ShapeDtypeStruct((ATT_KV_DIM, t) if k == av_index else (t, w), BF16)
                   for k, w in enumerate(widths)],
        compiler_params=_params(("parallel",), V7X_VMEM_LIMIT_BYTES),
        name="inproj",
    )(xf, norm_g, w_in, cos2, sin2, q_norm, k_norm)


def _retention_kernel(lg_ref, q_ref, k_ref, v_ref, sg_ref, gain_ref, o_ref, of_ref, ob_ref):
    hd = pl.program_id(1)
    lgf = lg_ref[0, hd]
    lgb = lg_ref[1, hd]
    n_chunks = q_ref.shape[0] // CHUNK
    ii = lax.broadcasted_iota(I32, (CHUNK, CHUNK), 0)
    jj = lax.broadcasted_iota(I32, (CHUNK, CHUNK), 1)
    dist = (ii - jj).astype(F32)
    dec_f = jnp.where(dist >= 0, jnp.exp(lgf * jnp.maximum(dist, 0.0)), 0.0)
    dec_b = jnp.where(dist < 0, jnp.exp(lgb * jnp.maximum(-dist, 0.0)), 0.0)
    t = lax.broadcasted_iota(I32, (CHUNK, 1), 0).astype(F32)
    kd_f = jnp.exp(lgf * (CHUNK - 1.0 - t))
    qd_f = jnp.exp(lgf * (t + 1.0))
    kd_b = jnp.exp(lgb * t)
    qd_b = jnp.exp(lgb * (CHUNK - t))
    zero_row = jnp.zeros((1, HEAD_DIM), F32)
    cd_f = jnp.exp(zero_row + lgf * CHUNK)
    cd_b = jnp.exp(zero_row + lgb * CHUNK)

    def one_chunk(n, dec, kd, qd, cd, state):
        r0 = pl.multiple_of(n * CHUNK, CHUNK)
        q = q_ref[pl.ds(r0, CHUNK), :]
        k = k_ref[pl.ds(r0, CHUNK), :]
        v = v_ref[pl.ds(r0, CHUNK), :]
        s = lax.dot_general(q, k, (((1,), (1,)), ((), ())), preferred_element_type=F32) * dec
        intra = jnp.dot(s.astype(BF16), v, preferred_element_type=F32)
        q_dec = (q.astype(F32) * qd).astype(BF16)
        inter = jnp.dot(q_dec, state.astype(BF16), preferred_element_type=F32)
        k_dec_t = (k.astype(F32) * kd).T.astype(BF16)
        new_state = state * cd + jnp.dot(k_dec_t, v, preferred_element_type=F32)
        return r0, intra + inter, new_state

    def scan_body(n, carry):
        sf, sb = carry
        r0, of, sf = one_chunk(n, dec_f, kd_f, qd_f, cd_f, sf)
        of_ref[pl.ds(r0, CHUNK), :] = of
        r0, ob, sb = one_chunk(n_chunks - 1 - n, dec_b, kd_b, qd_b, cd_b, sb)
        ob_ref[pl.ds(r0, CHUNK), :] = ob
        return sf, sb

    zeros = jnp.zeros((HEAD_DIM, HEAD_DIM), F32)
    lax.fori_loop(0, n_chunks, scan_body, (zeros, zeros), unroll=RET_UNROLL)

    rows = RET_NORM_ROWS

    def norm_body(n, carry):
        r0 = pl.multiple_of(n * rows, rows)
        o = of_ref[pl.ds(r0, rows), :] + ob_ref[pl.ds(r0, rows), :]
        mu = jnp.mean(o, axis=-1, keepdims=True)
        var = jnp.mean(jnp.square(o - mu), axis=-1, keepdims=True)
        yn = (o - mu) * lax.rsqrt(var + EPS) * gain_ref[...]
        o_ref[pl.ds(r0, rows), :] = (sg_ref[pl.ds(r0, rows), :].astype(F32) * yn).astype(BF16)
        return carry

    lax.fori_loop(0, q_ref.shape[0] // rows, norm_body, 0)


def _retention(lg, rq, rk, rv, sg, gn_gain, batch, seq):
    t = rq.shape[0]
    blk = pl.BlockSpec((seq, HEAD_DIM), lambda b, h: (b, h))
    return pl.pallas_call(
        _retention_kernel,
        grid=(batch, RET_HEADS),
        in_specs=[
            pl.BlockSpec(memory_space=pltpu.SMEM),
            blk, blk, blk, blk,
            pl.BlockSpec((1, HEAD_DIM), lambda b, h: (0, h)),
        ],
        out_specs=blk,
        out_shape=jax.ShapeDtypeStruct((t, RET_DIM), BF16),
        scratch_shapes=[pltpu.VMEM((seq, HEAD_DIM), F32), pltpu.VMEM((seq, HEAD_DIM), F32)],
        compiler_params=_params(("parallel", "parallel"), V7X_VMEM_LIMIT_BYTES),
        name="retention",
    )(lg, rq, rk, rv, sg, gn_gain)


def _attention_kernel(q_ref, k_ref, vt_ref, o_ref, qa_ref):
    tq = q_ref.shape[0]
    n_rest = (k_ref.shape[0] - ATT_FIRST_KV) // ATT_KV_TILE
    lane = lax.broadcasted_iota(I32, (tq, 2 * HEAD_DIM), 1)

    def set_query(hh, shift):
        q = q_ref[:, hh * HEAD_DIM:(hh + 1) * HEAD_DIM]
        qa = jnp.concatenate([q, jnp.zeros((tq, HEAD_DIM), BF16)], axis=1)
        if shift is not None:
            qa = jnp.where(lane == HEAD_DIM, (-shift.T).astype(BF16), qa)
        qa_ref[hh] = qa

    def shifted_scores(hh, r0, rows):
        return lax.dot_general(k_ref[pl.ds(r0, rows), :], qa_ref[hh], (((1,), (1,)), ((), ())),
                               preferred_element_type=F32)

    def weigh(p, vt):
        return jnp.sum(p, axis=0, keepdims=True), jnp.dot(vt, p.astype(BF16), preferred_element_type=F32)

    vt = vt_ref[:, pl.ds(0, ATT_FIRST_KV)]
    state = []
    for hh in range(ATT_GROUP):
        set_query(hh, None)
        s = shifted_scores(hh, 0, ATT_FIRST_KV)
        shift = jnp.max(s, axis=0, keepdims=True).astype(BF16).astype(F32)
        l, acc = weigh(jnp.exp2(s - shift), vt)
        set_query(hh, shift)
        state.append((shift, l, acc))

    def body(c, carry):
        r0 = pl.multiple_of(ATT_FIRST_KV + c * ATT_KV_TILE, 2 * LANES)
        vt = vt_ref[:, pl.ds(r0, ATT_KV_TILE)]
        fast = []
        overflow = None
        for hh, (shift, l, acc) in enumerate(carry):
            s = shifted_scores(hh, r0, ATT_KV_TILE)
            dl, dacc = weigh(jnp.exp2(s), vt)
            l_new = l + dl
            acc_new = acc + dacc
            fast.append((shift, l_new, acc_new))
            bad = jnp.maximum(jnp.max(jnp.where(jnp.isfinite(l_new), 0.0, 1.0)),
                              jnp.max(jnp.where(jnp.isfinite(acc_new), 0.0, 1.0)))
            overflow = bad if overflow is None else jnp.maximum(overflow, bad)

        def redo():
            out = []
            for hh, (shift, l, acc) in enumerate(carry):
                s = shifted_scores(hh, r0, ATT_KV_TILE)
                raised = (shift + jnp.maximum(jnp.max(s, axis=0, keepdims=True), 0.0)).astype(BF16).astype(F32)
                step = raised - shift
                alpha = jnp.exp2(-step)
                dl, dacc = weigh(jnp.exp2(s - step), vt)
                out.append((raised, alpha * l + dl, alpha * acc + dacc))
                set_query(hh, raised)
            return tuple(out)

        return lax.cond(overflow == 0.0, lambda: tuple(fast), redo)

    final = lax.fori_loop(0, n_rest, body, tuple(state))
    for hh, (_, l, acc) in enumerate(final):
        o_ref[:, hh * HEAD_DIM:(hh + 1) * HEAD_DIM] = (acc / l).T.astype(BF16)


def _attention(aq, ak, av_t, batch, seq):
    t = aq.shape[0]
    tq = ATT_Q_TILE
    nq = seq // tq
    gw = ATT_GROUP * HEAD_DIM
    q_spec = pl.BlockSpec((tq, gw), lambda b, g, i: (b * nq + i, g))
    k_spec = pl.BlockSpec((seq, 2 * HEAD_DIM), lambda b, g, i: (b, g))
    vt_spec = pl.BlockSpec((HEAD_DIM, seq), lambda b, g, i: (g, b))
    return pl.pallas_call(
        _attention_kernel,
        grid=(batch, ATT_KV_HEADS, nq),
        in_specs=[q_spec, k_spec, vt_spec],
        out_specs=q_spec,
        out_shape=jax.ShapeDtypeStruct((t, ATT_DIM), BF16),
        scratch_shapes=[pltpu.VMEM((ATT_GROUP, tq, 2 * HEAD_DIM), BF16)],
        compiler_params=_params(("parallel", "parallel", "parallel"), V7X_VMEM_LIMIT_BYTES),
        name="attention",
    )(aq, ak, av_t)


def _merge_kernel(x_ref, ur_ref, ya_ref, gr_ref, ga_ref, wr_ref, wa_ref, wo_ref, nf_ref, wrt_ref,
                  x1_ref, h2a_ref, aff_ref, *, n_exp):
    d = x_ref.shape[1]
    y_ret = jnp.dot(ur_ref[...], wr_ref[...].astype(BF16), preferred_element_type=F32)
    y_att = jnp.dot(ya_ref[...], wa_ref[...].astype(BF16), preferred_element_type=F32)
    mixed = gr_ref[...].astype(F32) * y_ret + ga_ref[...].astype(F32) * y_att
    x1 = x_ref[...] + jnp.dot(mixed.astype(BF16), wo_ref[...].astype(BF16), preferred_element_type=F32)
    x1_ref[...] = x1
    h2 = x1 * lax.rsqrt(jnp.mean(x1 * x1, axis=-1, keepdims=True) + EPS) * nf_ref[...]
    h2a_ref[:, :d] = h2
    h_hi = h2.astype(BF16)
    h_lo = (h2 - h_hi.astype(F32)).astype(BF16)
    r_hi = jnp.dot(h_hi, wrt_ref[...], preferred_element_type=F32)
    r_lo = jnp.dot(h_lo, wrt_ref[:, :LANES], preferred_element_type=F32)
    logits = r_hi[:, :LANES] + r_hi[:, LANES:] + r_lo
    lane = lax.broadcasted_iota(I32, logits.shape, 1)
    logits = jnp.where(lane < n_exp, logits, -jnp.inf)
    e = jnp.exp(logits - jnp.max(logits, axis=-1, keepdims=True))
    aff = e / jnp.sum(e, axis=-1, keepdims=True)
    h2a_ref[:, d:] = aff
    aff_ref[...] = aff[:, :n_exp]


def _merge(xf, u_ret, y_att, gr, ga, w_ret, w_att, w_o, norm_ffn, w_router_split, n_exp):
    t, d = xf.shape
    tm = TOKEN_TILE
    row = lambda i: (i, 0)
    const = lambda i: (0, 0)
    return pl.pallas_call(
        functools.partial(_merge_kernel, n_exp=n_exp),
        grid=(t // tm,),
        in_specs=[
            pl.BlockSpec((tm, d), row),
            pl.BlockSpec((tm, RET_DIM), row),
            pl.BlockSpec((tm, ATT_DIM), row),
            pl.BlockSpec((tm, d), row),
            pl.BlockSpec((tm, d), row),
            pl.BlockSpec((RET_DIM, d), const, pipeline_mode=pl.Buffered(1)),
            pl.BlockSpec((ATT_DIM, d), const, pipeline_mode=pl.Buffered(1)),
            pl.BlockSpec((d, d), const, pipeline_mode=pl.Buffered(1)),
            pl.BlockSpec((1, d), const),
            pl.BlockSpec((d, 2 * LANES), const),
        ],
        out_specs=[pl.BlockSpec((tm, d), row), pl.BlockSpec((tm, d + LANES), row),
                   pl.BlockSpec((tm, n_exp), row)],
        out_shape=[jax.ShapeDtypeStruct((t, d), F32), jax.ShapeDtypeStruct((t, d + LANES), F32),
                   jax.ShapeDtypeStruct((t, n_exp), F32)],
        compiler_params=_params(("parallel",), V7X_VMEM_LIMIT_BYTES),
        name="merge",
    )(xf, u_ret, y_att, gr, ga, w_ret, w_att, w_o, norm_ffn, w_router_split)


def _route_kernel(aff_ref, affc_ref, idx_ref, pos_ref, slo_ref, msk_ref, cum_ref, cum_t_ref, bnd_ref, bnd_smem,
                  cnt_ref, sem, *, cap, n_exp):
    b = pl.program_id(0)
    seq = aff_ref.shape[0]
    n_tiles = seq // ROUTE_TILE
    aff = aff_ref[...]
    affc = affc_ref[...]
    capf = float(cap)

    def per_expert(x, op, reduce):
        r = reduce(x.reshape(x.shape[0] // 8, 8, LANES), axis=0)
        shift = n_exp
        while shift < LANES:
            r = op(r, pltpu.roll(r, shift, axis=1))
            shift *= 2
        return reduce(r, axis=0, keepdims=True)

    def bisect(lo, hi):
        mid = 0.5 * (lo + hi)
        ok = per_expert(jnp.where(affc >= mid, 1.0, 0.0), jnp.add, jnp.sum) >= capf
        return jnp.where(ok, mid, lo), jnp.where(ok, hi, mid)

    def bracket(lo, hi):
        mn = per_expert(jnp.where(affc >= lo, affc, jnp.inf), jnp.minimum, jnp.min)
        mx = per_expert(jnp.where(affc < hi, affc, -jnp.inf), jnp.maximum, jnp.max)
        return mn, mx

    def not_isolated(lo, hi):
        mn, mx = bracket(lo, hi)
        return jnp.max(jnp.where(mn < mx, 1.0, 0.0)) > 0.0

    lo, hi = lax.fori_loop(0, BISECT_STEPS, lambda i, c: bisect(*c),
                           (jnp.zeros((1, LANES), F32), jnp.full((1, LANES), 2.0, F32)))

    def refine(c):
        lo, hi = bisect(c[0], c[1])
        return lo, hi, not_isolated(lo, hi)

    lo, hi, _ = lax.while_loop(lambda c: c[2], refine, (lo, hi, not_isolated(lo, hi)))
    thr, _ = bracket(lo, hi)
    gt = aff > thr
    eq = aff == thr
    n_gt = jnp.sum(jnp.where(gt, 1.0, 0.0).reshape(seq // 8, 8, LANES), axis=0)
    need = capf - jnp.sum(n_gt, axis=0, keepdims=True)

    tri = (lax.broadcasted_iota(I32, (ROUTE_TILE, ROUTE_TILE), 0)
           >= lax.broadcasted_iota(I32, (ROUTE_TILE, ROUTE_TILE), 1))
    tri = jnp.where(tri, 1.0, 0.0).astype(BF16)

    def cumsum_tokens(store_tile_start):
        def body(c, carry):
            r0 = pl.multiple_of(c * ROUTE_TILE, ROUTE_TILE)
            if store_tile_start:
                slo_ref[0, pl.ds(c, 1), :] = carry[:, :n_exp].astype(I32)
                bnd_ref[pl.ds(c, 1), :] = carry.astype(I32)
            cs = jnp.dot(tri, msk_ref[pl.ds(r0, ROUTE_TILE), :], preferred_element_type=F32) + carry
            cum_ref[pl.ds(r0, ROUTE_TILE), :] = cs
            return cs[ROUTE_TILE - 1:ROUTE_TILE, :]

        return lax.fori_loop(0, n_tiles, body, jnp.zeros((1, LANES), F32))

    msk_ref[...] = jnp.where(eq, 1.0, 0.0).astype(BF16)
    cumsum_tokens(False)
    take = jnp.logical_and(eq, cum_ref[...] - 1.0 < need)
    mask = jnp.logical_or(gt, take)
    msk_ref[...] = jnp.where(mask, 1.0, 0.0).astype(BF16)
    bnd_ref[...] = jnp.zeros(bnd_ref.shape, I32)
    bnd_ref[pl.ds(n_tiles, 1), :] = cumsum_tokens(True).astype(I32)
    pos_ref[...] = jnp.where(mask, cum_ref[...] - 1.0, -1.0)[:, :n_exp].astype(I32)

    to_smem = pltpu.make_async_copy(bnd_ref, bnd_smem, sem)
    to_smem.start()
    to_smem.wait()

    def transpose_tile(c, carry):
        r0 = pl.multiple_of(c * ROUTE_TILE, ROUTE_TILE)
        cum_t_ref[:, pl.ds(r0, ROUTE_TILE)] = cum_ref[pl.ds(r0, ROUTE_TILE), :].T
        return carry

    lax.fori_loop(0, n_tiles, transpose_tile, 0)
    slot_row = lax.broadcasted_iota(I32, (1, cap), 1).astype(F32)
    slot_sub = lax.broadcasted_iota(I32, (LANES, ROUTE_TILE), 0)
    eye = lax.broadcasted_iota(I32, (LANES, LANES), 0) == lax.broadcasted_iota(I32, (LANES, LANES), 1)
    for e in range(n_exp):
        ends = bnd_ref[pl.ds(1, n_tiles), e:e + 1].astype(F32)
        whole = jnp.sum(jnp.where(slot_row >= ends, float(ROUTE_TILE), 0.0), axis=0, keepdims=True)
        cnt_ref[...] = jnp.zeros(cnt_ref.shape, F32)

        def tile_body(c, carry, e=e):
            lo = bnd_smem[c, e]
            hi = bnd_smem[c + 1, e]
            r0 = pl.multiple_of(c * ROUTE_TILE, ROUTE_TILE)
            counts = cum_t_ref[e:e + 1, pl.ds(r0, ROUTE_TILE)]

            def lane_tile(k, carry2):
                slots = k * LANES + slot_sub
                below_hi = jnp.where(slots < hi, 1.0, 0.0)
                hit = jnp.where(counts <= slots.astype(F32), below_hi, 0.0)
                folded = hit[:, :LANES]
                for t0 in range(LANES, ROUTE_TILE, LANES):
                    folded = folded + hit[:, t0:t0 + LANES]
                cnt_ref[k] += folded
                return carry2

            lax.fori_loop(lo // LANES, (hi + LANES - 1) // LANES, lane_tile, 0)
            return carry

        lax.fori_loop(0, n_tiles, tile_body, 0)
        for k in range(cap // LANES):
            per_slot = jnp.sum(cnt_ref[k], axis=1, keepdims=True)
            part = jnp.sum(jnp.where(eye, per_slot, 0.0), axis=0, keepdims=True)
            part = part + whole[:, k * LANES:(k + 1) * LANES]
            idx_ref[0, e:e + 1, k * LANES:(k + 1) * LANES] = part.astype(I32) + b * seq


def _route(h2a, aff, batch, seq, cap):
    t, n_exp = aff.shape
    pack = LANES // n_exp
    aff_packed = aff.reshape(t // pack, LANES)
    aff_block = h2a.shape[1] // LANES - 1
    n_tiles = seq // ROUTE_TILE
    bnd_rows = -(-(n_tiles + 1) // 8) * 8
    return pl.pallas_call(
        functools.partial(_route_kernel, cap=cap, n_exp=n_exp),
        grid=(batch,),
        in_specs=[pl.BlockSpec((seq, LANES), lambda b: (b, aff_block)),
                  pl.BlockSpec((seq // pack, LANES), lambda b: (b, 0))],
        out_specs=[pl.BlockSpec((1, n_exp, cap), lambda b: (b, 0, 0)),
                   pl.BlockSpec((seq, n_exp), lambda b: (b, 0)),
                   pl.BlockSpec((1, n_tiles, n_exp), lambda b: (b, 0, 0))],
        out_shape=[jax.ShapeDtypeStruct((batch, n_exp, cap), I32),
                   jax.ShapeDtypeStruct((t, n_exp), I32),
                   jax.ShapeDtypeStruct((batch, n_tiles, n_exp), I32)],
        scratch_shapes=[pltpu.VMEM((seq, LANES), BF16), pltpu.VMEM((seq, LANES), F32),
                        pltpu.VMEM((LANES, seq), F32),
                        pltpu.VMEM((bnd_rows, LANES), I32), pltpu.SMEM((bnd_rows, LANES), I32),
                        pltpu.VMEM((cap // LANES, LANES, LANES), F32), pltpu.SemaphoreType.DMA(())],
        compiler_params=_params(("parallel",), V7X_VMEM_LIMIT_BYTES),
        name="route",
    )(h2a, aff_packed)


def _ffn_kernel(idx_ref, h2a_hbm, wg_hbm, wu_hbm, wd_hbm, y_ref, stage_ref, xe_ref, gate_ref, acc_ref,
                wg_buf, wu_buf, wd_buf, sem, wsem, *, n_exp, n_ff, rows_per_step):
    e = pl.program_id(0)
    mp = stage_ref.shape[0]
    m, d = xe_ref.shape
    batch = y_ref.shape[0]
    cap = y_ref.shape[2]

    def row_copy(expert, i):
        r = idx_ref[expert * mp + i]
        return pltpu.make_async_copy(h2a_hbm.at[pl.ds(r, 1)], stage_ref.at[pl.ds(i, 1)], sem)

    def wait_rows():
        pltpu.make_async_copy(h2a_hbm.at[pl.ds(0, mp)], stage_ref, sem).wait()

    def weight_copies(chunk, slot):
        ex = chunk // n_ff
        c0 = pl.multiple_of((chunk % n_ff) * FF_TILE, FF_TILE)
        return (pltpu.make_async_copy(wg_hbm.at[ex, :, pl.ds(c0, FF_TILE)], wg_buf.at[slot], wsem.at[slot]),
                pltpu.make_async_copy(wu_hbm.at[ex, :, pl.ds(c0, FF_TILE)], wu_buf.at[slot], wsem.at[slot]),
                pltpu.make_async_copy(wd_hbm.at[ex, pl.ds(c0, FF_TILE), :], wd_buf.at[slot], wsem.at[slot]))

    @pl.when(e == 0)
    def _():
        def issue(i, carry):
            row_copy(0, i).start()
            return carry

        lax.fori_loop(0, mp, issue, 0)
        for cp in weight_copies(0, 0):
            cp.start()

    wait_rows()
    xe_ref[...] = stage_ref[:m, :d].astype(BF16)
    lane = lax.broadcasted_iota(I32, (m, LANES), 1)
    gate_ref[...] = jnp.sum(jnp.where(lane == e, stage_ref[:m, d:], 0.0), axis=1, keepdims=True)
    acc_ref[...] = jnp.zeros(acc_ref.shape, F32)

    nxt = jnp.minimum(e + 1, n_exp - 1)

    def ff_tile(f, carry):
        chunk = e * n_ff + f
        slot = chunk % 2

        for cp in weight_copies(chunk, slot):
            cp.wait()

        @pl.when(chunk + 1 < n_exp * n_ff)
        def _():
            for cp in weight_copies(chunk + 1, 1 - slot):
                cp.start()

        for k in range(rows_per_step):
            row_copy(nxt, f * rows_per_step + k).start()
        wg = wg_buf[slot].astype(BF16)
        wu = wu_buf[slot].astype(BF16)
        wd = wd_buf[slot].astype(BF16)
        for rb in range(m // FFN_ROW_BLOCK):
            rows = pl.ds(rb * FFN_ROW_BLOCK, FFN_ROW_BLOCK)
            xb = xe_ref[rows, :]
            a = jnp.dot(xb, wg, preferred_element_type=F32)
            u = jnp.dot(xb, wu, preferred_element_type=F32)
            hm = (a * jax.nn.sigmoid(a) * u).astype(BF16)
            acc_ref[rows, :] += jnp.dot(hm, wd, preferred_element_type=F32)
        return carry

    lax.fori_loop(0, n_ff, ff_tile, 0)

    for bb in range(batch):
        rows = pl.ds(bb * cap, cap)
        y_ref[bb, 0] = (acc_ref[rows, :] * gate_ref[rows, :]).astype(BF16)

    @pl.when(e == n_exp - 1)
    def _():
        wait_rows()


def _ffn(idx, h2a, w_gate, w_up, w_down):
    batch, n_exp, cap = idx.shape
    d = h2a.shape[1] - LANES
    ff = w_gate.shape[2]
    n_ff = ff // FF_TILE
    m = batch * cap
    rows_per_step = -(-pl.cdiv(m, n_ff) // 8) * 8
    mp = n_ff * rows_per_step
    idx_flat = jnp.pad(idx.transpose(1, 0, 2).reshape(n_exp, m), ((0, 0), (0, mp - m))).reshape(-1)
    any_spec = pl.BlockSpec(memory_space=pl.ANY)
    grid_spec = pltpu.PrefetchScalarGridSpec(
        num_scalar_prefetch=1,
        grid=(n_exp,),
        in_specs=[any_spec, any_spec, any_spec, any_spec],
        out_specs=pl.BlockSpec((batch, 1, cap, d), lambda e, idx: (0, e, 0, 0)),
        scratch_shapes=[pltpu.VMEM((mp, d + LANES), F32), pltpu.VMEM((m, d), BF16),
                        pltpu.VMEM((m, 1), F32), pltpu.VMEM((m, d), F32),
                        pltpu.VMEM((2, d, FF_TILE), F32), pltpu.VMEM((2, d, FF_TILE), F32),
                        pltpu.VMEM((2, FF_TILE, d), F32),
                        pltpu.SemaphoreType.DMA(()), pltpu.SemaphoreType.DMA((2,))],
    )
    return pl.pallas_call(
        functools.partial(_ffn_kernel, n_exp=n_exp, n_ff=n_ff, rows_per_step=rows_per_step),
        grid_spec=grid_spec,
        out_shape=jax.ShapeDtypeStruct((batch, n_exp, cap, d), BF16),
        compiler_params=_params(("arbitrary",), V7X_VMEM_LIMIT_BYTES),
        name="ffn",
    )(idx_flat, h2a, w_gate, w_up, w_down)


def _window_fits(s_hi, start, rows):
    return s_hi - start <= rows


def _combine_kernel(slo_ref, x1_ref, pos_ref, y_hbm, nf_ref, o_ref, slab_ref, big_ref, acc_ref, sem, big_sem,
                    *, n_exp, cap):
    b = pl.program_id(0)
    j = pl.program_id(1)
    n_batch = pl.num_programs(0)
    n_tiles = pl.num_programs(1)
    step = b * n_tiles + j
    slot = step % 2

    def bounds(bb, jj, e):
        base = (bb * (n_tiles + 1) + jj) * n_exp + e
        return slo_ref[base], slo_ref[base + n_exp]

    def window_start(bb, jj, e, rows):
        s_lo, _ = bounds(bb, jj, e)
        return pl.multiple_of(jnp.minimum((s_lo // SLAB_ALIGN) * SLAB_ALIGN, cap - rows), SLAB_ALIGN)

    def all_small(bb, jj):
        ok = None
        for e in range(n_exp):
            fits = _window_fits(bounds(bb, jj, e)[1], window_start(bb, jj, e, SMALL_ROWS), SMALL_ROWS)
            ok = fits if ok is None else jnp.logical_and(ok, fits)
        return ok

    def small_copy(bb, jj, e, dst_slot):
        return pltpu.make_async_copy(
            y_hbm.at[bb, e, pl.ds(window_start(bb, jj, e, SMALL_ROWS), SMALL_ROWS)],
            slab_ref.at[dst_slot, pl.ds(e * SMALL_ROWS, SMALL_ROWS)], sem.at[dst_slot])

    def finish(moe):
        x2 = x1_ref[...] + moe
        o_ref[...] = x2 * lax.rsqrt(jnp.mean(x2 * x2, axis=-1, keepdims=True) + EPS) * nf_ref[...]

    last = step + 1 == n_batch * n_tiles

    @pl.when(step == 0)
    def _():
        big_ref[...] = jnp.zeros(big_ref.shape, big_ref.dtype)
        for e in range(n_exp):
            small_copy(b, j, e, slot).start()

    for e in range(n_exp):
        small_copy(b, j, e, slot).wait()
    wrap = j + 1 == n_tiles
    nb = jnp.where(last, b, jnp.where(wrap, b + 1, b))
    nj = jnp.where(last, j, jnp.where(wrap, 0, j + 1))
    for e in range(n_exp):
        small_copy(nb, nj, e, 1 - slot).start()

    lane = lax.broadcasted_iota(I32, (1, SMALL_ROWS), 1)
    acc = None
    for e0 in range(0, n_exp, COMBINE_GROUP):
        tiles = []
        for e in range(e0, min(e0 + COMBINE_GROUP, n_exp)):
            rel = pos_ref[:, e:e + 1] - window_start(b, j, e, SMALL_ROWS)
            tiles.append(jnp.where(rel == lane, 1.0, 0.0).astype(BF16))
        rows = pl.ds(e0 * SMALL_ROWS, len(tiles) * SMALL_ROWS)
        part = jnp.dot(jnp.concatenate(tiles, axis=1), slab_ref[slot, rows, :],
                       preferred_element_type=F32)
        acc = part if acc is None else acc + part

    small_now = all_small(b, j)

    @pl.when(small_now)
    def _():
        finish(acc)

    @pl.when(jnp.logical_not(small_now))
    def _():
        wide_lane = lax.broadcasted_iota(I32, (1, BIG_PAD), 1)
        for e in range(n_exp):
            st = window_start(b, j, e, BIG_ROWS)
            cp = pltpu.make_async_copy(y_hbm.at[b, e, pl.ds(st, BIG_ROWS)],
                                       big_ref.at[pl.ds(0, BIG_ROWS)], big_sem)
            cp.start()
            cp.wait()
            onehot = jnp.where(pos_ref[:, e:e + 1] - st == wide_lane, 1.0, 0.0).astype(BF16)
            part = jnp.dot(onehot, big_ref[...], preferred_element_type=F32)
            if e == 0:
                acc_ref[...] = part
            else:
                acc_ref[...] += part
        finish(acc_ref[...])

    @pl.when(last)
    def _():
        for e in range(n_exp):
            small_copy(b, j, e, 1 - slot).wait()


def _combine(slo, x1, pos, y_slots, norm_final, seq):
    t, d = x1.shape
    batch, n_exp, cap, _ = y_slots.shape
    n_tiles = seq // ROUTE_TILE
    slo_flat = jnp.concatenate([slo, jnp.full((batch, 1, n_exp), cap, I32)], axis=1).reshape(-1)
    row = lambda b, j, slo: (b * n_tiles + j, 0)
    grid_spec = pltpu.PrefetchScalarGridSpec(
        num_scalar_prefetch=1,
        grid=(batch, n_tiles),
        in_specs=[
            pl.BlockSpec((ROUTE_TILE, d), row),
            pl.BlockSpec((ROUTE_TILE, n_exp), row),
            pl.BlockSpec(memory_space=pl.ANY),
            pl.BlockSpec((1, d), lambda b, j, slo: (0, 0)),
        ],
        out_specs=pl.BlockSpec((ROUTE_TILE, d), row),
        scratch_shapes=[pltpu.VMEM((2, n_exp * SMALL_ROWS, d), BF16), pltpu.VMEM((BIG_PAD, d), BF16),
                        pltpu.VMEM((ROUTE_TILE, d), F32),
                        pltpu.SemaphoreType.DMA((2,)), pltpu.SemaphoreType.DMA(())],
    )
    return pl.pallas_call(
        functools.partial(_combine_kernel, n_exp=n_exp, cap=cap),
        grid_spec=grid_spec,
        out_shape=jax.ShapeDtypeStruct((t, d), F32),
        compiler_params=_params(("arbitrary", "arbitrary"), V7X_VMEM_LIMIT_BYTES),
        name="combine",
    )(slo_flat, x1, pos, y_slots, norm_final)


def _rope_tables(seq):
    rows = seq // GRID_W
    row = np.repeat(np.arange(rows, dtype=np.float64), GRID_W)
    col = np.tile(np.arange(GRID_W, dtype=np.float64), rows)
    n_freq = HALF_ROT // 2
    freqs = ROPE_THETA ** (-np.arange(n_freq, dtype=np.float64) / n_freq)
    ang = np.concatenate([row[:, None] * freqs, col[:, None] * freqs], axis=-1)
    cos, sin = np.cos(ang), np.sin(ang)
    return (jnp.asarray(np.concatenate([cos, cos], axis=-1), F32),
            jnp.asarray(np.concatenate([-sin, sin], axis=-1), F32))


def kernel(x, norm_mix, w_in, ret_decay_fwd, ret_decay_bwd, ret_gn_gain, w_ret_branch, q_norm, k_norm,
           w_att_branch, w_o, norm_ffn, w_router, w_expert_gate, w_expert_up, w_expert_down, norm_final):
    batch, seq, d = x.shape
    depth = norm_mix.shape[0]
    n_exp = w_router.shape[2]
    cap = CAPACITY_FACTOR * seq // n_exp
    assert seq % TOKEN_TILE == 0 and seq % ROUTE_TILE == 0
    assert (seq - ATT_FIRST_KV) % ATT_KV_TILE == 0 and ATT_KV_TILE % (2 * LANES) == 0
    assert ATT_FIRST_KV % (2 * LANES) == 0 and seq % ATT_Q_TILE == 0
    assert seq % CHUNK == 0 and (seq // CHUNK) % RET_UNROLL == 0 and seq % RET_NORM_ROWS == 0
    assert LANES % n_exp == 0 and d % LANES == 0 and seq % (8 * LANES // n_exp) == 0
    assert cap >= BIG_ROWS and (cap - BIG_ROWS) % SLAB_ALIGN == 0 and (cap - SMALL_ROWS) % SLAB_ALIGN == 0
    assert w_expert_gate.shape[3] % FF_TILE == 0 and (batch * cap) % FFN_ROW_BLOCK == 0
    assert depth == 1, "the final RMSNorm is fused into the combine kernel of the single layer"
    l = 0
    cos2, sin2 = _rope_tables(seq)
    xf = x.reshape(batch * seq, d)
    lg = jnp.stack([jnp.log1p(-jnp.exp(ret_decay_fwd[l].astype(F32))),
                    jnp.log1p(-jnp.exp(ret_decay_bwd[l].astype(F32)))])
    rq, rk, rv, sg, aq, ak, av, gr, ga = _inproj(
        xf, norm_mix[l][None], w_in[l], cos2, sin2, q_norm[l][None], k_norm[l][None], seq)
    u_ret = _retention(lg, rq, rk, rv, sg, ret_gn_gain[l][None], batch, seq)
    y_att = _attention(aq, ak, av, batch, seq)
    w_r = jnp.pad(w_router[l].astype(F32), ((0, 0), (0, LANES - n_exp)))
    w_r_hi = w_r.astype(BF16)
    w_r_lo = (w_r - w_r_hi.astype(F32)).astype(BF16)
    x1, h2a, aff = _merge(xf, u_ret, y_att, gr, ga, w_ret_branch[l], w_att_branch[l], w_o[l],
                          norm_ffn[l][None], jnp.concatenate([w_r_hi, w_r_lo], axis=1), n_exp)
    idx, pos, slo = _route(h2a, aff, batch, seq, cap)
    y_slots = _ffn(idx, h2a, w_expert_gate[l], w_expert_up[l], w_expert_down[l])
    out = _combine(slo, x1, pos, y_slots, norm_final[None], seq)
    return out.reshape(batch, seq, d)
```

```python
import functools

import jax
import jax.numpy as jnp
import numpy as np
from jax import lax
from jax.experimental import pallas as pl
from jax.experimental.pallas import tpu as pltpu

F32 = jnp.float32
BF16 = jnp.bfloat16
I32 = jnp.int32

GRID_W = 64
HEAD_DIM = 128
HALF_ROT = HEAD_DIM // 2
RET_HEADS = 4
RET_DIM = RET_HEADS * HEAD_DIM
ATT_HEADS = 4
ATT_KV_HEADS = 2
ATT_GROUP = ATT_HEADS // ATT_KV_HEADS
ATT_DIM = ATT_HEADS * HEAD_DIM
ATT_KV_DIM = ATT_KV_HEADS * HEAD_DIM
CHUNK = 256
ROPE_THETA = 10000.0
CAPACITY_FACTOR = 2
EPS = 1e-6
LOG2_E = 1.4426950408889634

V7X_VMEM_LIMIT_BYTES = 56 * 1024 * 1024
TOKEN_TILE = 512
ATT_Q_TILE = 1024
ATT_FIRST_KV = 512
ATT_KV_TILE = 3840
RET_NORM_ROWS = 2048
RET_UNROLL = 4
LANES = 128
ROUTE_TILE = 512
SLAB_ALIGN = 16
SMALL_ROWS = LANES
BIG_ROWS = ROUTE_TILE + SLAB_ALIGN
BIG_PAD = -(-BIG_ROWS // LANES) * LANES
COMBINE_GROUP = 4
FF_TILE = 256
FFN_ROW_BLOCK = 512
BISECT_STEPS = 24


def _params(sem, vmem=None):
    return pltpu.CompilerParams(dimension_semantics=sem, vmem_limit_bytes=vmem)


def _inproj_kernel(x_ref, g_ref, w_ref, cos_ref, sin_ref, qn_ref, kn_ref,
                   rq_ref, rk_ref, rv_ref, sg_ref, aq_ref, ak_ref, av_ref, gr_ref, ga_ref):
    x = x_ref[...]
    h = (x * lax.rsqrt(jnp.mean(x * x, axis=-1, keepdims=True) + EPS) * g_ref[...]).astype(BF16)
    cos = cos_ref[...]
    sin = sin_ref[...]
    d_model = x.shape[1]
    scale = HEAD_DIM ** -0.5

    def proj(lo, width):
        return jnp.dot(h, w_ref[:, lo:lo + width].astype(BF16), preferred_element_type=F32)

    def rope(t):
        return t * cos + pltpu.roll(t, HALF_ROT, axis=1) * sin

    def head_norm(t, gain):
        return t * lax.rsqrt(jnp.mean(t * t, axis=-1, keepdims=True) + EPS) * gain

    def head(p, i):
        return p[:, i * HEAD_DIM:(i + 1) * HEAD_DIM]

    def put(ref, i, val):
        ref[:, i * HEAD_DIM:(i + 1) * HEAD_DIM] = val.astype(ref.dtype)

    off = 0
    p = proj(off, RET_DIM)
    for i in range(RET_HEADS):
        put(rq_ref, i, rope(head(p, i)))
    off += RET_DIM
    p = proj(off, RET_DIM)
    for i in range(RET_HEADS):
        put(rk_ref, i, rope(head(p, i)) * scale)
    off += RET_DIM
    rv_ref[...] = proj(off, RET_DIM).astype(BF16)
    off += RET_DIM
    p = proj(off, RET_DIM)
    sg_ref[...] = (p * jax.nn.sigmoid(p)).astype(BF16)
    off += RET_DIM
    p = proj(off, ATT_DIM)
    for i in range(ATT_HEADS):
        put(aq_ref, i, rope(head_norm(head(p, i), qn_ref[...])) * (scale * LOG2_E))
    off += ATT_DIM
    p = proj(off, ATT_KV_DIM)
    ones_col = jnp.where(lax.broadcasted_iota(I32, (x.shape[0], HEAD_DIM), 1) == 0, 1.0, 0.0)
    for i in range(ATT_KV_HEADS):
        put(ak_ref, 2 * i, rope(head_norm(head(p, i), kn_ref[...])))
        put(ak_ref, 2 * i + 1, ones_col)
    off += ATT_KV_DIM
    av_ref[...] = proj(off, ATT_KV_DIM).T.astype(BF16)
    off += ATT_KV_DIM
    gr_ref[...] = jax.nn.sigmoid(proj(off, d_model)).astype(BF16)
    off += d_model
    ga_ref[...] = jax.nn.sigmoid(proj(off, d_model)).astype(BF16)


def _inproj(xf, norm_g, w_in, cos2, sin2, q_norm, k_norm, seq):
    t, d = xf.shape
    tm = TOKEN_TILE
    in_width = w_in.shape[1]
    steps_per_seq = seq // tm
    row = lambda i: (i, 0)
    const = lambda i: (0, 0)
    pos = lambda i: (i % steps_per_seq, 0)
    widths = (RET_DIM, RET_DIM, RET_DIM, RET_DIM, ATT_DIM, 2 * ATT_KV_DIM, ATT_KV_DIM, d, d)
    av_index = 6
    return pl.pallas_call(
        _inproj_kernel,
        grid=(t // tm,),
        in_specs=[
            pl.BlockSpec((tm, d), row),
            pl.BlockSpec((1, d), const),
            pl.BlockSpec((d, in_width), const, pipeline_mode=pl.Buffered(1)),
            pl.BlockSpec((tm, HEAD_DIM), pos),
            pl.BlockSpec((tm, HEAD_DIM), pos),
            pl.BlockSpec((1, HEAD_DIM), const),
            pl.BlockSpec((1, HEAD_DIM), const),
        ],
        out_specs=[pl.BlockSpec((ATT_KV_DIM, tm), lambda i: (0, i)) if k == av_index
                   else pl.BlockSpec((tm, w), row) for k, w in enumerate(widths)],
        out_shape=[jax.ShapeDtypeStruct((ATT_KV_DIM, t) if k == av_index else (t, w), BF16)
                   for k, w in enumerate(widths)],
        compiler_params=_params(("parallel",), V7X_VMEM_LIMIT_BYTES),
        name="inproj",
    )(xf, norm_g, w_in, cos2, sin2, q_norm, k_norm)


def _retention_kernel(lg_ref, q_ref, k_ref, v_ref, sg_ref, gain_ref, o_ref, of_ref, ob_ref):
    hd = pl.program_id(1)
    lgf = lg_ref[0, hd]
    lgb = lg_ref[1, hd]
    n_chunks = q_ref.shape[0] // CHUNK
    ii = lax.broadcasted_iota(I32, (CHUNK, CHUNK), 0)
    jj = lax.broadcasted_iota(I32, (CHUNK, CHUNK), 1)
    dist = (ii - jj).astype(F32)
    dec_f = jnp.where(dist >= 0, jnp.exp(lgf * jnp.maximum(dist, 0.0)), 0.0)
    dec_b = jnp.where(dist < 0, jnp.exp(lgb * jnp.maximum(-dist, 0.0)), 0.0)
    t = lax.broadcasted_iota(I32, (CHUNK, 1), 0).astype(F32)
    kd_f = jnp.exp(lgf * (CHUNK - 1.0 - t))
    qd_f = jnp.exp(lgf * (t + 1.0))
    kd_b = jnp.exp(lgb * t)
    qd_b = jnp.exp(lgb * (CHUNK - t))
    zero_row = jnp.zeros((1, HEAD_DIM), F32)
    cd_f = jnp.exp(zero_row + lgf * CHUNK)
    cd_b = jnp.exp(zero_row + lgb * CHUNK)

    def one_chunk(n, dec, kd, qd, cd, state):
        r0 = pl.multiple_of(n * CHUNK, CHUNK)
        q = q_ref[pl.ds(r0, CHUNK), :]
        k = k_ref[pl.ds(r0, CHUNK), :]
        v = v_ref[pl.ds(r0, CHUNK), :]
        s = lax.dot_general(q, k, (((1,), (1,)), ((), ())), preferred_element_type=F32) * dec
        intra = jnp.dot(s.astype(BF16), v, preferred_element_type=F32)
        q_dec = (q.astype(F32) * qd).astype(BF16)
        inter = jnp.dot(q_dec, state.astype(BF16), preferred_element_type=F32)
        k_dec_t = (k.astype(F32) * kd).T.astype(BF16)
        new_state = state * cd + jnp.dot(k_dec_t, v, preferred_element_type=F32)
        return r0, intra + inter, new_state

    def scan_body(n, carry):
        sf, sb = carry
        r0, of, sf = one_chunk(n, dec_f, kd_f, qd_f, cd_f, sf)
        of_ref[pl.ds(r0, CHUNK), :] = of
        r0, ob, sb = one_chunk(n_chunks - 1 - n, dec_b, kd_b, qd_b, cd_b, sb)
        ob_ref[pl.ds(r0, CHUNK), :] = ob
        return sf, sb

    zeros = jnp.zeros((HEAD_DIM, HEAD_DIM), F32)
    lax.fori_loop(0, n_chunks, scan_body, (zeros, zeros), unroll=RET_UNROLL)

    rows = RET_NORM_ROWS

    def norm_body(n, carry):
        r0 = pl.multiple_of(n * rows, rows)
        o = of_ref[pl.ds(r0, rows), :] + ob_ref[pl.ds(r0, rows), :]
        mu = jnp.mean(o, axis=-1, keepdims=True)
        var = jnp.mean(jnp.square(o - mu), axis=-1, keepdims=True)
        yn = (o - mu) * lax.rsqrt(var + EPS) * gain_ref[...]
        o_ref[pl.ds(r0, rows), :] = (sg_ref[pl.ds(r0, rows), :].astype(F32) * yn).astype(BF16)
        return carry

    lax.fori_loop(0, q_ref.shape[0] // rows, norm_body, 0)


def _retention(lg, rq, rk, rv, sg, gn_gain, batch, seq):
    t = rq.shape[0]
    blk = pl.BlockSpec((seq, HEAD_DIM), lambda b, h: (b, h))
    return pl.pallas_call(
        _retention_kernel,
        grid=(batch, RET_HEADS),
        in_specs=[
            pl.BlockSpec(memory_space=pltpu.SMEM),
            blk, blk, blk, blk,
            pl.BlockSpec((1, HEAD_DIM), lambda b, h: (0, h)),
        ],
        out_specs=blk,
        out_shape=jax.ShapeDtypeStruct((t, RET_DIM), BF16),
        scratch_shapes=[pltpu.VMEM((seq, HEAD_DIM), F32), pltpu.VMEM((seq, HEAD_DIM), F32)],
        compiler_params=_params(("parallel", "parallel"), V7X_VMEM_LIMIT_BYTES),
        name="retention",
    )(lg, rq, rk, rv, sg, gn_gain)


def _attention_kernel(q_ref, k_ref, vt_ref, o_ref, qa_ref):
    tq = q_ref.shape[0]
    n_rest = (k_ref.shape[0] - ATT_FIRST_KV) // ATT_KV_TILE
    lane = lax.broadcasted_iota(I32, (tq, 2 * HEAD_DIM), 1)

    def set_query(hh, shift):
        q = q_ref[:, hh * HEAD_DIM:(hh + 1) * HEAD_DIM]
        qa = jnp.concatenate([q, jnp.zeros((tq, HEAD_DIM), BF16)], axis=1)
        if shift is not None:
            qa = jnp.where(lane == HEAD_DIM, (-shift.T).astype(BF16), qa)
        qa_ref[hh] = qa

    def shifted_scores(hh, r0, rows):
        return lax.dot_general(k_ref[pl.ds(r0, rows), :], qa_ref[hh], (((1,), (1,)), ((), ())),
                               preferred_element_type=F32)

    def weigh(p, vt):
        return jnp.sum(p, axis=0, keepdims=True), jnp.dot(vt, p.astype(BF16), preferred_element_type=F32)

    vt = vt_ref[:, pl.ds(0, ATT_FIRST_KV)]
    state = []
    for hh in range(ATT_GROUP):
        set_query(hh, None)
        s = shifted_scores(hh, 0, ATT_FIRST_KV)
        shift = jnp.max(s, axis=0, keepdims=True).astype(BF16).astype(F32)
        l, acc = weigh(jnp.exp2(s - shift), vt)
        set_query(hh, shift)
        state.append((shift, l, acc))

    def body(c, carry):
        r0 = pl.multiple_of(ATT_FIRST_KV + c * ATT_KV_TILE, 2 * LANES)
        vt = vt_ref[:, pl.ds(r0, ATT_KV_TILE)]
        fast = []
        overflow = None
        for hh, (shift, l, acc) in enumerate(carry):
            s = shifted_scores(hh, r0, ATT_KV_TILE)
            dl, dacc = weigh(jnp.exp2(s), vt)
            l_new = l + dl
            acc_new = acc + dacc
            fast.append((shift, l_new, acc_new))
            bad = jnp.maximum(jnp.max(jnp.where(jnp.isfinite(l_new), 0.0, 1.0)),
                              jnp.max(jnp.where(jnp.isfinite(acc_new), 0.0, 1.0)))
            overflow = bad if overflow is None else jnp.maximum(overflow, bad)

        def redo():
            out = []
            for hh, (shift, l, acc) in enumerate(carry):
                s = shifted_scores(hh, r0, ATT_KV_TILE)
                raised = (shift + jnp.maximum(jnp.max(s, axis=0, keepdims=True), 0.0)).astype(BF16).astype(F32)
                step = raised - shift
                alpha = jnp.exp2(-step)
                dl, dacc = weigh(jnp.exp2(s - step), vt)
                out.append((raised, alpha * l + dl, alpha * acc + dacc))
                set_query(hh, raised)
            return tuple(out)

        return lax.cond(overflow == 0.0, lambda: tuple(fast), redo)

    final = lax.fori_loop(0, n_rest, body, tuple(state))
    for hh, (_, l, acc) in enumerate(final):
        o_ref[:, hh * HEAD_DIM:(hh + 1) * HEAD_DIM] = (acc / l).T.astype(BF16)


def _attention(aq, ak, av_t, batch, seq):
    t = aq.shape[0]
    tq = ATT_Q_TILE
    nq = seq // tq
    gw = ATT_GROUP * HEAD_DIM
    q_spec = pl.BlockSpec((tq, gw), lambda b, g, i: (b * nq + i, g))
    k_spec = pl.BlockSpec((seq, 2 * HEAD_DIM), lambda b, g, i: (b, g))
    vt_spec = pl.BlockSpec((HEAD_DIM, seq), lambda b, g, i: (g, b))
    return pl.pallas_call(
        _attention_kernel,
        grid=(batch, ATT_KV_HEADS, nq),
        in_specs=[q_spec, k_spec, vt_spec],
        out_specs=q_spec,
        out_shape=jax.ShapeDtypeStruct((t, ATT_DIM), BF16),
        scratch_shapes=[pltpu.VMEM((ATT_GROUP, tq, 2 * HEAD_DIM), BF16)],
        compiler_params=_params(("parallel", "parallel", "parallel"), V7X_VMEM_LIMIT_BYTES),
        name="attention",
    )(aq, ak, av_t)


def _merge_kernel(x_ref, ur_ref, ya_ref, gr_ref, ga_ref, wr_ref, wa_ref, wo_ref, nf_ref, wrt_ref,
                  x1_ref, h2a_ref, affp_ref, aff_scr, *, n_exp):
    d = x_ref.shape[1]
    y_ret = jnp.dot(ur_ref[...], wr_ref[...].astype(BF16), preferred_element_type=F32)
    y_att = jnp.dot(ya_ref[...], wa_ref[...].astype(BF16), preferred_element_type=F32)
    mixed = gr_ref[...].astype(F32) * y_ret + ga_ref[...].astype(F32) * y_att
    x1 = x_ref[...] + jnp.dot(mixed.astype(BF16), wo_ref[...].astype(BF16), preferred_element_type=F32)
    x1_ref[...] = x1
    h2 = x1 * lax.rsqrt(jnp.mean(x1 * x1, axis=-1, keepdims=True) + EPS) * nf_ref[...]
    h2a_ref[:, :d] = h2
    h_hi = h2.astype(BF16)
    h_lo = (h2 - h_hi.astype(F32)).astype(BF16)
    r_hi = jnp.dot(h_hi, wrt_ref[...], preferred_element_type=F32)
    r_lo = jnp.dot(h_lo, wrt_ref[:, :LANES], preferred_element_type=F32)
    logits = r_hi[:, :LANES] + r_hi[:, LANES:] + r_lo
    lane = lax.broadcasted_iota(I32, logits.shape, 1)
    logits = jnp.where(lane < n_exp, logits, -jnp.inf)
    e = jnp.exp(logits - jnp.max(logits, axis=-1, keepdims=True))
    aff = e / jnp.sum(e, axis=-1, keepdims=True)
    h2a_ref[:, d:] = aff
    pack = LANES // n_exp
    aff_scr[...] = aff
    packed = None
    for i in range(pack):
        part = aff_scr[pl.ds(i, x_ref.shape[0] // pack, stride=pack), :]
        part = pltpu.roll(part, i * n_exp, axis=1) if i else part
        packed = part if packed is None else packed + part
    affp_ref[...] = packed


def _merge(xf, u_ret, y_att, gr, ga, w_ret, w_att, w_o, norm_ffn, w_router_split, n_exp):
    t, d = xf.shape
    tm = TOKEN_TILE
    row = lambda i: (i, 0)
    const = lambda i: (0, 0)
    return pl.pallas_call(
        functools.partial(_merge_kernel, n_exp=n_exp),
        grid=(t // tm,),
        in_specs=[
            pl.BlockSpec((tm, d), row),
            pl.BlockSpec((tm, RET_DIM), row),
            pl.BlockSpec((tm, ATT_DIM), row),
            pl.BlockSpec((tm, d), row),
            pl.BlockSpec((tm, d), row),
            pl.BlockSpec((RET_DIM, d), const, pipeline_mode=pl.Buffered(1)),
            pl.BlockSpec((ATT_DIM, d), const, pipeline_mode=pl.Buffered(1)),
            pl.BlockSpec((d, d), const, pipeline_mode=pl.Buffered(1)),
            pl.BlockSpec((1, d), const),
            pl.BlockSpec((d, 2 * LANES), const),
        ],
        out_specs=[pl.BlockSpec((tm, d), row), pl.BlockSpec((tm, d + LANES), row),
                   pl.BlockSpec((tm * n_exp // LANES, LANES), row)],
        out_shape=[jax.ShapeDtypeStruct((t, d), F32), jax.ShapeDtypeStruct((t, d + LANES), F32),
                   jax.ShapeDtypeStruct((t * n_exp // LANES, LANES), F32)],
        scratch_shapes=[pltpu.VMEM((tm, LANES), F32)],
        compiler_params=_params(("parallel",), V7X_VMEM_LIMIT_BYTES),
        name="merge",
    )(xf, u_ret, y_att, gr, ga, w_ret, w_att, w_o, norm_ffn, w_router_split)


def _route_kernel(aff_ref, affc_ref, idx_ref, pos_ref, slo_ref, msk_ref, cum_ref, cum_t_ref, bnd_ref, bnd_smem,
                  cnt_ref, sem, *, cap, n_exp):
    b = pl.program_id(0)
    seq = aff_ref.shape[0]
    n_tiles = seq // ROUTE_TILE
    aff = aff_ref[...]
    affc = affc_ref[...]
    capf = float(cap)

    def per_expert(x, op, reduce):
        r = reduce(x.reshape(x.shape[0] // 8, 8, LANES), axis=0)
        shift = n_exp
        while shift < LANES:
            r = op(r, pltpu.roll(r, shift, axis=1))
            shift *= 2
        return reduce(r, axis=0, keepdims=True)

    def bisect(lo, hi):
        mid = 0.5 * (lo + hi)
        ok = per_expert(jnp.where(affc >= mid, 1.0, 0.0), jnp.add, jnp.sum) >= capf
        return jnp.where(ok, mid, lo), jnp.where(ok, hi, mid)

    def bracket(lo, hi):
        mn = per_expert(jnp.where(affc >= lo, affc, jnp.inf), jnp.minimum, jnp.min)
        mx = per_expert(jnp.where(affc < hi, affc, -jnp.inf), jnp.maximum, jnp.max)
        return mn, mx

    def not_isolated(lo, hi):
        mn, mx = bracket(lo, hi)
        return jnp.max(jnp.where(mn < mx, 1.0, 0.0)) > 0.0

    lo, hi = lax.fori_loop(0, BISECT_STEPS, lambda i, c: bisect(*c),
                           (jnp.zeros((1, LANES), F32), jnp.full((1, LANES), 2.0, F32)))

    def refine(c):
        lo, hi = bisect(c[0], c[1])
        return lo, hi, not_isolated(lo, hi)

    lo, hi, _ = lax.while_loop(lambda c: c[2], refine, (lo, hi, not_isolated(lo, hi)))
    thr, _ = bracket(lo, hi)
    gt = aff > thr
    eq = aff == thr
    n_gt = jnp.sum(jnp.where(gt, 1.0, 0.0).reshape(seq // 8, 8, LANES), axis=0)
    need = capf - jnp.sum(n_gt, axis=0, keepdims=True)

    tri = (lax.broadcasted_iota(I32, (ROUTE_TILE, ROUTE_TILE), 0)
           >= lax.broadcasted_iota(I32, (ROUTE_TILE, ROUTE_TILE), 1))
    tri = jnp.where(tri, 1.0, 0.0).astype(BF16)

    def cumsum_tokens(store_tile_start):
        def body(c, carry):
            r0 = pl.multiple_of(c * ROUTE_TILE, ROUTE_TILE)
            if store_tile_start:
                slo_ref[0, pl.ds(c, 1), :] = carry[:, :n_exp].astype(I32)
                bnd_ref[pl.ds(c, 1), :] = carry.astype(I32)
            cs = jnp.dot(tri, msk_ref[pl.ds(r0, ROUTE_TILE), :], preferred_element_type=F32) + carry
            cum_ref[pl.ds(r0, ROUTE_TILE), :] = cs
            return cs[ROUTE_TILE - 1:ROUTE_TILE, :]

        return lax.fori_loop(0, n_tiles, body, jnp.zeros((1, LANES), F32))

    msk_ref[...] = jnp.where(eq, 1.0, 0.0).astype(BF16)
    cumsum_tokens(False)
    take = jnp.logical_and(eq, cum_ref[...] - 1.0 < need)
    mask = jnp.logical_or(gt, take)
    msk_ref[...] = jnp.where(mask, 1.0, 0.0).astype(BF16)
    bnd_ref[...] = jnp.zeros(bnd_ref.shape, I32)
    bnd_ref[pl.ds(n_tiles, 1), :] = cumsum_tokens(True).astype(I32)
    pos_ref[...] = jnp.where(mask, cum_ref[...] - 1.0, -1.0)[:, :n_exp].astype(I32)

    to_smem = pltpu.make_async_copy(bnd_ref, bnd_smem, sem)
    to_smem.start()
    to_smem.wait()

    def transpose_tile(c, carry):
        r0 = pl.multiple_of(c * ROUTE_TILE, ROUTE_TILE)
        cum_t_ref[:, pl.ds(r0, ROUTE_TILE)] = cum_ref[pl.ds(r0, ROUTE_TILE), :].T
        return carry

    lax.fori_loop(0, n_tiles, transpose_tile, 0)
    slot_row = lax.broadcasted_iota(I32, (1, cap), 1).astype(F32)
    slot_sub = lax.broadcasted_iota(I32, (LANES, ROUTE_TILE), 0)
    eye = lax.broadcasted_iota(I32, (LANES, LANES), 0) == lax.broadcasted_iota(I32, (LANES, LANES), 1)
    for e in range(n_exp):
        ends = bnd_ref[pl.ds(1, n_tiles), e:e + 1].astype(F32)
        whole = jnp.sum(jnp.where(slot_row >= ends, float(ROUTE_TILE), 0.0), axis=0, keepdims=True)
        cnt_ref[...] = jnp.zeros(cnt_ref.shape, F32)

        def tile_body(c, carry, e=e):
            lo = bnd_smem[c, e]
            hi = bnd_smem[c + 1, e]
            r0 = pl.multiple_of(c * ROUTE_TILE, ROUTE_TILE)
            counts = cum_t_ref[e:e + 1, pl.ds(r0, ROUTE_TILE)]

            def lane_tile(k, carry2):
                slots = k * LANES + slot_sub
                below_hi = jnp.where(slots < hi, 1.0, 0.0)
                hit = jnp.where(counts <= slots.astype(F32), below_hi, 0.0)
                folded = hit[:, :LANES]
                for t0 in range(LANES, ROUTE_TILE, LANES):
                    folded = folded + hit[:, t0:t0 + LANES]
                cnt_ref[k] += folded
                return carry2

            lax.fori_loop(lo // LANES, (hi + LANES - 1) // LANES, lane_tile, 0)
            return carry

        lax.fori_loop(0, n_tiles, tile_body, 0)
        for k in range(cap // LANES):
            per_slot = jnp.sum(cnt_ref[k], axis=1, keepdims=True)
            part = jnp.sum(jnp.where(eye, per_slot, 0.0), axis=0, keepdims=True)
            part = part + whole[:, k * LANES:(k + 1) * LANES]
            idx_ref[0, e:e + 1, k * LANES:(k + 1) * LANES] = part.astype(I32) + b * seq


def _route(h2a, aff_packed, batch, seq, cap, n_exp):
    t = h2a.shape[0]
    pack = LANES // n_exp
    aff_block = h2a.shape[1] // LANES - 1
    n_tiles = seq // ROUTE_TILE
    bnd_rows = -(-(n_tiles + 1) // 8) * 8
    return pl.pallas_call(
        functools.partial(_route_kernel, cap=cap, n_exp=n_exp),
        grid=(batch,),
        in_specs=[pl.BlockSpec((seq, LANES), lambda b: (b, aff_block)),
                  pl.BlockSpec((seq // pack, LANES), lambda b: (b, 0))],
        out_specs=[pl.BlockSpec((1, n_exp, cap), lambda b: (b, 0, 0)),
                   pl.BlockSpec((seq, n_exp), lambda b: (b, 0)),
                   pl.BlockSpec((1, n_tiles, n_exp), lambda b: (b, 0, 0))],
        out_shape=[jax.ShapeDtypeStruct((batch, n_exp, cap), I32),
                   jax.ShapeDtypeStruct((t, n_exp), I32),
                   jax.ShapeDtypeStruct((batch, n_tiles, n_exp), I32)],
        scratch_shapes=[pltpu.VMEM((seq, LANES), BF16), pltpu.VMEM((seq, LANES), F32),
                        pltpu.VMEM((LANES, seq), F32),
                        pltpu.VMEM((bnd_rows, LANES), I32), pltpu.SMEM((bnd_rows, LANES), I32),
                        pltpu.VMEM((cap // LANES, LANES, LANES), F32), pltpu.SemaphoreType.DMA(())],
        compiler_params=_params(("parallel",), V7X_VMEM_LIMIT_BYTES),
        name="route",
    )(h2a, aff_packed)


def _ffn_kernel(idx_ref, h2a_hbm, wg_hbm, wu_hbm, wd_hbm, y_ref, stage_ref, xe_ref, gate_ref, acc_ref,
                wg_buf, wu_buf, wd_buf, sem, wsem, *, n_exp, n_ff, rows_per_step):
    e = pl.program_id(0)
    mp = stage_ref.shape[0]
    m, d = xe_ref.shape
    batch = y_ref.shape[0]
    cap = y_ref.shape[2]

    def row_copy(expert, i):
        r = idx_ref[expert * mp + i]
        return pltpu.make_async_copy(h2a_hbm.at[pl.ds(r, 1)], stage_ref.at[pl.ds(i, 1)], sem)

    def wait_rows():
        pltpu.make_async_copy(h2a_hbm.at[pl.ds(0, mp)], stage_ref, sem).wait()

    def weight_copies(chunk, slot):
        ex = chunk // n_ff
        c0 = pl.multiple_of((chunk % n_ff) * FF_TILE, FF_TILE)
        return (pltpu.make_async_copy(wg_hbm.at[ex, :, pl.ds(c0, FF_TILE)], wg_buf.at[slot], wsem.at[slot]),
                pltpu.make_async_copy(wu_hbm.at[ex, :, pl.ds(c0, FF_TILE)], wu_buf.at[slot], wsem.at[slot]),
                pltpu.make_async_copy(wd_hbm.at[ex, pl.ds(c0, FF_TILE), :], wd_buf.at[slot], wsem.at[slot]))

    @pl.when(e == 0)
    def _():
        def issue(i, carry):
            row_copy(0, i).start()
            return carry

        lax.fori_loop(0, mp, issue, 0)
        for cp in weight_copies(0, 0):
            cp.start()

    wait_rows()
    xe_ref[...] = stage_ref[:m, :d].astype(BF16)
    lane = lax.broadcasted_iota(I32, (m, LANES), 1)
    gate_ref[...] = jnp.sum(jnp.where(lane == e, stage_ref[:m, d:], 0.0), axis=1, keepdims=True)
    acc_ref[...] = jnp.zeros(acc_ref.shape, F32)

    nxt = jnp.minimum(e + 1, n_exp - 1)

    def ff_tile(f, carry):
        chunk = e * n_ff + f
        slot = chunk % 2

        for cp in weight_copies(chunk, slot):
            cp.wait()

        @pl.when(chunk + 1 < n_exp * n_ff)
        def _():
            for cp in weight_copies(chunk + 1, 1 - slot):
                cp.start()

        for k in range(rows_per_step):
            row_copy(nxt, f * rows_per_step + k).start()
        wg = wg_buf[slot].astype(BF16)
        wu = wu_buf[slot].astype(BF16)
        wd = wd_buf[slot].astype(BF16)
        for rb in range(m // FFN_ROW_BLOCK):
            rows = pl.ds(rb * FFN_ROW_BLOCK, FFN_ROW_BLOCK)
            xb = xe_ref[rows, :]
            a = jnp.dot(xb, wg, preferred_element_type=F32)
            u = jnp.dot(xb, wu, preferred_element_type=F32)
            hm = (a * jax.nn.sigmoid(a) * u).astype(BF16)
            acc_ref[rows, :] += jnp.dot(hm, wd, preferred_element_type=F32)
        return carry

    lax.fori_loop(0, n_ff, ff_tile, 0)

    for bb in range(batch):
        rows = pl.ds(bb * cap, cap)
        y_ref[bb, 0] = (acc_ref[rows, :] * gate_ref[rows, :]).astype(BF16)

    @pl.when(e == n_exp - 1)
    def _():
        wait_rows()


def _ffn(idx, h2a, w_gate, w_up, w_down):
    batch, n_exp, cap = idx.shape
    d = h2a.shape[1] - LANES
    ff = w_gate.shape[2]
    n_ff = ff // FF_TILE
    m = batch * cap
    rows_per_step = -(-pl.cdiv(m, n_ff) // 8) * 8
    mp = n_ff * rows_per_step
    idx_flat = jnp.pad(idx.transpose(1, 0, 2).reshape(n_exp, m), ((0, 0), (0, mp - m))).reshape(-1)
    any_spec = pl.BlockSpec(memory_space=pl.ANY)
    grid_spec = pltpu.PrefetchScalarGridSpec(
        num_scalar_prefetch=1,
        grid=(n_exp,),
        in_specs=[any_spec, any_spec, any_spec, any_spec],
        out_specs=pl.BlockSpec((batch, 1, cap, d), lambda e, idx: (0, e, 0, 0)),
        scratch_shapes=[pltpu.VMEM((mp, d + LANES), F32), pltpu.VMEM((m, d), BF16),
                        pltpu.VMEM((m, 1), F32), pltpu.VMEM((m, d), F32),
                        pltpu.VMEM((2, d, FF_TILE), F32), pltpu.VMEM((2, d, FF_TILE), F32),
                        pltpu.VMEM((2, FF_TILE, d), F32),
                        pltpu.SemaphoreType.DMA(()), pltpu.SemaphoreType.DMA((2,))],
    )
    return pl.pallas_call(
        functools.partial(_ffn_kernel, n_exp=n_exp, n_ff=n_ff, rows_per_step=rows_per_step),
        grid_spec=grid_spec,
        out_shape=jax.ShapeDtypeStruct((batch, n_exp, cap, d), BF16),
        compiler_params=_params(("arbitrary",), V7X_VMEM_LIMIT_BYTES),
        name="ffn",
    )(idx_flat, h2a, w_gate, w_up, w_down)


def _window_fits(s_hi, start, rows):
    return s_hi - start <= rows


def _combine_kernel(slo_ref, x1_ref, pos_ref, y_hbm, nf_ref, o_ref, slab_ref, big_ref, acc_ref, sem, big_sem,
                    *, n_exp, cap):
    b = pl.program_id(0)
    j = pl.program_id(1)
    n_batch = pl.num_programs(0)
    n_tiles = pl.num_programs(1)
    step = b * n_tiles + j
    slot = step % 2

    def bounds(bb, jj, e):
        base = (bb * (n_tiles + 1) + jj) * n_exp + e
        return slo_ref[base], slo_ref[base + n_exp]

    def window_start(bb, jj, e, rows):
        s_lo, _ = bounds(bb, jj, e)
        return pl.multiple_of(jnp.minimum((s_lo // SLAB_ALIGN) * SLAB_ALIGN, cap - rows), SLAB_ALIGN)

    def all_small(bb, jj):
        ok = None
        for e in range(n_exp):
            fits = _window_fits(bounds(bb, jj, e)[1], window_start(bb, jj, e, SMALL_ROWS), SMALL_ROWS)
            ok = fits if ok is None else jnp.logical_and(ok, fits)
        return ok

    def small_copy(bb, jj, e, dst_slot):
        return pltpu.make_async_copy(
            y_hbm.at[bb, e, pl.ds(window_start(bb, jj, e, SMALL_ROWS), SMALL_ROWS)],
            slab_ref.at[dst_slot, pl.ds(e * SMALL_ROWS, SMALL_ROWS)], sem.at[dst_slot])

    def finish(moe):
        x2 = x1_ref[...] + moe
        o_ref[...] = x2 * lax.rsqrt(jnp.mean(x2 * x2, axis=-1, keepdims=True) + EPS) * nf_ref[...]

    last = step + 1 == n_batch * n_tiles

    @pl.when(step == 0)
    def _():
        big_ref[...] = jnp.zeros(big_ref.shape, big_ref.dtype)
        for e in range(n_exp):
            small_copy(b, j, e, slot).start()

    for e in range(n_exp):
        small_copy(b, j, e, slot).wait()
    wrap = j + 1 == n_tiles
    nb = jnp.where(last, b, jnp.where(wrap, b + 1, b))
    nj = jnp.where(last, j, jnp.where(wrap, 0, j + 1))
    for e in range(n_exp):
        small_copy(nb, nj, e, 1 - slot).start()

    lane = lax.broadcasted_iota(I32, (1, SMALL_ROWS), 1)
    acc = None
    for e0 in range(0, n_exp, COMBINE_GROUP):
        tiles = []
        for e in range(e0, min(e0 + COMBINE_GROUP, n_exp)):
            rel = pos_ref[:, e:e + 1] - window_start(b, j, e, SMALL_ROWS)
            tiles.append(jnp.where(rel == lane, 1.0, 0.0).astype(BF16))
        rows = pl.ds(e0 * SMALL_ROWS, len(tiles) * SMALL_ROWS)
        part = jnp.dot(jnp.concatenate(tiles, axis=1), slab_ref[slot, rows, :],
                       preferred_element_type=F32)
        acc = part if acc is None else acc + part

    small_now = all_small(b, j)

    @pl.when(small_now)
    def _():
        finish(acc)

    @pl.when(jnp.logical_not(small_now))
    def _():
        wide_lane = lax.broadcasted_iota(I32, (1, BIG_PAD), 1)
        for e in range(n_exp):
            st = window_start(b, j, e, BIG_ROWS)
            cp = pltpu.make_async_copy(y_hbm.at[b, e, pl.ds(st, BIG_ROWS)],
                                       big_ref.at[pl.ds(0, BIG_ROWS)], big_sem)
            cp.start()
            cp.wait()
            onehot = jnp.where(pos_ref[:, e:e + 1] - st == wide_lane, 1.0, 0.0).astype(BF16)
            part = jnp.dot(onehot, big_ref[...], preferred_element_type=F32)
            if e == 0:
                acc_ref[...] = part
            else:
                acc_ref[...] += part
        finish(acc_ref[...])

    @pl.when(last)
    def _():
        for e in range(n_exp):
            small_copy(b, j, e, 1 - slot).wait()


def _combine(slo, x1, pos, y_slots, norm_final, seq):
    t, d = x1.shape
    batch, n_exp, cap, _ = y_slots.shape
    n_tiles = seq // ROUTE_TILE
    slo_flat = jnp.concatenate([slo, jnp.full((batch, 1, n_exp), cap, I32)], axis=1).reshape(-1)
    row = lambda b, j, slo: (b * n_tiles + j, 0)
    grid_spec = pltpu.PrefetchScalarGridSpec(
        num_scalar_prefetch=1,
        grid=(batch, n_tiles),
        in_specs=[
            pl.BlockSpec((ROUTE_TILE, d), row),
            pl.BlockSpec((ROUTE_TILE, n_exp), row),
            pl.BlockSpec(memory_space=pl.ANY),
            pl.BlockSpec((1, d), lambda b, j, slo: (0, 0)),
        ],
        out_specs=pl.BlockSpec((ROUTE_TILE, d), row),
        scratch_shapes=[pltpu.VMEM((2, n_exp * SMALL_ROWS, d), BF16), pltpu.VMEM((BIG_PAD, d), BF16),
                        pltpu.VMEM((ROUTE_TILE, d), F32),
                        pltpu.SemaphoreType.DMA((2,)), pltpu.SemaphoreType.DMA(())],
    )
    return pl.pallas_call(
        functools.partial(_combine_kernel, n_exp=n_exp, cap=cap),
        grid_spec=grid_spec,
        out_shape=jax.ShapeDtypeStruct((t, d), F32),
        compiler_params=_params(("arbitrary", "arbitrary"), V7X_VMEM_LIMIT_BYTES),
        name="combine",
    )(slo_flat, x1, pos, y_slots, norm_final)


def _rope_tables(seq):
    rows = seq // GRID_W
    row = np.repeat(np.arange(rows, dtype=np.float64), GRID_W)
    col = np.tile(np.arange(GRID_W, dtype=np.float64), rows)
    n_freq = HALF_ROT // 2
    freqs = ROPE_THETA ** (-np.arange(n_freq, dtype=np.float64) / n_freq)
    ang = np.concatenate([row[:, None] * freqs, col[:, None] * freqs], axis=-1)
    cos, sin = np.cos(ang), np.sin(ang)
    return (jnp.asarray(np.concatenate([cos, cos], axis=-1), F32),
            jnp.asarray(np.concatenate([-sin, sin], axis=-1), F32))


def kernel(x, norm_mix, w_in, ret_decay_fwd, ret_decay_bwd, ret_gn_gain, w_ret_branch, q_norm, k_norm,
           w_att_branch, w_o, norm_ffn, w_router, w_expert_gate, w_expert_up, w_expert_down, norm_final):
    batch, seq, d = x.shape
    depth = norm_mix.shape[0]
    n_exp = w_router.shape[2]
    cap = CAPACITY_FACTOR * seq // n_exp
    assert seq % TOKEN_TILE == 0 and seq % ROUTE_TILE == 0
    assert (seq - ATT_FIRST_KV) % ATT_KV_TILE == 0 and ATT_KV_TILE % (2 * LANES) == 0
    assert ATT_FIRST_KV % (2 * LANES) == 0 and seq % ATT_Q_TILE == 0
    assert seq % CHUNK == 0 and (seq // CHUNK) % RET_UNROLL == 0 and seq % RET_NORM_ROWS == 0
    assert LANES % n_exp == 0 and d % LANES == 0 and seq % (8 * LANES // n_exp) == 0
    assert cap >= BIG_ROWS and (cap - BIG_ROWS) % SLAB_ALIGN == 0 and (cap - SMALL_ROWS) % SLAB_ALIGN == 0
    assert w_expert_gate.shape[3] % FF_TILE == 0 and (batch * cap) % FFN_ROW_BLOCK == 0
    assert depth == 1, "the final RMSNorm is fused into the combine kernel of the single layer"
    l = 0
    cos2, sin2 = _rope_tables(seq)
    xf = x.reshape(batch * seq, d)
    lg = jnp.stack([jnp.log1p(-jnp.exp(ret_decay_fwd[l].astype(F32))),
                    jnp.log1p(-jnp.exp(ret_decay_bwd[l].astype(F32)))])
    rq, rk, rv, sg, aq, ak, av, gr, ga = _inproj(
        xf, norm_mix[l][None], w_in[l], cos2, sin2, q_norm[l][None], k_norm[l][None], seq)
    u_ret = _retention(lg, rq, rk, rv, sg, ret_gn_gain[l][None], batch, seq)
    y_att = _attention(aq, ak, av, batch, seq)
    w_r = jnp.pad(w_router[l].astype(F32), ((0, 0), (0, LANES - n_exp)))
    w_r_hi = w_r.astype(BF16)
    w_r_lo = (w_r - w_r_hi.astype(F32)).astype(BF16)
    x1, h2a, aff = _merge(xf, u_ret, y_att, gr, ga, w_ret_branch[l], w_att_branch[l], w_o[l],
                          norm_ffn[l][None], jnp.concatenate([w_r_hi, w_r_lo], axis=1), n_exp)
    idx, pos, slo = _route(h2a, aff, batch, seq, cap, n_exp)
    y_slots = _ffn(idx, h2a, w_expert_gate[l], w_expert_up[l], w_expert_down[l])
    out = _combine(slo, x1, pos, y_slots, norm_final[None], seq)
    return out.reshape(batch, seq, d)
```

```python
import functools

import jax
import jax.numpy as jnp
import numpy as np
from jax import lax
from jax.experimental import pallas as pl
from jax.experimental.pallas import tpu as pltpu

F32 = jnp.float32
BF16 = jnp.bfloat16
I32 = jnp.int32

GRID_W = 64
HEAD_DIM = 128
HALF_ROT = HEAD_DIM // 2
RET_HEADS = 4
RET_DIM = RET_HEADS * HEAD_DIM
ATT_HEADS = 4
ATT_KV_HEADS = 2
ATT_GROUP = ATT_HEADS // ATT_KV_HEADS
ATT_DIM = ATT_HEADS * HEAD_DIM
ATT_KV_DIM = ATT_KV_HEADS * HEAD_DIM
CHUNK = 256
ROPE_THETA = 10000.0
CAPACITY_FACTOR = 2
EPS = 1e-6
LOG2_E = 1.4426950408889634

V7X_VMEM_LIMIT_BYTES = 56 * 1024 * 1024
TOKEN_TILE = 512
ATT_Q_TILE = 1024
ATT_FIRST_KV = 512
ATT_KV_TILE = 3840
RET_NORM_ROWS = 2048
RET_UNROLL = 4
LANES = 128
ROUTE_TILE = 512
SLAB_ALIGN = 16
SMALL_ROWS = LANES
BIG_ROWS = ROUTE_TILE + SLAB_ALIGN
BIG_PAD = -(-BIG_ROWS // LANES) * LANES
COMBINE_GROUP = 4
FF_TILE = 256
FFN_ROW_BLOCK = 512
BISECT_STEPS = 24


def _params(sem, vmem=None):
    return pltpu.CompilerParams(dimension_semantics=sem, vmem_limit_bytes=vmem)


def _inproj_kernel(x_ref, g_ref, w_ref, cos_ref, sin_ref, qn_ref, kn_ref,
                   rq_ref, rk_ref, rv_ref, sg_ref, aq_ref, ak_ref, av_ref, gr_ref, ga_ref):
    x = x_ref[...]
    h = (x * lax.rsqrt(jnp.mean(x * x, axis=-1, keepdims=True) + EPS) * g_ref[...]).astype(BF16)
    cos = cos_ref[...]
    sin = sin_ref[...]
    d_model = x.shape[1]
    scale = HEAD_DIM ** -0.5

    def proj(lo, width):
        return jnp.dot(h, w_ref[:, lo:lo + width].astype(BF16), preferred_element_type=F32)

    def rope(t):
        return t * cos + pltpu.roll(t, HALF_ROT, axis=1) * sin

    def head_norm(t, gain):
        return t * lax.rsqrt(jnp.mean(t * t, axis=-1, keepdims=True) + EPS) * gain

    def head(p, i):
        return p[:, i * HEAD_DIM:(i + 1) * HEAD_DIM]

    def put(ref, i, val):
        ref[:, i * HEAD_DIM:(i + 1) * HEAD_DIM] = val.astype(ref.dtype)

    off = 0
    p = proj(off, RET_DIM)
    for i in range(RET_HEADS):
        put(rq_ref, i, rope(head(p, i)))
    off += RET_DIM
    p = proj(off, RET_DIM)
    for i in range(RET_HEADS):
        put(rk_ref, i, rope(head(p, i)) * scale)
    off += RET_DIM
    rv_ref[...] = proj(off, RET_DIM).astype(BF16)
    off += RET_DIM
    p = proj(off, RET_DIM)
    sg_ref[...] = (p * jax.nn.sigmoid(p)).astype(BF16)
    off += RET_DIM
    p = proj(off, ATT_DIM)
    for i in range(ATT_HEADS):
        put(aq_ref, i, rope(head_norm(head(p, i), qn_ref[...])) * (scale * LOG2_E))
    off += ATT_DIM
    p = proj(off, ATT_KV_DIM)
    ones_col = jnp.where(lax.broadcasted_iota(I32, (x.shape[0], HEAD_DIM), 1) == 0, 1.0, 0.0)
    for i in range(ATT_KV_HEADS):
        put(ak_ref, 2 * i, rope(head_norm(head(p, i), kn_ref[...])))
        put(ak_ref, 2 * i + 1, ones_col)
    off += ATT_KV_DIM
    av_ref[...] = proj(off, ATT_KV_DIM).T.astype(BF16)
    off += ATT_KV_DIM
    gr_ref[...] = jax.nn.sigmoid(proj(off, d_model)).astype(BF16)
    off += d_model
    ga_ref[...] = jax.nn.sigmoid(proj(off, d_model)).astype(BF16)


def _inproj(xf, norm_g, w_in, cos2, sin2, q_norm, k_norm, seq):
    t, d = xf.shape
    tm = TOKEN_TILE
    in_width = w_in.shape[1]
    steps_per_seq = seq // tm
    row = lambda i: (i, 0)
    const = lambda i: (0, 0)
    pos = lambda i: (i % steps_per_seq, 0)
    widths = (RET_DIM, RET_DIM, RET_DIM, RET_DIM, ATT_DIM, 2 * ATT_KV_DIM, ATT_KV_DIM, d, d)
    av_index = 6
    return pl.pallas_call(
        _inproj_kernel,
        grid=(t // tm,),
        in_specs=[
            pl.BlockSpec((tm, d), row),
            pl.BlockSpec((1, d), const),
            pl.BlockSpec((d, in_width), const, pipeline_mode=pl.Buffered(1)),
            pl.BlockSpec((tm, HEAD_DIM), pos),
            pl.BlockSpec((tm, HEAD_DIM), pos),
            pl.BlockSpec((1, HEAD_DIM), const),
            pl.BlockSpec((1, HEAD_DIM), const),
        ],
        out_specs=[pl.BlockSpec((ATT_KV_DIM, tm), lambda i: (0, i)) if k == av_index
                   else pl.BlockSpec((tm, w), row) for k, w in enumerate(widths)],
        out_shape=[jax.ShapeDtypeStruct((ATT_KV_DIM, t) if k == av_index else (t, w), BF16)
                   for k, w in enumerate(widths)],
        compiler_params=_params(("parallel",), V7X_VMEM_LIMIT_BYTES),
        name="inproj",
    )(xf, norm_g, w_in, cos2, sin2, q_norm, k_norm)


def _retention_kernel(lg_ref, q_ref, k_ref, v_ref, sg_ref, gain_ref, o_ref, of_ref, ob_ref):
    hd = pl.program_id(1)
    lgf = lg_ref[0, hd]
    lgb = lg_ref[1, hd]
    n_chunks = q_ref.shape[0] // CHUNK
    ii = lax.broadcasted_iota(I32, (CHUNK, CHUNK), 0)
    jj = lax.broadcasted_iota(I32, (CHUNK, CHUNK), 1)
    dist = (ii - jj).astype(F32)
    dec_f = jnp.where(dist >= 0, jnp.exp(lgf * jnp.maximum(dist, 0.0)), 0.0)
    dec_b = jnp.where(dist < 0, jnp.exp(lgb * jnp.maximum(-dist, 0.0)), 0.0)
    t = lax.broadcasted_iota(I32, (CHUNK, 1), 0).astype(F32)
    kd_f = jnp.exp(lgf * (CHUNK - 1.0 - t))
    qd_f = jnp.exp(lgf * (t + 1.0))
    kd_b = jnp.exp(lgb * t)
    qd_b = jnp.exp(lgb * (CHUNK - t))
    zero_row = jnp.zeros((1, HEAD_DIM), F32)
    cd_f = jnp.exp(zero_row + lgf * CHUNK)
    cd_b = jnp.exp(zero_row + lgb * CHUNK)

    def one_chunk(n, dec, kd, qd, cd, state):
        r0 = pl.multiple_of(n * CHUNK, CHUNK)
        q = q_ref[pl.ds(r0, CHUNK), :]
        k = k_ref[pl.ds(r0, CHUNK), :]
        v = v_ref[pl.ds(r0, CHUNK), :]
        s = lax.dot_general(q, k, (((1,), (1,)), ((), ())), preferred_element_type=F32) * dec
        intra = jnp.dot(s.astype(BF16), v, preferred_element_type=F32)
        q_dec = (q.astype(F32) * qd).astype(BF16)
        inter = jnp.dot(q_dec, state.astype(BF16), preferred_element_type=F32)
        k_dec_t = (k.astype(F32) * kd).T.astype(BF16)
        new_state = state * cd + jnp.dot(k_dec_t, v, preferred_element_type=F32)
        return r0, intra + inter, new_state

    def scan_body(n, carry):
        sf, sb = carry
        r0, of, sf = one_chunk(n, dec_f, kd_f, qd_f, cd_f, sf)
        of_ref[pl.ds(r0, CHUNK), :] = of
        r0, ob, sb = one_chunk(n_chunks - 1 - n, dec_b, kd_b, qd_b, cd_b, sb)
        ob_ref[pl.ds(r0, CHUNK), :] = ob
        return sf, sb

    zeros = jnp.zeros((HEAD_DIM, HEAD_DIM), F32)
    lax.fori_loop(0, n_chunks, scan_body, (zeros, zeros), unroll=RET_UNROLL)

    rows = RET_NORM_ROWS

    def norm_body(n, carry):
        r0 = pl.multiple_of(n * rows, rows)
        o = of_ref[pl.ds(r0, rows), :] + ob_ref[pl.ds(r0, rows), :]
        mu = jnp.mean(o, axis=-1, keepdims=True)
        var = jnp.mean(jnp.square(o - mu), axis=-1, keepdims=True)
        yn = (o - mu) * lax.rsqrt(var + EPS) * gain_ref[...]
        o_ref[pl.ds(r0, rows), :] = (sg_ref[pl.ds(r0, rows), :].astype(F32) * yn).astype(BF16)
        return carry

    lax.fori_loop(0, q_ref.shape[0] // rows, norm_body, 0)


def _retention(lg, rq, rk, rv, sg, gn_gain, batch, seq):
    t = rq.shape[0]
    blk = pl.BlockSpec((seq, HEAD_DIM), lambda b, h: (b, h))
    return pl.pallas_call(
        _retention_kernel,
        grid=(batch, RET_HEADS),
        in_specs=[
            pl.BlockSpec(memory_space=pltpu.SMEM),
            blk, blk, blk, blk,
            pl.BlockSpec((1, HEAD_DIM), lambda b, h: (0, h)),
        ],
        out_specs=blk,
        out_shape=jax.ShapeDtypeStruct((t, RET_DIM), BF16),
        scratch_shapes=[pltpu.VMEM((seq, HEAD_DIM), F32), pltpu.VMEM((seq, HEAD_DIM), F32)],
        compiler_params=_params(("parallel", "parallel"), V7X_VMEM_LIMIT_BYTES),
        name="retention",
    )(lg, rq, rk, rv, sg, gn_gain)


def _attention_kernel(q_ref, k_ref, vt_ref, o_ref, qa_ref):
    tq = q_ref.shape[0]
    n_rest = (k_ref.shape[0] - ATT_FIRST_KV) // ATT_KV_TILE
    lane = lax.broadcasted_iota(I32, (tq, 2 * HEAD_DIM), 1)

    def set_query(hh, shift):
        q = q_ref[:, hh * HEAD_DIM:(hh + 1) * HEAD_DIM]
        qa = jnp.concatenate([q, jnp.zeros((tq, HEAD_DIM), BF16)], axis=1)
        if shift is not None:
            qa = jnp.where(lane == HEAD_DIM, (-shift.T).astype(BF16), qa)
        qa_ref[hh] = qa

    def shifted_scores(hh, r0, rows):
        return lax.dot_general(k_ref[pl.ds(r0, rows), :], qa_ref[hh], (((1,), (1,)), ((), ())),
                               preferred_element_type=F32)

    def weigh(p, vt):
        return jnp.sum(p, axis=0, keepdims=True), jnp.dot(vt, p.astype(BF16), preferred_element_type=F32)

    vt = vt_ref[:, pl.ds(0, ATT_FIRST_KV)]
    state = []
    for hh in range(ATT_GROUP):
        set_query(hh, None)
        s = shifted_scores(hh, 0, ATT_FIRST_KV)
        shift = jnp.max(s, axis=0, keepdims=True).astype(BF16).astype(F32)
        l, acc = weigh(jnp.exp2(s - shift), vt)
        set_query(hh, shift)
        state.append((shift, l, acc))

    def body(c, carry):
        r0 = pl.multiple_of(ATT_FIRST_KV + c * ATT_KV_TILE, 2 * LANES)
        vt = vt_ref[:, pl.ds(r0, ATT_KV_TILE)]
        fast = []
        overflow = None
        for hh, (shift, l, acc) in enumerate(carry):
            s = shifted_scores(hh, r0, ATT_KV_TILE)
            dl, dacc = weigh(jnp.exp2(s), vt)
            l_new = l + dl
            acc_new = acc + dacc
            fast.append((shift, l_new, acc_new))
            bad = jnp.maximum(jnp.max(jnp.where(jnp.isfinite(l_new), 0.0, 1.0)),
                              jnp.max(jnp.where(jnp.isfinite(acc_new), 0.0, 1.0)))
            overflow = bad if overflow is None else jnp.maximum(overflow, bad)

        def redo():
            out = []
            for hh, (shift, l, acc) in enumerate(carry):
                s = shifted_scores(hh, r0, ATT_KV_TILE)
                raised = (shift + jnp.maximum(jnp.max(s, axis=0, keepdims=True), 0.0)).astype(BF16).astype(F32)
                step = raised - shift
                alpha = jnp.exp2(-step)
                dl, dacc = weigh(jnp.exp2(s - step), vt)
                out.append((raised, alpha * l + dl, alpha * acc + dacc))
                set_query(hh, raised)
            return tuple(out)

        return lax.cond(overflow == 0.0, lambda: tuple(fast), redo)

    final = lax.fori_loop(0, n_rest, body, tuple(state))
    for hh, (_, l, acc) in enumerate(final):
        o_ref[:, hh * HEAD_DIM:(hh + 1) * HEAD_DIM] = (acc / l).T.astype(BF16)


def _attention(aq, ak, av_t, batch, seq):
    t = aq.shape[0]
    tq = ATT_Q_TILE
    nq = seq // tq
    gw = ATT_GROUP * HEAD_DIM
    q_spec = pl.BlockSpec((tq, gw), lambda b, g, i: (b * nq + i, g))
    k_spec = pl.BlockSpec((seq, 2 * HEAD_DIM), lambda b, g, i: (b, g))
    vt_spec = pl.BlockSpec((HEAD_DIM, seq), lambda b, g, i: (g, b))
    return pl.pallas_call(
        _attention_kernel,
        grid=(batch, ATT_KV_HEADS, nq),
        in_specs=[q_spec, k_spec, vt_spec],
        out_specs=q_spec,
        out_shape=jax.ShapeDtypeStruct((t, ATT_DIM), BF16),
        scratch_shapes=[pltpu.VMEM((ATT_GROUP, tq, 2 * HEAD_DIM), BF16)],
        compiler_params=_params(("parallel", "parallel", "parallel"), V7X_VMEM_LIMIT_BYTES),
        name="attention",
    )(aq, ak, av_t)


def _merge_kernel(x_ref, ur_ref, ya_ref, gr_ref, ga_ref, wr_ref, wa_ref, wo_ref, nf_ref, wrt_ref,
                  x1_ref, h2a_ref, aff_ref, *, n_exp):
    d = x_ref.shape[1]
    y_ret = jnp.dot(ur_ref[...], wr_ref[...].astype(BF16), preferred_element_type=F32)
    y_att = jnp.dot(ya_ref[...], wa_ref[...].astype(BF16), preferred_element_type=F32)
    mixed = gr_ref[...].astype(F32) * y_ret + ga_ref[...].astype(F32) * y_att
    x1 = x_ref[...] + jnp.dot(mixed.astype(BF16), wo_ref[...].astype(BF16), preferred_element_type=F32)
    x1_ref[...] = x1
    h2 = x1 * lax.rsqrt(jnp.mean(x1 * x1, axis=-1, keepdims=True) + EPS) * nf_ref[...]
    h2a_ref[:, :d] = h2
    h_hi = h2.astype(BF16)
    h_lo = (h2 - h_hi.astype(F32)).astype(BF16)
    r_hi = jnp.dot(h_hi, wrt_ref[...], preferred_element_type=F32)
    r_lo = jnp.dot(h_lo, wrt_ref[:, :LANES], preferred_element_type=F32)
    logits = r_hi[:, :LANES] + r_hi[:, LANES:] + r_lo
    lane = lax.broadcasted_iota(I32, logits.shape, 1)
    logits = jnp.where(lane < n_exp, logits, -jnp.inf)
    e = jnp.exp(logits - jnp.max(logits, axis=-1, keepdims=True))
    aff = e / jnp.sum(e, axis=-1, keepdims=True)
    h2a_ref[:, d:] = aff
    aff_ref[...] = aff[:, :n_exp]


def _merge(xf, u_ret, y_att, gr, ga, w_ret, w_att, w_o, norm_ffn, w_router_split, n_exp):
    t, d = xf.shape
    tm = TOKEN_TILE
    row = lambda i: (i, 0)
    const = lambda i: (0, 0)
    return pl.pallas_call(
        functools.partial(_merge_kernel, n_exp=n_exp),
        grid=(t // tm,),
        in_specs=[
            pl.BlockSpec((tm, d), row),
            pl.BlockSpec((tm, RET_DIM), row),
            pl.BlockSpec((tm, ATT_DIM), row),
            pl.BlockSpec((tm, d), row),
            pl.BlockSpec((tm, d), row),
            pl.BlockSpec((RET_DIM, d), const, pipeline_mode=pl.Buffered(1)),
            pl.BlockSpec((ATT_DIM, d), const, pipeline_mode=pl.Buffered(1)),
            pl.BlockSpec((d, d), const, pipeline_mode=pl.Buffered(1)),
            pl.BlockSpec((1, d), const),
            pl.BlockSpec((d, 2 * LANES), const),
        ],
        out_specs=[pl.BlockSpec((tm, d), row), pl.BlockSpec((tm, d + LANES), row),
                   pl.BlockSpec((tm, n_exp), row)],
        out_shape=[jax.ShapeDtypeStruct((t, d), F32), jax.ShapeDtypeStruct((t, d + LANES), F32),
                   jax.ShapeDtypeStruct((t, n_exp), F32)],
        compiler_params=_params(("parallel",), V7X_VMEM_LIMIT_BYTES),
        name="merge",
    )(xf, u_ret, y_att, gr, ga, w_ret, w_att, w_o, norm_ffn, w_router_split)


def _route_kernel(aff_ref, affc_ref, idx_ref, pos_ref, slo_ref, msk_ref, cum_ref, cum_t_ref, bnd_ref, bnd_smem,
                  cnt_ref, sem, *, cap, n_exp):
    b = pl.program_id(0)
    seq = aff_ref.shape[0]
    n_tiles = seq // ROUTE_TILE
    aff = aff_ref[...]
    affc = affc_ref[...]
    capf = float(cap)

    def per_expert(x, op, reduce):
        r = reduce(x.reshape(x.shape[0] // 8, 8, LANES), axis=0)
        shift = n_exp
        while shift < LANES:
            r = op(r, pltpu.roll(r, shift, axis=1))
            shift *= 2
        return reduce(r, axis=0, keepdims=True)

    def bisect(lo, hi):
        mid = 0.5 * (lo + hi)
        ok = per_expert(jnp.where(affc >= mid, 1.0, 0.0), jnp.add, jnp.sum) >= capf
        return jnp.where(ok, mid, lo), jnp.where(ok, hi, mid)

    def bracket(lo, hi):
        mn = per_expert(jnp.where(affc >= lo, affc, jnp.inf), jnp.minimum, jnp.min)
        mx = per_expert(jnp.where(affc < hi, affc, -jnp.inf), jnp.maximum, jnp.max)
        return mn, mx

    def not_isolated(lo, hi):
        mn, mx = bracket(lo, hi)
        return jnp.max(jnp.where(mn < mx, 1.0, 0.0)) > 0.0

    lo, hi = lax.fori_loop(0, BISECT_STEPS, lambda i, c: bisect(*c),
                           (jnp.zeros((1, LANES), F32), jnp.full((1, LANES), 2.0, F32)))

    def refine(c):
        lo, hi = bisect(c[0], c[1])
        return lo, hi, not_isolated(lo, hi)

    lo, hi, _ = lax.while_loop(lambda c: c[2], refine, (lo, hi, not_isolated(lo, hi)))
    thr, _ = bracket(lo, hi)
    gt = aff > thr
    eq = aff == thr
    n_gt = jnp.sum(jnp.where(gt, 1.0, 0.0).reshape(seq // 8, 8, LANES), axis=0)
    need = capf - jnp.sum(n_gt, axis=0, keepdims=True)

    tri = (lax.broadcasted_iota(I32, (ROUTE_TILE, ROUTE_TILE), 0)
           >= lax.broadcasted_iota(I32, (ROUTE_TILE, ROUTE_TILE), 1))
    tri = jnp.where(tri, 1.0, 0.0).astype(BF16)

    def cumsum_tokens(store_tile_start):
        def body(c, carry):
            r0 = pl.multiple_of(c * ROUTE_TILE, ROUTE_TILE)
            if store_tile_start:
                slo_ref[0, pl.ds(c, 1), :] = carry[:, :n_exp].astype(I32)
                bnd_ref[pl.ds(c, 1), :] = carry.astype(I32)
            cs = jnp.dot(tri, msk_ref[pl.ds(r0, ROUTE_TILE), :], preferred_element_type=F32) + carry
            cum_ref[pl.ds(r0, ROUTE_TILE), :] = cs
            return cs[ROUTE_TILE - 1:ROUTE_TILE, :]

        return lax.fori_loop(0, n_tiles, body, jnp.zeros((1, LANES), F32))

    msk_ref[...] = jnp.where(eq, 1.0, 0.0).astype(BF16)
    cumsum_tokens(False)
    take = jnp.logical_and(eq, cum_ref[...] - 1.0 < need)
    mask = jnp.logical_or(gt, take)
    msk_ref[...] = jnp.where(mask, 1.0, 0.0).astype(BF16)
    bnd_ref[...] = jnp.zeros(bnd_ref.shape, I32)
    bnd_ref[pl.ds(n_tiles, 1), :] = cumsum_tokens(True).astype(I32)
    pos_ref[...] = jnp.where(mask, cum_ref[...] - 1.0, -1.0)[:, :n_exp].astype(I32)

    to_smem = pltpu.make_async_copy(bnd_ref, bnd_smem, sem)
    to_smem.start()
    to_smem.wait()

    def transpose_tile(c, carry):
        r0 = pl.multiple_of(c * ROUTE_TILE, ROUTE_TILE)
        cum_t_ref[:, pl.ds(r0, ROUTE_TILE)] = cum_ref[pl.ds(r0, ROUTE_TILE), :].T
        return carry

    lax.fori_loop(0, n_tiles, transpose_tile, 0)
    slot_row = lax.broadcasted_iota(I32, (1, cap), 1).astype(F32)
    slot_sub = lax.broadcasted_iota(I32, (LANES, ROUTE_TILE), 0)
    eye = lax.broadcasted_iota(I32, (LANES, LANES), 0) == lax.broadcasted_iota(I32, (LANES, LANES), 1)
    for e in range(n_exp):
        ends = bnd_ref[pl.ds(1, n_tiles), e:e + 1].astype(F32)
        whole = jnp.sum(jnp.where(slot_row >= ends, float(ROUTE_TILE), 0.0), axis=0, keepdims=True)
        cnt_ref[...] = jnp.zeros(cnt_ref.shape, F32)

        def tile_body(c, carry, e=e):
            lo = bnd_smem[c, e]
            hi = bnd_smem[c + 1, e]
            r0 = pl.multiple_of(c * ROUTE_TILE, ROUTE_TILE)
            counts = cum_t_ref[e:e + 1, pl.ds(r0, ROUTE_TILE)]

            def lane_tile(k, carry2):
                slots = k * LANES + slot_sub
                below_hi = jnp.where(slots < hi, 1.0, 0.0)
                hit = jnp.where(counts <= slots.astype(F32), below_hi, 0.0)
                folded = hit[:, :LANES]
                for t0 in range(LANES, ROUTE_TILE, LANES):
                    folded = folded + hit[:, t0:t0 + LANES]
                cnt_ref[k] += folded
                return carry2

            lax.fori_loop(lo // LANES, (hi + LANES - 1) // LANES, lane_tile, 0)
            return carry

        lax.fori_loop(0, n_tiles, tile_body, 0)
        for k in range(cap // LANES):
            per_slot = jnp.sum(cnt_ref[k], axis=1, keepdims=True)
            part = jnp.sum(jnp.where(eye, per_slot, 0.0), axis=0, keepdims=True)
            part = part + whole[:, k * LANES:(k + 1) * LANES]
            idx_ref[0, e:e + 1, k * LANES:(k + 1) * LANES] = part.astype(I32) + b * seq


def _route(h2a, aff, batch, seq, cap):
    t, n_exp = aff.shape
    pack = LANES // n_exp
    aff_packed = aff.reshape(t // pack, LANES)
    aff_block = h2a.shape[1] // LANES - 1
    n_tiles = seq // ROUTE_TILE
    bnd_rows = -(-(n_tiles + 1) // 8) * 8
    return pl.pallas_call(
        functools.partial(_route_kernel, cap=cap, n_exp=n_exp),
        grid=(batch,),
        in_specs=[pl.BlockSpec((seq, LANES), lambda b: (b, aff_block)),
                  pl.BlockSpec((seq // pack, LANES), lambda b: (b, 0))],
        out_specs=[pl.BlockSpec((1, n_exp, cap), lambda b: (b, 0, 0)),
                   pl.BlockSpec((seq, n_exp), lambda b: (b, 0)),
                   pl.BlockSpec((1, n_tiles, n_exp), lambda b: (b, 0, 0))],
        out_shape=[jax.ShapeDtypeStruct((batch, n_exp, cap), I32),
                   jax.ShapeDtypeStruct((t, n_exp), I32),
                   jax.ShapeDtypeStruct((batch, n_tiles, n_exp), I32)],
        scratch_shapes=[pltpu.VMEM((seq, LANES), BF16), pltpu.VMEM((seq, LANES), F32),
                        pltpu.VMEM((LANES, seq), F32),
                        pltpu.VMEM((bnd_rows, LANES), I32), pltpu.SMEM((bnd_rows, LANES), I32),
                        pltpu.VMEM((cap // LANES, LANES, LANES), F32), pltpu.SemaphoreType.DMA(())],
        compiler_params=_params(("parallel",), V7X_VMEM_LIMIT_BYTES),
        name="route",
    )(h2a, aff_packed)


def _ffn_kernel(idx_ref, h2a_hbm, wg_hbm, wu_hbm, wd_hbm, y_ref, stage_ref, xe_ref, gate_ref, acc_ref,
                wg_buf, wu_buf, wd_buf, sem, wsem, *, n_exp, n_ff, rows_per_step):
    e = pl.program_id(0)
    mp = stage_ref.shape[0]
    m, d = xe_ref.shape
    batch = y_ref.shape[0]
    cap = y_ref.shape[2]

    def row_copy(expert, i):
        r = idx_ref[expert * mp + i]
        return pltpu.make_async_copy(h2a_hbm.at[pl.ds(r, 1)], stage_ref.at[pl.ds(i, 1)], sem)

    def wait_rows():
        pltpu.make_async_copy(h2a_hbm.at[pl.ds(0, mp)], stage_ref, sem).wait()

    def weight_copies(chunk, slot):
        ex = chunk // n_ff
        c0 = pl.multiple_of((chunk % n_ff) * FF_TILE, FF_TILE)
        return (pltpu.make_async_copy(wg_hbm.at[ex, :, pl.ds(c0, FF_TILE)], wg_buf.at[slot], wsem.at[slot]),
                pltpu.make_async_copy(wu_hbm.at[ex, :, pl.ds(c0, FF_TILE)], wu_buf.at[slot], wsem.at[slot]),
                pltpu.make_async_copy(wd_hbm.at[ex, pl.ds(c0, FF_TILE), :], wd_buf.at[slot], wsem.at[slot]))

    @pl.when(e == 0)
    def _():
        def issue(i, carry):
            row_copy(0, i).start()
            return carry

        lax.fori_loop(0, mp, issue, 0)
        for cp in weight_copies(0, 0):
            cp.start()

    wait_rows()
    xe_ref[...] = stage_ref[:m, :d].astype(BF16)
    lane = lax.broadcasted_iota(I32, (m, LANES), 1)
    gate_ref[...] = jnp.sum(jnp.where(lane == e, stage_ref[:m, d:], 0.0), axis=1, keepdims=True)
    acc_ref[...] = jnp.zeros(acc_ref.shape, F32)

    nxt = jnp.minimum(e + 1, n_exp - 1)

    def ff_tile(f, carry):
        chunk = e * n_ff + f
        slot = chunk % 2

        for cp in weight_copies(chunk, slot):
            cp.wait()

        @pl.when(chunk + 1 < n_exp * n_ff)
        def _():
            for cp in weight_copies(chunk + 1, 1 - slot):
                cp.start()

        for k in range(rows_per_step):
            row_copy(nxt, f * rows_per_step + k).start()
        wg = wg_buf[slot].astype(BF16)
        wu = wu_buf[slot].astype(BF16)
        wd = wd_buf[slot].astype(BF16)
        for rb in range(m // FFN_ROW_BLOCK):
            rows = pl.ds(rb * FFN_ROW_BLOCK, FFN_ROW_BLOCK)
            xb = xe_ref[rows, :]
            a = jnp.dot(xb, wg, preferred_element_type=F32)
            u = jnp.dot(xb, wu, preferred_element_type=F32)
            hm = (a * jax.nn.sigmoid(a) * u).astype(BF16)
            acc_ref[rows, :] += jnp.dot(hm, wd, preferred_element_type=F32)
        return carry

    lax.fori_loop(0, n_ff, ff_tile, 0)

    for bb in range(batch):
        rows = pl.ds(bb * cap, cap)
        y_ref[bb, 0] = (acc_ref[rows, :] * gate_ref[rows, :]).astype(BF16)

    @pl.when(e == n_exp - 1)
    def _():
        wait_rows()


def _ffn(idx, h2a, w_gate, w_up, w_down):
    batch, n_exp, cap = idx.shape
    d = h2a.shape[1] - LANES
    ff = w_gate.shape[2]
    n_ff = ff // FF_TILE
    m = batch * cap
    rows_per_step = -(-pl.cdiv(m, n_ff) // 8) * 8
    mp = n_ff * rows_per_step
    idx_flat = jnp.pad(idx.transpose(1, 0, 2).reshape(n_exp, m), ((0, 0), (0, mp - m))).reshape(-1)
    any_spec = pl.BlockSpec(memory_space=pl.ANY)
    grid_spec = pltpu.PrefetchScalarGridSpec(
        num_scalar_prefetch=1,
        grid=(n_exp,),
        in_specs=[any_spec, any_spec, any_spec, any_spec],
        out_specs=pl.BlockSpec((batch, 1, cap, d), lambda e, idx: (0, e, 0, 0)),
        scratch_shapes=[pltpu.VMEM((mp, d + LANES), F32), pltpu.VMEM((m, d), BF16),
                        pltpu.VMEM((m, 1), F32), pltpu.VMEM((m, d), F32),
                        pltpu.VMEM((2, d, FF_TILE), F32), pltpu.VMEM((2, d, FF_TILE), F32),
                        pltpu.VMEM((2, FF_TILE, d), F32),
                        pltpu.SemaphoreType.DMA(()), pltpu.SemaphoreType.DMA((2,))],
    )
    return pl.pallas_call(
        functools.partial(_ffn_kernel, n_exp=n_exp, n_ff=n_ff, rows_per_step=rows_per_step),
        grid_spec=grid_spec,
        out_shape=jax.ShapeDtypeStruct((batch, n_exp, cap, d), BF16),
        compiler_params=_params(("arbitrary",), V7X_VMEM_LIMIT_BYTES),
        name="ffn",
    )(idx_flat, h2a, w_gate, w_up, w_down)


def _window_fits(s_hi, start, rows):
    return s_hi - start <= rows


def _combine_kernel(slo_ref, x1_ref, pos_ref, y_hbm, nf_ref, o_ref, slab_ref, big_ref, acc_ref, sem, big_sem,
                    *, n_exp, cap):
    b = pl.program_id(0)
    j = pl.program_id(1)
    n_batch = pl.num_programs(0)
    n_tiles = pl.num_programs(1)
    step = b * n_tiles + j
    slot = step % 2

    def bounds(bb, jj, e):
        base = (bb * (n_tiles + 1) + jj) * n_exp + e
        return slo_ref[base], slo_ref[base + n_exp]

    def window_start(bb, jj, e, rows):
        s_lo, _ = bounds(bb, jj, e)
        return pl.multiple_of(jnp.minimum((s_lo // SLAB_ALIGN) * SLAB_ALIGN, cap - rows), SLAB_ALIGN)

    def all_small(bb, jj):
        ok = None
        for e in range(n_exp):
            fits = _window_fits(bounds(bb, jj, e)[1], window_start(bb, jj, e, SMALL_ROWS), SMALL_ROWS)
            ok = fits if ok is None else jnp.logical_and(ok, fits)
        return ok

    def small_copy(bb, jj, e, dst_slot):
        return pltpu.make_async_copy(
            y_hbm.at[bb, e, pl.ds(window_start(bb, jj, e, SMALL_ROWS), SMALL_ROWS)],
            slab_ref.at[dst_slot, pl.ds(e * SMALL_ROWS, SMALL_ROWS)], sem.at[dst_slot])

    def finish(moe):
        x2 = x1_ref[...] + moe
        o_ref[...] = x2 * lax.rsqrt(jnp.mean(x2 * x2, axis=-1, keepdims=True) + EPS) * nf_ref[...]

    last = step + 1 == n_batch * n_tiles

    @pl.when(step == 0)
    def _():
        big_ref[...] = jnp.zeros(big_ref.shape, big_ref.dtype)
        for e in range(n_exp):
            small_copy(b, j, e, slot).start()

    for e in range(n_exp):
        small_copy(b, j, e, slot).wait()
    wrap = j + 1 == n_tiles
    nb = jnp.where(last, b, jnp.where(wrap, b + 1, b))
    nj = jnp.where(last, j, jnp.where(wrap, 0, j + 1))
    for e in range(n_exp):
        small_copy(nb, nj, e, 1 - slot).start()

    lane = lax.broadcasted_iota(I32, (1, SMALL_ROWS), 1)
    acc = None
    for e0 in range(0, n_exp, COMBINE_GROUP):
        tiles = []
        for e in range(e0, min(e0 + COMBINE_GROUP, n_exp)):
            rel = pos_ref[:, e:e + 1] - window_start(b, j, e, SMALL_ROWS)
            tiles.append(jnp.where(rel == lane, 1.0, 0.0).astype(BF16))
        rows = pl.ds(e0 * SMALL_ROWS, len(tiles) * SMALL_ROWS)
        part = jnp.dot(jnp.concatenate(tiles, axis=1), slab_ref[slot, rows, :],
                       preferred_element_type=F32)
        acc = part if acc is None else acc + part

    small_now = all_small(b, j)

    @pl.when(small_now)
    def _():
        finish(acc)

    @pl.when(jnp.logical_not(small_now))
    def _():
        wide_lane = lax.broadcasted_iota(I32, (1, BIG_PAD), 1)
        for e in range(n_exp):
            st = window_start(b, j, e, BIG_ROWS)
            cp = pltpu.make_async_copy(y_hbm.at[b, e, pl.ds(st, BIG_ROWS)],
                                       big_ref.at[pl.ds(0, BIG_ROWS)], big_sem)
            cp.start()
            cp.wait()
            onehot = jnp.where(pos_ref[:, e:e + 1] - st == wide_lane, 1.0, 0.0).astype(BF16)
            part = jnp.dot(onehot, big_ref[...], preferred_element_type=F32)
            if e == 0:
                acc_ref[...] = part
            else:
                acc_ref[...] += part
        finish(acc_ref[...])

    @pl.when(last)
    def _():
        for e in range(n_exp):
            small_copy(b, j, e, 1 - slot).wait()


def _combine(slo, x1, pos, y_slots, norm_final, seq):
    t, d = x1.shape
    batch, n_exp, cap, _ = y_slots.shape
    n_tiles = seq // ROUTE_TILE
    slo_flat = jnp.concatenate([slo, jnp.full((batch, 1, n_exp), cap, I32)], axis=1).reshape(-1)
    row = lambda b, j, slo: (b * n_tiles + j, 0)
    grid_spec = pltpu.PrefetchScalarGridSpec(
        num_scalar_prefetch=1,
        grid=(batch, n_tiles),
        in_specs=[
            pl.BlockSpec((ROUTE_TILE, d), row),
            pl.BlockSpec((ROUTE_TILE, n_exp), row),
            pl.BlockSpec(memory_space=pl.ANY),
            pl.BlockSpec((1, d), lambda b, j, slo: (0, 0)),
        ],
        out_specs=pl.BlockSpec((ROUTE_TILE, d), row),
        scratch_shapes=[pltpu.VMEM((2, n_exp * SMALL_ROWS, d), BF16), pltpu.VMEM((BIG_PAD, d), BF16),
                        pltpu.VMEM((ROUTE_TILE, d), F32),
                        pltpu.SemaphoreType.DMA((2,)), pltpu.SemaphoreType.DMA(())],
    )
    return pl.pallas_call(
        functools.partial(_combine_kernel, n_exp=n_exp, cap=cap),
        grid_spec=grid_spec,
        out_shape=jax.ShapeDtypeStruct((t, d), F32),
        compiler_params=_params(("arbitrary", "arbitrary"), V7X_VMEM_LIMIT_BYTES),
        name="combine",
    )(slo_flat, x1, pos, y_slots, norm_final)


def _rope_tables(seq):
    rows = seq // GRID_W
    row = np.repeat(np.arange(rows, dtype=np.float64), GRID_W)
    col = np.tile(np.arange(GRID_W, dtype=np.float64), rows)
    n_freq = HALF_ROT // 2
    freqs = ROPE_THETA ** (-np.arange(n_freq, dtype=np.float64) / n_freq)
    ang = np.concatenate([row[:, None] * freqs, col[:, None] * freqs], axis=-1)
    cos, sin = np.cos(ang), np.sin(ang)
    return (jnp.asarray(np.concatenate([cos, cos], axis=-1), F32),
            jnp.asarray(np.concatenate([-sin, sin], axis=-1), F32))


def kernel(x, norm_mix, w_in, ret_decay_fwd, ret_decay_bwd, ret_gn_gain, w_ret_branch, q_norm, k_norm,
           w_att_branch, w_o, norm_ffn, w_router, w_expert_gate, w_expert_up, w_expert_down, norm_final):
    batch, seq, d = x.shape
    depth = norm_mix.shape[0]
    n_exp = w_router.shape[2]
    cap = CAPACITY_FACTOR * seq // n_exp
    assert seq % TOKEN_TILE == 0 and seq % ROUTE_TILE == 0
    assert (seq - ATT_FIRST_KV) % ATT_KV_TILE == 0 and ATT_KV_TILE % (2 * LANES) == 0
    assert ATT_FIRST_KV % (2 * LANES) == 0 and seq % ATT_Q_TILE == 0
    assert seq % CHUNK == 0 and (seq // CHUNK) % RET_UNROLL == 0 and seq % RET_NORM_ROWS == 0
    assert LANES % n_exp == 0 and d % LANES == 0 and seq % (8 * LANES // n_exp) == 0
    assert cap >= BIG_ROWS and (cap - BIG_ROWS) % SLAB_ALIGN == 0 and (cap - SMALL_ROWS) % SLAB_ALIGN == 0
    assert w_expert_gate.shape[3] % FF_TILE == 0 and (batch * cap) % FFN_ROW_BLOCK == 0
    assert depth == 1, "the final RMSNorm is fused into the combine kernel of the single layer"
    l = 0
    cos2, sin2 = _rope_tables(seq)
    xf = x.reshape(batch * seq, d)
    lg = jnp.stack([jnp.log1p(-jnp.exp(ret_decay_fwd[l].astype(F32))),
                    jnp.log1p(-jnp.exp(ret_decay_bwd[l].astype(F32)))])
    rq, rk, rv, sg, aq, ak, av, gr, ga = _inproj(
        xf, norm_mix[l][None], w_in[l], cos2, sin2, q_norm[l][None], k_norm[l][None], seq)
    u_ret = _retention(lg, rq, rk, rv, sg, ret_gn_gain[l][None], batch, seq)
    y_att = _attention(aq, ak, av, batch, seq)
    w_r = jnp.pad(w_router[l].astype(F32), ((0, 0), (0, LANES - n_exp)))
    w_r_hi = w_r.astype(BF16)
    w_r_lo = (w_r - w_r_hi.astype(F32)).astype(BF16)
    x1, h2a, aff = _merge(xf, u_ret, y_att, gr, ga, w_ret_branch[l], w_att_branch[l], w_o[l],
                          norm_ffn[l][None], jnp.concatenate([w_r_hi, w_r_lo], axis=1), n_exp)
    idx, pos, slo = _route(h2a, aff, batch, seq, cap)
    y_slots = _ffn(idx, h2a, w_expert_gate[l], w_expert_up[l], w_expert_down[l])
    out = _combine(slo, x1, pos, y_slots, norm_final[None], seq)
    return out.reshape(batch, seq, d)
```

```python
import functools

import jax
import jax.numpy as jnp
import numpy as np
from jax import lax
from jax.experimental import pallas as pl
from jax.experimental.pallas import tpu as pltpu

F32 = jnp.float32
BF16 = jnp.bfloat16
I32 = jnp.int32

GRID_W = 64
HEAD_DIM = 128
HALF_ROT = HEAD_DIM // 2
RET_HEADS = 4
RET_DIM = RET_HEADS * HEAD_DIM
ATT_HEADS = 4
ATT_KV_HEADS = 2
ATT_GROUP = ATT_HEADS // ATT_KV_HEADS
ATT_DIM = ATT_HEADS * HEAD_DIM
ATT_KV_DIM = ATT_KV_HEADS * HEAD_DIM
CHUNK = 256
ROPE_THETA = 10000.0
CAPACITY_FACTOR = 2
EPS = 1e-6
LOG2_E = 1.4426950408889634

V7X_VMEM_LIMIT_BYTES = 56 * 1024 * 1024
TOKEN_TILE = 512
ATT_Q_TILE = 1024
ATT_FIRST_KV = 512
ATT_KV_TILE = 3840
RET_NORM_ROWS = 2048
RET_UNROLL = 4
LANES = 128
ROUTE_TILE = 512
SLAB_ALIGN = 16
SMALL_ROWS = LANES
BIG_ROWS = ROUTE_TILE + SLAB_ALIGN
BIG_PAD = -(-BIG_ROWS // LANES) * LANES
COMBINE_GROUP = 4
FF_TILE = 256
FFN_ROW_BLOCK = 512
BISECT_STEPS = 24


def _params(sem, vmem=None):
    return pltpu.CompilerParams(dimension_semantics=sem, vmem_limit_bytes=vmem)


def _inproj_kernel(x_ref, g_ref, w_ref, cos_ref, sin_ref, qn_ref, kn_ref,
                   rq_ref, rk_ref, rv_ref, sg_ref, aq_ref, ak_ref, av_ref, gr_ref, ga_ref):
    x = x_ref[...]
    h = (x * lax.rsqrt(jnp.mean(x * x, axis=-1, keepdims=True) + EPS) * g_ref[...]).astype(BF16)
    cos = cos_ref[...]
    sin = sin_ref[...]
    d_model = x.shape[1]
    scale = HEAD_DIM ** -0.5

    def proj(lo, width):
        return jnp.dot(h, w_ref[:, lo:lo + width].astype(BF16), preferred_element_type=F32)

    def rope(t):
        return t * cos + pltpu.roll(t, HALF_ROT, axis=1) * sin

    def head_norm(t, gain):
        return t * lax.rsqrt(jnp.mean(t * t, axis=-1, keepdims=True) + EPS) * gain

    def head(p, i):
        return p[:, i * HEAD_DIM:(i + 1) * HEAD_DIM]

    def put(ref, i, val):
        ref[:, i * HEAD_DIM:(i + 1) * HEAD_DIM] = val.astype(ref.dtype)

    off = 0
    p = proj(off, RET_DIM)
    for i in range(RET_HEADS):
        put(rq_ref, i, rope(head(p, i)))
    off += RET_DIM
    p = proj(off, RET_DIM)
    for i in range(RET_HEADS):
        put(rk_ref, i, rope(head(p, i)) * scale)
    off += RET_DIM
    rv_ref[...] = proj(off, RET_DIM).astype(BF16)
    off += RET_DIM
    p = proj(off, RET_DIM)
    sg_ref[...] = (p * jax.nn.sigmoid(p)).astype(BF16)
    off += RET_DIM
    p = proj(off, ATT_DIM)
    for i in range(ATT_HEADS):
        put(aq_ref, i, rope(head_norm(head(p, i), qn_ref[...])) * (scale * LOG2_E))
    off += ATT_DIM
    p = proj(off, ATT_KV_DIM)
    ones_col = jnp.where(lax.broadcasted_iota(I32, (x.shape[0], HEAD_DIM), 1) == 0, 1.0, 0.0)
    for i in range(ATT_KV_HEADS):
        put(ak_ref, 2 * i, rope(head_norm(head(p, i), kn_ref[...])))
        put(ak_ref, 2 * i + 1, ones_col)
    off += ATT_KV_DIM
    av_ref[...] = proj(off, ATT_KV_DIM).T.astype(BF16)
    off += ATT_KV_DIM
    gr_ref[...] = jax.nn.sigmoid(proj(off, d_model)).astype(BF16)
    off += d_model
    ga_ref[...] = jax.nn.sigmoid(proj(off, d_model)).astype(BF16)


def _inproj(xf, norm_g, w_in, cos2, sin2, q_norm, k_norm, seq):
    t, d = xf.shape
    tm = TOKEN_TILE
    in_width = w_in.shape[1]
    steps_per_seq = seq // tm
    row = lambda i: (i, 0)
    const = lambda i: (0, 0)
    pos = lambda i: (i % steps_per_seq, 0)
    widths = (RET_DIM, RET_DIM, RET_DIM, RET_DIM, ATT_DIM, 2 * ATT_KV_DIM, ATT_KV_DIM, d, d)
    av_index = 6
    return pl.pallas_call(
        _inproj_kernel,
        grid=(t // tm,),
        in_specs=[
            pl.BlockSpec((tm, d), row),
            pl.BlockSpec((1, d), const),
            pl.BlockSpec((d, in_width), const, pipeline_mode=pl.Buffered(1)),
            pl.BlockSpec((tm, HEAD_DIM), pos),
            pl.BlockSpec((tm, HEAD_DIM), pos),
            pl.BlockSpec((1, HEAD_DIM), const),
            pl.BlockSpec((1, HEAD_DIM), const),
        ],
        out_specs=[pl.BlockSpec((ATT_KV_DIM, tm), lambda i: (0, i)) if k == av_index
                   else pl.BlockSpec((tm, w), row) for k, w in enumerate(widths)],
        out_shape=[jax.ShapeDtypeStruct((ATT_KV_DIM, t) if k == av_index else (t, w), BF16)
                   for k, w in enumerate(widths)],
        compiler_params=_params(("parallel",), V7X_VMEM_LIMIT_BYTES),
        name="inproj",
    )(xf, norm_g, w_in, cos2, sin2, q_norm, k_norm)


def _retention_kernel(lg_ref, q_ref, k_ref, v_ref, sg_ref, gain_ref, o_ref, of_ref, ob_ref):
    hd = pl.program_id(1)
    lgf = lg_ref[0, hd]
    lgb = lg_ref[1, hd]
    n_chunks = q_ref.shape[0] // CHUNK
    ii = lax.broadcasted_iota(I32, (CHUNK, CHUNK), 0)
    jj = lax.broadcasted_iota(I32, (CHUNK, CHUNK), 1)
    dist = (ii - jj).astype(F32)
    dec_f = jnp.where(dist >= 0, jnp.exp(lgf * jnp.maximum(dist, 0.0)), 0.0)
    dec_b = jnp.where(dist < 0, jnp.exp(lgb * jnp.maximum(-dist, 0.0)), 0.0)
    t = lax.broadcasted_iota(I32, (CHUNK, 1), 0).astype(F32)
    kd_f = jnp.exp(lgf * (CHUNK - 1.0 - t))
    qd_f = jnp.exp(lgf * (t + 1.0))
    kd_b = jnp.exp(lgb * t)
    qd_b = jnp.exp(lgb * (CHUNK - t))
    zero_row = jnp.zeros((1, HEAD_DIM), F32)
    cd_f = jnp.exp(zero_row + lgf * CHUNK)
    cd_b = jnp.exp(zero_row + lgb * CHUNK)

    def one_chunk(n, dec, kd, qd, cd, state):
        r0 = pl.multiple_of(n * CHUNK, CHUNK)
        q = q_ref[pl.ds(r0, CHUNK), :]
        k = k_ref[pl.ds(r0, CHUNK), :]
        v = v_ref[pl.ds(r0, CHUNK), :]
        s = lax.dot_general(q, k, (((1,), (1,)), ((), ())), preferred_element_type=F32) * dec
        intra = jnp.dot(s.astype(BF16), v, preferred_element_type=F32)
        q_dec = (q.astype(F32) * qd).astype(BF16)
        inter = jnp.dot(q_dec, state.astype(BF16), preferred_element_type=F32)
        k_dec_t = (k.astype(F32) * kd).T.astype(BF16)
        new_state = state * cd + jnp.dot(k_dec_t, v, preferred_element_type=F32)
        return r0, intra + inter, new_state

    def scan_body(n, carry):
        sf, sb = carry
        r0, of, sf = one_chunk(n, dec_f, kd_f, qd_f, cd_f, sf)
        of_ref[pl.ds(r0, CHUNK), :] = of
        r0, ob, sb = one_chunk(n_chunks - 1 - n, dec_b, kd_b, qd_b, cd_b, sb)
        ob_ref[pl.ds(r0, CHUNK), :] = ob
        return sf, sb

    zeros = jnp.zeros((HEAD_DIM, HEAD_DIM), F32)
    lax.fori_loop(0, n_chunks, scan_body, (zeros, zeros), unroll=RET_UNROLL)

    rows = RET_NORM_ROWS

    def norm_body(n, carry):
        r0 = pl.multiple_of(n * rows, rows)
        o = of_ref[pl.ds(r0, rows), :] + ob_ref[pl.ds(r0, rows), :]
        mu = jnp.mean(o, axis=-1, keepdims=True)
        var = jnp.mean(jnp.square(o - mu), axis=-1, keepdims=True)
        yn = (o - mu) * lax.rsqrt(var + EPS) * gain_ref[...]
        o_ref[pl.ds(r0, rows), :] = (sg_ref[pl.ds(r0, rows), :].astype(F32) * yn).astype(BF16)
        return carry

    lax.fori_loop(0, q_ref.shape[0] // rows, norm_body, 0)


def _retention(lg, rq, rk, rv, sg, gn_gain, batch, seq):
    t = rq.shape[0]
    blk = pl.BlockSpec((seq, HEAD_DIM), lambda b, h: (b, h))
    return pl.pallas_call(
        _retention_kernel,
        grid=(batch, RET_HEADS),
        in_specs=[
            pl.BlockSpec(memory_space=pltpu.SMEM),
            blk, blk, blk, blk,
            pl.BlockSpec((1, HEAD_DIM), lambda b, h: (0, h)),
        ],
        out_specs=blk,
        out_shape=jax.ShapeDtypeStruct((t, RET_DIM), BF16),
        scratch_shapes=[pltpu.VMEM((seq, HEAD_DIM), F32), pltpu.VMEM((seq, HEAD_DIM), F32)],
        compiler_params=_params(("parallel", "parallel"), V7X_VMEM_LIMIT_BYTES),
        name="retention",
    )(lg, rq, rk, rv, sg, gn_gain)


def _attention_kernel(q_ref, k_ref, vt_ref, o_ref, qa_ref):
    tq = q_ref.shape[0]
    n_rest = (k_ref.shape[0] - ATT_FIRST_KV) // ATT_KV_TILE
    lane = lax.broadcasted_iota(I32, (tq, 2 * HEAD_DIM), 1)

    def set_query(hh, shift):
        q = q_ref[:, hh * HEAD_DIM:(hh + 1) * HEAD_DIM]
        qa = jnp.concatenate([q, jnp.zeros((tq, HEAD_DIM), BF16)], axis=1)
        if shift is not None:
            qa = jnp.where(lane == HEAD_DIM, (-shift.T).astype(BF16), qa)
        qa_ref[hh] = qa

    def shifted_scores(hh, r0, rows):
        return lax.dot_general(k_ref[pl.ds(r0, rows), :], qa_ref[hh], (((1,), (1,)), ((), ())),
                               preferred_element_type=F32)

    def weigh(p, vt):
        return jnp.sum(p, axis=0, keepdims=True), jnp.dot(vt, p.astype(BF16), preferred_element_type=F32)

    vt = vt_ref[:, pl.ds(0, ATT_FIRST_KV)]
    state = []
    for hh in range(ATT_GROUP):
        set_query(hh, None)
        s = shifted_scores(hh, 0, ATT_FIRST_KV)
        shift = jnp.max(s, axis=0, keepdims=True).astype(BF16).astype(F32)
        l, acc = weigh(jnp.exp2(s - shift), vt)
        set_query(hh, shift)
        state.append((shift, l, acc))

    def body(c, carry):
        r0 = pl.multiple_of(ATT_FIRST_KV + c * ATT_KV_TILE, 2 * LANES)
        vt = vt_ref[:, pl.ds(r0, ATT_KV_TILE)]
        fast = []
        overflow = None
        for hh, (shift, l, acc) in enumerate(carry):
            s = shifted_scores(hh, r0, ATT_KV_TILE)
            dl, dacc = weigh(jnp.exp2(s), vt)
            l_new = l + dl
            acc_new = acc + dacc
            fast.append((shift, l_new, acc_new))
            bad = jnp.maximum(jnp.max(jnp.where(jnp.isfinite(l_new), 0.0, 1.0)),
                              jnp.max(jnp.where(jnp.isfinite(acc_new), 0.0, 1.0)))
            overflow = bad if overflow is None else jnp.maximum(overflow, bad)

        def redo():
            out = []
            for hh, (shift, l, acc) in enumerate(carry):
                s = shifted_scores(hh, r0, ATT_KV_TILE)
                raised = (shift + jnp.maximum(jnp.max(s, axis=0, keepdims=True), 0.0)).astype(BF16).astype(F32)
                step = raised - shift
                alpha = jnp.exp2(-step)
                dl, dacc = weigh(jnp.exp2(s - step), vt)
                out.append((raised, alpha * l + dl, alpha * acc + dacc))
                set_query(hh, raised)
            return tuple(out)

        return lax.cond(overflow == 0.0, lambda: tuple(fast), redo)

    final = lax.fori_loop(0, n_rest, body, tuple(state))
    for hh, (_, l, acc) in enumerate(final):
        o_ref[:, hh * HEAD_DIM:(hh + 1) * HEAD_DIM] = (acc / l).T.astype(BF16)


def _attention(aq, ak, av_t, batch, seq):
    t = aq.shape[0]
    tq = ATT_Q_TILE
    nq = seq // tq
    gw = ATT_GROUP * HEAD_DIM
    q_spec = pl.BlockSpec((tq, gw), lambda b, g, i: (b * nq + i, g))
    k_spec = pl.BlockSpec((seq, 2 * HEAD_DIM), lambda b, g, i: (b, g))
    vt_spec = pl.BlockSpec((HEAD_DIM, seq), lambda b, g, i: (g, b))
    return pl.pallas_call(
        _attention_kernel,
        grid=(batch, ATT_KV_HEADS, nq),
        in_specs=[q_spec, k_spec, vt_spec],
        out_specs=q_spec,
        out_shape=jax.ShapeDtypeStruct((t, ATT_DIM), BF16),
        scratch_shapes=[pltpu.VMEM((ATT_GROUP, tq, 2 * HEAD_DIM), BF16)],
        compiler_params=_params(("parallel", "parallel", "parallel"), V7X_VMEM_LIMIT_BYTES),
        name="attention",
    )(aq, ak, av_t)


def _merge_kernel(x_ref, ur_ref, ya_ref, gr_ref, ga_ref, wr_ref, wa_ref, wo_ref, nf_ref, wrt_ref,
                  x1_ref, h2a_ref, aff_ref, *, n_exp):
    d = x_ref.shape[1]
    y_ret = jnp.dot(ur_ref[...], wr_ref[...].astype(BF16), preferred_element_type=F32)
    y_att = jnp.dot(ya_ref[...], wa_ref[...].astype(BF16), preferred_element_type=F32)
    mixed = gr_ref[...].astype(F32) * y_ret + ga_ref[...].astype(F32) * y_att
    x1 = x_ref[...] + jnp.dot(mixed.astype(BF16), wo_ref[...].astype(BF16), preferred_element_type=F32)
    x1_ref[...] = x1
    h2 = x1 * lax.rsqrt(jnp.mean(x1 * x1, axis=-1, keepdims=True) + EPS) * nf_ref[...]
    h2a_ref[:, :d] = h2
    h_hi = h2.astype(BF16)
    h_lo = (h2 - h_hi.astype(F32)).astype(BF16)
    r_hi = jnp.dot(h_hi, wrt_ref[...], preferred_element_type=F32)
    r_lo = jnp.dot(h_lo, wrt_ref[:, :LANES], preferred_element_type=F32)
    logits = r_hi[:, :LANES] + r_hi[:, LANES:] + r_lo
    lane = lax.broadcasted_iota(I32, logits.shape, 1)
    logits = jnp.where(lane < n_exp, logits, -jnp.inf)
    e = jnp.exp(logits - jnp.max(logits, axis=-1, keepdims=True))
    aff = e / jnp.sum(e, axis=-1, keepdims=True)
    h2a_ref[:, d:] = aff
    aff_ref[...] = aff[:, :n_exp]


def _merge(xf, u_ret, y_att, gr, ga, w_ret, w_att, w_o, norm_ffn, w_router_split, n_exp):
    t, d = xf.shape
    tm = TOKEN_TILE
    row = lambda i: (i, 0)
    const = lambda i: (0, 0)
    return pl.pallas_call(
        functools.partial(_merge_kernel, n_exp=n_exp),
        grid=(t // tm,),
        in_specs=[
            pl.BlockSpec((tm, d), row),
            pl.BlockSpec((tm, RET_DIM), row),
            pl.BlockSpec((tm, ATT_DIM), row),
            pl.BlockSpec((tm, d), row),
            pl.BlockSpec((tm, d), row),
            pl.BlockSpec((RET_DIM, d), const, pipeline_mode=pl.Buffered(1)),
            pl.BlockSpec((ATT_DIM, d), const, pipeline_mode=pl.Buffered(1)),
            pl.BlockSpec((d, d), const, pipeline_mode=pl.Buffered(1)),
            pl.BlockSpec((1, d), const),
            pl.BlockSpec((d, 2 * LANES), const),
        ],
        out_specs=[pl.BlockSpec((tm, d), row), pl.BlockSpec((tm, d + LANES), row),
                   pl.BlockSpec((tm, n_exp), row)],
        out_shape=[jax.ShapeDtypeStruct((t, d), F32), jax.ShapeDtypeStruct((t, d + LANES), F32),
                   jax.ShapeDtypeStruct((t, n_exp), F32)],
        compiler_params=_params(("parallel",), V7X_VMEM_LIMIT_BYTES),
        name="merge",
    )(xf, u_ret, y_att, gr, ga, w_ret, w_att, w_o, norm_ffn, w_router_split)


def _route_kernel(aff_ref, affc_ref, idx_ref, pos_ref, slo_ref, msk_ref, cum_ref, cum_t_ref, bnd_ref, bnd_smem,
                  cnt_ref, sem, *, cap, n_exp):
    b = pl.program_id(0)
    seq = aff_ref.shape[0]
    n_tiles = seq // ROUTE_TILE
    aff = aff_ref[...]
    affc = affc_ref[...]
    capf = float(cap)

    def per_expert(x, op, reduce):
        r = reduce(x.reshape(x.shape[0] // 8, 8, LANES), axis=0)
        shift = n_exp
        while shift < LANES:
            r = op(r, pltpu.roll(r, shift, axis=1))
            shift *= 2
        return reduce(r, axis=0, keepdims=True)

    def bisect(lo, hi):
        mid = 0.5 * (lo + hi)
        ok = per_expert(jnp.where(affc >= mid, 1.0, 0.0), jnp.add, jnp.sum) >= capf
        return jnp.where(ok, mid, lo), jnp.where(ok, hi, mid)

    def bracket(lo, hi):
        mn = per_expert(jnp.where(affc >= lo, affc, jnp.inf), jnp.minimum, jnp.min)
        mx = per_expert(jnp.where(affc < hi, affc, -jnp.inf), jnp.maximum, jnp.max)
        return mn, mx

    def not_isolated(lo, hi):
        mn, mx = bracket(lo, hi)
        return jnp.max(jnp.where(mn < mx, 1.0, 0.0)) > 0.0

    lo, hi = lax.fori_loop(0, BISECT_STEPS, lambda i, c: bisect(*c),
                           (jnp.zeros((1, LANES), F32), jnp.full((1, LANES), 2.0, F32)))

    def refine(c):
        lo, hi = bisect(c[0], c[1])
        return lo, hi, not_isolated(lo, hi)

    lo, hi, _ = lax.while_loop(lambda c: c[2], refine, (lo, hi, not_isolated(lo, hi)))
    thr, _ = bracket(lo, hi)
    gt = aff > thr
    eq = aff == thr
    n_gt = jnp.sum(jnp.where(gt, 1.0, 0.0).reshape(seq // 8, 8, LANES), axis=0)
    need = capf - jnp.sum(n_gt, axis=0, keepdims=True)

    tri = (lax.broadcasted_iota(I32, (ROUTE_TILE, ROUTE_TILE), 0)
           >= lax.broadcasted_iota(I32, (ROUTE_TILE, ROUTE_TILE), 1))
    tri = jnp.where(tri, 1.0, 0.0).astype(BF16)

    def cumsum_tokens(store_tile_start):
        def body(c, carry):
            r0 = pl.multiple_of(c * ROUTE_TILE, ROUTE_TILE)
            if store_tile_start:
                slo_ref[0, pl.ds(c, 1), :] = carry[:, :n_exp].astype(I32)
                bnd_ref[pl.ds(c, 1), :] = carry.astype(I32)
            cs = jnp.dot(tri, msk_ref[pl.ds(r0, ROUTE_TILE), :], preferred_element_type=F32) + carry
            cum_ref[pl.ds(r0, ROUTE_TILE), :] = cs
            return cs[ROUTE_TILE - 1:ROUTE_TILE, :]

        return lax.fori_loop(0, n_tiles, body, jnp.zeros((1, LANES), F32))

    msk_ref[...] = jnp.where(eq, 1.0, 0.0).astype(BF16)
    cumsum_tokens(False)
    take = jnp.logical_and(eq, cum_ref[...] - 1.0 < need)
    mask = jnp.logical_or(gt, take)
    msk_ref[...] = jnp.where(mask, 1.0, 0.0).astype(BF16)
    bnd_ref[...] = jnp.zeros(bnd_ref.shape, I32)
    bnd_ref[pl.ds(n_tiles, 1), :] = cumsum_tokens(True).astype(I32)
    pos_ref[...] = jnp.where(mask, cum_ref[...] - 1.0, -1.0)[:, :n_exp].astype(I32)

    to_smem = pltpu.make_async_copy(bnd_ref, bnd_smem, sem)
    to_smem.start()
    to_smem.wait()

    def transpose_tile(c, carry):
        r0 = pl.multiple_of(c * ROUTE_TILE, ROUTE_TILE)
        cum_t_ref[:, pl.ds(r0, ROUTE_TILE)] = cum_ref[pl.ds(r0, ROUTE_TILE), :].T
        return carry

    lax.fori_loop(0, n_tiles, transpose_tile, 0)
    slot_row = lax.broadcasted_iota(I32, (1, cap), 1).astype(F32)
    slot_sub = lax.broadcasted_iota(I32, (LANES, ROUTE_TILE), 0)
    eye = lax.broadcasted_iota(I32, (LANES, LANES), 0) == lax.broadcasted_iota(I32, (LANES, LANES), 1)
    for e in range(n_exp):
        ends = bnd_ref[pl.ds(1, n_tiles), e:e + 1].astype(F32)
        whole = jnp.sum(jnp.where(slot_row >= ends, float(ROUTE_TILE), 0.0), axis=0, keepdims=True)
        cnt_ref[...] = jnp.zeros(cnt_ref.shape, F32)

        def tile_body(c, carry, e=e):
            lo = bnd_smem[c, e]
            hi = bnd_smem[c + 1, e]
            r0 = pl.multiple_of(c * ROUTE_TILE, ROUTE_TILE)
            counts = cum_t_ref[e:e + 1, pl.ds(r0, ROUTE_TILE)]

            def lane_tile(k, carry2):
                slots = k * LANES + slot_sub
                below_hi = jnp.where(slots < hi, 1.0, 0.0)
                hit = jnp.where(counts <= slots.astype(F32), below_hi, 0.0)
                folded = hit[:, :LANES]
                for t0 in range(LANES, ROUTE_TILE, LANES):
                    folded = folded + hit[:, t0:t0 + LANES]
                cnt_ref[k] += folded
                return carry2

            lax.fori_loop(lo // LANES, (hi + LANES - 1) // LANES, lane_tile, 0)
            return carry

        lax.fori_loop(0, n_tiles, tile_body, 0)
        for k in range(cap // LANES):
            per_slot = jnp.sum(cnt_ref[k], axis=1, keepdims=True)
            part = jnp.sum(jnp.where(eye, per_slot, 0.0), axis=0, keepdims=True)
            part = part + whole[:, k * LANES:(k + 1) * LANES]
            idx_ref[0, e:e + 1, k * LANES:(k + 1) * LANES] = part.astype(I32) + b * seq


def _route(h2a, aff, batch, seq, cap):
    t, n_exp = aff.shape
    pack = LANES // n_exp
    aff_packed = aff.reshape(t // pack, LANES)
    aff_block = h2a.shape[1] // LANES - 1
    n_tiles = seq // ROUTE_TILE
    bnd_rows = -(-(n_tiles + 1) // 8) * 8
    return pl.pallas_call(
        functools.partial(_route_kernel, cap=cap, n_exp=n_exp),
        grid=(batch,),
        in_specs=[pl.BlockSpec((seq, LANES), lambda b: (b, aff_block)),
                  pl.BlockSpec((seq // pack, LANES), lambda b: (b, 0))],
        out_specs=[pl.BlockSpec((1, n_exp, cap), lambda b: (b, 0, 0)),
                   pl.BlockSpec((seq, n_exp), lambda b: (b, 0)),
                   pl.BlockSpec((1, n_tiles, n_exp), lambda b: (b, 0, 0))],
        out_shape=[jax.ShapeDtypeStruct((batch, n_exp, cap), I32),
                   jax.ShapeDtypeStruct((t, n_exp), I32),
                   jax.ShapeDtypeStruct((batch, n_tiles, n_exp), I32)],
        scratch_shapes=[pltpu.VMEM((seq, LANES), BF16), pltpu.VMEM((seq, LANES), F32),
                        pltpu.VMEM((LANES, seq), F32),
                        pltpu.VMEM((bnd_rows, LANES), I32), pltpu.SMEM((bnd_rows, LANES), I32),
                        pltpu.VMEM((cap // LANES, LANES, LANES), F32), pltpu.SemaphoreType.DMA(())],
        compiler_params=_params(("parallel",), V7X_VMEM_LIMIT_BYTES),
        name="route",
    )(h2a, aff_packed)


def _ffn_kernel(idx_ref, h2a_hbm, wg_hbm, wu_hbm, wd_hbm, y_ref, stage_ref, xe_ref, gate_ref, acc_ref,
                wg_buf, wu_buf, wd_buf, sem, wsem, *, n_exp, n_ff, rows_per_step):
    e = pl.program_id(0)
    mp = stage_ref.shape[0]
    m, d = xe_ref.shape
    batch = y_ref.shape[0]
    cap = y_ref.shape[2]

    def row_copy(expert, i):
        r = idx_ref[expert * mp + i]
        return pltpu.make_async_copy(h2a_hbm.at[pl.ds(r, 1)], stage_ref.at[pl.ds(i, 1)], sem)

    def wait_rows():
        pltpu.make_async_copy(h2a_hbm.at[pl.ds(0, mp)], stage_ref, sem).wait()

    def weight_copies(chunk, slot):
        ex = chunk // n_ff
        c0 = pl.multiple_of((chunk % n_ff) * FF_TILE, FF_TILE)
        return (pltpu.make_async_copy(wg_hbm.at[ex, :, pl.ds(c0, FF_TILE)], wg_buf.at[slot], wsem.at[slot]),
                pltpu.make_async_copy(wu_hbm.at[ex, :, pl.ds(c0, FF_TILE)], wu_buf.at[slot], wsem.at[slot]),
                pltpu.make_async_copy(wd_hbm.at[ex, pl.ds(c0, FF_TILE), :], wd_buf.at[slot], wsem.at[slot]))

    @pl.when(e == 0)
    def _():
        def issue(i, carry):
            row_copy(0, i).start()
            return carry

        lax.fori_loop(0, mp, issue, 0)
        for cp in weight_copies(0, 0):
            cp.start()

    wait_rows()
    lane = lax.broadcasted_iota(I32, (m, LANES), 1)
    gate_ref[...] = jnp.sum(jnp.where(lane == e, stage_ref[:m, d:], 0.0), axis=1, keepdims=True)

    nxt = jnp.minimum(e + 1, n_exp - 1)

    def gather_slice(j):
        for k in range(rows_per_step):
            row_copy(nxt, j * rows_per_step + k).start()

    def ff_tile(f, first):
        chunk = e * n_ff + f
        slot = chunk % 2

        for cp in weight_copies(chunk, slot):
            cp.wait()

        @pl.when(chunk + 1 < n_exp * n_ff)
        def _():
            for cp in weight_copies(chunk + 1, 1 - slot):
                cp.start()

        if not first:
            gather_slice(f - 1)

        wg = wg_buf[slot].astype(BF16)
        wu = wu_buf[slot].astype(BF16)
        wd = wd_buf[slot].astype(BF16)
        for rb in range(m // FFN_ROW_BLOCK):
            rows = pl.ds(rb * FFN_ROW_BLOCK, FFN_ROW_BLOCK)
            if first:
                xb = stage_ref[rows, :d].astype(BF16)
                xe_ref[rows, :] = xb
            else:
                xb = xe_ref[rows, :]
            a = jnp.dot(xb, wg, preferred_element_type=F32)
            u = jnp.dot(xb, wu, preferred_element_type=F32)
            hm = (a * jax.nn.sigmoid(a) * u).astype(BF16)
            contrib = jnp.dot(hm, wd, preferred_element_type=F32)
            if first:
                acc_ref[rows, :] = contrib
            else:
                acc_ref[rows, :] += contrib

    ff_tile(0, True)

    def later_tile(f, carry):
        ff_tile(f, False)
        return carry

    lax.fori_loop(1, n_ff, later_tile, 0)

    for bb in range(batch):
        rows = pl.ds(bb * cap, cap)
        y_ref[bb, 0] = (acc_ref[rows, :] * gate_ref[rows, :]).astype(BF16)

    @pl.when(e == n_exp - 1)
    def _():
        wait_rows()


def _ffn(idx, h2a, w_gate, w_up, w_down):
    batch, n_exp, cap = idx.shape
    d = h2a.shape[1] - LANES
    ff = w_gate.shape[2]
    n_ff = ff // FF_TILE
    assert n_ff >= 2, "the next expert's rows are gathered during ff tiles 1.."
    m = batch * cap
    rows_per_step = -(-pl.cdiv(m, n_ff - 1) // 8) * 8
    mp = (n_ff - 1) * rows_per_step
    idx_flat = jnp.pad(idx.transpose(1, 0, 2).reshape(n_exp, m), ((0, 0), (0, mp - m))).reshape(-1)
    any_spec = pl.BlockSpec(memory_space=pl.ANY)
    grid_spec = pltpu.PrefetchScalarGridSpec(
        num_scalar_prefetch=1,
        grid=(n_exp,),
        in_specs=[any_spec, any_spec, any_spec, any_spec],
        out_specs=pl.BlockSpec((batch, 1, cap, d), lambda e, idx: (0, e, 0, 0)),
        scratch_shapes=[pltpu.VMEM((mp, d + LANES), F32), pltpu.VMEM((m, d), BF16),
                        pltpu.VMEM((m, 1), F32), pltpu.VMEM((m, d), F32),
                        pltpu.VMEM((2, d, FF_TILE), F32), pltpu.VMEM((2, d, FF_TILE), F32),
                        pltpu.VMEM((2, FF_TILE, d), F32),
                        pltpu.SemaphoreType.DMA(()), pltpu.SemaphoreType.DMA((2,))],
    )
    return pl.pallas_call(
        functools.partial(_ffn_kernel, n_exp=n_exp, n_ff=n_ff, rows_per_step=rows_per_step),
        grid_spec=grid_spec,
        out_shape=jax.ShapeDtypeStruct((batch, n_exp, cap, d), BF16),
        compiler_params=_params(("arbitrary",), V7X_VMEM_LIMIT_BYTES),
        name="ffn",
    )(idx_flat, h2a, w_gate, w_up, w_down)


def _window_fits(s_hi, start, rows):
    return s_hi - start <= rows


def _combine_kernel(slo_ref, x1_ref, pos_ref, y_hbm, nf_ref, o_ref, slab_ref, big_ref, acc_ref, sem, big_sem,
                    *, n_exp, cap):
    b = pl.program_id(0)
    j = pl.program_id(1)
    n_batch = pl.num_programs(0)
    n_tiles = pl.num_programs(1)
    step = b * n_tiles + j
    slot = step % 2

    def bounds(bb, jj, e):
        base = (bb * (n_tiles + 1) + jj) * n_exp + e
        return slo_ref[base], slo_ref[base + n_exp]

    def window_start(bb, jj, e, rows):
        s_lo, _ = bounds(bb, jj, e)
        return pl.multiple_of(jnp.minimum((s_lo // SLAB_ALIGN) * SLAB_ALIGN, cap - rows), SLAB_ALIGN)

    def all_small(bb, jj):
        ok = None
        for e in range(n_exp):
            fits = _window_fits(bounds(bb, jj, e)[1], window_start(bb, jj, e, SMALL_ROWS), SMALL_ROWS)
            ok = fits if ok is None else jnp.logical_and(ok, fits)
        return ok

    def small_copy(bb, jj, e, dst_slot):
        return pltpu.make_async_copy(
            y_hbm.at[bb, e, pl.ds(window_start(bb, jj, e, SMALL_ROWS), SMALL_ROWS)],
            slab_ref.at[dst_slot, pl.ds(e * SMALL_ROWS, SMALL_ROWS)], sem.at[dst_slot])

    def finish(moe):
        x2 = x1_ref[...] + moe
        o_ref[...] = x2 * lax.rsqrt(jnp.mean(x2 * x2, axis=-1, keepdims=True) + EPS) * nf_ref[...]

    last = step + 1 == n_batch * n_tiles

    @pl.when(step == 0)
    def _():
        big_ref[...] = jnp.zeros(big_ref.shape, big_ref.dtype)
        for e in range(n_exp):
            small_copy(b, j, e, slot).start()

    for e in range(n_exp):
        small_copy(b, j, e, slot).wait()
    wrap = j + 1 == n_tiles
    nb = jnp.where(last, b, jnp.where(wrap, b + 1, b))
    nj = jnp.where(last, j, jnp.where(wrap, 0, j + 1))
    for e in range(n_exp):
        small_copy(nb, nj, e, 1 - slot).start()

    lane = lax.broadcasted_iota(I32, (1, SMALL_ROWS), 1)
    acc = None
    for e0 in range(0, n_exp, COMBINE_GROUP):
        tiles = []
        for e in range(e0, min(e0 + COMBINE_GROUP, n_exp)):
            rel = pos_ref[:, e:e + 1] - window_start(b, j, e, SMALL_ROWS)
            tiles.append(jnp.where(rel == lane, 1.0, 0.0).astype(BF16))
        rows = pl.ds(e0 * SMALL_ROWS, len(tiles) * SMALL_ROWS)
        part = jnp.dot(jnp.concatenate(tiles, axis=1), slab_ref[slot, rows, :],
                       preferred_element_type=F32)
        acc = part if acc is None else acc + part

    small_now = all_small(b, j)

    @pl.when(small_now)
    def _():
        finish(acc)

    @pl.when(jnp.logical_not(small_now))
    def _():
        wide_lane = lax.broadcasted_iota(I32, (1, BIG_PAD), 1)
        for e in range(n_exp):
            st = window_start(b, j, e, BIG_ROWS)
            cp = pltpu.make_async_copy(y_hbm.at[b, e, pl.ds(st, BIG_ROWS)],
                                       big_ref.at[pl.ds(0, BIG_ROWS)], big_sem)
            cp.start()
            cp.wait()
            onehot = jnp.where(pos_ref[:, e:e + 1] - st == wide_lane, 1.0, 0.0).astype(BF16)
            part = jnp.dot(onehot, big_ref[...], preferred_element_type=F32)
            if e == 0:
                acc_ref[...] = part
            else:
                acc_ref[...] += part
        finish(acc_ref[...])

    @pl.when(last)
    def _():
        for e in range(n_exp):
            small_copy(b, j, e, 1 - slot).wait()


def _combine(slo, x1, pos, y_slots, norm_final, seq):
    t, d = x1.shape
    batch, n_exp, cap, _ = y_slots.shape
    n_tiles = seq // ROUTE_TILE
    slo_flat = jnp.concatenate([slo, jnp.full((batch, 1, n_exp), cap, I32)], axis=1).reshape(-1)
    row = lambda b, j, slo: (b * n_tiles + j, 0)
    grid_spec = pltpu.PrefetchScalarGridSpec(
        num_scalar_prefetch=1,
        grid=(batch, n_tiles),
        in_specs=[
            pl.BlockSpec((ROUTE_TILE, d), row),
            pl.BlockSpec((ROUTE_TILE, n_exp), row),
            pl.BlockSpec(memory_space=pl.ANY),
            pl.BlockSpec((1, d), lambda b, j, slo: (0, 0)),
        ],
        out_specs=pl.BlockSpec((ROUTE_TILE, d), row),
        scratch_shapes=[pltpu.VMEM((2, n_exp * SMALL_ROWS, d), BF16), pltpu.VMEM((BIG_PAD, d), BF16),
                        pltpu.VMEM((ROUTE_TILE, d), F32),
                        pltpu.SemaphoreType.DMA((2,)), pltpu.SemaphoreType.DMA(())],
    )
    return pl.pallas_call(
        functools.partial(_combine_kernel, n_exp=n_exp, cap=cap),
        grid_spec=grid_spec,
        out_shape=jax.ShapeDtypeStruct((t, d), F32),
        compiler_params=_params(("arbitrary", "arbitrary"), V7X_VMEM_LIMIT_BYTES),
        name="combine",
    )(slo_flat, x1, pos, y_slots, norm_final)


def _rope_tables(seq):
    rows = seq // GRID_W
    row = np.repeat(np.arange(rows, dtype=np.float64), GRID_W)
    col = np.tile(np.arange(GRID_W, dtype=np.float64), rows)
    n_freq = HALF_ROT // 2
    freqs = ROPE_THETA ** (-np.arange(n_freq, dtype=np.float64) / n_freq)
    ang = np.concatenate([row[:, None] * freqs, col[:, None] * freqs], axis=-1)
    cos, sin = np.cos(ang), np.sin(ang)
    return (jnp.asarray(np.concatenate([cos, cos], axis=-1), F32),
            jnp.asarray(np.concatenate([-sin, sin], axis=-1), F32))


def kernel(x, norm_mix, w_in, ret_decay_fwd, ret_decay_bwd, ret_gn_gain, w_ret_branch, q_norm, k_norm,
           w_att_branch, w_o, norm_ffn, w_router, w_expert_gate, w_expert_up, w_expert_down, norm_final):
    batch, seq, d = x.shape
    depth = norm_mix.shape[0]
    n_exp = w_router.shape[2]
    cap = CAPACITY_FACTOR * seq // n_exp
    assert seq % TOKEN_TILE == 0 and seq % ROUTE_TILE == 0
    assert (seq - ATT_FIRST_KV) % ATT_KV_TILE == 0 and ATT_KV_TILE % (2 * LANES) == 0
    assert ATT_FIRST_KV % (2 * LANES) == 0 and seq % ATT_Q_TILE == 0
    assert seq % CHUNK == 0 and (seq // CHUNK) % RET_UNROLL == 0 and seq % RET_NORM_ROWS == 0
    assert LANES % n_exp == 0 and d % LANES == 0 and seq % (8 * LANES // n_exp) == 0
    assert cap >= BIG_ROWS and (cap - BIG_ROWS) % SLAB_ALIGN == 0 and (cap - SMALL_ROWS) % SLAB_ALIGN == 0
    assert w_expert_gate.shape[3] % FF_TILE == 0 and (batch * cap) % FFN_ROW_BLOCK == 0
    assert depth == 1, "the final RMSNorm is fused into the combine kernel of the single layer"
    l = 0
    cos2, sin2 = _rope_tables(seq)
    xf = x.reshape(batch * seq, d)
    lg = jnp.stack([jnp.log1p(-jnp.exp(ret_decay_fwd[l].astype(F32))),
                    jnp.log1p(-jnp.exp(ret_decay_bwd[l].astype(F32)))])
    rq, rk, rv, sg, aq, ak, av, gr, ga = _inproj(
        xf, norm_mix[l][None], w_in[l], cos2, sin2, q_norm[l][None], k_norm[l][None], seq)
    u_ret = _retention(lg, rq, rk, rv, sg, ret_gn_gain[l][None], batch, seq)
    y_att = _attention(aq, ak, av, batch, seq)
    w_r = jnp.pad(w_router[l].astype(F32), ((0, 0), (0, LANES - n_exp)))
    w_r_hi = w_r.astype(BF16)
    w_r_lo = (w_r - w_r_hi.astype(F32)).astype(BF16)
    x1, h2a, aff = _merge(xf, u_ret, y_att, gr, ga, w_ret_branch[l], w_att_branch[l], w_o[l],
                          norm_ffn[l][None], jnp.concatenate([w_r_hi, w_r_lo], axis=1), n_exp)
    idx, pos, slo = _route(h2a, aff, batch, seq, cap)
    y_slots = _ffn(idx, h2a, w_expert_gate[l], w_expert_up[l], w_expert_down[l])
    out = _combine(slo, x1, pos, y_slots, norm_final[None], seq)
    return out.reshape(batch, seq, d)
```

```python
import functools

import jax
import jax.numpy as jnp
import numpy as np
from jax import lax
from jax.experimental import pallas as pl
from jax.experimental.pallas import tpu as pltpu

F32 = jnp.float32
BF16 = jnp.bfloat16
I32 = jnp.int32

GRID_W = 64
HEAD_DIM = 128
HALF_ROT = HEAD_DIM // 2
RET_HEADS = 4
RET_DIM = RET_HEADS * HEAD_DIM
ATT_HEADS = 4
ATT_KV_HEADS = 2
ATT_GROUP = ATT_HEADS // ATT_KV_HEADS
ATT_DIM = ATT_HEADS * HEAD_DIM
ATT_KV_DIM = ATT_KV_HEADS * HEAD_DIM
CHUNK = 256
ROPE_THETA = 10000.0
CAPACITY_FACTOR = 2
EPS = 1e-6
LOG2_E = 1.4426950408889634

V7X_VMEM_LIMIT_BYTES = 56 * 1024 * 1024
TOKEN_TILE = 512
ATT_Q_TILE = 1024
ATT_FIRST_KV = 512
ATT_KV_TILE = 3840
RET_NORM_ROWS = 2048
RET_UNROLL = 4
LANES = 128
ROUTE_TILE = 512
SLAB_ALIGN = 16
SMALL_ROWS = LANES
BIG_ROWS = ROUTE_TILE + SLAB_ALIGN
BIG_PAD = -(-BIG_ROWS // LANES) * LANES
COMBINE_GROUP = 4
FF_TILE = 256
FFN_ROW_BLOCK = 512
BISECT_STEPS = 24


def _params(sem, vmem=None):
    return pltpu.CompilerParams(dimension_semantics=sem, vmem_limit_bytes=vmem)


def _inproj_kernel(x_ref, g_ref, w_ref, cos_ref, sin_ref, qn_ref, kn_ref,
                   rq_ref, rk_ref, rv_ref, sg_ref, aq_ref, ak_ref, av_ref, gr_ref, ga_ref):
    x = x_ref[...]
    h = (x * lax.rsqrt(jnp.mean(x * x, axis=-1, keepdims=True) + EPS) * g_ref[...]).astype(BF16)
    cos = cos_ref[...]
    sin = sin_ref[...]
    d_model = x.shape[1]
    scale = HEAD_DIM ** -0.5

    def proj(lo, width):
        return jnp.dot(h, w_ref[:, lo:lo + width].astype(BF16), preferred_element_type=F32)

    def rope(t):
        return t * cos + pltpu.roll(t, HALF_ROT, axis=1) * sin

    def head_norm(t, gain):
        return t * lax.rsqrt(jnp.mean(t * t, axis=-1, keepdims=True) + EPS) * gain

    def head(p, i):
        return p[:, i * HEAD_DIM:(i + 1) * HEAD_DIM]

    def put(ref, i, val):
        ref[:, i * HEAD_DIM:(i + 1) * HEAD_DIM] = val.astype(ref.dtype)

    off = 0
    p = proj(off, RET_DIM)
    for i in range(RET_HEADS):
        put(rq_ref, i, rope(head(p, i)))
    off += RET_DIM
    p = proj(off, RET_DIM)
    for i in range(RET_HEADS):
        put(rk_ref, i, rope(head(p, i)) * scale)
    off += RET_DIM
    rv_ref[...] = proj(off, RET_DIM).astype(BF16)
    off += RET_DIM
    p = proj(off, RET_DIM)
    sg_ref[...] = (p * jax.nn.sigmoid(p)).astype(BF16)
    off += RET_DIM
    p = proj(off, ATT_DIM)
    for i in range(ATT_HEADS):
        put(aq_ref, i, rope(head_norm(head(p, i), qn_ref[...])) * (scale * LOG2_E))
    off += ATT_DIM
    p = proj(off, ATT_KV_DIM)
    ones_col = jnp.where(lax.broadcasted_iota(I32, (x.shape[0], HEAD_DIM), 1) == 0, 1.0, 0.0)
    for i in range(ATT_KV_HEADS):
        put(ak_ref, 2 * i, rope(head_norm(head(p, i), kn_ref[...])))
        put(ak_ref, 2 * i + 1, ones_col)
    off += ATT_KV_DIM
    av_ref[...] = proj(off, ATT_KV_DIM).T.astype(BF16)
    off += ATT_KV_DIM
    gr_ref[...] = jax.nn.sigmoid(proj(off, d_model)).astype(BF16)
    off += d_model
    ga_ref[...] = jax.nn.sigmoid(proj(off, d_model)).astype(BF16)


def _inproj(xf, norm_g, w_in, cos2, sin2, q_norm, k_norm, seq):
    t, d = xf.shape
    tm = TOKEN_TILE
    in_width = w_in.shape[1]
    steps_per_seq = seq // tm
    row = lambda i: (i, 0)
    const = lambda i: (0, 0)
    pos = lambda i: (i % steps_per_seq, 0)
    widths = (RET_DIM, RET_DIM, RET_DIM, RET_DIM, ATT_DIM, 2 * ATT_KV_DIM, ATT_KV_DIM, d, d)
    av_index = 6
    return pl.pallas_call(
        _inproj_kernel,
        grid=(t // tm,),
        in_specs=[
            pl.BlockSpec((tm, d), row),
            pl.BlockSpec((1, d), const),
            pl.BlockSpec((d, in_width), const, pipeline_mode=pl.Buffered(1)),
            pl.BlockSpec((tm, HEAD_DIM), pos),
            pl.BlockSpec((tm, HEAD_DIM), pos),
            pl.BlockSpec((1, HEAD_DIM), const),
            pl.BlockSpec((1, HEAD_DIM), const),
        ],
        out_specs=[pl.BlockSpec((ATT_KV_DIM, tm), lambda i: (0, i)) if k == av_index
                   else pl.BlockSpec((tm, w), row) for k, w in enumerate(widths)],
        out_shape=[jax.ShapeDtypeStruct((ATT_KV_DIM, t) if k == av_index else (t, w), BF16)
                   for k, w in enumerate(widths)],
        compiler_params=_params(("parallel",), V7X_VMEM_LIMIT_BYTES),
        name="inproj",
    )(xf, norm_g, w_in, cos2, sin2, q_norm, k_norm)


def _retention_kernel(lg_ref, q_ref, k_ref, v_ref, sg_ref, gain_ref, o_ref, of_ref, ob_ref):
    hd = pl.program_id(1)
    lgf = lg_ref[0, hd]
    lgb = lg_ref[1, hd]
    n_chunks = q_ref.shape[0] // CHUNK
    ii = lax.broadcasted_iota(I32, (CHUNK, CHUNK), 0)
    jj = lax.broadcasted_iota(I32, (CHUNK, CHUNK), 1)
    dist = (ii - jj).astype(F32)
    dec_f = jnp.where(dist >= 0, jnp.exp(lgf * jnp.maximum(dist, 0.0)), 0.0)
    dec_b = jnp.where(dist < 0, jnp.exp(lgb * jnp.maximum(-dist, 0.0)), 0.0)
    t = lax.broadcasted_iota(I32, (CHUNK, 1), 0).astype(F32)
    kd_f = jnp.exp(lgf * (CHUNK - 1.0 - t))
    qd_f = jnp.exp(lgf * (t + 1.0))
    kd_b = jnp.exp(lgb * t)
    qd_b = jnp.exp(lgb * (CHUNK - t))
    zero_row = jnp.zeros((1, HEAD_DIM), F32)
    cd_f = jnp.exp(zero_row + lgf * CHUNK)
    cd_b = jnp.exp(zero_row + lgb * CHUNK)

    def one_chunk(n, dec, kd, qd, cd, state):
        r0 = pl.multiple_of(n * CHUNK, CHUNK)
        q = q_ref[pl.ds(r0, CHUNK), :]
        k = k_ref[pl.ds(r0, CHUNK), :]
        v = v_ref[pl.ds(r0, CHUNK), :]
        s = lax.dot_general(q, k, (((1,), (1,)), ((), ())), preferred_element_type=F32) * dec
        intra = jnp.dot(s.astype(BF16), v, preferred_element_type=F32)
        q_dec = (q.astype(F32) * qd).astype(BF16)
        inter = jnp.dot(q_dec, state.astype(BF16), preferred_element_type=F32)
        k_dec_t = (k.astype(F32) * kd).T.astype(BF16)
        new_state = state * cd + jnp.dot(k_dec_t, v, preferred_element_type=F32)
        return r0, intra + inter, new_state

    def scan_body(n, carry):
        sf, sb = carry
        r0, of, sf = one_chunk(n, dec_f, kd_f, qd_f, cd_f, sf)
        of_ref[pl.ds(r0, CHUNK), :] = of
        r0, ob, sb = one_chunk(n_chunks - 1 - n, dec_b, kd_b, qd_b, cd_b, sb)
        ob_ref[pl.ds(r0, CHUNK), :] = ob
        return sf, sb

    zeros = jnp.zeros((HEAD_DIM, HEAD_DIM), F32)
    lax.fori_loop(0, n_chunks, scan_body, (zeros, zeros), unroll=RET_UNROLL)

    rows = RET_NORM_ROWS

    def norm_body(n, carry):
        r0 = pl.multiple_of(n * rows, rows)
        o = of_ref[pl.ds(r0, rows), :] + ob_ref[pl.ds(r0, rows), :]
        mu = jnp.mean(o, axis=-1, keepdims=True)
        var = jnp.mean(jnp.square(o - mu), axis=-1, keepdims=True)
        yn = (o - mu) * lax.rsqrt(var + EPS) * gain_ref[...]
        o_ref[pl.ds(r0, rows), :] = (sg_ref[pl.ds(r0, rows), :].astype(F32) * yn).astype(BF16)
        return carry

    lax.fori_loop(0, q_ref.shape[0] // rows, norm_body, 0)


def _retention(lg, rq, rk, rv, sg, gn_gain, batch, seq):
    t = rq.shape[0]
    blk = pl.BlockSpec((seq, HEAD_DIM), lambda b, h: (b, h))
    return pl.pallas_call(
        _retention_kernel,
        grid=(batch, RET_HEADS),
        in_specs=[
            pl.BlockSpec(memory_space=pltpu.SMEM),
            blk, blk, blk, blk,
            pl.BlockSpec((1, HEAD_DIM), lambda b, h: (0, h)),
        ],
        out_specs=blk,
        out_shape=jax.ShapeDtypeStruct((t, RET_DIM), BF16),
        scratch_shapes=[pltpu.VMEM((seq, HEAD_DIM), F32), pltpu.VMEM((seq, HEAD_DIM), F32)],
        compiler_params=_params(("parallel", "parallel"), V7X_VMEM_LIMIT_BYTES),
        name="retention",
    )(lg, rq, rk, rv, sg, gn_gain)


def _attention_kernel(q_ref, k_ref, vt_ref, o_ref, qa_ref):
    tq = q_ref.shape[0]
    n_rest = (k_ref.shape[0] - ATT_FIRST_KV) // ATT_KV_TILE
    lane = lax.broadcasted_iota(I32, (tq, 2 * HEAD_DIM), 1)

    def set_query(hh, shift):
        q = q_ref[:, hh * HEAD_DIM:(hh + 1) * HEAD_DIM]
        qa = jnp.concatenate([q, jnp.zeros((tq, HEAD_DIM), BF16)], axis=1)
        if shift is not None:
            qa = jnp.where(lane == HEAD_DIM, (-shift.T).astype(BF16), qa)
        qa_ref[hh] = qa

    def shifted_scores(hh, r0, rows):
        return lax.dot_general(k_ref[pl.ds(r0, rows), :], qa_ref[hh], (((1,), (1,)), ((), ())),
                               preferred_element_type=F32)

    def weigh(p, vt):
        return jnp.sum(p, axis=0, keepdims=True), jnp.dot(vt, p.astype(BF16), preferred_element_type=F32)

    vt = vt_ref[:, pl.ds(0, ATT_FIRST_KV)]
    state = []
    for hh in range(ATT_GROUP):
        set_query(hh, None)
        s = shifted_scores(hh, 0, ATT_FIRST_KV)
        shift = jnp.max(s, axis=0, keepdims=True).astype(BF16).astype(F32)
        l, acc = weigh(jnp.exp2(s - shift), vt)
        set_query(hh, shift)
        state.append((shift, l, acc))

    def body(c, carry):
        r0 = pl.multiple_of(ATT_FIRST_KV + c * ATT_KV_TILE, 2 * LANES)
        vt = vt_ref[:, pl.ds(r0, ATT_KV_TILE)]
        fast = []
        overflow = None
        for hh, (shift, l, acc) in enumerate(carry):
            s = shifted_scores(hh, r0, ATT_KV_TILE)
            dl, dacc = weigh(jnp.exp2(s), vt)
            l_new = l + dl
            acc_new = acc + dacc
            fast.append((shift, l_new, acc_new))
            bad = jnp.maximum(jnp.max(jnp.where(jnp.isfinite(l_new), 0.0, 1.0)),
                              jnp.max(jnp.where(jnp.isfinite(acc_new), 0.0, 1.0)))
            overflow = bad if overflow is None else jnp.maximum(overflow, bad)

        def redo():
            out = []
            for hh, (shift, l, acc) in enumerate(carry):
                s = shifted_scores(hh, r0, ATT_KV_TILE)
                raised = (shift + jnp.maximum(jnp.max(s, axis=0, keepdims=True), 0.0)).astype(BF16).astype(F32)
                step = raised - shift
                alpha = jnp.exp2(-step)
                dl, dacc = weigh(jnp.exp2(s - step), vt)
                out.append((raised, alpha * l + dl, alpha * acc + dacc))
                set_query(hh, raised)
            return tuple(out)

        return lax.cond(overflow == 0.0, lambda: tuple(fast), redo)

    final = lax.fori_loop(0, n_rest, body, tuple(state))
    for hh, (_, l, acc) in enumerate(final):
        o_ref[:, hh * HEAD_DIM:(hh + 1) * HEAD_DIM] = (acc / l).T.astype(BF16)


def _attention(aq, ak, av_t, batch, seq):
    t = aq.shape[0]
    tq = ATT_Q_TILE
    nq = seq // tq
    gw = ATT_GROUP * HEAD_DIM
    q_spec = pl.BlockSpec((tq, gw), lambda b, g, i: (b * nq + i, g))
    k_spec = pl.BlockSpec((seq, 2 * HEAD_DIM), lambda b, g, i: (b, g))
    vt_spec = pl.BlockSpec((HEAD_DIM, seq), lambda b, g, i: (g, b))
    return pl.pallas_call(
        _attention_kernel,
        grid=(batch, ATT_KV_HEADS, nq),
        in_specs=[q_spec, k_spec, vt_spec],
        out_specs=q_spec,
        out_shape=jax.ShapeDtypeStruct((t, ATT_DIM), BF16),
        scratch_shapes=[pltpu.VMEM((ATT_GROUP, tq, 2 * HEAD_DIM), BF16)],
        compiler_params=_params(("parallel", "parallel", "parallel"), V7X_VMEM_LIMIT_BYTES),
        name="attention",
    )(aq, ak, av_t)


def _merge_kernel(x_ref, ur_ref, ya_ref, gr_ref, ga_ref, wr_ref, wa_ref, wo_ref, nf_ref, wrt_ref,
                  x1_ref, h2a_ref, aff_ref, *, n_exp):
    d = x_ref.shape[1]
    y_ret = jnp.dot(ur_ref[...], wr_ref[...].astype(BF16), preferred_element_type=F32)
    y_att = jnp.dot(ya_ref[...], wa_ref[...].astype(BF16), preferred_element_type=F32)
    mixed = gr_ref[...].astype(F32) * y_ret + ga_ref[...].astype(F32) * y_att
    x1 = x_ref[...] + jnp.dot(mixed.astype(BF16), wo_ref[...].astype(BF16), preferred_element_type=F32)
    x1_ref[...] = x1
    h2 = x1 * lax.rsqrt(jnp.mean(x1 * x1, axis=-1, keepdims=True) + EPS) * nf_ref[...]
    h2a_ref[:, :d] = h2
    h_hi = h2.astype(BF16)
    h_lo = (h2 - h_hi.astype(F32)).astype(BF16)
    r_hi = jnp.dot(h_hi, wrt_ref[...], preferred_element_type=F32)
    r_lo = jnp.dot(h_lo, wrt_ref[:, :LANES], preferred_element_type=F32)
    logits = r_hi[:, :LANES] + r_hi[:, LANES:] + r_lo
    lane = lax.broadcasted_iota(I32, logits.shape, 1)
    logits = jnp.where(lane < n_exp, logits, -jnp.inf)
    e = jnp.exp(logits - jnp.max(logits, axis=-1, keepdims=True))
    aff = e / jnp.sum(e, axis=-1, keepdims=True)
    h2a_ref[:, d:] = aff
    aff_ref[...] = aff[:, :n_exp]


def _merge(xf, u_ret, y_att, gr, ga, w_ret, w_att, w_o, norm_ffn, w_router_split, n_exp):
    t, d = xf.shape
    tm = TOKEN_TILE
    row = lambda i: (i, 0)
    const = lambda i: (0, 0)
    return pl.pallas_call(
        functools.partial(_merge_kernel, n_exp=n_exp),
        grid=(t // tm,),
        in_specs=[
            pl.BlockSpec((tm, d), row),
            pl.BlockSpec((tm, RET_DIM), row),
            pl.BlockSpec((tm, ATT_DIM), row),
            pl.BlockSpec((tm, d), row),
            pl.BlockSpec((tm, d), row),
            pl.BlockSpec((RET_DIM, d), const, pipeline_mode=pl.Buffered(1)),
            pl.BlockSpec((ATT_DIM, d), const, pipeline_mode=pl.Buffered(1)),
            pl.BlockSpec((d, d), const, pipeline_mode=pl.Buffered(1)),
            pl.BlockSpec((1, d), const),
            pl.BlockSpec((d, 2 * LANES), const),
        ],
        out_specs=[pl.BlockSpec((tm, d), row), pl.BlockSpec((tm, d + LANES), row),
                   pl.BlockSpec((tm, n_exp), row)],
        out_shape=[jax.ShapeDtypeStruct((t, d), F32), jax.ShapeDtypeStruct((t, d + LANES), F32),
                   jax.ShapeDtypeStruct((t, n_exp), F32)],
        compiler_params=_params(("parallel",), V7X_VMEM_LIMIT_BYTES),
        name="merge",
    )(xf, u_ret, y_att, gr, ga, w_ret, w_att, w_o, norm_ffn, w_router_split)


def _route_kernel(aff_ref, affc_ref, idx_ref, pos_ref, slo_ref, msk_ref, cum_ref, cum_t_ref, bnd_ref, bnd_smem,
                  cnt_ref, sem, *, cap, n_exp):
    b = pl.program_id(0)
    seq = aff_ref.shape[0]
    n_tiles = seq // ROUTE_TILE
    aff = aff_ref[...]
    affc = affc_ref[...]
    capf = float(cap)

    def per_expert(x, op, reduce):
        r = reduce(x.reshape(x.shape[0] // 8, 8, LANES), axis=0)
        shift = n_exp
        while shift < LANES:
            r = op(r, pltpu.roll(r, shift, axis=1))
            shift *= 2
        return reduce(r, axis=0, keepdims=True)

    def bisect(lo, hi):
        mid = 0.5 * (lo + hi)
        ok = per_expert(jnp.where(affc >= mid, 1.0, 0.0), jnp.add, jnp.sum) >= capf
        return jnp.where(ok, mid, lo), jnp.where(ok, hi, mid)

    def bracket(lo, hi):
        mn = per_expert(jnp.where(affc >= lo, affc, jnp.inf), jnp.minimum, jnp.min)
        mx = per_expert(jnp.where(affc < hi, affc, -jnp.inf), jnp.maximum, jnp.max)
        return mn, mx

    def not_isolated(lo, hi):
        mn, mx = bracket(lo, hi)
        return jnp.max(jnp.where(mn < mx, 1.0, 0.0)) > 0.0

    lo, hi = lax.fori_loop(0, BISECT_STEPS, lambda i, c: bisect(*c),
                           (jnp.zeros((1, LANES), F32), jnp.full((1, LANES), 2.0, F32)))

    def refine(c):
        lo, hi = bisect(c[0], c[1])
        return lo, hi, not_isolated(lo, hi)

    lo, hi, _ = lax.while_loop(lambda c: c[2], refine, (lo, hi, not_isolated(lo, hi)))
    thr, _ = bracket(lo, hi)
    gt = aff > thr
    eq = aff == thr
    n_gt = jnp.sum(jnp.where(gt, 1.0, 0.0).reshape(seq // 8, 8, LANES), axis=0)
    need = capf - jnp.sum(n_gt, axis=0, keepdims=True)

    tri = (lax.broadcasted_iota(I32, (ROUTE_TILE, ROUTE_TILE), 0)
           >= lax.broadcasted_iota(I32, (ROUTE_TILE, ROUTE_TILE), 1))
    tri = jnp.where(tri, 1.0, 0.0).astype(BF16)

    def cumsum_tokens(store_tile_start):
        def body(c, carry):
            r0 = pl.multiple_of(c * ROUTE_TILE, ROUTE_TILE)
            if store_tile_start:
                slo_ref[0, pl.ds(c, 1), :] = carry[:, :n_exp].astype(I32)
                bnd_ref[pl.ds(c, 1), :] = carry.astype(I32)
            cs = jnp.dot(tri, msk_ref[pl.ds(r0, ROUTE_TILE), :], preferred_element_type=F32) + carry
            cum_ref[pl.ds(r0, ROUTE_TILE), :] = cs
            return cs[ROUTE_TILE - 1:ROUTE_TILE, :]

        return lax.fori_loop(0, n_tiles, body, jnp.zeros((1, LANES), F32))

    msk_ref[...] = jnp.where(eq, 1.0, 0.0).astype(BF16)
    cumsum_tokens(False)
    take = jnp.logical_and(eq, cum_ref[...] - 1.0 < need)
    mask = jnp.logical_or(gt, take)
    msk_ref[...] = jnp.where(mask, 1.0, 0.0).astype(BF16)
    bnd_ref[...] = jnp.zeros(bnd_ref.shape, I32)
    bnd_ref[pl.ds(n_tiles, 1), :] = cumsum_tokens(True).astype(I32)
    pos_ref[...] = jnp.where(mask, cum_ref[...] - 1.0, -1.0)[:, :n_exp].astype(I32)

    to_smem = pltpu.make_async_copy(bnd_ref, bnd_smem, sem)
    to_smem.start()
    to_smem.wait()

    def transpose_tile(c, carry):
        r0 = pl.multiple_of(c * ROUTE_TILE, ROUTE_TILE)
        cum_t_ref[:, pl.ds(r0, ROUTE_TILE)] = cum_ref[pl.ds(r0, ROUTE_TILE), :].T
        return carry

    lax.fori_loop(0, n_tiles, transpose_tile, 0)
    slot_row = lax.broadcasted_iota(I32, (1, cap), 1).astype(F32)
    slot_sub = lax.broadcasted_iota(I32, (LANES, ROUTE_TILE), 0)
    eye = lax.broadcasted_iota(I32, (LANES, LANES), 0) == lax.broadcasted_iota(I32, (LANES, LANES), 1)
    for e in range(n_exp):
        ends = bnd_ref[pl.ds(1, n_tiles), e:e + 1].astype(F32)
        whole = jnp.sum(jnp.where(slot_row >= ends, float(ROUTE_TILE), 0.0), axis=0, keepdims=True)
        cnt_ref[...] = jnp.zeros(cnt_ref.shape, F32)

        def tile_body(c, carry, e=e):
            lo = bnd_smem[c, e]
            hi = bnd_smem[c + 1, e]
            r0 = pl.multiple_of(c * ROUTE_TILE, ROUTE_TILE)
            counts = cum_t_ref[e:e + 1, pl.ds(r0, ROUTE_TILE)]

            def lane_tile(k, carry2):
                slots = k * LANES + slot_sub
                below_hi = jnp.where(slots < hi, 1.0, 0.0)
                hit = jnp.where(counts <= slots.astype(F32), below_hi, 0.0)
                folded = hit[:, :LANES]
                for t0 in range(LANES, ROUTE_TILE, LANES):
                    folded = folded + hit[:, t0:t0 + LANES]
                cnt_ref[k] += folded
                return carry2

            lax.fori_loop(lo // LANES, (hi + LANES - 1) // LANES, lane_tile, 0)
            return carry

        lax.fori_loop(0, n_tiles, tile_body, 0)
        for k in range(cap // LANES):
            per_slot = jnp.sum(cnt_ref[k], axis=1, keepdims=True)
            part = jnp.sum(jnp.where(eye, per_slot, 0.0), axis=0, keepdims=True)
            part = part + whole[:, k * LANES:(k + 1) * LANES]
            idx_ref[0, e:e + 1, k * LANES:(k + 1) * LANES] = part.astype(I32) + b * seq


def _route(h2a, aff, batch, seq, cap):
    t, n_exp = aff.shape
    pack = LANES // n_exp
    aff_packed = aff.reshape(t // pack, LANES)
    aff_block = h2a.shape[1] // LANES - 1
    n_tiles = seq // ROUTE_TILE
    bnd_rows = -(-(n_tiles + 1) // 8) * 8
    return pl.pallas_call(
        functools.partial(_route_kernel, cap=cap, n_exp=n_exp),
        grid=(batch,),
        in_specs=[pl.BlockSpec((seq, LANES), lambda b: (b, aff_block)),
                  pl.BlockSpec((seq // pack, LANES), lambda b: (b, 0))],
        out_specs=[pl.BlockSpec((1, n_exp, cap), lambda b: (b, 0, 0)),
                   pl.BlockSpec((seq, n_exp), lambda b: (b, 0)),
                   pl.BlockSpec((1, n_tiles, n_exp), lambda b: (b, 0, 0))],
        out_shape=[jax.ShapeDtypeStruct((batch, n_exp, cap), I32),
                   jax.ShapeDtypeStruct((t, n_exp), I32),
                   jax.ShapeDtypeStruct((batch, n_tiles, n_exp), I32)],
        scratch_shapes=[pltpu.VMEM((seq, LANES), BF16), pltpu.VMEM((seq, LANES), F32),
                        pltpu.VMEM((LANES, seq), F32),
                        pltpu.VMEM((bnd_rows, LANES), I32), pltpu.SMEM((bnd_rows, LANES), I32),
                        pltpu.VMEM((cap // LANES, LANES, LANES), F32), pltpu.SemaphoreType.DMA(())],
        compiler_params=_params(("parallel",), V7X_VMEM_LIMIT_BYTES),
        name="route",
    )(h2a, aff_packed)


def _ffn_kernel(idx_ref, h2a_hbm, wg_hbm, wu_hbm, wd_hbm, y_ref, stage_ref, xe_ref, gate_ref, acc_ref,
                wg_buf, wu_buf, wd_buf, sem, wsem, *, n_exp, n_ff, rows_per_step):
    e = pl.program_id(0)
    mp = stage_ref.shape[0]
    m, d = xe_ref.shape
    batch = y_ref.shape[0]
    cap = y_ref.shape[2]

    def row_copy(expert, i):
        r = idx_ref[expert * mp + i]
        return pltpu.make_async_copy(h2a_hbm.at[pl.ds(r, 1)], stage_ref.at[pl.ds(i, 1)], sem)

    def wait_rows():
        pltpu.make_async_copy(h2a_hbm.at[pl.ds(0, mp)], stage_ref, sem).wait()

    def weight_copies(chunk, slot):
        ex = chunk // n_ff
        c0 = pl.multiple_of((chunk % n_ff) * FF_TILE, FF_TILE)
        return (pltpu.make_async_copy(wg_hbm.at[ex, :, pl.ds(c0, FF_TILE)], wg_buf.at[slot], wsem.at[slot]),
                pltpu.make_async_copy(wu_hbm.at[ex, :, pl.ds(c0, FF_TILE)], wu_buf.at[slot], wsem.at[slot]),
                pltpu.make_async_copy(wd_hbm.at[ex, pl.ds(c0, FF_TILE), :], wd_buf.at[slot], wsem.at[slot]))

    @pl.when(e == 0)
    def _():
        def issue(i, carry):
            row_copy(0, i).start()
            return carry

        lax.fori_loop(0, mp, issue, 0)
        for cp in weight_copies(0, 0):
            cp.start()

    wait_rows()
    lane = lax.broadcasted_iota(I32, (m, LANES), 1)
    gate_ref[...] = jnp.sum(jnp.where(lane == e, stage_ref[:m, d:], 0.0), axis=1, keepdims=True)

    nxt = jnp.minimum(e + 1, n_exp - 1)

    def gather_slice(j):
        for k in range(rows_per_step):
            row_copy(nxt, j * rows_per_step + k).start()

    def ff_tile(f, first=False, last=False):
        chunk = e * n_ff + f
        slot = chunk % 2

        for cp in weight_copies(chunk, slot):
            cp.wait()

        @pl.when(chunk + 1 < n_exp * n_ff)
        def _():
            for cp in weight_copies(chunk + 1, 1 - slot):
                cp.start()

        if not first:
            gather_slice(f - 1)

        wg = wg_buf[slot].astype(BF16)
        wu = wu_buf[slot].astype(BF16)
        wd = wd_buf[slot].astype(BF16)
        for rb in range(m // FFN_ROW_BLOCK):
            rows = pl.ds(rb * FFN_ROW_BLOCK, FFN_ROW_BLOCK)
            if first:
                xb = stage_ref[rows, :d].astype(BF16)
                xe_ref[rows, :] = xb
            else:
                xb = xe_ref[rows, :]
            a = jnp.dot(xb, wg, preferred_element_type=F32)
            u = jnp.dot(xb, wu, preferred_element_type=F32)
            hm = (a * jax.nn.sigmoid(a) * u).astype(BF16)
            contrib = jnp.dot(hm, wd, preferred_element_type=F32)
            if first:
                acc_ref[rows, :] = contrib
            elif last:
                bb, r0 = divmod(rb * FFN_ROW_BLOCK, cap)
                y_ref[bb, 0, pl.ds(r0, FFN_ROW_BLOCK), :] = (
                    (acc_ref[rows, :] + contrib) * gate_ref[rows, :]).astype(BF16)
            else:
                acc_ref[rows, :] += contrib

    ff_tile(0, first=True)

    def middle_tile(f, carry):
        ff_tile(f)
        return carry

    lax.fori_loop(1, n_ff - 1, middle_tile, 0)
    ff_tile(n_ff - 1, last=True)

    @pl.when(e == n_exp - 1)
    def _():
        wait_rows()


def _ffn(idx, h2a, w_gate, w_up, w_down):
    batch, n_exp, cap = idx.shape
    d = h2a.shape[1] - LANES
    ff = w_gate.shape[2]
    n_ff = ff // FF_TILE
    assert n_ff >= 2, "the next expert's rows are gathered during ff tiles 1.."
    m = batch * cap
    rows_per_step = -(-pl.cdiv(m, n_ff - 1) // 8) * 8
    mp = (n_ff - 1) * rows_per_step
    idx_flat = jnp.pad(idx.transpose(1, 0, 2).reshape(n_exp, m), ((0, 0), (0, mp - m))).reshape(-1)
    any_spec = pl.BlockSpec(memory_space=pl.ANY)
    grid_spec = pltpu.PrefetchScalarGridSpec(
        num_scalar_prefetch=1,
        grid=(n_exp,),
        in_specs=[any_spec, any_spec, any_spec, any_spec],
        out_specs=pl.BlockSpec((batch, 1, cap, d), lambda e, idx: (0, e, 0, 0)),
        scratch_shapes=[pltpu.VMEM((mp, d + LANES), F32), pltpu.VMEM((m, d), BF16),
                        pltpu.VMEM((m, 1), F32), pltpu.VMEM((m, d), F32),
                        pltpu.VMEM((2, d, FF_TILE), F32), pltpu.VMEM((2, d, FF_TILE), F32),
                        pltpu.VMEM((2, FF_TILE, d), F32),
                        pltpu.SemaphoreType.DMA(()), pltpu.SemaphoreType.DMA((2,))],
    )
    return pl.pallas_call(
        functools.partial(_ffn_kernel, n_exp=n_exp, n_ff=n_ff, rows_per_step=rows_per_step),
        grid_spec=grid_spec,
        out_shape=jax.ShapeDtypeStruct((batch, n_exp, cap, d), BF16),
        compiler_params=_params(("arbitrary",), V7X_VMEM_LIMIT_BYTES),
        name="ffn",
    )(idx_flat, h2a, w_gate, w_up, w_down)


def _window_fits(s_hi, start, rows):
    return s_hi - start <= rows


def _combine_kernel(slo_ref, x1_ref, pos_ref, y_hbm, nf_ref, o_ref, slab_ref, big_ref, acc_ref, sem, big_sem,
                    *, n_exp, cap):
    b = pl.program_id(0)
    j = pl.program_id(1)
    n_batch = pl.num_programs(0)
    n_tiles = pl.num_programs(1)
    step = b * n_tiles + j
    slot = step % 2

    def bounds(bb, jj, e):
        base = (bb * (n_tiles + 1) + jj) * n_exp + e
        return slo_ref[base], slo_ref[base + n_exp]

    def window_start(bb, jj, e, rows):
        s_lo, _ = bounds(bb, jj, e)
        return pl.multiple_of(jnp.minimum((s_lo // SLAB_ALIGN) * SLAB_ALIGN, cap - rows), SLAB_ALIGN)

    def all_small(bb, jj):
        ok = None
        for e in range(n_exp):
            fits = _window_fits(bounds(bb, jj, e)[1], window_start(bb, jj, e, SMALL_ROWS), SMALL_ROWS)
            ok = fits if ok is None else jnp.logical_and(ok, fits)
        return ok

    def small_copy(bb, jj, e, dst_slot):
        return pltpu.make_async_copy(
            y_hbm.at[bb, e, pl.ds(window_start(bb, jj, e, SMALL_ROWS), SMALL_ROWS)],
            slab_ref.at[dst_slot, pl.ds(e * SMALL_ROWS, SMALL_ROWS)], sem.at[dst_slot])

    def finish(moe):
        x2 = x1_ref[...] + moe
        o_ref[...] = x2 * lax.rsqrt(jnp.mean(x2 * x2, axis=-1, keepdims=True) + EPS) * nf_ref[...]

    last = step + 1 == n_batch * n_tiles

    @pl.when(step == 0)
    def _():
        big_ref[...] = jnp.zeros(big_ref.shape, big_ref.dtype)
        for e in range(n_exp):
            small_copy(b, j, e, slot).start()

    for e in range(n_exp):
        small_copy(b, j, e, slot).wait()
    wrap = j + 1 == n_tiles
    nb = jnp.where(last, b, jnp.where(wrap, b + 1, b))
    nj = jnp.where(last, j, jnp.where(wrap, 0, j + 1))
    for e in range(n_exp):
        small_copy(nb, nj, e, 1 - slot).start()

    lane = lax.broadcasted_iota(I32, (1, SMALL_ROWS), 1)
    acc = None
    for e0 in range(0, n_exp, COMBINE_GROUP):
        tiles = []
        for e in range(e0, min(e0 + COMBINE_GROUP, n_exp)):
            rel = pos_ref[:, e:e + 1] - window_start(b, j, e, SMALL_ROWS)
            tiles.append(jnp.where(rel == lane, 1.0, 0.0).astype(BF16))
        rows = pl.ds(e0 * SMALL_ROWS, len(tiles) * SMALL_ROWS)
        part = jnp.dot(jnp.concatenate(tiles, axis=1), slab_ref[slot, rows, :],
                       preferred_element_type=F32)
        acc = part if acc is None else acc + part

    small_now = all_small(b, j)

    @pl.when(small_now)
    def _():
        finish(acc)

    @pl.when(jnp.logical_not(small_now))
    def _():
        wide_lane = lax.broadcasted_iota(I32, (1, BIG_PAD), 1)
        for e in range(n_exp):
            st = window_start(b, j, e, BIG_ROWS)
            cp = pltpu.make_async_copy(y_hbm.at[b, e, pl.ds(st, BIG_ROWS)],
                                       big_ref.at[pl.ds(0, BIG_ROWS)], big_sem)
            cp.start()
            cp.wait()
            onehot = jnp.where(pos_ref[:, e:e + 1] - st == wide_lane, 1.0, 0.0).astype(BF16)
            part = jnp.dot(onehot, big_ref[...], preferred_element_type=F32)
            if e == 0:
                acc_ref[...] = part
            else:
                acc_ref[...] += part
        finish(acc_ref[...])

    @pl.when(last)
    def _():
        for e in range(n_exp):
            small_copy(b, j, e, 1 - slot).wait()


def _combine(slo, x1, pos, y_slots, norm_final, seq):
    t, d = x1.shape
    batch, n_exp, cap, _ = y_slots.shape
    n_tiles = seq // ROUTE_TILE
    slo_flat = jnp.concatenate([slo, jnp.full((batch, 1, n_exp), cap, I32)], axis=1).reshape(-1)
    row = lambda b, j, slo: (b * n_tiles + j, 0)
    grid_spec = pltpu.PrefetchScalarGridSpec(
        num_scalar_prefetch=1,
        grid=(batch, n_tiles),
        in_specs=[
            pl.BlockSpec((ROUTE_TILE, d), row),
            pl.BlockSpec((ROUTE_TILE, n_exp), row),
            pl.BlockSpec(memory_space=pl.ANY),
            pl.BlockSpec((1, d), lambda b, j, slo: (0, 0)),
        ],
        out_specs=pl.BlockSpec((ROUTE_TILE, d), row),
        scratch_shapes=[pltpu.VMEM((2, n_exp * SMALL_ROWS, d), BF16), pltpu.VMEM((BIG_PAD, d), BF16),
                        pltpu.VMEM((ROUTE_TILE, d), F32),
                        pltpu.SemaphoreType.DMA((2,)), pltpu.SemaphoreType.DMA(())],
    )
    return pl.pallas_call(
        functools.partial(_combine_kernel, n_exp=n_exp, cap=cap),
        grid_spec=grid_spec,
        out_shape=jax.ShapeDtypeStruct((t, d), F32),
        compiler_params=_params(("arbitrary", "arbitrary"), V7X_VMEM_LIMIT_BYTES),
        name="combine",
    )(slo_flat, x1, pos, y_slots, norm_final)


def _rope_tables(seq):
    rows = seq // GRID_W
    row = np.repeat(np.arange(rows, dtype=np.float64), GRID_W)
    col = np.tile(np.arange(GRID_W, dtype=np.float64), rows)
    n_freq = HALF_ROT // 2
    freqs = ROPE_THETA ** (-np.arange(n_freq, dtype=np.float64) / n_freq)
    ang = np.concatenate([row[:, None] * freqs, col[:, None] * freqs], axis=-1)
    cos, sin = np.cos(ang), np.sin(ang)
    return (jnp.asarray(np.concatenate([cos, cos], axis=-1), F32),
            jnp.asarray(np.concatenate([-sin, sin], axis=-1), F32))


def kernel(x, norm_mix, w_in, ret_decay_fwd, ret_decay_bwd, ret_gn_gain, w_ret_branch, q_norm, k_norm,
           w_att_branch, w_o, norm_ffn, w_router, w_expert_gate, w_expert_up, w_expert_down, norm_final):
    batch, seq, d = x.shape
    depth = norm_mix.shape[0]
    n_exp = w_router.shape[2]
    cap = CAPACITY_FACTOR * seq // n_exp
    assert seq % TOKEN_TILE == 0 and seq % ROUTE_TILE == 0
    assert (seq - ATT_FIRST_KV) % ATT_KV_TILE == 0 and ATT_KV_TILE % (2 * LANES) == 0
    assert ATT_FIRST_KV % (2 * LANES) == 0 and seq % ATT_Q_TILE == 0
    assert seq % CHUNK == 0 and (seq // CHUNK) % RET_UNROLL == 0 and seq % RET_NORM_ROWS == 0
    assert LANES % n_exp == 0 and d % LANES == 0 and seq % (8 * LANES // n_exp) == 0
    assert cap >= BIG_ROWS and (cap - BIG_ROWS) % SLAB_ALIGN == 0 and (cap - SMALL_ROWS) % SLAB_ALIGN == 0
    assert w_expert_gate.shape[3] % FF_TILE == 0 and cap % FFN_ROW_BLOCK == 0
    assert depth == 1, "the final RMSNorm is fused into the combine kernel of the single layer"
    l = 0
    cos2, sin2 = _rope_tables(seq)
    xf = x.reshape(batch * seq, d)
    lg = jnp.stack([jnp.log1p(-jnp.exp(ret_decay_fwd[l].astype(F32))),
                    jnp.log1p(-jnp.exp(ret_decay_bwd[l].astype(F32)))])
    rq, rk, rv, sg, aq, ak, av, gr, ga = _inproj(
        xf, norm_mix[l][None], w_in[l], cos2, sin2, q_norm[l][None], k_norm[l][None], seq)
    u_ret = _retention(lg, rq, rk, rv, sg, ret_gn_gain[l][None], batch, seq)
    y_att = _attention(aq, ak, av, batch, seq)
    w_r = jnp.pad(w_router[l].astype(F32), ((0, 0), (0, LANES - n_exp)))
    w_r_hi = w_r.astype(BF16)
    w_r_lo = (w_r - w_r_hi.astype(F32)).astype(BF16)
    x1, h2a, aff = _merge(xf, u_ret, y_att, gr, ga, w_ret_branch[l], w_att_branch[l], w_o[l],
                          norm_ffn[l][None], jnp.concatenate([w_r_hi, w_r_lo], axis=1), n_exp)
    idx, pos, slo = _route(h2a, aff, batch, seq, cap)
    y_slots = _ffn(idx, h2a, w_expert_gate[l], w_expert_up[l], w_expert_down[l])
    out = _combine(slo, x1, pos, y_slots, norm_final[None], seq)
    return out.reshape(batch, seq, d)
```

```python
import functools

import jax
import jax.numpy as jnp
import numpy as np
from jax import lax
from jax.experimental import pallas as pl
from jax.experimental.pallas import tpu as pltpu

F32 = jnp.float32
BF16 = jnp.bfloat16
I32 = jnp.int32

GRID_W = 64
HEAD_DIM = 128
HALF_ROT = HEAD_DIM // 2
RET_HEADS = 4
RET_DIM = RET_HEADS * HEAD_DIM
ATT_HEADS = 4
ATT_KV_HEADS = 2
ATT_GROUP = ATT_HEADS // ATT_KV_HEADS
ATT_DIM = ATT_HEADS * HEAD_DIM
ATT_KV_DIM = ATT_KV_HEADS * HEAD_DIM
CHUNK = 256
ROPE_THETA = 10000.0
CAPACITY_FACTOR = 2
EPS = 1e-6
LOG2_E = 1.4426950408889634

V7X_VMEM_LIMIT_BYTES = 56 * 1024 * 1024
TOKEN_TILE = 512
ATT_Q_TILE = 1024
ATT_FIRST_KV = 512
ATT_KV_TILE = 3840
RET_NORM_ROWS = 2048
RET_UNROLL = 4
LANES = 128
ROUTE_TILE = 512
SLAB_ALIGN = 16
SMALL_ROWS = LANES
BIG_ROWS = ROUTE_TILE + SLAB_ALIGN
BIG_PAD = -(-BIG_ROWS // LANES) * LANES
COMBINE_GROUP = 4
FF_TILE = 256
FFN_ROW_BLOCK = 512
BISECT_STEPS = 24


def _params(sem, vmem=None):
    return pltpu.CompilerParams(dimension_semantics=sem, vmem_limit_bytes=vmem)


def _inproj_kernel(x_ref, g_ref, w_ref, cos_ref, sin_ref, qn_ref, kn_ref,
                   rq_ref, rk_ref, rv_ref, sg_ref, aq_ref, ak_ref, av_ref, gr_ref, ga_ref):
    x = x_ref[...]
    h = (x * lax.rsqrt(jnp.mean(x * x, axis=-1, keepdims=True) + EPS) * g_ref[...]).astype(BF16)
    cos = cos_ref[...]
    sin = sin_ref[...]
    d_model = x.shape[1]
    scale = HEAD_DIM ** -0.5

    def proj(lo, width):
        return jnp.dot(h, w_ref[:, lo:lo + width].astype(BF16), preferred_element_type=F32)

    def rope(t):
        return t * cos + pltpu.roll(t, HALF_ROT, axis=1) * sin

    def head_norm(t, gain):
        return t * lax.rsqrt(jnp.mean(t * t, axis=-1, keepdims=True) + EPS) * gain

    def head(p, i):
        return p[:, i * HEAD_DIM:(i + 1) * HEAD_DIM]

    def put(ref, i, val):
        ref[:, i * HEAD_DIM:(i + 1) * HEAD_DIM] = val.astype(ref.dtype)

    off = 0
    p = proj(off, RET_DIM)
    for i in range(RET_HEADS):
        put(rq_ref, i, rope(head(p, i)))
    off += RET_DIM
    p = proj(off, RET_DIM)
    for i in range(RET_HEADS):
        put(rk_ref, i, rope(head(p, i)) * scale)
    off += RET_DIM
    rv_ref[...] = proj(off, RET_DIM).astype(BF16)
    off += RET_DIM
    p = proj(off, RET_DIM)
    sg_ref[...] = (p * jax.nn.sigmoid(p)).astype(BF16)
    off += RET_DIM
    p = proj(off, ATT_DIM)
    for i in range(ATT_HEADS):
        put(aq_ref, i, rope(head_norm(head(p, i), qn_ref[...])) * (scale * LOG2_E))
    off += ATT_DIM
    p = proj(off, ATT_KV_DIM)
    ones_col = jnp.where(lax.broadcasted_iota(I32, (x.shape[0], HEAD_DIM), 1) == 0, 1.0, 0.0)
    for i in range(ATT_KV_HEADS):
        put(ak_ref, 2 * i, rope(head_norm(head(p, i), kn_ref[...])))
        put(ak_ref, 2 * i + 1, ones_col)
    off += ATT_KV_DIM
    av_ref[...] = proj(off, ATT_KV_DIM).T.astype(BF16)
    off += ATT_KV_DIM
    gr_ref[...] = jax.nn.sigmoid(proj(off, d_model)).astype(BF16)
    off += d_model
    ga_ref[...] = jax.nn.sigmoid(proj(off, d_model)).astype(BF16)


def _inproj(xf, norm_g, w_in, cos2, sin2, q_norm, k_norm, seq):
    t, d = xf.shape
    tm = TOKEN_TILE
    in_width = w_in.shape[1]
    steps_per_seq = seq // tm
    row = lambda i: (i, 0)
    const = lambda i: (0, 0)
    pos = lambda i: (i % steps_per_seq, 0)
    widths = (RET_DIM, RET_DIM, RET_DIM, RET_DIM, ATT_DIM, 2 * ATT_KV_DIM, ATT_KV_DIM, d, d)
    av_index = 6
    return pl.pallas_call(
        _inproj_kernel,
        grid=(t // tm,),
        in_specs=[
            pl.BlockSpec((tm, d), row),
            pl.BlockSpec((1, d), const),
            pl.BlockSpec((d, in_width), const, pipeline_mode=pl.Buffered(1)),
            pl.BlockSpec((tm, HEAD_DIM), pos),
            pl.BlockSpec((tm, HEAD_DIM), pos),
            pl.BlockSpec((1, HEAD_DIM), const),
            pl.BlockSpec((1, HEAD_DIM), const),
        ],
        out_specs=[pl.BlockSpec((ATT_KV_DIM, tm), lambda i: (0, i)) if k == av_index
                   else pl.BlockSpec((tm, w), row) for k, w in enumerate(widths)],
        out_shape=[jax.ShapeDtypeStruct((ATT_KV_DIM, t) if k == av_index else (t, w), BF16)
                   for k, w in enumerate(widths)],
        compiler_params=_params(("parallel",), V7X_VMEM_LIMIT_BYTES),
        name="inproj",
    )(xf, norm_g, w_in, cos2, sin2, q_norm, k_norm)


def _retention_kernel(lg_ref, q_ref, k_ref, v_ref, sg_ref, gain_ref, o_ref, of_ref, ob_ref):
    hd = pl.program_id(1)
    lgf = lg_ref[0, hd]
    lgb = lg_ref[1, hd]
    n_chunks = q_ref.shape[0] // CHUNK
    ii = lax.broadcasted_iota(I32, (CHUNK, CHUNK), 0)
    jj = lax.broadcasted_iota(I32, (CHUNK, CHUNK), 1)
    dist = (ii - jj).astype(F32)
    dec_f = jnp.where(dist >= 0, jnp.exp(lgf * jnp.maximum(dist, 0.0)), 0.0)
    dec_b = jnp.where(dist < 0, jnp.exp(lgb * jnp.maximum(-dist, 0.0)), 0.0)
    t = lax.broadcasted_iota(I32, (CHUNK, 1), 0).astype(F32)
    kd_f = jnp.exp(lgf * (CHUNK - 1.0 - t))
    qd_f = jnp.exp(lgf * (t + 1.0))
    kd_b = jnp.exp(lgb * t)
    qd_b = jnp.exp(lgb * (CHUNK - t))
    zero_row = jnp.zeros((1, HEAD_DIM), F32)
    cd_f = jnp.exp(zero_row + lgf * CHUNK)
    cd_b = jnp.exp(zero_row + lgb * CHUNK)

    def one_chunk(n, dec, kd, qd, cd, state):
        r0 = pl.multiple_of(n * CHUNK, CHUNK)
        q = q_ref[pl.ds(r0, CHUNK), :]
        k = k_ref[pl.ds(r0, CHUNK), :]
        v = v_ref[pl.ds(r0, CHUNK), :]
        s = lax.dot_general(q, k, (((1,), (1,)), ((), ())), preferred_element_type=F32) * dec
        intra = jnp.dot(s.astype(BF16), v, preferred_element_type=F32)
        q_dec = (q.astype(F32) * qd).astype(BF16)
        inter = jnp.dot(q_dec, state.astype(BF16), preferred_element_type=F32)
        k_dec_t = (k.astype(F32) * kd).T.astype(BF16)
        new_state = state * cd + jnp.dot(k_dec_t, v, preferred_element_type=F32)
        return r0, intra + inter, new_state

    def scan_body(n, carry):
        sf, sb = carry
        r0, of, sf = one_chunk(n, dec_f, kd_f, qd_f, cd_f, sf)
        of_ref[pl.ds(r0, CHUNK), :] = of
        r0, ob, sb = one_chunk(n_chunks - 1 - n, dec_b, kd_b, qd_b, cd_b, sb)
        ob_ref[pl.ds(r0, CHUNK), :] = ob
        return sf, sb

    zeros = jnp.zeros((HEAD_DIM, HEAD_DIM), F32)
    lax.fori_loop(0, n_chunks, scan_body, (zeros, zeros), unroll=RET_UNROLL)

    rows = RET_NORM_ROWS

    def norm_body(n, carry):
        r0 = pl.multiple_of(n * rows, rows)
        o = of_ref[pl.ds(r0, rows), :] + ob_ref[pl.ds(r0, rows), :]
        mu = jnp.mean(o, axis=-1, keepdims=True)
        var = jnp.mean(jnp.square(o - mu), axis=-1, keepdims=True)
        yn = (o - mu) * lax.rsqrt(var + EPS) * gain_ref[...]
        o_ref[pl.ds(r0, rows), :] = (sg_ref[pl.ds(r0, rows), :].astype(F32) * yn).astype(BF16)
        return carry

    lax.fori_loop(0, q_ref.shape[0] // rows, norm_body, 0)


def _retention(lg, rq, rk, rv, sg, gn_gain, batch, seq):
    t = rq.shape[0]
    blk = pl.BlockSpec((seq, HEAD_DIM), lambda b, h: (b, h))
    return pl.pallas_call(
        _retention_kernel,
        grid=(batch, RET_HEADS),
        in_specs=[
            pl.BlockSpec(memory_space=pltpu.SMEM),
            blk, blk, blk, blk,
            pl.BlockSpec((1, HEAD_DIM), lambda b, h: (0, h)),
        ],
        out_specs=blk,
        out_shape=jax.ShapeDtypeStruct((t, RET_DIM), BF16),
        scratch_shapes=[pltpu.VMEM((seq, HEAD_DIM), F32), pltpu.VMEM((seq, HEAD_DIM), F32)],
        compiler_params=_params(("parallel", "parallel"), V7X_VMEM_LIMIT_BYTES),
        name="retention",
    )(lg, rq, rk, rv, sg, gn_gain)


def _attention_kernel(q_ref, k_ref, vt_ref, o_ref, qa_ref):
    tq = q_ref.shape[0]
    n_rest = (k_ref.shape[0] - ATT_FIRST_KV) // ATT_KV_TILE
    lane = lax.broadcasted_iota(I32, (tq, 2 * HEAD_DIM), 1)

    def set_query(hh, shift):
        q = q_ref[:, hh * HEAD_DIM:(hh + 1) * HEAD_DIM]
        qa = jnp.concatenate([q, jnp.zeros((tq, HEAD_DIM), BF16)], axis=1)
        if shift is not None:
            qa = jnp.where(lane == HEAD_DIM, (-shift.T).astype(BF16), qa)
        qa_ref[hh] = qa

    def shifted_scores(hh, r0, rows):
        return lax.dot_general(k_ref[pl.ds(r0, rows), :], qa_ref[hh], (((1,), (1,)), ((), ())),
                               preferred_element_type=F32)

    def weigh(p, vt):
        return jnp.sum(p, axis=0, keepdims=True), jnp.dot(vt, p.astype(BF16), preferred_element_type=F32)

    vt = vt_ref[:, pl.ds(0, ATT_FIRST_KV)]
    state = []
    for hh in range(ATT_GROUP):
        set_query(hh, None)
        s = shifted_scores(hh, 0, ATT_FIRST_KV)
        shift = jnp.max(s, axis=0, keepdims=True).astype(BF16).astype(F32)
        l, acc = weigh(jnp.exp2(s - shift), vt)
        set_query(hh, shift)
        state.append((shift, l, acc))

    def body(c, carry):
        r0 = pl.multiple_of(ATT_FIRST_KV + c * ATT_KV_TILE, 2 * LANES)
        vt = vt_ref[:, pl.ds(r0, ATT_KV_TILE)]
        fast = []
        overflow = None
        for hh, (shift, l, acc) in enumerate(carry):
            s = shifted_scores(hh, r0, ATT_KV_TILE)
            dl, dacc = weigh(jnp.exp2(s), vt)
            l_new = l + dl
            acc_new = acc + dacc
            fast.append((shift, l_new, acc_new))
            bad = jnp.maximum(jnp.max(jnp.where(jnp.isfinite(l_new), 0.0, 1.0)),
                              jnp.max(jnp.where(jnp.isfinite(acc_new), 0.0, 1.0)))
            overflow = bad if overflow is None else jnp.maximum(overflow, bad)

        def redo():
            out = []
            for hh, (shift, l, acc) in enumerate(carry):
                s = shifted_scores(hh, r0, ATT_KV_TILE)
                raised = (shift + jnp.maximum(jnp.max(s, axis=0, keepdims=True), 0.0)).astype(BF16).astype(F32)
                step = raised - shift
                alpha = jnp.exp2(-step)
                dl, dacc = weigh(jnp.exp2(s - step), vt)
                out.append((raised, alpha * l + dl, alpha * acc + dacc))
                set_query(hh, raised)
            return tuple(out)

        return lax.cond(overflow == 0.0, lambda: tuple(fast), redo)

    final = lax.fori_loop(0, n_rest, body, tuple(state))
    for hh, (_, l, acc) in enumerate(final):
        o_ref[:, hh * HEAD_DIM:(hh + 1) * HEAD_DIM] = (acc / l).T.astype(BF16)


def _attention(aq, ak, av_t, batch, seq):
    t = aq.shape[0]
    tq = ATT_Q_TILE
    nq = seq // tq
    gw = ATT_GROUP * HEAD_DIM
    q_spec = pl.BlockSpec((tq, gw), lambda b, g, i: (b * nq + i, g))
    k_spec = pl.BlockSpec((seq, 2 * HEAD_DIM), lambda b, g, i: (b, g))
    vt_spec = pl.BlockSpec((HEAD_DIM, seq), lambda b, g, i: (g, b))
    return pl.pallas_call(
        _attention_kernel,
        grid=(batch, ATT_KV_HEADS, nq),
        in_specs=[q_spec, k_spec, vt_spec],
        out_specs=q_spec,
        out_shape=jax.ShapeDtypeStruct((t, ATT_DIM), BF16),
        scratch_shapes=[pltpu.VMEM((ATT_GROUP, tq, 2 * HEAD_DIM), BF16)],
        compiler_params=_params(("parallel", "parallel", "parallel"), V7X_VMEM_LIMIT_BYTES),
        name="attention",
    )(aq, ak, av_t)


def _merge_kernel(x_ref, ur_ref, ya_ref, gr_ref, ga_ref, wr_ref, wa_ref, wo_ref, nf_ref, wrt_ref,
                  x1_ref, h2a_ref, aff_ref, *, n_exp):
    d = x_ref.shape[1]
    y_ret = jnp.dot(ur_ref[...], wr_ref[...].astype(BF16), preferred_element_type=F32)
    y_att = jnp.dot(ya_ref[...], wa_ref[...].astype(BF16), preferred_element_type=F32)
    mixed = gr_ref[...].astype(F32) * y_ret + ga_ref[...].astype(F32) * y_att
    x1 = x_ref[...] + jnp.dot(mixed.astype(BF16), wo_ref[...].astype(BF16), preferred_element_type=F32)
    x1_ref[...] = x1
    h2 = x1 * lax.rsqrt(jnp.mean(x1 * x1, axis=-1, keepdims=True) + EPS) * nf_ref[...]
    h2a_ref[:, :d] = h2
    h_hi = h2.astype(BF16)
    h_lo = (h2 - h_hi.astype(F32)).astype(BF16)
    r_hi = jnp.dot(h_hi, wrt_ref[...], preferred_element_type=F32)
    r_lo = jnp.dot(h_lo, wrt_ref[:, :LANES], preferred_element_type=F32)
    logits = r_hi[:, :LANES] + r_hi[:, LANES:] + r_lo
    lane = lax.broadcasted_iota(I32, logits.shape, 1)
    logits = jnp.where(lane < n_exp, logits, -jnp.inf)
    e = jnp.exp(logits - jnp.max(logits, axis=-1, keepdims=True))
    aff = e / jnp.sum(e, axis=-1, keepdims=True)
    h2a_ref[:, d:] = aff
    aff_ref[...] = aff[:, :n_exp]


def _merge(xf, u_ret, y_att, gr, ga, w_ret, w_att, w_o, norm_ffn, w_router_split, n_exp):
    t, d = xf.shape
    tm = TOKEN_TILE
    row = lambda i: (i, 0)
    const = lambda i: (0, 0)
    return pl.pallas_call(
        functools.partial(_merge_kernel, n_exp=n_exp),
        grid=(t // tm,),
        in_specs=[
            pl.BlockSpec((tm, d), row),
            pl.BlockSpec((tm, RET_DIM), row),
            pl.BlockSpec((tm, ATT_DIM), row),
            pl.BlockSpec((tm, d), row),
            pl.BlockSpec((tm, d), row),
            pl.BlockSpec((RET_DIM, d), const, pipeline_mode=pl.Buffered(1)),
            pl.BlockSpec((ATT_DIM, d), const, pipeline_mode=pl.Buffered(1)),
            pl.BlockSpec((d, d), const, pipeline_mode=pl.Buffered(1)),
            pl.BlockSpec((1, d), const),
            pl.BlockSpec((d, 2 * LANES), const),
        ],
        out_specs=[pl.BlockSpec((tm, d), row), pl.BlockSpec((tm, d + LANES), row),
                   pl.BlockSpec((tm, n_exp), row)],
        out_shape=[jax.ShapeDtypeStruct((t, d), F32), jax.ShapeDtypeStruct((t, d + LANES), F32),
                   jax.ShapeDtypeStruct((t, n_exp), F32)],
        compiler_params=_params(("parallel",), V7X_VMEM_LIMIT_BYTES),
        name="merge",
    )(xf, u_ret, y_att, gr, ga, w_ret, w_att, w_o, norm_ffn, w_router_split)


def _route_kernel(aff_ref, affc_ref, idx_ref, pos_ref, slo_ref, msk_ref, cum_ref, cum_t_ref, bnd_ref, bnd_smem,
                  cnt_ref, sem, *, cap, n_exp):
    b = pl.program_id(0)
    seq = aff_ref.shape[0]
    n_tiles = seq // ROUTE_TILE
    aff = aff_ref[...]
    affc = affc_ref[...]
    capf = float(cap)

    def per_expert(x, op, reduce):
        r = reduce(x.reshape(x.shape[0] // 8, 8, LANES), axis=0)
        shift = n_exp
        while shift < LANES:
            r = op(r, pltpu.roll(r, shift, axis=1))
            shift *= 2
        return reduce(r, axis=0, keepdims=True)

    def bisect(lo, hi):
        mid = 0.5 * (lo + hi)
        ok = per_expert(jnp.where(affc >= mid, 1.0, 0.0), jnp.add, jnp.sum) >= capf
        return jnp.where(ok, mid, lo), jnp.where(ok, hi, mid)

    def bracket(lo, hi):
        mn = per_expert(jnp.where(affc >= lo, affc, jnp.inf), jnp.minimum, jnp.min)
        mx = per_expert(jnp.where(affc < hi, affc, -jnp.inf), jnp.maximum, jnp.max)
        return mn, mx

    def not_isolated(lo, hi):
        mn, mx = bracket(lo, hi)
        return jnp.max(jnp.where(mn < mx, 1.0, 0.0)) > 0.0

    lo, hi = lax.fori_loop(0, BISECT_STEPS, lambda i, c: bisect(*c),
                           (jnp.zeros((1, LANES), F32), jnp.full((1, LANES), 2.0, F32)))

    def refine(c):
        lo, hi = bisect(c[0], c[1])
        return lo, hi, not_isolated(lo, hi)

    lo, hi, _ = lax.while_loop(lambda c: c[2], refine, (lo, hi, not_isolated(lo, hi)))
    thr, _ = bracket(lo, hi)
    gt = aff > thr
    eq = aff == thr
    n_gt = jnp.sum(jnp.where(gt, 1.0, 0.0).reshape(seq // 8, 8, LANES), axis=0)
    need = capf - jnp.sum(n_gt, axis=0, keepdims=True)

    tri = (lax.broadcasted_iota(I32, (ROUTE_TILE, ROUTE_TILE), 0)
           >= lax.broadcasted_iota(I32, (ROUTE_TILE, ROUTE_TILE), 1))
    tri = jnp.where(tri, 1.0, 0.0).astype(BF16)

    def cumsum_tokens(store_tile_start):
        def body(c, carry):
            r0 = pl.multiple_of(c * ROUTE_TILE, ROUTE_TILE)
            if store_tile_start:
                slo_ref[0, pl.ds(c, 1), :] = carry[:, :n_exp].astype(I32)
                bnd_ref[pl.ds(c, 1), :] = carry.astype(I32)
            cs = jnp.dot(tri, msk_ref[pl.ds(r0, ROUTE_TILE), :], preferred_element_type=F32) + carry
            cum_ref[pl.ds(r0, ROUTE_TILE), :] = cs
            return cs[ROUTE_TILE - 1:ROUTE_TILE, :]

        return lax.fori_loop(0, n_tiles, body, jnp.zeros((1, LANES), F32))

    msk_ref[...] = jnp.where(eq, 1.0, 0.0).astype(BF16)
    cumsum_tokens(False)
    take = jnp.logical_and(eq, cum_ref[...] - 1.0 < need)
    mask = jnp.logical_or(gt, take)
    msk_ref[...] = jnp.where(mask, 1.0, 0.0).astype(BF16)
    bnd_ref[...] = jnp.zeros(bnd_ref.shape, I32)
    bnd_ref[pl.ds(n_tiles, 1), :] = cumsum_tokens(True).astype(I32)
    pos_ref[...] = jnp.where(mask, cum_ref[...] - 1.0, -1.0)[:, :n_exp].astype(I32)

    to_smem = pltpu.make_async_copy(bnd_ref, bnd_smem, sem)
    to_smem.start()
    to_smem.wait()

    def transpose_tile(c, carry):
        r0 = pl.multiple_of(c * ROUTE_TILE, ROUTE_TILE)
        cum_t_ref[:, pl.ds(r0, ROUTE_TILE)] = cum_ref[pl.ds(r0, ROUTE_TILE), :].T
        return carry

    lax.fori_loop(0, n_tiles, transpose_tile, 0)
    slot_row = lax.broadcasted_iota(I32, (1, cap), 1).astype(F32)
    slot_sub = lax.broadcasted_iota(I32, (LANES, ROUTE_TILE), 0)
    eye = lax.broadcasted_iota(I32, (LANES, LANES), 0) == lax.broadcasted_iota(I32, (LANES, LANES), 1)
    for e in range(n_exp):
        ends = bnd_ref[pl.ds(1, n_tiles), e:e + 1].astype(F32)
        whole = jnp.sum(jnp.where(slot_row >= ends, float(ROUTE_TILE), 0.0), axis=0, keepdims=True)
        cnt_ref[...] = jnp.zeros(cnt_ref.shape, F32)

        def tile_body(c, carry, e=e):
            lo = bnd_smem[c, e]
            hi = bnd_smem[c + 1, e]
            r0 = pl.multiple_of(c * ROUTE_TILE, ROUTE_TILE)
            counts = cum_t_ref[e:e + 1, pl.ds(r0, ROUTE_TILE)]

            def lane_tile(k, carry2):
                slots = k * LANES + slot_sub
                below_hi = jnp.where(slots < hi, 1.0, 0.0)
                hit = jnp.where(counts <= slots.astype(F32), below_hi, 0.0)
                folded = hit[:, :LANES]
                for t0 in range(LANES, ROUTE_TILE, LANES):
                    folded = folded + hit[:, t0:t0 + LANES]
                cnt_ref[k] += folded
                return carry2

            lax.fori_loop(lo // LANES, (hi + LANES - 1) // LANES, lane_tile, 0)
            return carry

        lax.fori_loop(0, n_tiles, tile_body, 0)
        for k in range(cap // LANES):
            per_slot = jnp.sum(cnt_ref[k], axis=1, keepdims=True)
            part = jnp.sum(jnp.where(eye, per_slot, 0.0), axis=0, keepdims=True)
            part = part + whole[:, k * LANES:(k + 1) * LANES]
            idx_ref[0, e:e + 1, k * LANES:(k + 1) * LANES] = part.astype(I32) + b * seq


def _route(h2a, aff, batch, seq, cap):
    t, n_exp = aff.shape
    pack = LANES // n_exp
    aff_packed = aff.reshape(t // pack, LANES)
    aff_block = h2a.shape[1] // LANES - 1
    n_tiles = seq // ROUTE_TILE
    bnd_rows = -(-(n_tiles + 1) // 8) * 8
    return pl.pallas_call(
        functools.partial(_route_kernel, cap=cap, n_exp=n_exp),
        grid=(batch,),
        in_specs=[pl.BlockSpec((seq, LANES), lambda b: (b, aff_block)),
                  pl.BlockSpec((seq // pack, LANES), lambda b: (b, 0))],
        out_specs=[pl.BlockSpec((1, n_exp, cap), lambda b: (b, 0, 0)),
                   pl.BlockSpec((seq, n_exp), lambda b: (b, 0)),
                   pl.BlockSpec((1, n_tiles, n_exp), lambda b: (b, 0, 0))],
        out_shape=[jax.ShapeDtypeStruct((batch, n_exp, cap), I32),
                   jax.ShapeDtypeStruct((t, n_exp), I32),
                   jax.ShapeDtypeStruct((batch, n_tiles, n_exp), I32)],
        scratch_shapes=[pltpu.VMEM((seq, LANES), BF16), pltpu.VMEM((seq, LANES), F32),
                        pltpu.VMEM((LANES, seq), F32),
                        pltpu.VMEM((bnd_rows, LANES), I32), pltpu.SMEM((bnd_rows, LANES), I32),
                        pltpu.VMEM((cap // LANES, LANES, LANES), F32), pltpu.SemaphoreType.DMA(())],
        compiler_params=_params(("parallel",), V7X_VMEM_LIMIT_BYTES),
        name="route",
    )(h2a, aff_packed)


def _ffn_kernel(idx_ref, h2a_hbm, wg_hbm, wu_hbm, wd_hbm, y_ref, stage_ref, xe_ref, gate_ref, acc_ref,
                wg_buf, wu_buf, wd_buf, sem, wsem, *, n_exp, n_ff, rows_per_step):
    e = pl.program_id(0)
    mp = stage_ref.shape[0]
    m, d = xe_ref.shape
    batch = y_ref.shape[0]
    cap = y_ref.shape[2]

    def row_copy(expert, i):
        r = idx_ref[expert * mp + i]
        return pltpu.make_async_copy(h2a_hbm.at[pl.ds(r, 1)], stage_ref.at[pl.ds(i, 1)], sem)

    def wait_rows():
        pltpu.make_async_copy(h2a_hbm.at[pl.ds(0, mp)], stage_ref, sem).wait()

    def weight_copies(chunk, slot):
        ex = chunk // n_ff
        c0 = pl.multiple_of((chunk % n_ff) * FF_TILE, FF_TILE)
        return (pltpu.make_async_copy(wg_hbm.at[ex, :, pl.ds(c0, FF_TILE)], wg_buf.at[slot], wsem.at[slot]),
                pltpu.make_async_copy(wu_hbm.at[ex, :, pl.ds(c0, FF_TILE)], wu_buf.at[slot], wsem.at[slot]),
                pltpu.make_async_copy(wd_hbm.at[ex, pl.ds(c0, FF_TILE), :], wd_buf.at[slot], wsem.at[slot]))

    @pl.when(e == 0)
    def _():
        def issue(i, carry):
            row_copy(0, i).start()
            return carry

        lax.fori_loop(0, mp, issue, 0)
        for cp in weight_copies(0, 0):
            cp.start()

    wait_rows()

    nxt = jnp.minimum(e + 1, n_exp - 1)

    def gather_slice(j):
        for k in range(rows_per_step):
            row_copy(nxt, j * rows_per_step + k).start()

    def ff_tile(f, first=False, last=False):
        chunk = e * n_ff + f
        slot = chunk % 2

        for cp in weight_copies(chunk, slot):
            cp.wait()

        @pl.when(chunk + 1 < n_exp * n_ff)
        def _():
            for cp in weight_copies(chunk + 1, 1 - slot):
                cp.start()

        if not first:
            gather_slice(f - 1)

        if first:
            lane = lax.broadcasted_iota(I32, (m, LANES), 1)
            gate_ref[...] = jnp.sum(jnp.where(lane == e, stage_ref[:m, d:], 0.0), axis=1, keepdims=True)
        wg = wg_buf[slot].astype(BF16)
        wu = wu_buf[slot].astype(BF16)
        wd = wd_buf[slot].astype(BF16)
        for rb in range(m // FFN_ROW_BLOCK):
            rows = pl.ds(rb * FFN_ROW_BLOCK, FFN_ROW_BLOCK)
            if first:
                xb = stage_ref[rows, :d].astype(BF16)
                xe_ref[rows, :] = xb
            else:
                xb = xe_ref[rows, :]
            a = jnp.dot(xb, wg, preferred_element_type=F32)
            u = jnp.dot(xb, wu, preferred_element_type=F32)
            hm = (a * jax.nn.sigmoid(a) * u).astype(BF16)
            contrib = jnp.dot(hm, wd, preferred_element_type=F32)
            if first:
                acc_ref[rows, :] = contrib
            elif last:
                bb, r0 = divmod(rb * FFN_ROW_BLOCK, cap)
                y_ref[bb, 0, pl.ds(r0, FFN_ROW_BLOCK), :] = (
                    (acc_ref[rows, :] + contrib) * gate_ref[rows, :]).astype(BF16)
            else:
                acc_ref[rows, :] += contrib

    ff_tile(0, first=True)

    def middle_tile(f, carry):
        ff_tile(f)
        return carry

    lax.fori_loop(1, n_ff - 1, middle_tile, 0)
    ff_tile(n_ff - 1, last=True)

    @pl.when(e == n_exp - 1)
    def _():
        wait_rows()


def _ffn(idx, h2a, w_gate, w_up, w_down):
    batch, n_exp, cap = idx.shape
    d = h2a.shape[1] - LANES
    ff = w_gate.shape[2]
    n_ff = ff // FF_TILE
    assert n_ff >= 2, "the next expert's rows are gathered during ff tiles 1.."
    m = batch * cap
    rows_per_step = -(-pl.cdiv(m, n_ff - 1) // 8) * 8
    mp = (n_ff - 1) * rows_per_step
    idx_flat = jnp.pad(idx.transpose(1, 0, 2).reshape(n_exp, m), ((0, 0), (0, mp - m))).reshape(-1)
    any_spec = pl.BlockSpec(memory_space=pl.ANY)
    grid_spec = pltpu.PrefetchScalarGridSpec(
        num_scalar_prefetch=1,
        grid=(n_exp,),
        in_specs=[any_spec, any_spec, any_spec, any_spec],
        out_specs=pl.BlockSpec((batch, 1, cap, d), lambda e, idx: (0, e, 0, 0)),
        scratch_shapes=[pltpu.VMEM((mp, d + LANES), F32), pltpu.VMEM((m, d), BF16),
                        pltpu.VMEM((m, 1), F32), pltpu.VMEM((m, d), F32),
                        pltpu.VMEM((2, d, FF_TILE), F32), pltpu.VMEM((2, d, FF_TILE), F32),
                        pltpu.VMEM((2, FF_TILE, d), F32),
                        pltpu.SemaphoreType.DMA(()), pltpu.SemaphoreType.DMA((2,))],
    )
    return pl.pallas_call(
        functools.partial(_ffn_kernel, n_exp=n_exp, n_ff=n_ff, rows_per_step=rows_per_step),
        grid_spec=grid_spec,
        out_shape=jax.ShapeDtypeStruct((batch, n_exp, cap, d), BF16),
        compiler_params=_params(("arbitrary",), V7X_VMEM_LIMIT_BYTES),
        name="ffn",
    )(idx_flat, h2a, w_gate, w_up, w_down)


def _window_fits(s_hi, start, rows):
    return s_hi - start <= rows


def _combine_kernel(slo_ref, x1_ref, pos_ref, y_hbm, nf_ref, o_ref, slab_ref, big_ref, acc_ref, sem, big_sem,
                    *, n_exp, cap):
    b = pl.program_id(0)
    j = pl.program_id(1)
    n_batch = pl.num_programs(0)
    n_tiles = pl.num_programs(1)
    step = b * n_tiles + j
    slot = step % 2

    def bounds(bb, jj, e):
        base = (bb * (n_tiles + 1) + jj) * n_exp + e
        return slo_ref[base], slo_ref[base + n_exp]

    def window_start(bb, jj, e, rows):
        s_lo, _ = bounds(bb, jj, e)
        return pl.multiple_of(jnp.minimum((s_lo // SLAB_ALIGN) * SLAB_ALIGN, cap - rows), SLAB_ALIGN)

    def all_small(bb, jj):
        ok = None
        for e in range(n_exp):
            fits = _window_fits(bounds(bb, jj, e)[1], window_start(bb, jj, e, SMALL_ROWS), SMALL_ROWS)
            ok = fits if ok is None else jnp.logical_and(ok, fits)
        return ok

    def small_copy(bb, jj, e, dst_slot):
        return pltpu.make_async_copy(
            y_hbm.at[bb, e, pl.ds(window_start(bb, jj, e, SMALL_ROWS), SMALL_ROWS)],
            slab_ref.at[dst_slot, pl.ds(e * SMALL_ROWS, SMALL_ROWS)], sem.at[dst_slot])

    def finish(moe):
        x2 = x1_ref[...] + moe
        o_ref[...] = x2 * lax.rsqrt(jnp.mean(x2 * x2, axis=-1, keepdims=True) + EPS) * nf_ref[...]

    last = step + 1 == n_batch * n_tiles

    @pl.when(step == 0)
    def _():
        big_ref[...] = jnp.zeros(big_ref.shape, big_ref.dtype)
        for e in range(n_exp):
            small_copy(b, j, e, slot).start()

    for e in range(n_exp):
        small_copy(b, j, e, slot).wait()
    wrap = j + 1 == n_tiles
    nb = jnp.where(last, b, jnp.where(wrap, b + 1, b))
    nj = jnp.where(last, j, jnp.where(wrap, 0, j + 1))
    for e in range(n_exp):
        small_copy(nb, nj, e, 1 - slot).start()

    lane = lax.broadcasted_iota(I32, (1, SMALL_ROWS), 1)
    acc = None
    for e0 in range(0, n_exp, COMBINE_GROUP):
        tiles = []
        for e in range(e0, min(e0 + COMBINE_GROUP, n_exp)):
            rel = pos_ref[:, e:e + 1] - window_start(b, j, e, SMALL_ROWS)
            tiles.append(jnp.where(rel == lane, 1.0, 0.0).astype(BF16))
        rows = pl.ds(e0 * SMALL_ROWS, len(tiles) * SMALL_ROWS)
        part = jnp.dot(jnp.concatenate(tiles, axis=1), slab_ref[slot, rows, :],
                       preferred_element_type=F32)
        acc = part if acc is None else acc + part

    small_now = all_small(b, j)

    @pl.when(small_now)
    def _():
        finish(acc)

    @pl.when(jnp.logical_not(small_now))
    def _():
        wide_lane = lax.broadcasted_iota(I32, (1, BIG_PAD), 1)
        for e in range(n_exp):
            st = window_start(b, j, e, BIG_ROWS)
            cp = pltpu.make_async_copy(y_hbm.at[b, e, pl.ds(st, BIG_ROWS)],
                                       big_ref.at[pl.ds(0, BIG_ROWS)], big_sem)
            cp.start()
            cp.wait()
            onehot = jnp.where(pos_ref[:, e:e + 1] - st == wide_lane, 1.0, 0.0).astype(BF16)
            part = jnp.dot(onehot, big_ref[...], preferred_element_type=F32)
            if e == 0:
                acc_ref[...] = part
            else:
                acc_ref[...] += part
        finish(acc_ref[...])

    @pl.when(last)
    def _():
        for e in range(n_exp):
            small_copy(b, j, e, 1 - slot).wait()


def _combine(slo, x1, pos, y_slots, norm_final, seq):
    t, d = x1.shape
    batch, n_exp, cap, _ = y_slots.shape
    n_tiles = seq // ROUTE_TILE
    slo_flat = jnp.concatenate([slo, jnp.full((batch, 1, n_exp), cap, I32)], axis=1).reshape(-1)
    row = lambda b, j, slo: (b * n_tiles + j, 0)
    grid_spec = pltpu.PrefetchScalarGridSpec(
        num_scalar_prefetch=1,
        grid=(batch, n_tiles),
        in_specs=[
            pl.BlockSpec((ROUTE_TILE, d), row),
            pl.BlockSpec((ROUTE_TILE, n_exp), row),
            pl.BlockSpec(memory_space=pl.ANY),
            pl.BlockSpec((1, d), lambda b, j, slo: (0, 0)),
        ],
        out_specs=pl.BlockSpec((ROUTE_TILE, d), row),
        scratch_shapes=[pltpu.VMEM((2, n_exp * SMALL_ROWS, d), BF16), pltpu.VMEM((BIG_PAD, d), BF16),
                        pltpu.VMEM((ROUTE_TILE, d), F32),
                        pltpu.SemaphoreType.DMA((2,)), pltpu.SemaphoreType.DMA(())],
    )
    return pl.pallas_call(
        functools.partial(_combine_kernel, n_exp=n_exp, cap=cap),
        grid_spec=grid_spec,
        out_shape=jax.ShapeDtypeStruct((t, d), F32),
        compiler_params=_params(("arbitrary", "arbitrary"), V7X_VMEM_LIMIT_BYTES),
        name="combine",
    )(slo_flat, x1, pos, y_slots, norm_final)


def _rope_tables(seq):
    rows = seq // GRID_W
    row = np.repeat(np.arange(rows, dtype=np.float64), GRID_W)
    col = np.tile(np.arange(GRID_W, dtype=np.float64), rows)
    n_freq = HALF_ROT // 2
    freqs = ROPE_THETA ** (-np.arange(n_freq, dtype=np.float64) / n_freq)
    ang = np.concatenate([row[:, None] * freqs, col[:, None] * freqs], axis=-1)
    cos, sin = np.cos(ang), np.sin(ang)
    return (jnp.asarray(np.concatenate([cos, cos], axis=-1), F32),
            jnp.asarray(np.concatenate([-sin, sin], axis=-1), F32))


def kernel(x, norm_mix, w_in, ret_decay_fwd, ret_decay_bwd, ret_gn_gain, w_ret_branch, q_norm, k_norm,
           w_att_branch, w_o, norm_ffn, w_router, w_expert_gate, w_expert_up, w_expert_down, norm_final):
    batch, seq, d = x.shape
    depth = norm_mix.shape[0]
    n_exp = w_router.shape[2]
    cap = CAPACITY_FACTOR * seq // n_exp
    assert seq % TOKEN_TILE == 0 and seq % ROUTE_TILE == 0
    assert (seq - ATT_FIRST_KV) % ATT_KV_TILE == 0 and ATT_KV_TILE % (2 * LANES) == 0
    assert ATT_FIRST_KV % (2 * LANES) == 0 and seq % ATT_Q_TILE == 0
    assert seq % CHUNK == 0 and (seq // CHUNK) % RET_UNROLL == 0 and seq % RET_NORM_ROWS == 0
    assert LANES % n_exp == 0 and d % LANES == 0 and seq % (8 * LANES // n_exp) == 0
    assert cap >= BIG_ROWS and (cap - BIG_ROWS) % SLAB_ALIGN == 0 and (cap - SMALL_ROWS) % SLAB_ALIGN == 0
    assert w_expert_gate.shape[3] % FF_TILE == 0 and cap % FFN_ROW_BLOCK == 0
    assert depth == 1, "the final RMSNorm is fused into the combine kernel of the single layer"
    l = 0
    cos2, sin2 = _rope_tables(seq)
    xf = x.reshape(batch * seq, d)
    lg = jnp.stack([jnp.log1p(-jnp.exp(ret_decay_fwd[l].astype(F32))),
                    jnp.log1p(-jnp.exp(ret_decay_bwd[l].astype(F32)))])
    rq, rk, rv, sg, aq, ak, av, gr, ga = _inproj(
        xf, norm_mix[l][None], w_in[l], cos2, sin2, q_norm[l][None], k_norm[l][None], seq)
    u_ret = _retention(lg, rq, rk, rv, sg, ret_gn_gain[l][None], batch, seq)
    y_att = _attention(aq, ak, av, batch, seq)
    w_r = jnp.pad(w_router[l].astype(F32), ((0, 0), (0, LANES - n_exp)))
    w_r_hi = w_r.astype(BF16)
    w_r_lo = (w_r - w_r_hi.astype(F32)).astype(BF16)
    x1, h2a, aff = _merge(xf, u_ret, y_att, gr, ga, w_ret_branch[l], w_att_branch[l], w_o[l],
                          norm_ffn[l][None], jnp.concatenate([w_r_hi, w_r_lo], axis=1), n_exp)
    idx, pos, slo = _route(h2a, aff, batch, seq, cap)
    y_slots = _ffn(idx, h2a, w_expert_gate[l], w_expert_up[l], w_expert_down[l])
    out = _combine(slo, x1, pos, y_slots, norm_final[None], seq)
    return out.reshape(batch, seq, d)
```

```python
import functools

import jax
import jax.numpy as jnp
import numpy as np
from jax import lax
from jax.experimental import pallas as pl
from jax.experimental.pallas import tpu as pltpu

F32 = jnp.float32
BF16 = jnp.bfloat16
I32 = jnp.int32

GRID_W = 64
HEAD_DIM = 128
HALF_ROT = HEAD_DIM // 2
RET_HEADS = 4
RET_DIM = RET_HEADS * HEAD_DIM
ATT_HEADS = 4
ATT_KV_HEADS = 2
ATT_GROUP = ATT_HEADS // ATT_KV_HEADS
ATT_DIM = ATT_HEADS * HEAD_DIM
ATT_KV_DIM = ATT_KV_HEADS * HEAD_DIM
CHUNK = 256
ROPE_THETA = 10000.0
CAPACITY_FACTOR = 2
EPS = 1e-6
LOG2_E = 1.4426950408889634

V7X_VMEM_LIMIT_BYTES = 56 * 1024 * 1024
TOKEN_TILE = 512
ATT_Q_TILE = 1024
ATT_FIRST_KV = 512
ATT_KV_TILE = 3840
RET_NORM_ROWS = 2048
RET_UNROLL = 8
LANES = 128
ROUTE_TILE = 512
SLAB_ALIGN = 16
SMALL_ROWS = LANES
BIG_ROWS = ROUTE_TILE + SLAB_ALIGN
BIG_PAD = -(-BIG_ROWS // LANES) * LANES
COMBINE_GROUP = 4
FF_TILE = 256
FFN_ROW_BLOCK = 512
BISECT_STEPS = 24


def _params(sem, vmem=None):
    return pltpu.CompilerParams(dimension_semantics=sem, vmem_limit_bytes=vmem)


def _inproj_kernel(x_ref, g_ref, w_ref, cos_ref, sin_ref, qn_ref, kn_ref,
                   rq_ref, rk_ref, rv_ref, sg_ref, aq_ref, ak_ref, av_ref, gr_ref, ga_ref):
    x = x_ref[...]
    h = (x * lax.rsqrt(jnp.mean(x * x, axis=-1, keepdims=True) + EPS) * g_ref[...]).astype(BF16)
    cos = cos_ref[...]
    sin = sin_ref[...]
    d_model = x.shape[1]
    scale = HEAD_DIM ** -0.5

    def proj(lo, width):
        return jnp.dot(h, w_ref[:, lo:lo + width].astype(BF16), preferred_element_type=F32)

    def rope(t):
        return t * cos + pltpu.roll(t, HALF_ROT, axis=1) * sin

    def head_norm(t, gain):
        return t * lax.rsqrt(jnp.mean(t * t, axis=-1, keepdims=True) + EPS) * gain

    def head(p, i):
        return p[:, i * HEAD_DIM:(i + 1) * HEAD_DIM]

    def put(ref, i, val):
        ref[:, i * HEAD_DIM:(i + 1) * HEAD_DIM] = val.astype(ref.dtype)

    off = 0
    p = proj(off, RET_DIM)
    for i in range(RET_HEADS):
        put(rq_ref, i, rope(head(p, i)))
    off += RET_DIM
    p = proj(off, RET_DIM)
    for i in range(RET_HEADS):
        put(rk_ref, i, rope(head(p, i)) * scale)
    off += RET_DIM
    rv_ref[...] = proj(off, RET_DIM).astype(BF16)
    off += RET_DIM
    p = proj(off, RET_DIM)
    sg_ref[...] = (p * jax.nn.sigmoid(p)).astype(BF16)
    off += RET_DIM
    p = proj(off, ATT_DIM)
    for i in range(ATT_HEADS):
        put(aq_ref, i, rope(head_norm(head(p, i), qn_ref[...])) * (scale * LOG2_E))
    off += ATT_DIM
    p = proj(off, ATT_KV_DIM)
    ones_col = jnp.where(lax.broadcasted_iota(I32, (x.shape[0], HEAD_DIM), 1) == 0, 1.0, 0.0)
    for i in range(ATT_KV_HEADS):
        put(ak_ref, 2 * i, rope(head_norm(head(p, i), kn_ref[...])))
        put(ak_ref, 2 * i + 1, ones_col)
    off += ATT_KV_DIM
    av_ref[...] = proj(off, ATT_KV_DIM).T.astype(BF16)
    off += ATT_KV_DIM
    gr_ref[...] = jax.nn.sigmoid(proj(off, d_model)).astype(BF16)
    off += d_model
    ga_ref[...] = jax.nn.sigmoid(proj(off, d_model)).astype(BF16)


def _inproj(xf, norm_g, w_in, cos2, sin2, q_norm, k_norm, seq):
    t, d = xf.shape
    tm = TOKEN_TILE
    in_width = w_in.shape[1]
    steps_per_seq = seq // tm
    row = lambda i: (i, 0)
    const = lambda i: (0, 0)
    pos = lambda i: (i % steps_per_seq, 0)
    widths = (RET_DIM, RET_DIM, RET_DIM, RET_DIM, ATT_DIM, 2 * ATT_KV_DIM, ATT_KV_DIM, d, d)
    av_index = 6
    return pl.pallas_call(
        _inproj_kernel,
        grid=(t // tm,),
        in_specs=[
            pl.BlockSpec((tm, d), row),
            pl.BlockSpec((1, d), const),
            pl.BlockSpec((d, in_width), const, pipeline_mode=pl.Buffered(1)),
            pl.BlockSpec((tm, HEAD_DIM), pos),
            pl.BlockSpec((tm, HEAD_DIM), pos),
            pl.BlockSpec((1, HEAD_DIM), const),
            pl.BlockSpec((1, HEAD_DIM), const),
        ],
        out_specs=[pl.BlockSpec((ATT_KV_DIM, tm), lambda i: (0, i)) if k == av_index
                   else pl.BlockSpec((tm, w), row) for k, w in enumerate(widths)],
        out_shape=[jax.ShapeDtypeStruct((ATT_KV_DIM, t) if k == av_index else (t, w), BF16)
                   for k, w in enumerate(widths)],
        compiler_params=_params(("parallel",), V7X_VMEM_LIMIT_BYTES),
        name="inproj",
    )(xf, norm_g, w_in, cos2, sin2, q_norm, k_norm)


def _retention_kernel(lg_ref, q_ref, k_ref, v_ref, sg_ref, gain_ref, o_ref, of_ref, ob_ref):
    hd = pl.program_id(1)
    lgf = lg_ref[0, hd]
    lgb = lg_ref[1, hd]
    n_chunks = q_ref.shape[0] // CHUNK
    ii = lax.broadcasted_iota(I32, (CHUNK, CHUNK), 0)
    jj = lax.broadcasted_iota(I32, (CHUNK, CHUNK), 1)
    dist = (ii - jj).astype(F32)
    dec_f = jnp.where(dist >= 0, jnp.exp(lgf * jnp.maximum(dist, 0.0)), 0.0)
    dec_b = jnp.where(dist < 0, jnp.exp(lgb * jnp.maximum(-dist, 0.0)), 0.0)
    t = lax.broadcasted_iota(I32, (CHUNK, 1), 0).astype(F32)
    kd_f = jnp.exp(lgf * (CHUNK - 1.0 - t))
    qd_f = jnp.exp(lgf * (t + 1.0))
    kd_b = jnp.exp(lgb * t)
    qd_b = jnp.exp(lgb * (CHUNK - t))
    zero_row = jnp.zeros((1, HEAD_DIM), F32)
    cd_f = jnp.exp(zero_row + lgf * CHUNK)
    cd_b = jnp.exp(zero_row + lgb * CHUNK)

    def one_chunk(n, dec, kd, qd, cd, state):
        r0 = pl.multiple_of(n * CHUNK, CHUNK)
        q = q_ref[pl.ds(r0, CHUNK), :]
        k = k_ref[pl.ds(r0, CHUNK), :]
        v = v_ref[pl.ds(r0, CHUNK), :]
        s = lax.dot_general(q, k, (((1,), (1,)), ((), ())), preferred_element_type=F32) * dec
        intra = jnp.dot(s.astype(BF16), v, preferred_element_type=F32)
        q_dec = (q.astype(F32) * qd).astype(BF16)
        inter = jnp.dot(q_dec, state.astype(BF16), preferred_element_type=F32)
        k_dec_t = (k.astype(F32) * kd).T.astype(BF16)
        new_state = state * cd + jnp.dot(k_dec_t, v, preferred_element_type=F32)
        return r0, intra + inter, new_state

    def scan_body(n, carry):
        sf, sb = carry
        r0, of, sf = one_chunk(n, dec_f, kd_f, qd_f, cd_f, sf)
        of_ref[pl.ds(r0, CHUNK), :] = of
        r0, ob, sb = one_chunk(n_chunks - 1 - n, dec_b, kd_b, qd_b, cd_b, sb)
        ob_ref[pl.ds(r0, CHUNK), :] = ob
        return sf, sb

    zeros = jnp.zeros((HEAD_DIM, HEAD_DIM), F32)
    lax.fori_loop(0, n_chunks, scan_body, (zeros, zeros), unroll=RET_UNROLL)

    rows = RET_NORM_ROWS

    def norm_body(n, carry):
        r0 = pl.multiple_of(n * rows, rows)
        o = of_ref[pl.ds(r0, rows), :] + ob_ref[pl.ds(r0, rows), :]
        mu = jnp.mean(o, axis=-1, keepdims=True)
        var = jnp.mean(jnp.square(o - mu), axis=-1, keepdims=True)
        yn = (o - mu) * lax.rsqrt(var + EPS) * gain_ref[...]
        o_ref[pl.ds(r0, rows), :] = (sg_ref[pl.ds(r0, rows), :].astype(F32) * yn).astype(BF16)
        return carry

    lax.fori_loop(0, q_ref.shape[0] // rows, norm_body, 0)


def _retention(lg, rq, rk, rv, sg, gn_gain, batch, seq):
    t = rq.shape[0]
    blk = pl.BlockSpec((seq, HEAD_DIM), lambda b, h: (b, h))
    return pl.pallas_call(
        _retention_kernel,
        grid=(batch, RET_HEADS),
        in_specs=[
            pl.BlockSpec(memory_space=pltpu.SMEM),
            blk, blk, blk, blk,
            pl.BlockSpec((1, HEAD_DIM), lambda b, h: (0, h)),
        ],
        out_specs=blk,
        out_shape=jax.ShapeDtypeStruct((t, RET_DIM), BF16),
        scratch_shapes=[pltpu.VMEM((seq, HEAD_DIM), F32), pltpu.VMEM((seq, HEAD_DIM), F32)],
        compiler_params=_params(("parallel", "parallel"), V7X_VMEM_LIMIT_BYTES),
        name="retention",
    )(lg, rq, rk, rv, sg, gn_gain)


def _attention_kernel(q_ref, k_ref, vt_ref, o_ref, qa_ref):
    tq = q_ref.shape[0]
    n_rest = (k_ref.shape[0] - ATT_FIRST_KV) // ATT_KV_TILE
    lane = lax.broadcasted_iota(I32, (tq, 2 * HEAD_DIM), 1)

    def set_query(hh, shift):
        q = q_ref[:, hh * HEAD_DIM:(hh + 1) * HEAD_DIM]
        qa = jnp.concatenate([q, jnp.zeros((tq, HEAD_DIM), BF16)], axis=1)
        if shift is not None:
            qa = jnp.where(lane == HEAD_DIM, (-shift.T).astype(BF16), qa)
        qa_ref[hh] = qa

    def shifted_scores(hh, r0, rows):
        return lax.dot_general(k_ref[pl.ds(r0, rows), :], qa_ref[hh], (((1,), (1,)), ((), ())),
                               preferred_element_type=F32)

    def weigh(p, vt):
        return jnp.sum(p, axis=0, keepdims=True), jnp.dot(vt, p.astype(BF16), preferred_element_type=F32)

    vt = vt_ref[:, pl.ds(0, ATT_FIRST_KV)]
    state = []
    for hh in range(ATT_GROUP):
        set_query(hh, None)
        s = shifted_scores(hh, 0, ATT_FIRST_KV)
        shift = jnp.max(s, axis=0, keepdims=True).astype(BF16).astype(F32)
        l, acc = weigh(jnp.exp2(s - shift), vt)
        set_query(hh, shift)
        state.append((shift, l, acc))

    def body(c, carry):
        r0 = pl.multiple_of(ATT_FIRST_KV + c * ATT_KV_TILE, 2 * LANES)
        vt = vt_ref[:, pl.ds(r0, ATT_KV_TILE)]
        fast = []
        overflow = None
        for hh, (shift, l, acc) in enumerate(carry):
            s = shifted_scores(hh, r0, ATT_KV_TILE)
            dl, dacc = weigh(jnp.exp2(s), vt)
            l_new = l + dl
            acc_new = acc + dacc
            fast.append((shift, l_new, acc_new))
            bad = jnp.maximum(jnp.max(jnp.where(jnp.isfinite(l_new), 0.0, 1.0)),
                              jnp.max(jnp.where(jnp.isfinite(acc_new), 0.0, 1.0)))
            overflow = bad if overflow is None else jnp.maximum(overflow, bad)

        def redo():
            out = []
            for hh, (shift, l, acc) in enumerate(carry):
                s = shifted_scores(hh, r0, ATT_KV_TILE)
                raised = (shift + jnp.maximum(jnp.max(s, axis=0, keepdims=True), 0.0)).astype(BF16).astype(F32)
                step = raised - shift
                alpha = jnp.exp2(-step)
                dl, dacc = weigh(jnp.exp2(s - step), vt)
                out.append((raised, alpha * l + dl, alpha * acc + dacc))
                set_query(hh, raised)
            return tuple(out)

        return lax.cond(overflow == 0.0, lambda: tuple(fast), redo)

    final = lax.fori_loop(0, n_rest, body, tuple(state))
    for hh, (_, l, acc) in enumerate(final):
        o_ref[:, hh * HEAD_DIM:(hh + 1) * HEAD_DIM] = (acc / l).T.astype(BF16)


def _attention(aq, ak, av_t, batch, seq):
    t = aq.shape[0]
    tq = ATT_Q_TILE
    nq = seq // tq
    gw = ATT_GROUP * HEAD_DIM
    q_spec = pl.BlockSpec((tq, gw), lambda b, g, i: (b * nq + i, g))
    k_spec = pl.BlockSpec((seq, 2 * HEAD_DIM), lambda b, g, i: (b, g))
    vt_spec = pl.BlockSpec((HEAD_DIM, seq), lambda b, g, i: (g, b))
    return pl.pallas_call(
        _attention_kernel,
        grid=(batch, ATT_KV_HEADS, nq),
        in_specs=[q_spec, k_spec, vt_spec],
        out_specs=q_spec,
        out_shape=jax.ShapeDtypeStruct((t, ATT_DIM), BF16),
        scratch_shapes=[pltpu.VMEM((ATT_GROUP, tq, 2 * HEAD_DIM), BF16)],
        compiler_params=_params(("parallel", "parallel", "parallel"), V7X_VMEM_LIMIT_BYTES),
        name="attention",
    )(aq, ak, av_t)


def _merge_kernel(x_ref, ur_ref, ya_ref, gr_ref, ga_ref, wr_ref, wa_ref, wo_ref, nf_ref, wrt_ref,
                  x1_ref, h2a_ref, aff_ref, *, n_exp):
    d = x_ref.shape[1]
    y_ret = jnp.dot(ur_ref[...], wr_ref[...].astype(BF16), preferred_element_type=F32)
    y_att = jnp.dot(ya_ref[...], wa_ref[...].astype(BF16), preferred_element_type=F32)
    mixed = gr_ref[...].astype(F32) * y_ret + ga_ref[...].astype(F32) * y_att
    x1 = x_ref[...] + jnp.dot(mixed.astype(BF16), wo_ref[...].astype(BF16), preferred_element_type=F32)
    x1_ref[...] = x1
    h2 = x1 * lax.rsqrt(jnp.mean(x1 * x1, axis=-1, keepdims=True) + EPS) * nf_ref[...]
    h2a_ref[:, :d] = h2
    h_hi = h2.astype(BF16)
    h_lo = (h2 - h_hi.astype(F32)).astype(BF16)
    r_hi = jnp.dot(h_hi, wrt_ref[...], preferred_element_type=F32)
    r_lo = jnp.dot(h_lo, wrt_ref[:, :LANES], preferred_element_type=F32)
    logits = r_hi[:, :LANES] + r_hi[:, LANES:] + r_lo
    lane = lax.broadcasted_iota(I32, logits.shape, 1)
    logits = jnp.where(lane < n_exp, logits, -jnp.inf)
    e = jnp.exp(logits - jnp.max(logits, axis=-1, keepdims=True))
    aff = e / jnp.sum(e, axis=-1, keepdims=True)
    h2a_ref[:, d:] = aff
    aff_ref[...] = aff[:, :n_exp]


def _merge(xf, u_ret, y_att, gr, ga, w_ret, w_att, w_o, norm_ffn, w_router_split, n_exp):
    t, d = xf.shape
    tm = TOKEN_TILE
    row = lambda i: (i, 0)
    const = lambda i: (0, 0)
    return pl.pallas_call(
        functools.partial(_merge_kernel, n_exp=n_exp),
        grid=(t // tm,),
        in_specs=[
            pl.BlockSpec((tm, d), row),
            pl.BlockSpec((tm, RET_DIM), row),
            pl.BlockSpec((tm, ATT_DIM), row),
            pl.BlockSpec((tm, d), row),
            pl.BlockSpec((tm, d), row),
            pl.BlockSpec((RET_DIM, d), const, pipeline_mode=pl.Buffered(1)),
            pl.BlockSpec((ATT_DIM, d), const, pipeline_mode=pl.Buffered(1)),
            pl.BlockSpec((d, d), const, pipeline_mode=pl.Buffered(1)),
            pl.BlockSpec((1, d), const),
            pl.BlockSpec((d, 2 * LANES), const),
        ],
        out_specs=[pl.BlockSpec((tm, d), row), pl.BlockSpec((tm, d + LANES), row),
                   pl.BlockSpec((tm, n_exp), row)],
        out_shape=[jax.ShapeDtypeStruct((t, d), F32), jax.ShapeDtypeStruct((t, d + LANES), F32),
                   jax.ShapeDtypeStruct((t, n_exp), F32)],
        compiler_params=_params(("parallel",), V7X_VMEM_LIMIT_BYTES),
        name="merge",
    )(xf, u_ret, y_att, gr, ga, w_ret, w_att, w_o, norm_ffn, w_router_split)


def _route_kernel(aff_ref, affc_ref, idx_ref, pos_ref, slo_ref, msk_ref, cum_ref, cum_t_ref, bnd_ref, bnd_smem,
                  cnt_ref, sem, *, cap, n_exp):
    b = pl.program_id(0)
    seq = aff_ref.shape[0]
    n_tiles = seq // ROUTE_TILE
    aff = aff_ref[...]
    affc = affc_ref[...]
    capf = float(cap)

    def per_expert(x, op, reduce):
        r = reduce(x.reshape(x.shape[0] // 8, 8, LANES), axis=0)
        shift = n_exp
        while shift < LANES:
            r = op(r, pltpu.roll(r, shift, axis=1))
            shift *= 2
        return reduce(r, axis=0, keepdims=True)

    def bisect(lo, hi):
        mid = 0.5 * (lo + hi)
        ok = per_expert(jnp.where(affc >= mid, 1.0, 0.0), jnp.add, jnp.sum) >= capf
        return jnp.where(ok, mid, lo), jnp.where(ok, hi, mid)

    def bracket(lo, hi):
        mn = per_expert(jnp.where(affc >= lo, affc, jnp.inf), jnp.minimum, jnp.min)
        mx = per_expert(jnp.where(affc < hi, affc, -jnp.inf), jnp.maximum, jnp.max)
        return mn, mx

    def not_isolated(lo, hi):
        mn, mx = bracket(lo, hi)
        return jnp.max(jnp.where(mn < mx, 1.0, 0.0)) > 0.0

    lo, hi = lax.fori_loop(0, BISECT_STEPS, lambda i, c: bisect(*c),
                           (jnp.zeros((1, LANES), F32), jnp.full((1, LANES), 2.0, F32)))

    def refine(c):
        lo, hi = bisect(c[0], c[1])
        return lo, hi, not_isolated(lo, hi)

    lo, hi, _ = lax.while_loop(lambda c: c[2], refine, (lo, hi, not_isolated(lo, hi)))
    thr, _ = bracket(lo, hi)
    gt = aff > thr
    eq = aff == thr
    n_gt = jnp.sum(jnp.where(gt, 1.0, 0.0).reshape(seq // 8, 8, LANES), axis=0)
    need = capf - jnp.sum(n_gt, axis=0, keepdims=True)

    tri = (lax.broadcasted_iota(I32, (ROUTE_TILE, ROUTE_TILE), 0)
           >= lax.broadcasted_iota(I32, (ROUTE_TILE, ROUTE_TILE), 1))
    tri = jnp.where(tri, 1.0, 0.0).astype(BF16)

    def cumsum_tokens(store_tile_start):
        def body(c, carry):
            r0 = pl.multiple_of(c * ROUTE_TILE, ROUTE_TILE)
            if store_tile_start:
                slo_ref[0, pl.ds(c, 1), :] = carry[:, :n_exp].astype(I32)
                bnd_ref[pl.ds(c, 1), :] = carry.astype(I32)
            cs = jnp.dot(tri, msk_ref[pl.ds(r0, ROUTE_TILE), :], preferred_element_type=F32) + carry
            cum_ref[pl.ds(r0, ROUTE_TILE), :] = cs
            return cs[ROUTE_TILE - 1:ROUTE_TILE, :]

        return lax.fori_loop(0, n_tiles, body, jnp.zeros((1, LANES), F32))

    msk_ref[...] = jnp.where(eq, 1.0, 0.0).astype(BF16)
    cumsum_tokens(False)
    take = jnp.logical_and(eq, cum_ref[...] - 1.0 < need)
    mask = jnp.logical_or(gt, take)
    msk_ref[...] = jnp.where(mask, 1.0, 0.0).astype(BF16)
    bnd_ref[...] = jnp.zeros(bnd_ref.shape, I32)
    bnd_ref[pl.ds(n_tiles, 1), :] = cumsum_tokens(True).astype(I32)
    pos_ref[...] = jnp.where(mask, cum_ref[...] - 1.0, -1.0)[:, :n_exp].astype(I32)

    to_smem = pltpu.make_async_copy(bnd_ref, bnd_smem, sem)
    to_smem.start()
    to_smem.wait()

    def transpose_tile(c, carry):
        r0 = pl.multiple_of(c * ROUTE_TILE, ROUTE_TILE)
        cum_t_ref[:, pl.ds(r0, ROUTE_TILE)] = cum_ref[pl.ds(r0, ROUTE_TILE), :].T
        return carry

    lax.fori_loop(0, n_tiles, transpose_tile, 0)
    slot_row = lax.broadcasted_iota(I32, (1, cap), 1).astype(F32)
    slot_sub = lax.broadcasted_iota(I32, (LANES, ROUTE_TILE), 0)
    eye = lax.broadcasted_iota(I32, (LANES, LANES), 0) == lax.broadcasted_iota(I32, (LANES, LANES), 1)
    for e in range(n_exp):
        ends = bnd_ref[pl.ds(1, n_tiles), e:e + 1].astype(F32)
        whole = jnp.sum(jnp.where(slot_row >= ends, float(ROUTE_TILE), 0.0), axis=0, keepdims=True)
        cnt_ref[...] = jnp.zeros(cnt_ref.shape, F32)

        def tile_body(c, carry, e=e):
            lo = bnd_smem[c, e]
            hi = bnd_smem[c + 1, e]
            r0 = pl.multiple_of(c * ROUTE_TILE, ROUTE_TILE)
            counts = cum_t_ref[e:e + 1, pl.ds(r0, ROUTE_TILE)]

            def lane_tile(k, carry2):
                slots = k * LANES + slot_sub
                below_hi = jnp.where(slots < hi, 1.0, 0.0)
                hit = jnp.where(counts <= slots.astype(F32), below_hi, 0.0)
                folded = hit[:, :LANES]
                for t0 in range(LANES, ROUTE_TILE, LANES):
                    folded = folded + hit[:, t0:t0 + LANES]
                cnt_ref[k] += folded
                return carry2

            lax.fori_loop(lo // LANES, (hi + LANES - 1) // LANES, lane_tile, 0)
            return carry

        lax.fori_loop(0, n_tiles, tile_body, 0)
        for k in range(cap // LANES):
            per_slot = jnp.sum(cnt_ref[k], axis=1, keepdims=True)
            part = jnp.sum(jnp.where(eye, per_slot, 0.0), axis=0, keepdims=True)
            part = part + whole[:, k * LANES:(k + 1) * LANES]
            idx_ref[0, e:e + 1, k * LANES:(k + 1) * LANES] = part.astype(I32) + b * seq


def _route(h2a, aff, batch, seq, cap):
    t, n_exp = aff.shape
    pack = LANES // n_exp
    aff_packed = aff.reshape(t // pack, LANES)
    aff_block = h2a.shape[1] // LANES - 1
    n_tiles = seq // ROUTE_TILE
    bnd_rows = -(-(n_tiles + 1) // 8) * 8
    return pl.pallas_call(
        functools.partial(_route_kernel, cap=cap, n_exp=n_exp),
        grid=(batch,),
        in_specs=[pl.BlockSpec((seq, LANES), lambda b: (b, aff_block)),
                  pl.BlockSpec((seq // pack, LANES), lambda b: (b, 0))],
        out_specs=[pl.BlockSpec((1, n_exp, cap), lambda b: (b, 0, 0)),
                   pl.BlockSpec((seq, n_exp), lambda b: (b, 0)),
                   pl.BlockSpec((1, n_tiles, n_exp), lambda b: (b, 0, 0))],
        out_shape=[jax.ShapeDtypeStruct((batch, n_exp, cap), I32),
                   jax.ShapeDtypeStruct((t, n_exp), I32),
                   jax.ShapeDtypeStruct((batch, n_tiles, n_exp), I32)],
        scratch_shapes=[pltpu.VMEM((seq, LANES), BF16), pltpu.VMEM((seq, LANES), F32),
                        pltpu.VMEM((LANES, seq), F32),
                        pltpu.VMEM((bnd_rows, LANES), I32), pltpu.SMEM((bnd_rows, LANES), I32),
                        pltpu.VMEM((cap // LANES, LANES, LANES), F32), pltpu.SemaphoreType.DMA(())],
        compiler_params=_params(("parallel",), V7X_VMEM_LIMIT_BYTES),
        name="route",
    )(h2a, aff_packed)


def _ffn_kernel(idx_ref, h2a_hbm, wg_hbm, wu_hbm, wd_hbm, y_ref, stage_ref, xe_ref, gate_ref, acc_ref,
                wg_buf, wu_buf, wd_buf, sem, wsem, *, n_exp, n_ff, rows_per_step):
    e = pl.program_id(0)
    mp = stage_ref.shape[0]
    m, d = xe_ref.shape
    batch = y_ref.shape[0]
    cap = y_ref.shape[2]

    def row_copy(expert, i):
        r = idx_ref[expert * mp + i]
        return pltpu.make_async_copy(h2a_hbm.at[pl.ds(r, 1)], stage_ref.at[pl.ds(i, 1)], sem)

    def wait_rows():
        pltpu.make_async_copy(h2a_hbm.at[pl.ds(0, mp)], stage_ref, sem).wait()

    def weight_copies(chunk, slot):
        ex = chunk // n_ff
        c0 = pl.multiple_of((chunk % n_ff) * FF_TILE, FF_TILE)
        return (pltpu.make_async_copy(wg_hbm.at[ex, :, pl.ds(c0, FF_TILE)], wg_buf.at[slot], wsem.at[slot]),
                pltpu.make_async_copy(wu_hbm.at[ex, :, pl.ds(c0, FF_TILE)], wu_buf.at[slot], wsem.at[slot]),
                pltpu.make_async_copy(wd_hbm.at[ex, pl.ds(c0, FF_TILE), :], wd_buf.at[slot], wsem.at[slot]))

    @pl.when(e == 0)
    def _():
        def issue(i, carry):
            row_copy(0, i).start()
            return carry

        lax.fori_loop(0, mp, issue, 0)
        for cp in weight_copies(0, 0):
            cp.start()

    wait_rows()

    nxt = jnp.minimum(e + 1, n_exp - 1)

    def gather_slice(j):
        for k in range(rows_per_step):
            row_copy(nxt, j * rows_per_step + k).start()

    def ff_tile(f, first=False, last=False):
        chunk = e * n_ff + f
        slot = chunk % 2

        for cp in weight_copies(chunk, slot):
            cp.wait()

        @pl.when(chunk + 1 < n_exp * n_ff)
        def _():
            for cp in weight_copies(chunk + 1, 1 - slot):
                cp.start()

        if not first:
            gather_slice(f - 1)

        if first:
            lane = lax.broadcasted_iota(I32, (m, LANES), 1)
            gate_ref[...] = jnp.sum(jnp.where(lane == e, stage_ref[:m, d:], 0.0), axis=1, keepdims=True)
        wg = wg_buf[slot].astype(BF16)
        wu = wu_buf[slot].astype(BF16)
        wd = wd_buf[slot].astype(BF16)
        for rb in range(m // FFN_ROW_BLOCK):
            rows = pl.ds(rb * FFN_ROW_BLOCK, FFN_ROW_BLOCK)
            if first:
                xb = stage_ref[rows, :d].astype(BF16)
                xe_ref[rows, :] = xb
            else:
                xb = xe_ref[rows, :]
            a = jnp.dot(xb, wg, preferred_element_type=F32)
            u = jnp.dot(xb, wu, preferred_element_type=F32)
            hm = (a * jax.nn.sigmoid(a) * u).astype(BF16)
            contrib = jnp.dot(hm, wd, preferred_element_type=F32)
            if first:
                acc_ref[rows, :] = contrib
            elif last:
                bb, r0 = divmod(rb * FFN_ROW_BLOCK, cap)
                y_ref[bb, 0, pl.ds(r0, FFN_ROW_BLOCK), :] = (
                    (acc_ref[rows, :] + contrib) * gate_ref[rows, :]).astype(BF16)
            else:
                acc_ref[rows, :] += contrib

    ff_tile(0, first=True)

    def middle_tile(f, carry):
        ff_tile(f)
        return carry

    lax.fori_loop(1, n_ff - 1, middle_tile, 0)
    ff_tile(n_ff - 1, last=True)

    @pl.when(e == n_exp - 1)
    def _():
        wait_rows()


def _ffn(idx, h2a, w_gate, w_up, w_down):
    batch, n_exp, cap = idx.shape
    d = h2a.shape[1] - LANES
    ff = w_gate.shape[2]
    n_ff = ff // FF_TILE
    assert n_ff >= 2, "the next expert's rows are gathered during ff tiles 1.."
    m = batch * cap
    rows_per_step = -(-pl.cdiv(m, n_ff - 1) // 8) * 8
    mp = (n_ff - 1) * rows_per_step
    idx_flat = jnp.pad(idx.transpose(1, 0, 2).reshape(n_exp, m), ((0, 0), (0, mp - m))).reshape(-1)
    any_spec = pl.BlockSpec(memory_space=pl.ANY)
    grid_spec = pltpu.PrefetchScalarGridSpec(
        num_scalar_prefetch=1,
        grid=(n_exp,),
        in_specs=[any_spec, any_spec, any_spec, any_spec],
        out_specs=pl.BlockSpec((batch, 1, cap, d), lambda e, idx: (0, e, 0, 0)),
        scratch_shapes=[pltpu.VMEM((mp, d + LANES), F32), pltpu.VMEM((m, d), BF16),
                        pltpu.VMEM((m, 1), F32), pltpu.VMEM((m, d), F32),
                        pltpu.VMEM((2, d, FF_TILE), F32), pltpu.VMEM((2, d, FF_TILE), F32),
                        pltpu.VMEM((2, FF_TILE, d), F32),
                        pltpu.SemaphoreType.DMA(()), pltpu.SemaphoreType.DMA((2,))],
    )
    return pl.pallas_call(
        functools.partial(_ffn_kernel, n_exp=n_exp, n_ff=n_ff, rows_per_step=rows_per_step),
        grid_spec=grid_spec,
        out_shape=jax.ShapeDtypeStruct((batch, n_exp, cap, d), BF16),
        compiler_params=_params(("arbitrary",), V7X_VMEM_LIMIT_BYTES),
        name="ffn",
    )(idx_flat, h2a, w_gate, w_up, w_down)


def _window_fits(s_hi, start, rows):
    return s_hi - start <= rows


def _combine_kernel(slo_ref, x1_ref, pos_ref, y_hbm, nf_ref, o_ref, slab_ref, big_ref, acc_ref, sem, big_sem,
                    *, n_exp, cap):
    b = pl.program_id(0)
    j = pl.program_id(1)
    n_batch = pl.num_programs(0)
    n_tiles = pl.num_programs(1)
    step = b * n_tiles + j
    slot = step % 2

    def bounds(bb, jj, e):
        base = (bb * (n_tiles + 1) + jj) * n_exp + e
        return slo_ref[base], slo_ref[base + n_exp]

    def window_start(bb, jj, e, rows):
        s_lo, _ = bounds(bb, jj, e)
        return pl.multiple_of(jnp.minimum((s_lo // SLAB_ALIGN) * SLAB_ALIGN, cap - rows), SLAB_ALIGN)

    def all_small(bb, jj):
        ok = None
        for e in range(n_exp):
            fits = _window_fits(bounds(bb, jj, e)[1], window_start(bb, jj, e, SMALL_ROWS), SMALL_ROWS)
            ok = fits if ok is None else jnp.logical_and(ok, fits)
        return ok

    def small_copy(bb, jj, e, dst_slot):
        return pltpu.make_async_copy(
            y_hbm.at[bb, e, pl.ds(window_start(bb, jj, e, SMALL_ROWS), SMALL_ROWS)],
            slab_ref.at[dst_slot, pl.ds(e * SMALL_ROWS, SMALL_ROWS)], sem.at[dst_slot])

    def finish(moe):
        x2 = x1_ref[...] + moe
        o_ref[...] = x2 * lax.rsqrt(jnp.mean(x2 * x2, axis=-1, keepdims=True) + EPS) * nf_ref[...]

    last = step + 1 == n_batch * n_tiles

    @pl.when(step == 0)
    def _():
        big_ref[...] = jnp.zeros(big_ref.shape, big_ref.dtype)
        for e in range(n_exp):
            small_copy(b, j, e, slot).start()

    for e in range(n_exp):
        small_copy(b, j, e, slot).wait()
    wrap = j + 1 == n_tiles
    nb = jnp.where(last, b, jnp.where(wrap, b + 1, b))
    nj = jnp.where(last, j, jnp.where(wrap, 0, j + 1))
    for e in range(n_exp):
        small_copy(nb, nj, e, 1 - slot).start()

    lane = lax.broadcasted_iota(I32, (1, SMALL_ROWS), 1)
    acc = None
    for e0 in range(0, n_exp, COMBINE_GROUP):
        tiles = []
        for e in range(e0, min(e0 + COMBINE_GROUP, n_exp)):
            rel = pos_ref[:, e:e + 1] - window_start(b, j, e, SMALL_ROWS)
            tiles.append(jnp.where(rel == lane, 1.0, 0.0).astype(BF16))
        rows = pl.ds(e0 * SMALL_ROWS, len(tiles) * SMALL_ROWS)
        part = jnp.dot(jnp.concatenate(tiles, axis=1), slab_ref[slot, rows, :],
                       preferred_element_type=F32)
        acc = part if acc is None else acc + part

    small_now = all_small(b, j)

    @pl.when(small_now)
    def _():
        finish(acc)

    @pl.when(jnp.logical_not(small_now))
    def _():
        wide_lane = lax.broadcasted_iota(I32, (1, BIG_PAD), 1)
        for e in range(n_exp):
            st = window_start(b, j, e, BIG_ROWS)
            cp = pltpu.make_async_copy(y_hbm.at[b, e, pl.ds(st, BIG_ROWS)],
                                       big_ref.at[pl.ds(0, BIG_ROWS)], big_sem)
            cp.start()
            cp.wait()
            onehot = jnp.where(pos_ref[:, e:e + 1] - st == wide_lane, 1.0, 0.0).astype(BF16)
            part = jnp.dot(onehot, big_ref[...], preferred_element_type=F32)
            if e == 0:
                acc_ref[...] = part
            else:
                acc_ref[...] += part
        finish(acc_ref[...])

    @pl.when(last)
    def _():
        for e in range(n_exp):
            small_copy(b, j, e, 1 - slot).wait()


def _combine(slo, x1, pos, y_slots, norm_final, seq):
    t, d = x1.shape
    batch, n_exp, cap, _ = y_slots.shape
    n_tiles = seq // ROUTE_TILE
    slo_flat = jnp.concatenate([slo, jnp.full((batch, 1, n_exp), cap, I32)], axis=1).reshape(-1)
    row = lambda b, j, slo: (b * n_tiles + j, 0)
    grid_spec = pltpu.PrefetchScalarGridSpec(
        num_scalar_prefetch=1,
        grid=(batch, n_tiles),
        in_specs=[
            pl.BlockSpec((ROUTE_TILE, d), row),
            pl.BlockSpec((ROUTE_TILE, n_exp), row),
            pl.BlockSpec(memory_space=pl.ANY),
            pl.BlockSpec((1, d), lambda b, j, slo: (0, 0)),
        ],
        out_specs=pl.BlockSpec((ROUTE_TILE, d), row),
        scratch_shapes=[pltpu.VMEM((2, n_exp * SMALL_ROWS, d), BF16), pltpu.VMEM((BIG_PAD, d), BF16),
                        pltpu.VMEM((ROUTE_TILE, d), F32),
                        pltpu.SemaphoreType.DMA((2,)), pltpu.SemaphoreType.DMA(())],
    )
    return pl.pallas_call(
        functools.partial(_combine_kernel, n_exp=n_exp, cap=cap),
        grid_spec=grid_spec,
        out_shape=jax.ShapeDtypeStruct((t, d), F32),
        compiler_params=_params(("arbitrary", "arbitrary"), V7X_VMEM_LIMIT_BYTES),
        name="combine",
    )(slo_flat, x1, pos, y_slots, norm_final)


def _rope_tables(seq):
    rows = seq // GRID_W
    row = np.repeat(np.arange(rows, dtype=np.float64), GRID_W)
    col = np.tile(np.arange(GRID_W, dtype=np.float64), rows)
    n_freq = HALF_ROT // 2
    freqs = ROPE_THETA ** (-np.arange(n_freq, dtype=np.float64) / n_freq)
    ang = np.concatenate([row[:, None] * freqs, col[:, None] * freqs], axis=-1)
    cos, sin = np.cos(ang), np.sin(ang)
    return (jnp.asarray(np.concatenate([cos, cos], axis=-1), F32),
            jnp.asarray(np.concatenate([-sin, sin], axis=-1), F32))


def kernel(x, norm_mix, w_in, ret_decay_fwd, ret_decay_bwd, ret_gn_gain, w_ret_branch, q_norm, k_norm,
           w_att_branch, w_o, norm_ffn, w_router, w_expert_gate, w_expert_up, w_expert_down, norm_final):
    batch, seq, d = x.shape
    depth = norm_mix.shape[0]
    n_exp = w_router.shape[2]
    cap = CAPACITY_FACTOR * seq // n_exp
    assert seq % TOKEN_TILE == 0 and seq % ROUTE_TILE == 0
    assert (seq - ATT_FIRST_KV) % ATT_KV_TILE == 0 and ATT_KV_TILE % (2 * LANES) == 0
    assert ATT_FIRST_KV % (2 * LANES) == 0 and seq % ATT_Q_TILE == 0
    assert seq % CHUNK == 0 and (seq // CHUNK) % RET_UNROLL == 0 and seq % RET_NORM_ROWS == 0
    assert LANES % n_exp == 0 and d % LANES == 0 and seq % (8 * LANES // n_exp) == 0
    assert cap >= BIG_ROWS and (cap - BIG_ROWS) % SLAB_ALIGN == 0 and (cap - SMALL_ROWS) % SLAB_ALIGN == 0
    assert w_expert_gate.shape[3] % FF_TILE == 0 and cap % FFN_ROW_BLOCK == 0
    assert depth == 1, "the final RMSNorm is fused into the combine kernel of the single layer"
    l = 0
    cos2, sin2 = _rope_tables(seq)
    xf = x.reshape(batch * seq, d)
    lg = jnp.stack([jnp.log1p(-jnp.exp(ret_decay_fwd[l].astype(F32))),
                    jnp.log1p(-jnp.exp(ret_decay_bwd[l].astype(F32)))])
    rq, rk, rv, sg, aq, ak, av, gr, ga = _inproj(
        xf, norm_mix[l][None], w_in[l], cos2, sin2, q_norm[l][None], k_norm[l][None], seq)
    u_ret = _retention(lg, rq, rk, rv, sg, ret_gn_gain[l][None], batch, seq)
    y_att = _attention(aq, ak, av, batch, seq)
    w_r = jnp.pad(w_router[l].astype(F32), ((0, 0), (0, LANES - n_exp)))
    w_r_hi = w_r.astype(BF16)
    w_r_lo = (w_r - w_r_hi.astype(F32)).astype(BF16)
    x1, h2a, aff = _merge(xf, u_ret, y_att, gr, ga, w_ret_branch[l], w_att_branch[l], w_o[l],
                          norm_ffn[l][None], jnp.concatenate([w_r_hi, w_r_lo], axis=1), n_exp)
    idx, pos, slo = _route(h2a, aff, batch, seq, cap)
    y_slots = _ffn(idx, h2a, w_expert_gate[l], w_expert_up[l], w_expert_down[l])
    out = _combine(slo, x1, pos, y_slots, norm_final[None], seq)
    return out.reshape(batch, seq, d)
```
